```python
import math
import jax
import jax.numpy as jnp
from jax import lax
import numpy as np

D_MODEL = 2048
BATCH = 2
SEQ = 16384
DEPTH = 4

GRID_W = 64
CTX_LEN = 256
MIX_WIDTH = D_MODEL
ATTN_WIDTH = MIX_WIDTH // 2
HYENA_WIDTH = MIX_WIDTH // 4
POOL_WIDTH = MIX_WIDTH - ATTN_WIDTH - HYENA_WIDTH
HEAD_DIM = 128
N_HEADS = ATTN_WIDTH // HEAD_DIM
N_KV_HEADS = 2
KV_GROUP = N_HEADS // N_KV_HEADS
KV_WIDTH = N_KV_HEADS * HEAD_DIM
WINDOW = 128
BLOCK = 128
ROPE_BASE = 10000.0
HYENA_ORDER = 2
HYENA_SHORT = 3
HYENA_EMB_DIM = 33
HYENA_FILTER_HIDDEN = 64
HYENA_FAST_DECAY_PCT = 0.3
HYENA_SLOW_DECAY_PCT = 1.5
HYENA_DECAY_TARGET = 1e-2
POOL_WINDOWS = (2, 4, 8, 16)
POOL_GROUP = POOL_WIDTH // len(POOL_WINDOWS)
MLP_HIDDEN = 4 * D_MODEL
N_MOD = 6
EPS = 1e-6
NEG_INF = -1e30

Q_END = ATTN_WIDTH
K_END = Q_END + KV_WIDTH
V_END = K_END + KV_WIDTH
HY_END = V_END + (HYENA_ORDER + 1) * HYENA_WIDTH
IN_WIDTH = HY_END + POOL_WIDTH

kernel_name = "hybrid_dit_swa_hyena_pool_trunk"


def rmsnorm(x, g):
    xf = x.astype(jnp.float32)
    y = xf * lax.rsqrt(jnp.mean(xf * xf, axis=-1, keepdims=True) + EPS)
    return (y * g.astype(jnp.float32)).astype(x.dtype)


def modulate(h, shift, scale):
    return h * (1 + scale) + shift


def axial_rope(x, row, col):
    half = HEAD_DIM // 2
    quarter = half // 2
    inv_freq = ROPE_BASE ** (-jnp.arange(quarter, dtype=jnp.float32) / quarter)

    def rotate(xh, pos):
        ang = pos.astype(jnp.float32)[:, None] * inv_freq[None, :]
        cos = jnp.cos(ang)[None, :, None, :].astype(xh.dtype)
        sin = jnp.sin(ang)[None, :, None, :].astype(xh.dtype)
        x1, x2 = xh[..., :quarter], xh[..., quarter:]
        return jnp.concatenate([x1 * cos - x2 * sin, x2 * cos + x1 * sin], axis=-1)

    return jnp.concatenate([rotate(x[..., :half], row), rotate(x[..., half:], col)], axis=-1)


def sink_softmax(s, sink):
    m = jnp.maximum(jnp.max(s, axis=-1, keepdims=True), sink)
    p = jnp.exp(s - m)
    return p / (jnp.sum(p, axis=-1, keepdims=True) + jnp.exp(sink - m))


def context_attention(q, k, v, sink):
    b, n = q.shape[:2]
    qg = q.reshape(b, n, N_KV_HEADS, KV_GROUP, HEAD_DIM)
    s = jnp.einsum('bqkgd,bskd->bkgqs', qg, k).astype(jnp.float32) * HEAD_DIM ** -0.5
    p = sink_softmax(s, sink.astype(jnp.float32).reshape(1, N_KV_HEADS, KV_GROUP, 1, 1))
    o = jnp.einsum('bkgqs,bskd->bqkgd', p.astype(v.dtype), v)
    return o.reshape(b, n, ATTN_WIDTH)


def window_attention(q, k, v, k_ctx, v_ctx, sink):
    b, n = q.shape[:2]
    nb = n // BLOCK
    qb = q.reshape(b, nb, BLOCK, N_KV_HEADS, KV_GROUP, HEAD_DIM)

    def band(t):
        tp = jnp.pad(t, ((0, 0), (BLOCK, BLOCK), (0, 0), (0, 0))).reshape(b, nb + 2, BLOCK, N_KV_HEADS, HEAD_DIM)
        return jnp.concatenate([tp[:, :-2], tp[:, 1:-1], tp[:, 2:]], axis=2)

    kb, vb = band(k), band(v)
    scale = HEAD_DIM ** -0.5
    s_loc = jnp.einsum('bnqkgd,bnskd->bnkgqs', qb, kb).astype(jnp.float32) * scale
    blk = jnp.arange(nb)[:, None, None]
    qi = jnp.arange(BLOCK)[None, :, None]
    sj = jnp.arange(3 * BLOCK)[None, None, :]
    kpos = (blk - 1) * BLOCK + sj
    valid = (jnp.abs(sj - BLOCK - qi) <= WINDOW) & (kpos >= 0) & (kpos < n)
    s_loc = jnp.where(valid[None, :, None, None], s_loc, NEG_INF)
    s_ctx = jnp.einsum('bnqkgd,bskd->bnkgqs', qb, k_ctx).astype(jnp.float32) * scale
    p = sink_softmax(jnp.concatenate([s_loc, s_ctx], axis=-1),
                     sink.astype(jnp.float32).reshape(1, 1, N_KV_HEADS, KV_GROUP, 1, 1))
    p = p.astype(v.dtype)
    o = (jnp.einsum('bnkgqs,bnskd->bnqkgd', p[..., :3 * BLOCK], vb)
         + jnp.einsum('bnkgqs,bskd->bnqkgd', p[..., 3 * BLOCK:], v_ctx))
    return o.reshape(b, n, ATTN_WIDTH)


def short_conv(u, w, bias):
    n = u.shape[1]
    up = jnp.pad(u, ((0, 0), (1, 1), (0, 0)))
    return up[:, :n] * w[0] + up[:, 1:n + 1] * w[1] + up[:, 2:] * w[2] + bias


def hyena_filter(n, w1, b1, f1, w2, b2, f2, w3):
    t = jnp.linspace(0.0, 1.0, n, dtype=jnp.float32)[:, None]
    bands = (HYENA_EMB_DIM - 1) // 2
    omega = 2.0 * math.pi * jnp.arange(n, dtype=jnp.float32)[:, None] / n
    f = jnp.linspace(1e-4, bands - 1, bands, dtype=jnp.float32)[None, :]
    feats = jnp.concatenate([t, jnp.cos(f * omega), -jnp.sin(f * omega)], axis=-1).astype(w1.dtype)
    h = jnp.sin(f1 * (feats @ w1 + b1))
    h = jnp.sin(f2 * (h @ w2 + b2))
    h = (h @ w3).astype(jnp.float32).reshape(n, 2, HYENA_WIDTH)
    max_decay = math.log(HYENA_DECAY_TARGET) / HYENA_FAST_DECAY_PCT
    min_decay = math.log(HYENA_DECAY_TARGET) / HYENA_SLOW_DECAY_PCT
    deltas = jnp.abs(jnp.linspace(min_decay, max_decay, HYENA_WIDTH, dtype=jnp.float32))
    h = h * jnp.exp(-t[:, :, None] * deltas)
    h = h / jnp.sum(jnp.abs(h), axis=(0, 1), keepdims=True)
    h_fwd, h_bwd = h[:, 0], h[:, 1]
    k = jnp.concatenate([h_fwd, jnp.zeros((1, HYENA_WIDTH), jnp.float32), h_bwd[:0:-1]], axis=0)
    return k.at[0].add(h_bwd[0])


def fft_long_conv(u, k):
    n = u.shape[1]
    uf = jnp.fft.rfft(u.astype(jnp.float32), n=2 * n, axis=1)
    kf = jnp.fft.rfft(k, axis=0)
    y = jnp.fft.irfft(uf * kf[None], n=2 * n, axis=1)[:, :n]
    return y.astype(u.dtype)


def hyena_mixer(u, conv_w, conv_b, w1, b1, f1, w2, b2, f2, w3, bias):
    n = u.shape[1]
    z = short_conv(u, conv_w, conv_b)
    x0, x1, v = jnp.split(z, HYENA_ORDER + 1, axis=-1)
    k = hyena_filter(n, w1, b1, f1, w2, b2, f2, w3)
    v = v * x1
    y = fft_long_conv(v, k) + v * bias
    return y * x0


def pool_mixer(p, w_pool, scale):
    b, n, _ = p.shape
    pf = p.astype(jnp.float32)
    t = jnp.arange(n)
    outs = []
    for g, w in enumerate(POOL_WINDOWS):
        xg = pf[..., g * POOL_GROUP:(g + 1) * POOL_GROUP]
        h = w // 2
        cs = jnp.pad(jnp.cumsum(xg, axis=1), ((0, 0), (1, 0), (0, 0)))
        cs = jnp.pad(cs, ((0, 0), (h, h), (0, 0)), mode='edge')
        total = cs[:, 2 * h:2 * h + n] - cs[:, :n]
        count = (jnp.minimum(t + h, n) - jnp.maximum(t - h, 0)).astype(jnp.float32)
        outs.append(total / count[None, :, None] - xg)
    y = jnp.stack(outs, axis=2).astype(p.dtype)
    y = jnp.einsum('blgc,gcd->blgd', y, w_pool).reshape(b, n, POOL_WIDTH)
    return y * scale


def mix_branches(p, attn_out, hy_params, w_pool, pool_scale, g_branch, w_out):
    hy = hyena_mixer(p[..., V_END:HY_END], *hy_params)
    po = pool_mixer(p[..., HY_END:], w_pool, pool_scale)
    merged = jnp.concatenate([
        rmsnorm(attn_out, g_branch[:ATTN_WIDTH]),
        rmsnorm(hy, g_branch[ATTN_WIDTH:ATTN_WIDTH + HYENA_WIDTH]),
        rmsnorm(po, g_branch[ATTN_WIDTH + HYENA_WIDTH:]),
    ], axis=-1)
    return merged @ w_out


def sq_relu_mlp(h, w_up, w_down):
    return jnp.square(jax.nn.relu(h @ w_up)) @ w_down


def setup_inputs(seed: int = 0) -> dict:
    key = jax.random.key(seed)
    ks = jax.random.split(key, 32)
    f32 = jnp.float32

    def nrm(k, shape, s):
        return jax.random.normal(k, shape, f32) * s

    def gain(k, shape):
        return 1.0 + 0.05 * jax.random.normal(k, shape, f32)

    D, L = D_MODEL, DEPTH
    return {
        "x": nrm(ks[0], (BATCH, SEQ, D), 1.0),
        "c": nrm(ks[1], (BATCH, D), 1.0),
        "ctx": nrm(ks[2], (BATCH, CTX_LEN, D), 1.0),
        "c_ctx": nrm(ks[3], (D,), 1.0),
        "w_mod": nrm(ks[4], (L, D, N_MOD * D), D ** -0.5),
        "b_mod": nrm(ks[5], (L, N_MOD * D), 0.02),
        "g_pre_mix": gain(ks[6], (L, D)),
        "g_post_mix": gain(ks[7], (L, D)),
        "g_pre_mlp": gain(ks[8], (L, D)),
        "g_post_mlp": gain(ks[9], (L, D)),
        "w_in": nrm(ks[10], (L, D, IN_WIDTH), D ** -0.5),
        "w_out": nrm(ks[11], (L, MIX_WIDTH, D), MIX_WIDTH ** -0.5),
        "g_branch": gain(ks[12], (L, MIX_WIDTH)),
        "attn_sink": nrm(ks[13], (L, N_HEADS), 0.5),
        "hy_conv_w": nrm(ks[14], (L, HYENA_SHORT, (HYENA_ORDER + 1) * HYENA_WIDTH), HYENA_SHORT ** -0.5),
        "hy_conv_b": nrm(ks[15], (L, (HYENA_ORDER + 1) * HYENA_WIDTH), 0.02),
        "hy_w1": nrm(ks[16], (L, HYENA_EMB_DIM, HYENA_FILTER_HIDDEN), HYENA_EMB_DIM ** -0.5),
        "hy_b1": nrm(ks[17], (L, HYENA_FILTER_HIDDEN), 0.02),
        "hy_freq1": gain(ks[18], (L, HYENA_FILTER_HIDDEN)),
        "hy_w2": nrm(ks[19], (L, HYENA_FILTER_HIDDEN, HYENA_FILTER_HIDDEN), HYENA_FILTER_HIDDEN ** -0.5),
        "hy_b2": nrm(ks[20], (L, HYENA_FILTER_HIDDEN), 0.02),
        "hy_freq2": gain(ks[21], (L, HYENA_FILTER_HIDDEN)),
        "hy_w3": nrm(ks[22], (L, HYENA_FILTER_HIDDEN, 2 * (HYENA_ORDER - 1) * HYENA_WIDTH), HYENA_FILTER_HIDDEN ** -0.5),
        "hy_bias": nrm(ks[23], (L, HYENA_WIDTH), 1.0),
        "pool_w": nrm(ks[24], (L, len(POOL_WINDOWS), POOL_GROUP, POOL_GROUP), POOL_GROUP ** -0.5),
        "pool_scale": 1.0 + 0.1 * jax.random.normal(ks[25], (L, POOL_WIDTH), f32),
        "w_up": nrm(ks[26], (L, D, MLP_HIDDEN), D ** -0.5),
        "w_down": nrm(ks[27], (L, MLP_HIDDEN, D), MLP_HIDDEN ** -0.5),
    }


def reference(x, c, ctx, c_ctx, w_mod, b_mod, g_pre_mix, g_post_mix, g_pre_mlp, g_post_mlp,
              w_in, w_out, g_branch, attn_sink, hy_conv_w, hy_conv_b, hy_w1, hy_b1, hy_freq1,
              hy_w2, hy_b2, hy_freq2, hy_w3, hy_bias, pool_w, pool_scale, w_up, w_down):
    b, n, _ = x.shape
    rows = n // GRID_W
    row = jnp.repeat(jnp.arange(rows, dtype=jnp.int32), GRID_W)
    col = jnp.tile(jnp.arange(GRID_W, dtype=jnp.int32), rows)
    for i in range(DEPTH):
        last = i == DEPTH - 1
        hy_params = (hy_conv_w[i], hy_conv_b[i], hy_w1[i], hy_b1[i], hy_freq1[i],
                     hy_w2[i], hy_b2[i], hy_freq2[i], hy_w3[i], hy_bias[i])
        mod_x = jax.nn.silu(c) @ w_mod[i] + b_mod[i]
        mod_c = jax.nn.silu(c_ctx) @ w_mod[i] + b_mod[i]
        sh1, sc1, gt1, sh2, sc2, gt2 = jnp.split(mod_x[:, None, :], N_MOD, axis=-1)
        csh1, csc1, cgt1, csh2, csc2, cgt2 = jnp.split(mod_c, N_MOD, axis=-1)

        hx = modulate(rmsnorm(x, g_pre_mix[i]), sh1, sc1)
        hc = modulate(rmsnorm(ctx, g_pre_mix[i]), csh1, csc1)
        px = hx @ w_in[i]
        if last:
            kv_ctx = hc @ w_in[i][:, Q_END:V_END]
        else:
            pc = hc @ w_in[i]
            kv_ctx = pc[..., Q_END:V_END]
        k_ctx = kv_ctx[..., :KV_WIDTH].reshape(b, -1, N_KV_HEADS, HEAD_DIM)
        v_ctx = kv_ctx[..., KV_WIDTH:].reshape(b, -1, N_KV_HEADS, HEAD_DIM)

        q = axial_rope(px[..., :Q_END].reshape(b, n, N_HEADS, HEAD_DIM), row, col)
        k = axial_rope(px[..., Q_END:K_END].reshape(b, n, N_KV_HEADS, HEAD_DIM), row, col)
        v = px[..., K_END:V_END].reshape(b, n, N_KV_HEADS, HEAD_DIM)
        attn_x = window_attention(q, k, v, k_ctx, v_ctx, attn_sink[i])
        ox = mix_branches(px, attn_x, hy_params, pool_w[i], pool_scale[i], g_branch[i], w_out[i])
        x = x + gt1 * rmsnorm(ox, g_post_mix[i])

        hx2 = modulate(rmsnorm(x, g_pre_mlp[i]), sh2, sc2)
        x = x + gt2 * rmsnorm(sq_relu_mlp(hx2, w_up[i], w_down[i]), g_post_mlp[i])

        if not last:
            q_ctx = pc[..., :Q_END].reshape(b, -1, N_HEADS, HEAD_DIM)
            attn_c = context_attention(q_ctx, k_ctx, v_ctx, attn_sink[i])
            oc = mix_branches(pc, attn_c, hy_params, pool_w[i], pool_scale[i], g_branch[i], w_out[i])
            ctx = ctx + cgt1 * rmsnorm(oc, g_post_mix[i])
            hc2 = modulate(rmsnorm(ctx, g_pre_mlp[i]), csh2, csc2)
            ctx = ctx + cgt2 * rmsnorm(sq_relu_mlp(hc2, w_up[i], w_down[i]), g_post_mlp[i])
    return x
```

```python
import functools
import math

import numpy as np
import jax
import jax.numpy as jnp
from jax import lax
from jax.experimental import pallas as pl
from jax.experimental.pallas import tpu as pltpu

F32 = jnp.float32
BF16 = jnp.bfloat16

D_MODEL = 2048
DEPTH = 4
GRID_W = 64
ATTN_WIDTH = D_MODEL // 2
HYENA_WIDTH = D_MODEL // 4
POOL_WIDTH = D_MODEL - ATTN_WIDTH - HYENA_WIDTH
HEAD_DIM = 128
N_HEADS = ATTN_WIDTH // HEAD_DIM
N_KV_HEADS = 2
KV_GROUP = N_HEADS // N_KV_HEADS
KV_WIDTH = N_KV_HEADS * HEAD_DIM
WINDOW = 128
BLOCK = 128
ROPE_BASE = 10000.0
HYENA_EMB_DIM = 33
HYENA_FILTER_HIDDEN = 64
HYENA_FAST_DECAY_PCT = 0.3
HYENA_SLOW_DECAY_PCT = 1.5
HYENA_DECAY_TARGET = 1e-2
POOL_WINDOWS = (2, 4, 8, 16)
POOL_GROUP = POOL_WIDTH // len(POOL_WINDOWS)
MLP_HIDDEN = 4 * D_MODEL
N_MOD = 6
EPS = 1e-6
NEG_INF = -1e30

Q_END = ATTN_WIDTH
K_END = Q_END + KV_WIDTH
V_END = K_END + KV_WIDTH
HY_END = V_END + 3 * HYENA_WIDTH
IN_WIDTH = HY_END + POOL_WIDTH

LANES = 128
SUBLANES = 8
FFT_N2 = 128
POOL_HALO = 16
VMEM_LIMIT = 56 * 1024 * 1024


def _params(sem, vmem=VMEM_LIMIT):
    return pltpu.CompilerParams(dimension_semantics=sem, vmem_limit_bytes=vmem)


def _split(x):
    hi = x.astype(BF16)
    lo = (x - hi.astype(F32)).astype(BF16)
    return hi, lo


def _dot(a, b):
    return jnp.dot(a, b, preferred_element_type=F32)


def _dot3(ah, al, bh, bl):
    return _dot(ah, bh) + _dot(ah, bl) + _dot(al, bh)


def _rms(x, g):
    return x * lax.rsqrt(jnp.mean(x * x, axis=-1, keepdims=True) + EPS) * g


def _mod_kernel(c_ref, w_ref, b_ref, o_ref):
    c = c_ref[...]
    s = c / (1.0 + jnp.exp(-c))
    sh, sl = _split(s)
    wh, wl = _split(w_ref[...])
    o_ref[...] = _dot3(sh, sl, wh, wl) + b_ref[...]


def _modulation(cond, w_mod, b_mod):
    depth, d, width = w_mod.shape
    tn = 1024
    return pl.pallas_call(
        _mod_kernel,
        out_shape=jax.ShapeDtypeStruct((depth, SUBLANES, width), F32),
        grid=(depth, width // tn),
        in_specs=[
            pl.BlockSpec((SUBLANES, d), lambda l, j: (0, 0)),
            pl.BlockSpec((None, d, tn), lambda l, j: (l, 0, j)),
            pl.BlockSpec((None, 1, tn), lambda l, j: (l, 0, j)),
        ],
        out_specs=pl.BlockSpec((None, SUBLANES, tn), lambda l, j: (l, 0, j)),
        compiler_params=_params(("parallel", "parallel")),
        name="modulation",
    )(cond, w_mod, b_mod.reshape(depth, 1, width))


def _inproj_kernel(*refs, tm, tn, rope):
    if rope:
        x_ref, g_ref, sh_ref, sc_ref, w_ref, cos_ref, sin_ref, o_ref, h_ref = refs
    else:
        x_ref, g_ref, sh_ref, sc_ref, w_ref, o_ref, h_ref = refs
    j = pl.program_id(2)

    @pl.when(j == 0)
    def _():
        y = _rms(x_ref[...], g_ref[...])
        h_ref[...] = (y * (1.0 + sc_ref[...]) + sh_ref[...]).astype(BF16)

    acc = _dot(h_ref[...], w_ref[...])
    if not rope:
        o_ref[...] = acc
        return

    nch = tn // LANES
    rope_chunks = K_END // LANES
    nfull, nrem = divmod(rope_chunks, nch)
    cos = cos_ref[...]
    sin = sin_ref[...]
    lane = lax.broadcasted_iota(jnp.int32, (tm, LANES), 1)
    first = (lane % (HEAD_DIM // 2)) < (HEAD_DIM // 4)

    def rot(a):
        swapped = jnp.where(first, pltpu.roll(a, LANES - HEAD_DIM // 4, 1), pltpu.roll(a, HEAD_DIM // 4, 1))
        return a * cos + swapped * sin

    def write(n_rot):
        for ch in range(nch):
            a = acc[:, ch * LANES:(ch + 1) * LANES]
            o_ref[:, ch * LANES:(ch + 1) * LANES] = rot(a) if ch < n_rot else a

    @pl.when(j < nfull)
    def _():
        write(nch)

    if nrem:
        @pl.when(j == nfull)
        def _():
            write(nrem)

    @pl.when(j >= nfull + (1 if nrem else 0))
    def _():
        write(0)


def _in_projection(x, g, shift, scale, w, rope_tables=None):
    b, n, d = x.shape
    width = w.shape[1]
    tm = min(1024, n)
    tn = 512
    rope = rope_tables is not None
    in_specs = [
        pl.BlockSpec((None, tm, d), lambda bi, i, j: (bi, i, 0)),
        pl.BlockSpec((1, d), lambda bi, i, j: (0, 0)),
        pl.BlockSpec((None, 1, d), lambda bi, i, j: (bi, 0, 0)),
        pl.BlockSpec((None, 1, d), lambda bi, i, j: (bi, 0, 0)),
        pl.BlockSpec((d, tn), lambda bi, i, j: (0, j)),
    ]
    args = [x, g.reshape(1, d), shift, scale, w]
    if rope:
        in_specs += [pl.BlockSpec((tm, LANES), lambda bi, i, j: (i, 0))] * 2
        args += list(rope_tables)
    return pl.pallas_call(
        functools.partial(_inproj_kernel, tm=tm, tn=tn, rope=rope),
        out_shape=jax.ShapeDtypeStruct((b, n, width), F32),
        grid=(b, n // tm, width // tn),
        in_specs=in_specs,
        out_specs=pl.BlockSpec((None, tm, tn), lambda bi, i, j: (bi, i, j)),
        scratch_shapes=[pltpu.VMEM((tm, d), BF16)],
        compiler_params=_params(("parallel", "parallel", "arbitrary")),
        name="in_projection_rope" if rope else "in_projection",
    )(*args)


def _rope_tables(n):
    quarter = HEAD_DIM // 4
    inv_freq = ROPE_BASE ** (-jnp.arange(quarter, dtype=F32) / quarter)
    t = jnp.arange(n, dtype=jnp.int32)
    row = (t // GRID_W).astype(F32)[:, None] * inv_freq[None, :]
    col = (t % GRID_W).astype(F32)[:, None] * inv_freq[None, :]
    cos = jnp.concatenate([jnp.cos(row), jnp.cos(row), jnp.cos(col), jnp.cos(col)], axis=-1)
    sin = jnp.concatenate([-jnp.sin(row), jnp.sin(row), -jnp.sin(col), jnp.sin(col)], axis=-1)
    return cos, sin


def _softmax_pv(parts, sink_col):
    m = sink_col
    for s, _ in parts:
        m = jnp.maximum(m, jnp.max(s, axis=-1, keepdims=True))
    den = jnp.exp(sink_col - m)
    out = None
    for s, v in parts:
        p = jnp.exp(s - m)
        den = den + jnp.sum(p, axis=-1, keepdims=True)
        pv = _dot(p.astype(BF16), v)
        out = pv if out is None else out + pv
    return out / den


def _qk(q, k):
    return lax.dot_general(q, k, (((1,), (1,)), ((), ())), preferred_element_type=F32)


def _attn_kernel(*refs, tq, nb, local):
    if local:
        (sink_ref, q_ref, km_ref, kp_ref, kn_ref, vm_ref, vp_ref, vn_ref, kvc_ref, band_ref,
         o_ref, kall, vall) = refs
    else:
        sink_ref, q_ref, kvc_ref, o_ref = refs
    i = pl.program_id(1)
    scale = HEAD_DIM ** -0.5
    kvc = kvc_ref[...].astype(BF16)
    nsub = tq // BLOCK
    if local:
        kall[0:BLOCK] = kp_ref[...].astype(BF16)
        kall[BLOCK:BLOCK + tq] = km_ref[...].astype(BF16)
        kall[BLOCK + tq:] = kn_ref[...].astype(BF16)
        vall[0:BLOCK] = vp_ref[...].astype(BF16)
        vall[BLOCK:BLOCK + tq] = vm_ref[...].astype(BF16)
        vall[BLOCK + tq:] = vn_ref[...].astype(BF16)
        band = band_ref[...]
        col = lax.broadcasted_iota(jnp.int32, (1, 3 * BLOCK), 1)
    for jb in range(nsub):
        rows = slice(jb * BLOCK, (jb + 1) * BLOCK)
        if local:
            blk = i * nsub + jb
            pen_prev = jnp.where(blk == 0, NEG_INF, 0.0).astype(F32)
            pen_next = jnp.where(blk == nb - 1, NEG_INF, 0.0).astype(F32)
            rowbias = jnp.where(col < BLOCK, pen_prev, jnp.where(col >= 2 * BLOCK, pen_next, 0.0))
        for g in range(N_KV_HEADS):
            heads = [g * KV_GROUP + h for h in range(KV_GROUP)]
            qg = jnp.concatenate([q_ref[rows, h * HEAD_DIM:(h + 1) * HEAD_DIM] for h in heads], axis=0).astype(BF16)
            sink_col = jnp.concatenate([jnp.full((BLOCK, 1), sink_ref[h], F32) for h in heads], axis=0)
            kc = kvc[:, g * HEAD_DIM:(g + 1) * HEAD_DIM]
            vc = kvc[:, KV_WIDTH + g * HEAD_DIM:KV_WIDTH + (g + 1) * HEAD_DIM]
            parts = []
            if local:
                kl = kall[jb * BLOCK:(jb + 3) * BLOCK, g * HEAD_DIM:(g + 1) * HEAD_DIM]
                vl = vall[jb * BLOCK:(jb + 3) * BLOCK, g * HEAD_DIM:(g + 1) * HEAD_DIM]
                parts.append((_qk(qg, kl) * scale + band + rowbias, vl))
            parts.append((_qk(qg, kc) * scale, vc))
            o = _softmax_pv(parts, sink_col)
            for hi, h in enumerate(heads):
                o_ref[rows, h * HEAD_DIM:(h + 1) * HEAD_DIM] = o[hi * BLOCK:(hi + 1) * BLOCK]


def _band_bias():
    qi = np.arange(KV_GROUP * BLOCK)[:, None] % BLOCK
    sj = np.arange(3 * BLOCK)[None, :]
    return jnp.asarray(np.where(np.abs(sj - BLOCK - qi) <= WINDOW, 0.0, NEG_INF), dtype=F32)


def _attention(p, kv_ctx, sink, local):
    b, n, _ = p.shape
    c = kv_ctx.shape[1]
    tq = min(512, n)
    nsub = tq // BLOCK
    nb = n // BLOCK
    kcol = Q_END // KV_WIDTH
    vcol = K_END // KV_WIDTH
    smem = pl.BlockSpec(memory_space=pltpu.SMEM)
    q_spec = pl.BlockSpec((None, tq, ATTN_WIDTH), lambda bi, i: (bi, i, 0))
    kvc_spec = pl.BlockSpec((None, c, 2 * KV_WIDTH), lambda bi, i: (bi, 0, 0))
    if local:
        def main(colblk):
            return pl.BlockSpec((None, tq, KV_WIDTH), lambda bi, i: (bi, i, colblk))

        def prev(colblk):
            return pl.BlockSpec((None, BLOCK, KV_WIDTH), lambda bi, i: (bi, jnp.maximum(i * nsub - 1, 0), colblk))

        def nxt(colblk):
            return pl.BlockSpec((None, BLOCK, KV_WIDTH), lambda bi, i: (bi, jnp.minimum((i + 1) * nsub, nb - 1), colblk))

        in_specs = [smem, q_spec, main(kcol), prev(kcol), nxt(kcol), main(vcol), prev(vcol), nxt(vcol), kvc_spec,
                    pl.BlockSpec((KV_GROUP * BLOCK, 3 * BLOCK), lambda bi, i: (0, 0))]
        args = [sink, p, p, p, p, p, p, p, kv_ctx, _band_bias()]
        scratch = [pltpu.VMEM((tq + 2 * BLOCK, KV_WIDTH), BF16)] * 2
    else:
        in_specs = [smem, q_spec, kvc_spec]
        args = [sink, p, kv_ctx]
        scratch = []
    return pl.pallas_call(
        functools.partial(_attn_kernel, tq=tq, nb=nb, local=local),
        out_shape=jax.ShapeDtypeStruct((b, n, ATTN_WIDTH), F32),
        grid=(b, n // tq),
        in_specs=in_specs,
        out_specs=pl.BlockSpec((None, tq, ATTN_WIDTH), lambda bi, i: (bi, i, 0)),
        scratch_shapes=scratch,
        compiler_params=_params(("parallel", "parallel")),
        name="window_attention" if local else "context_attention",
    )(*args)


GROUPS = FFT_N2 // SUBLANES


def _gm_shape(lead, t1, width):
    return (*lead, width // LANES, GROUPS, t1, SUBLANES, LANES)


def _store_group_major(o_ref, val, tl):
    for t1 in range(tl // FFT_N2):
        for jg in range(GROUPS):
            r0 = (t1 * GROUPS + jg) * SUBLANES
            for cc in range(val.shape[1] // LANES):
                o_ref[cc, jg, t1] = val[r0:r0 + SUBLANES, cc * LANES:(cc + 1) * LANES]


def _load_group_major(ref, tl):
    ncc = ref.shape[0]
    return jnp.concatenate(
        [jnp.concatenate([ref[cc, jg, t1] for cc in range(ncc)], axis=1)
         for t1 in range(tl // FFT_N2) for jg in range(GROUPS)], axis=0)


def _hyena_prep_kernel(u_ref, p_ref, n_ref, w_ref, b_ref, vx_ref, x0_ref, *, tl, nt, group_major):
    i = pl.program_id(1)
    u = u_ref[...]
    prev_row = jnp.where(i > 0, p_ref[SUBLANES - 1:SUBLANES, :], 0.0)
    next_row = jnp.where(i < nt - 1, n_ref[0:1, :], 0.0)
    row = lax.broadcasted_iota(jnp.int32, u.shape, 0)
    um = jnp.where(row == 0, prev_row, pltpu.roll(u, 1, 0))
    up = jnp.where(row == tl - 1, next_row, pltpu.roll(u, tl - 1, 0))
    z = um * w_ref[0:1, :] + u * w_ref[1:2, :] + up * w_ref[2:3, :] + b_ref[...]
    x0 = z[:, :HYENA_WIDTH]
    vx = z[:, 2 * HYENA_WIDTH:] * z[:, HYENA_WIDTH:2 * HYENA_WIDTH]
    if group_major:
        _store_group_major(vx_ref, vx, tl)
        _store_group_major(x0_ref, x0, tl)
    else:
        vx_ref[...] = vx
        x0_ref[...] = x0


def _hyena_prep(p, conv_w, conv_b, group_major):
    b, n, _ = p.shape
    tl = min(512, n)
    nt = n // tl
    hw = 3 * HYENA_WIDTH
    colblk = V_END // hw
    assert colblk * hw == V_END
    nrow8 = n // SUBLANES
    per = tl // SUBLANES
    if group_major:
        shape = _gm_shape((b,), n // FFT_N2, HYENA_WIDTH)
        out_spec = pl.BlockSpec((None, *_gm_shape((), tl // FFT_N2, HYENA_WIDTH)), lambda bi, i: (bi, 0, 0, i, 0, 0))
    else:
        shape = (b, n, HYENA_WIDTH)
        out_spec = pl.BlockSpec((None, tl, HYENA_WIDTH), lambda bi, i: (bi, i, 0))
    return pl.pallas_call(
        functools.partial(_hyena_prep_kernel, tl=tl, nt=nt, group_major=group_major),
        out_shape=[jax.ShapeDtypeStruct(shape, F32)] * 2,
        grid=(b, nt),
        in_specs=[
            pl.BlockSpec((None, tl, hw), lambda bi, i: (bi, i, colblk)),
            pl.BlockSpec((None, SUBLANES, hw), lambda bi, i: (bi, jnp.maximum(i * per - 1, 0), colblk)),
            pl.BlockSpec((None, SUBLANES, hw), lambda bi, i: (bi, jnp.minimum((i + 1) * per, nrow8 - 1), colblk)),
            pl.BlockSpec((3, hw), lambda bi, i: (0, 0)),
            pl.BlockSpec((1, hw), lambda bi, i: (0, 0)),
        ],
        out_specs=[out_spec, out_spec],
        compiler_params=_params(("parallel", "parallel")),
        name="hyena_prep",
    )(p, p, p, conv_w, conv_b.reshape(1, hw))


def _filter_kernel(ft_ref, w1_ref, b1_ref, f1_ref, w2_ref, b2_ref, f2_ref, w3_ref, dl_ref, h_ref, s_ref,
                   *, tl, n, group_major):
    i = pl.program_id(0)

    def dense(a, w_ref):
        ah, al = _split(a)
        wh, wl = _split(w_ref[...])
        return _dot3(ah, al, wh, wl)

    h = jnp.sin(f1_ref[...] * (dense(ft_ref[...], w1_ref) + b1_ref[...]))
    h = jnp.sin(f2_ref[...] * (dense(h, w2_ref) + b2_ref[...]))
    h = dense(h, w3_ref)
    t = (i * tl + lax.broadcasted_iota(jnp.int32, (tl, HYENA_WIDTH), 0)).astype(F32) / float(n - 1)
    decay = jnp.exp(-t * dl_ref[...])
    h = h * jnp.concatenate([decay, decay], axis=1)
    if group_major:
        _store_group_major(h_ref, h, tl)
    else:
        h_ref[...] = h

    @pl.when(i == 0)
    def _():
        s_ref[...] = jnp.zeros_like(s_ref)

    s_ref[...] += jnp.sum(jnp.abs(h).reshape(tl // SUBLANES, SUBLANES, 2 * HYENA_WIDTH), axis=0)


def _filter_features(n):
    t = jnp.linspace(0.0, 1.0, n, dtype=F32)[:, None]
    bands = (HYENA_EMB_DIM - 1) // 2
    omega = 2.0 * math.pi * jnp.arange(n, dtype=F32)[:, None] / n
    f = jnp.linspace(1e-4, bands - 1, bands, dtype=F32)[None, :]
    feats = jnp.concatenate([t, jnp.cos(f * omega), -jnp.sin(f * omega)], axis=-1)
    return jnp.pad(feats, ((0, 0), (0, LANES - HYENA_EMB_DIM)))


def _filter_deltas():
    max_decay = math.log(HYENA_DECAY_TARGET) / HYENA_FAST_DECAY_PCT
    min_decay = math.log(HYENA_DECAY_TARGET) / HYENA_SLOW_DECAY_PCT
    return jnp.abs(jnp.linspace(min_decay, max_decay, HYENA_WIDTH, dtype=F32)).reshape(1, HYENA_WIDTH)


def _hyena_filter(n, feats, deltas, w1, b1, f1, w2, b2, f2, w3, group_major):
    tl = min(512, n)
    hid = LANES
    pad_h = hid - HYENA_FILTER_HIDDEN
    w1p = jnp.pad(w1, ((0, LANES - HYENA_EMB_DIM), (0, pad_h)))
    w2p = jnp.pad(w2, ((0, pad_h), (0, pad_h)))
    w3p = jnp.pad(w3, ((0, pad_h), (0, 0)))
    vec = lambda v: jnp.pad(v, (0, pad_h)).reshape(1, hid)
    hw2 = 2 * HYENA_WIDTH
    full = lambda shape: pl.BlockSpec(shape, lambda i: (0,) * len(shape))
    if group_major:
        shape = _gm_shape((), n // FFT_N2, hw2)
        out_spec = pl.BlockSpec(_gm_shape((), tl // FFT_N2, hw2), lambda i: (0, 0, i, 0, 0))
    else:
        shape = (n, hw2)
        out_spec = pl.BlockSpec((tl, hw2), lambda i: (i, 0))
    return pl.pallas_call(
        functools.partial(_filter_kernel, tl=tl, n=n, group_major=group_major),
        out_shape=[jax.ShapeDtypeStruct(shape, F32), jax.ShapeDtypeStruct((SUBLANES, hw2), F32)],
        grid=(n // tl,),
        in_specs=[pl.BlockSpec((tl, LANES), lambda i: (i, 0)), full((LANES, hid)), full((1, hid)), full((1, hid)),
                  full((hid, hid)), full((1, hid)), full((1, hid)), full((hid, hw2)), full((1, HYENA_WIDTH))],
        out_specs=[out_spec, full((SUBLANES, hw2))],
        compiler_params=_params(("arbitrary",)),
        name="hyena_filter",
    )(feats, w1p, vec(b1), vec(f1), w2p, vec(b2), vec(f2), w3p, deltas)


def _stack_complex(m):
    return np.block([[m.real, -m.imag], [m.imag, m.real]])


def _hilo(m):
    m = jnp.asarray(m, dtype=F32)
    return _split(m)


@functools.lru_cache(maxsize=None)
def _fft_constants(n):
    m = 2 * n
    n2 = FFT_N2
    n1 = m // n2
    n1h = n1 // 2
    k1 = np.arange(n1)
    t1 = np.arange(n1h)
    f1 = np.exp(-2j * np.pi * np.outer(k1, t1) / n1)
    f3 = np.exp(2j * np.pi * np.outer(t1, k1) / n1) / m
    k2 = np.arange(n2)
    t2 = np.arange(n2)
    w2 = np.exp(-2j * np.pi * np.outer(k2, t2) / n2)
    tw = np.exp(-2j * np.pi * np.outer(k1, t2) / m)
    return dict(
        n1=n1, n1h=n1h,
        f1c=_stack_complex(f1), f1r=np.concatenate([f1.real, f1.imag], axis=0),
        f3c=_stack_complex(f3),
        w2r=w2.real.astype(np.float32), w2i=w2.imag.astype(np.float32),
        twr=tw.real.astype(np.float32), twi=tw.imag.astype(np.float32),
    )


def _stage2_tables(n):
    c = _fft_constants(n)
    w2r, w2i = jnp.asarray(c["w2r"])[None], jnp.asarray(c["w2i"])[None]
    twr, twi = jnp.asarray(c["twr"])[:, None, :], jnp.asarray(c["twi"])[:, None, :]
    gr = w2r * twr - w2i * twi
    gi = w2r * twi + w2i * twr
    g = jnp.concatenate([jnp.concatenate([gr, -gi], axis=2), jnp.concatenate([gi, gr], axis=2)], axis=1)
    gh, gl = _split(g)
    gt = jnp.swapaxes(g, 1, 2)
    gth, gtl = _split(gt)
    return gh, gl, gth, gtl


def _s1_kernel(x_ref, fh_ref, fl_ref, o_ref, *, nparts, ncw, n1, n1h):
    fh = fh_ref[...]
    fl = fl_ref[...]
    for r in range(SUBLANES):
        rows = pl.ds(r, n1h, stride=SUBLANES)
        xs = jnp.concatenate(
            [jnp.concatenate([x_ref[p, cc, rows, :] for p in range(nparts)], axis=0) for cc in range(ncw)], axis=1)
        xh, xl = _split(xs)
        res = _dot3(fh, fl, xh, xl)
        for ri in range(2):
            for cc in range(ncw):
                o_ref[ri, cc, pl.ds(r, n1, stride=SUBLANES), :] = res[ri * n1:(ri + 1) * n1, cc * LANES:(cc + 1) * LANES]


def _fft_stage1(x, fmat, n1, n1h, ncw):
    nparts, ncc, groups = x.shape[:3]
    fh, fl = _hilo(fmat)
    return pl.pallas_call(
        functools.partial(_s1_kernel, nparts=nparts, ncw=ncw, n1=n1, n1h=n1h),
        out_shape=jax.ShapeDtypeStruct((2, ncc, groups, n1 * SUBLANES, LANES), F32),
        grid=(groups, ncc // ncw),
        in_specs=[
            pl.BlockSpec((nparts, ncw, None, n1h * SUBLANES, LANES), lambda j, ci: (0, ci, j, 0, 0)),
            pl.BlockSpec(fh.shape, lambda j, ci: (0, 0)),
            pl.BlockSpec(fl.shape, lambda j, ci: (0, 0)),
        ],
        out_specs=pl.BlockSpec((2, ncw, None, n1 * SUBLANES, LANES), lambda j, ci: (0, ci, j, 0, 0)),
        compiler_params=_params(("parallel", "parallel")),
        name="fft_stage1",
    )(x, fh, fl)


def _load_k1(a_ref, q):
    ncc = a_ref.shape[1]
    return jnp.concatenate([a_ref[:, cc, :, q].reshape(2 * FFT_N2, LANES) for cc in range(ncc)], axis=1)


def _filter_spectrum_kernel(a_ref, gh_ref, gl_ref, s_ref, o_ref, *, kg):
    s = jnp.sum(s_ref[...], axis=0, keepdims=True)
    inv = 1.0 / (s[:, :HYENA_WIDTH] + s[:, HYENA_WIDTH:])
    half = FFT_N2
    for q in range(kg):
        ah, al = _split(_load_k1(a_ref, q))
        h = _dot3(gh_ref[q], gl_ref[q], ah, al)
        hf = h[:, :HYENA_WIDTH]
        hb = h[:, HYENA_WIDTH:]
        o_ref[q, :half] = (hf[:half] + hb[:half]) * inv
        o_ref[q, half:] = (hf[half:] - hb[half:]) * inv


def _k1_spec(kg, width):
    return pl.BlockSpec((2, width // LANES, GROUPS, kg, SUBLANES, LANES), lambda i: (0, 0, 0, i, 0, 0))


def _filter_spectrum(a, gh, gl, sums, n1):
    kg = 4
    a6 = a.reshape(2, 2 * HYENA_WIDTH // LANES, GROUPS, n1, SUBLANES, LANES)
    tspec = pl.BlockSpec((kg, 2 * FFT_N2, 2 * FFT_N2), lambda i: (i, 0, 0))
    return pl.pallas_call(
        functools.partial(_filter_spectrum_kernel, kg=kg),
        out_shape=jax.ShapeDtypeStruct((n1, 2 * FFT_N2, HYENA_WIDTH), F32),
        grid=(n1 // kg,),
        in_specs=[_k1_spec(kg, 2 * HYENA_WIDTH), tspec, tspec,
                  pl.BlockSpec((SUBLANES, 2 * HYENA_WIDTH), lambda i: (0, 0))],
        out_specs=pl.BlockSpec((kg, 2 * FFT_N2, HYENA_WIDTH), lambda i: (i, 0, 0)),
        compiler_params=_params(("parallel",)),
        name="filter_spectrum",
    )(a6, gh, gl, sums)


def _s2_kernel(a_ref, kf_ref, gh_ref, gl_ref, gth_ref, gtl_ref, o_ref, *, kg):
    half = FFT_N2
    for q in range(kg):
        ah, al = _split(_load_k1(a_ref, q))
        x = _dot3(gh_ref[q], gl_ref[q], ah, al)
        xr, xi = x[:half], x[half:]
        kr, ki = kf_ref[q, :half], kf_ref[q, half:]
        y = jnp.concatenate([xr * kr - xi * ki, xr * ki + xi * kr], axis=0)
        yh, yl = _split(y)
        bt = _dot3(gth_ref[q], gtl_ref[q], yh, yl)
        for cc in range(HYENA_WIDTH // LANES):
            o_ref[:, cc, :, q] = bt[:, cc * LANES:(cc + 1) * LANES].reshape(2, GROUPS, SUBLANES, LANES)


def _fft_stage2(a, kf, tables, n1):
    kg = 4
    a6 = a.reshape(2, HYENA_WIDTH // LANES, GROUPS, n1, SUBLANES, LANES)
    tspec = pl.BlockSpec((kg, 2 * FFT_N2, 2 * FFT_N2), lambda i: (i, 0, 0))
    dspec = _k1_spec(kg, HYENA_WIDTH)
    out = pl.pallas_call(
        functools.partial(_s2_kernel, kg=kg),
        out_shape=jax.ShapeDtypeStruct(a6.shape, F32),
        grid=(n1 // kg,),
        in_specs=[dspec, pl.BlockSpec((kg, 2 * FFT_N2, HYENA_WIDTH), lambda i: (i, 0, 0)), tspec, tspec, tspec, tspec],
        out_specs=dspec,
        compiler_params=_params(("parallel",)),
        name="fft_stage2",
    )(a6, kf, *tables)
    return out.reshape(a.shape)


def _s3_kernel(b_ref, vx_ref, x0_ref, bias_ref, fh_ref, fl_ref, o_ref, *, ncw, n1, n1h):
    fh = fh_ref[...]
    fl = fl_ref[...]
    for r in range(SUBLANES):
        krows = pl.ds(r, n1, stride=SUBLANES)
        z = jnp.concatenate(
            [jnp.concatenate([b_ref[ri, cc, krows, :] for ri in range(2)], axis=0) for cc in range(ncw)], axis=1)
        zh, zl = _split(z)
        y = _dot3(fh, fl, zh, zl)
        trows = pl.ds(r, n1h, stride=SUBLANES)
        for p in range(2):
            for cc in range(ncw):
                yy = y[p * n1h:(p + 1) * n1h, cc * LANES:(cc + 1) * LANES]
                o_ref[p, cc, trows, :] = (yy + vx_ref[p, cc, trows, :] * bias_ref[cc]) * x0_ref[p, cc, trows, :]


def _fft_stage3(bt, vx, x0, bias, fmat, n1, n1h, ncw):
    ncc, groups = bt.shape[1:3]
    fh, fl = _hilo(fmat)
    tspec = pl.BlockSpec((2, ncw, None, n1h * SUBLANES, LANES), lambda j, ci: (0, ci, j, 0, 0))
    return pl.pallas_call(
        functools.partial(_s3_kernel, ncw=ncw, n1=n1, n1h=n1h),
        out_shape=jax.ShapeDtypeStruct(vx.shape, F32),
        grid=(groups, ncc // ncw),
        in_specs=[
            pl.BlockSpec((2, ncw, None, n1 * SUBLANES, LANES), lambda j, ci: (0, ci, j, 0, 0)),
            tspec, tspec,
            pl.BlockSpec((ncw, 1, LANES), lambda j, ci: (ci, 0, 0)),
            pl.BlockSpec(fh.shape, lambda j, ci: (0, 0)),
            pl.BlockSpec(fl.shape, lambda j, ci: (0, 0)),
        ],
        out_specs=tspec,
        compiler_params=_params(("parallel", "parallel")),
        name="fft_stage3",
    )(bt, vx, x0, bias, fh, fl)


def _hyena_latent(p, hy, feats, deltas, tables, n):
    conv_w, conv_b, w1, b1, f1, w2, b2, f2, w3, bias = hy
    c = _fft_constants(n)
    n1, n1h = c["n1"], c["n1h"]
    vx, x0 = _hyena_prep(p, conv_w, conv_b, group_major=True)
    gm_shape = vx.shape
    rows = lambda a: a.reshape(*a.shape[:-3], n1h * SUBLANES, LANES)
    vx, x0 = rows(vx), rows(x0)
    taps, sums = _hyena_filter(n, feats, deltas, w1, b1, f1, w2, b2, f2, w3, group_major=True)
    ncw = 2
    a_f = _fft_stage1(rows(taps)[None], c["f1r"], n1, n1h, ncw)
    kf = _filter_spectrum(a_f, tables[0], tables[1], sums, n1)
    a = _fft_stage1(vx, c["f1c"], n1, n1h, ncw)
    bt = _fft_stage2(a, kf, tables, n1)
    out = _fft_stage3(bt, vx, x0, bias.reshape(HYENA_WIDTH // LANES, 1, LANES), c["f3c"], n1, n1h, ncw)
    return out.reshape(gm_shape)


@functools.lru_cache(maxsize=None)
def _dense_dft_constants(n):
    m = 2 * n
    k = np.arange(m)
    t = np.arange(n)
    f = np.exp(-2j * np.pi * np.outer(k, t) / m)
    finv = np.exp(2j * np.pi * np.outer(t, k) / m) / m
    return _stack_complex(f), np.concatenate([f.real, f.imag], axis=0), _stack_complex(finv)


def _dense_conv_kernel(vx_ref, x0_ref, hf_ref, hb_ref, sf_ref, sb_ref, bias_ref,
                       fch, fcl, frh, frl, fih, fil, o_ref, *, n):
    m = 2 * n
    z = jnp.concatenate([vx_ref[0], vx_ref[1]], axis=0)
    zh, zl = _split(z)
    zf = _dot3(fch[...], fcl[...], zh, zl)
    hfh, hfl = _split(hf_ref[...])
    hbh, hbl = _split(hb_ref[...])
    hf = _dot3(frh[...], frl[...], hfh, hfl)
    hb = _dot3(frh[...], frl[...], hbh, hbl)
    inv = 1.0 / (jnp.sum(sf_ref[...], axis=0, keepdims=True) + jnp.sum(sb_ref[...], axis=0, keepdims=True))
    kr = (hf[:m] + hb[:m]) * inv
    ki = (hf[m:] - hb[m:]) * inv
    zr, zi = zf[:m], zf[m:]
    y = jnp.concatenate([zr * kr - zi * ki, zr * ki + zi * kr], axis=0)
    yh, yl = _split(y)
    out = _dot3(fih[...], fil[...], yh, yl)
    bias = bias_ref[...]
    for p in range(2):
        o_ref[p] = (out[p * n:(p + 1) * n] + vx_ref[p] * bias) * x0_ref[p]


def _hyena_context(p, hy, feats, deltas, n):
    conv_w, conv_b, w1, b1, f1, w2, b2, f2, w3, bias = hy
    vx, x0 = _hyena_prep(p, conv_w, conv_b, group_major=False)
    taps, sums = _hyena_filter(n, feats, deltas, w1, b1, f1, w2, b2, f2, w3, group_major=False)
    fc, fr, fi = _dense_dft_constants(n)
    mats = [*_hilo(fc), *_hilo(fr), *_hilo(fi)]
    cw = 256
    nct = HYENA_WIDTH // cw
    dspec = pl.BlockSpec((2, n, cw), lambda ci: (0, 0, ci))
    return pl.pallas_call(
        functools.partial(_dense_conv_kernel, n=n),
        out_shape=jax.ShapeDtypeStruct(vx.shape, F32),
        grid=(nct,),
        in_specs=[dspec, dspec,
                  pl.BlockSpec((n, cw), lambda ci: (0, ci)), pl.BlockSpec((n, cw), lambda ci: (0, nct + ci)),
                  pl.BlockSpec((SUBLANES, cw), lambda ci: (0, ci)), pl.BlockSpec((SUBLANES, cw), lambda ci: (0, nct + ci)),
                  pl.BlockSpec((1, cw), lambda ci: (0, ci))]
                 + [pl.BlockSpec(mt.shape, lambda ci: (0, 0)) for mt in mats],
        out_specs=dspec,
        compiler_params=_params(("parallel",)),
        name="context_long_conv",
    )(vx, x0, taps, taps, sums, sums, bias.reshape(1, HYENA_WIDTH), *mats)


def _pool_kernel(x_ref, p_ref, n_ref, w_ref, sc_ref, o_ref, *, tl, nt, n):
    i = pl.program_id(1)
    x = x_ref[...]
    pv = jnp.where(i > 0, p_ref[...], 0.0)
    nx = jnp.where(i < nt - 1, n_ref[...], 0.0)
    ext = jnp.concatenate([pv, x, nx], axis=0)
    rows = tl + 2 * POOL_HALO
    t = i * tl + lax.broadcasted_iota(jnp.int32, (tl, POOL_GROUP), 0)
    for g, w in enumerate(POOL_WINDOWS):
        lanes = slice(g * POOL_GROUP, (g + 1) * POOL_GROUP)
        a = ext[:, lanes]
        c = a + pltpu.roll(a, 1, 0)
        h = 1
        while 2 * h < w:
            c = pltpu.roll(c, h, 0) + pltpu.roll(c, rows - h, 0)
            h *= 2
        total = c[POOL_HALO:POOL_HALO + tl]
        count = (jnp.minimum(t + h, n) - jnp.maximum(t - h, 0)).astype(F32)
        y = (total / count - x[:, lanes]).astype(BF16)
        o_ref[:, lanes] = _dot(y, w_ref[g].astype(BF16)) * sc_ref[:, lanes]


def _pool_mixer(p, w_pool, scale):
    b, n, _ = p.shape
    tl = min(512, n)
    nt = n // tl
    colblk = HY_END // POOL_WIDTH
    assert colblk * POOL_WIDTH == HY_END
    per = tl // POOL_HALO
    nrow = n // POOL_HALO
    return pl.pallas_call(
        functools.partial(_pool_kernel, tl=tl, nt=nt, n=n),
        out_shape=jax.ShapeDtypeStruct((b, n, POOL_WIDTH), F32),
        grid=(b, nt),
        in_specs=[
            pl.BlockSpec((None, tl, POOL_WIDTH), lambda bi, i: (bi, i, colblk)),
            pl.BlockSpec((None, POOL_HALO, POOL_WIDTH), lambda bi, i: (bi, jnp.maximum(i * per - 1, 0), colblk)),
            pl.BlockSpec((None, POOL_HALO, POOL_WIDTH), lambda bi, i: (bi, jnp.minimum((i + 1) * per, nrow - 1), colblk)),
            pl.BlockSpec(w_pool.shape, lambda bi, i: (0, 0, 0)),
            pl.BlockSpec((1, POOL_WIDTH), lambda bi, i: (0, 0)),
        ],
        out_specs=pl.BlockSpec((None, tl, POOL_WIDTH), lambda bi, i: (bi, i, 0)),
        compiler_params=_params(("parallel", "parallel")),
        name="pool_mixer",
    )(p, p, p, w_pool, scale.reshape(1, POOL_WIDTH))


def _outproj_kernel(at_ref, hy_ref, po_ref, x_ref, gb_ref, w_ref, gp_ref, gt_ref, o_ref, *, tm, group_major):
    hy = _load_group_major(hy_ref, tm) if group_major else hy_ref[...]
    a0, a1 = ATTN_WIDTH, ATTN_WIDTH + HYENA_WIDTH
    ox = _dot(_rms(at_ref[...], gb_ref[:, :a0]).astype(BF16), w_ref[:a0])
    ox += _dot(_rms(hy, gb_ref[:, a0:a1]).astype(BF16), w_ref[a0:a1])
    ox += _dot(_rms(po_ref[...], gb_ref[:, a1:]).astype(BF16), w_ref[a1:])
    o_ref[...] = x_ref[...] + gt_ref[...] * _rms(ox, gp_ref[...])


def _out_projection(attn, hy, po, x, g_branch, w_out, g_post, gate, group_major):
    b, n, d = x.shape
    tm = min(512, n)
    row = lambda width: pl.BlockSpec((None, tm, width), lambda bi, i: (bi, i, 0))
    vec = lambda width: pl.BlockSpec((1, width), lambda bi, i: (0, 0))
    if group_major:
        hy_spec = pl.BlockSpec((None, *_gm_shape((), tm // FFT_N2, HYENA_WIDTH)), lambda bi, i: (bi, 0, 0, i, 0, 0))
    else:
        hy_spec = row(HYENA_WIDTH)
    return pl.pallas_call(
        functools.partial(_outproj_kernel, tm=tm, group_major=group_major),
        out_shape=jax.ShapeDtypeStruct(x.shape, F32),
        grid=(b, n // tm),
        in_specs=[row(ATTN_WIDTH), hy_spec, row(POOL_WIDTH), row(d), vec(d),
                  pl.BlockSpec(w_out.shape, lambda bi, i: (0, 0)), vec(d),
                  pl.BlockSpec((None, 1, d), lambda bi, i: (bi, 0, 0))],
        out_specs=row(d),
        compiler_params=_params(("parallel", "parallel")),
        name="out_projection",
    )(attn, hy, po, x, g_branch.reshape(1, d), w_out, g_post.reshape(1, d), gate)


def _mlp_kernel(x_ref, g_ref, sh_ref, sc_ref, wu_ref, wd_ref, gp_ref, gt_ref, o_ref, h_ref, acc_ref):
    k = pl.program_id(2)

    @pl.when(k == 0)
    def _():
        y = _rms(x_ref[...], g_ref[...])
        h_ref[...] = (y * (1.0 + sc_ref[...]) + sh_ref[...]).astype(BF16)
        acc_ref[...] = jnp.zeros_like(acc_ref)

    a = jnp.maximum(_dot(h_ref[...], wu_ref[...]), 0.0)
    acc_ref[...] += _dot((a * a).astype(BF16), wd_ref[...])

    @pl.when(k == pl.num_programs(2) - 1)
    def _():
        o_ref[...] = x_ref[...] + gt_ref[...] * _rms(acc_ref[...], gp_ref[...])


def _mlp(x, g_pre, shift, scale, w_up, w_down, g_post, gate):
    b, n, d = x.shape
    hidden = w_up.shape[1]
    tm = min(512, n)
    th = 512
    vec = pl.BlockSpec((1, d), lambda bi, i, k: (0, 0))
    bvec = pl.BlockSpec((None, 1, d), lambda bi, i, k: (bi, 0, 0))
    row = pl.BlockSpec((None, tm, d), lambda bi, i, k: (bi, i, 0))
    return pl.pallas_call(
        _mlp_kernel,
        out_shape=jax.ShapeDtypeStruct(x.shape, F32),
        grid=(b, n // tm, hidden // th),
        in_specs=[row, vec, bvec, bvec,
                  pl.BlockSpec((d, th), lambda bi, i, k: (0, k)),
                  pl.BlockSpec((th, d), lambda bi, i, k: (k, 0)),
                  vec, bvec],
        out_specs=row,
        scratch_shapes=[pltpu.VMEM((tm, d), BF16), pltpu.VMEM((tm, d), F32)],
        compiler_params=_params(("parallel", "parallel", "arbitrary")),
        name="mlp",
    )(x, g_pre.reshape(1, d), shift, scale, w_up, w_down, g_post.reshape(1, d), gate)


def kernel(x, c, ctx, c_ctx, w_mod, b_mod, g_pre_mix, g_post_mix, g_pre_mlp, g_post_mlp, w_in, w_out, g_branch,
           attn_sink, hy_conv_w, hy_conv_b, hy_w1, hy_b1, hy_freq1, hy_w2, hy_b2, hy_freq2, hy_w3, hy_bias,
           pool_w, pool_scale, w_up, w_down):
    b, n, d = x.shape
    n_ctx = ctx.shape[1]
    depth = w_mod.shape[0]
    assert b == 2 and d == D_MODEL and n % 512 == 0 and n_ctx % BLOCK == 0

    cond = jnp.concatenate([c, c_ctx[None], jnp.zeros((SUBLANES - b - 1, d), F32)], axis=0)
    mods = _modulation(cond, w_mod, b_mod)

    w_in_b, w_out_b, w_up_b, w_down_b = (w.astype(BF16) for w in (w_in, w_out, w_up, w_down))
    rope = _rope_tables(n)
    feats_x, feats_c = _filter_features(n), _filter_features(n_ctx)
    deltas = _filter_deltas()
    tables = _stage2_tables(n)

    for i in range(depth):
        last = i == depth - 1
        hy = (hy_conv_w[i], hy_conv_b[i], hy_w1[i], hy_b1[i], hy_freq1[i], hy_w2[i], hy_b2[i], hy_freq2[i],
              hy_w3[i], hy_bias[i])
        mx = [m[:, None, :] for m in jnp.split(mods[i, :b], N_MOD, axis=-1)]
        mc = [jnp.broadcast_to(m[None, None, :], (b, 1, d)) for m in jnp.split(mods[i, b], N_MOD, axis=-1)]

        px = _in_projection(x, g_pre_mix[i], mx[0], mx[1], w_in_b[i], rope)
        pc = _in_projection(ctx, g_pre_mix[i], mc[0], mc[1], w_in_b[i])
        kv_ctx = pc[..., Q_END:V_END]

        attn_x = _attention(px, kv_ctx, attn_sink[i], local=True)
        hy_x = _hyena_latent(px, hy, feats_x, deltas, tables, n)
        po_x = _pool_mixer(px, pool_w[i], pool_scale[i])
        x = _out_projection(attn_x, hy_x, po_x, x, g_branch[i], w_out_b[i], g_post_mix[i], mx[2], group_major=True)
        x = _mlp(x, g_pre_mlp[i], mx[3], mx[4], w_up_b[i], w_down_b[i], g_post_mlp[i], mx[5])

        if not last:
            attn_c = _attention(pc, kv_ctx, attn_sink[i], local=False)
            hy_c = _hyena_context(pc, hy, feats_c, deltas, n_ctx)
            po_c = _pool_mixer(pc, pool_w[i], pool_scale[i])
            ctx = _out_projection(attn_c, hy_c, po_c, ctx, g_branch[i], w_out_b[i], g_post_mix[i], mc[2],
                                  group_major=False)
            ctx = _mlp(ctx, g_pre_mlp[i], mc[3], mc[4], w_up_b[i], w_down_b[i], g_post_mlp[i], mc[5])
    return x
```

```python
import functools
import math

import numpy as np
import jax
import jax.numpy as jnp
from jax import lax
from jax.experimental import pallas as pl
from jax.experimental.pallas import tpu as pltpu

F32 = jnp.float32
BF16 = jnp.bfloat16

D_MODEL = 2048
DEPTH = 4
GRID_W = 64
ATTN_WIDTH = D_MODEL // 2
HYENA_WIDTH = D_MODEL // 4
POOL_WIDTH = D_MODEL - ATTN_WIDTH - HYENA_WIDTH
HEAD_DIM = 128
N_HEADS = ATTN_WIDTH // HEAD_DIM
N_KV_HEADS = 2
KV_GROUP = N_HEADS // N_KV_HEADS
KV_WIDTH = N_KV_HEADS * HEAD_DIM
WINDOW = 128
BLOCK = 128
ROPE_BASE = 10000.0
HYENA_EMB_DIM = 33
HYENA_FILTER_HIDDEN = 64
HYENA_FAST_DECAY_PCT = 0.3
HYENA_SLOW_DECAY_PCT = 1.5
HYENA_DECAY_TARGET = 1e-2
POOL_WINDOWS = (2, 4, 8, 16)
POOL_GROUP = POOL_WIDTH // len(POOL_WINDOWS)
MLP_HIDDEN = 4 * D_MODEL
N_MOD = 6
EPS = 1e-6
NEG_INF = -1e30

Q_END = ATTN_WIDTH
K_END = Q_END + KV_WIDTH
V_END = K_END + KV_WIDTH
HY_END = V_END + 3 * HYENA_WIDTH
IN_WIDTH = HY_END + POOL_WIDTH

LANES = 128
SUBLANES = 8
FFT_N2 = 128
POOL_HALO = 16
VMEM_LIMIT = 56 * 1024 * 1024


def _params(sem, vmem=VMEM_LIMIT):
    return pltpu.CompilerParams(dimension_semantics=sem, vmem_limit_bytes=vmem)


def _split(x):
    hi = x.astype(BF16)
    lo = (x - hi.astype(F32)).astype(BF16)
    return hi, lo


def _dot(a, b):
    return jnp.dot(a, b, preferred_element_type=F32)


def _dot3(ah, al, bh, bl):
    return _dot(ah, bh) + _dot(ah, bl) + _dot(al, bh)


def _rms(x, g):
    return x * lax.rsqrt(jnp.mean(x * x, axis=-1, keepdims=True) + EPS) * g


BF16_ROWS = 2 * SUBLANES


def _norm_scale_rows(x_ref, h_ref, a_ref, s_ref, rows, cols=None):
    cols = slice(None) if cols is None else cols
    nchunks = rows // BF16_ROWS

    def chunk(c):
        return pl.ds(pl.multiple_of(c * BF16_ROWS, BF16_ROWS), BF16_ROWS)

    def inv_rms(c):
        x = x_ref[chunk(c), :]
        return lax.rsqrt(jnp.mean(x * x, axis=-1, keepdims=True) + EPS)

    def body(c, inv):
        inv_next = inv_rms(jnp.minimum(c + 1, nchunks - 1))
        y = x_ref[chunk(c), :] * inv * a_ref[...]
        if s_ref is not None:
            y = y + s_ref[...]
        h_ref[chunk(c), cols] = y.astype(BF16)
        return inv_next

    lax.fori_loop(0, nchunks, body, inv_rms(0), unroll=8)


def _residual_norm_rows(x_ref, y_ref, o_ref, pg_ref, rows):
    nchunks = rows // SUBLANES

    def chunk(c):
        return pl.ds(pl.multiple_of(c * SUBLANES, SUBLANES), SUBLANES)

    def inv_rms(c):
        y = y_ref[chunk(c), :]
        return lax.rsqrt(jnp.mean(y * y, axis=-1, keepdims=True) + EPS)

    def body(c, inv):
        inv_next = inv_rms(jnp.minimum(c + 1, nchunks - 1))
        o_ref[chunk(c), :] = x_ref[chunk(c), :] + y_ref[chunk(c), :] * inv * pg_ref[...]
        return inv_next

    lax.fori_loop(0, nchunks, body, inv_rms(0), unroll=16)


def _mod_kernel(c_ref, w_ref, b_ref, o_ref):
    c = c_ref[...]
    s = c / (1.0 + jnp.exp(-c))
    sh, sl = _split(s)
    wh, wl = _split(w_ref[...])
    o_ref[...] = _dot3(sh, sl, wh, wl) + b_ref[...]


def _modulation(cond, w_mod, b_mod):
    depth, d, width = w_mod.shape
    tn = 1024
    return pl.pallas_call(
        _mod_kernel,
        out_shape=jax.ShapeDtypeStruct((depth, SUBLANES, width), F32),
        grid=(depth, width // tn),
        in_specs=[
            pl.BlockSpec((SUBLANES, d), lambda l, j: (0, 0)),
            pl.BlockSpec((None, d, tn), lambda l, j: (l, 0, j)),
            pl.BlockSpec((None, 1, tn), lambda l, j: (l, 0, j)),
        ],
        out_specs=pl.BlockSpec((None, SUBLANES, tn), lambda l, j: (l, 0, j)),
        compiler_params=_params(("parallel", "parallel")),
        name="modulation",
    )(cond, w_mod, b_mod.reshape(depth, 1, width))


QKV_WIDTH = V_END
REST_WIDTH = IN_WIDTH - V_END
SM_SCALE = HEAD_DIM ** -0.5


def _inproj_kernel(*refs, tm, tn, rope):
    if rope:
        x_ref, g_ref, sh_ref, sc_ref, w_ref, cq_ref, sq_ref, ck_ref, sk_ref, qkv_ref, rest_ref, h_ref, a_scr = refs
    else:
        x_ref, g_ref, sh_ref, sc_ref, w_ref, qkv_ref, rest_ref, h_ref, a_scr = refs
    j = pl.program_id(2)

    @pl.when(j == 0)
    def _():
        a_scr[...] = g_ref[...] * (1.0 + sc_ref[...])
        _norm_scale_rows(x_ref, h_ref, a_scr, sh_ref, tm)

    acc = _dot(h_ref[...], w_ref[...])
    nch = tn // LANES

    def write_qkv(t):
        for ch in range(nch):
            col = (t * nch + ch) * LANES
            a = acc[:, ch * LANES:(ch + 1) * LANES]
            if col < K_END and rope:
                cos, sin = (cq_ref, sq_ref) if col < Q_END else (ck_ref, sk_ref)
                a = a * cos[...] + pltpu.roll(a, HEAD_DIM // 2, 1) * sin[...]
            elif col < Q_END:
                a = a * SM_SCALE
            qkv_ref[:, ch * LANES:(ch + 1) * LANES] = a.astype(BF16)

    for t in range(QKV_WIDTH // tn):
        pl.when(j == t)(functools.partial(write_qkv, t))

    @pl.when(j >= QKV_WIDTH // tn)
    def _():
        rest_ref[...] = acc


def _in_projection(x, g, shift, scale, w, rope_tables=None):
    b, n, d = x.shape
    tm = min(1024, n)
    tn = 512
    nqkv = QKV_WIDTH // tn
    assert nqkv * tn == QKV_WIDTH and w.shape[1] == IN_WIDTH
    rope = rope_tables is not None
    in_specs = [
        pl.BlockSpec((None, tm, d), lambda bi, i, j: (bi, i, 0)),
        pl.BlockSpec((1, d), lambda bi, i, j: (0, 0)),
        pl.BlockSpec((None, 1, d), lambda bi, i, j: (bi, 0, 0)),
        pl.BlockSpec((None, 1, d), lambda bi, i, j: (bi, 0, 0)),
        pl.BlockSpec((d, tn), lambda bi, i, j: (0, j)),
    ]
    args = [x, g.reshape(1, d), shift, scale, w]
    if rope:
        in_specs += [pl.BlockSpec((tm, LANES), lambda bi, i, j: (i, 0))] * 4
        args += list(rope_tables)
    return pl.pallas_call(
        functools.partial(_inproj_kernel, tm=tm, tn=tn, rope=rope),
        out_shape=[jax.ShapeDtypeStruct((b, n, QKV_WIDTH), BF16), jax.ShapeDtypeStruct((b, n, REST_WIDTH), F32)],
        grid=(b, n // tm, IN_WIDTH // tn),
        in_specs=in_specs,
        out_specs=[pl.BlockSpec((None, tm, tn), lambda bi, i, j: (bi, i, jnp.minimum(j, nqkv - 1))),
                   pl.BlockSpec((None, tm, tn), lambda bi, i, j: (bi, i, jnp.maximum(j - nqkv, 0)))],
        scratch_shapes=[pltpu.VMEM((tm, d), BF16), pltpu.VMEM((1, d), F32)],
        compiler_params=_params(("parallel", "parallel", "arbitrary")),
        name="in_projection_rope" if rope else "in_projection",
    )(*args)


def _permute_rope_columns(w_in):
    lead = w_in.shape[:-1]
    quarter = HEAD_DIM // 4
    qk = w_in[..., :K_END].reshape(*lead, K_END // HEAD_DIM, 2, 2, quarter)
    qk = jnp.swapaxes(qk, -3, -2).reshape(*lead, K_END)
    return jnp.concatenate([qk, w_in[..., K_END:]], axis=-1)


def _rope_tables(n):
    quarter = HEAD_DIM // 4
    inv_freq = ROPE_BASE ** (-jnp.arange(quarter, dtype=F32) / quarter)
    t = jnp.arange(n, dtype=jnp.int32)
    row = (t // GRID_W).astype(F32)[:, None] * inv_freq[None, :]
    col = (t % GRID_W).astype(F32)[:, None] * inv_freq[None, :]
    cos = jnp.concatenate([jnp.cos(row), jnp.cos(col), jnp.cos(row), jnp.cos(col)], axis=-1)
    sin = jnp.concatenate([-jnp.sin(row), -jnp.sin(col), jnp.sin(row), jnp.sin(col)], axis=-1)
    return cos * SM_SCALE, sin * SM_SCALE, cos, sin


def _softmax_pv(parts, sink_col):
    m = sink_col
    for s, _ in parts:
        m = jnp.maximum(m, jnp.max(s, axis=-1, keepdims=True))
    den = jnp.exp(sink_col - m)
    out = None
    for s, v in parts:
        p = jnp.exp(s - m)
        den = den + jnp.sum(p, axis=-1, keepdims=True)
        pv = _dot(p.astype(BF16), v)
        out = pv if out is None else out + pv
    return out / den


def _ctx_attn_kernel(sink_ref, q_ref, kvc_ref, o_ref, *, tq):
    for g in range(N_KV_HEADS):
        heads = [g * KV_GROUP + h for h in range(KV_GROUP)]
        qg = jnp.concatenate([q_ref[:, h * HEAD_DIM:(h + 1) * HEAD_DIM] for h in heads], axis=0)
        sink_col = jnp.concatenate([jnp.full((tq, 1), sink_ref[h], F32) for h in heads], axis=0)
        kc = kvc_ref[:, g * HEAD_DIM:(g + 1) * HEAD_DIM]
        vc = kvc_ref[:, KV_WIDTH + g * HEAD_DIM:KV_WIDTH + (g + 1) * HEAD_DIM]
        s = lax.dot_general(qg, kc, (((1,), (1,)), ((), ())), preferred_element_type=F32)
        o = _softmax_pv([(s, vc)], sink_col)
        for hi, h in enumerate(heads):
            o_ref[:, h * HEAD_DIM:(h + 1) * HEAD_DIM] = o[hi * tq:(hi + 1) * tq]


ATTN_ROWS = KV_GROUP * BLOCK
SOFTMAX_CHUNK = 32


def _win_attn_kernel(sink_ref, q_ref, km_ref, kp_ref, kn_ref, vm_ref, vp_ref, vn_ref, kvc_ref, band_ref,
                     o_ref, ktw, vw, s_scr, p_scr, m_scr, *, tq, nb, nctx):
    i = pl.program_id(1)
    nsub = tq // BLOCK
    nloc = 3 * BLOCK

    def transposed(x):
        return x.astype(F32).T.astype(BF16)

    def block_rows(main_ref, prev_ref, next_ref, w, lanes):
        if w == 0:
            return prev_ref[:, lanes]
        if w == nsub + 1:
            return next_ref[:, lanes]
        return main_ref[(w - 1) * BLOCK:w * BLOCK, lanes]

    ones = jnp.ones((nloc + nctx, HEAD_DIM), BF16)
    for g in range(N_KV_HEADS):
        lanes = slice(g * HEAD_DIM, (g + 1) * HEAD_DIM)
        vlanes = slice(KV_WIDTH + g * HEAD_DIM, KV_WIDTH + (g + 1) * HEAD_DIM)
        kts = [transposed(block_rows(km_ref, kp_ref, kn_ref, w, lanes)) for w in range(nsub + 2)]
        kct = [transposed(kvc_ref[cb * BLOCK:(cb + 1) * BLOCK, lanes]) for cb in range(nctx // BLOCK)]
        for jb in range(nsub):
            for w in range(3):
                ktw[jb, g, :, w * BLOCK:(w + 1) * BLOCK] = kts[jb + w]
                vw[jb, g, w * BLOCK:(w + 1) * BLOCK, :HEAD_DIM] = block_rows(vm_ref, vp_ref, vn_ref, jb + w, lanes)
            for cb in range(nctx // BLOCK):
                ktw[jb, g, :, nloc + cb * BLOCK:nloc + (cb + 1) * BLOCK] = kct[cb]
            vw[jb, g, nloc:, :HEAD_DIM] = kvc_ref[:, vlanes]
            vw[jb, g, :, HEAD_DIM:] = ones

    col = lax.broadcasted_iota(jnp.int32, (1, nloc), 1)

    def rows_of(jb):
        start = jb * BLOCK
        return pl.ds(start if isinstance(start, int) else pl.multiple_of(start, BLOCK), BLOCK)

    def stage_a(jb, g):
        qg = jnp.concatenate([q_ref[rows_of(jb), (g * KV_GROUP + h) * HEAD_DIM:(g * KV_GROUP + h + 1) * HEAD_DIM]
                              for h in range(KV_GROUP)], axis=0)
        s_scr[g] = _dot(qg, ktw[jb, g])

    def stage_b(jb, g):
        blk = i * nsub + jb
        pen_prev = jnp.where(blk == 0, NEG_INF, 0.0).astype(F32)
        pen_next = jnp.where(blk == nb - 1, NEG_INF, 0.0).astype(F32)
        rowbias = jnp.where(col < BLOCK, pen_prev, jnp.where(col >= 2 * BLOCK, pen_next, 0.0))
        for c in range(ATTN_ROWS // SOFTMAX_CHUNK):
            rows = slice(c * SOFTMAX_CHUNK, (c + 1) * SOFTMAX_CHUNK)
            sink = sink_ref[g * KV_GROUP + (c * SOFTMAX_CHUNK) // BLOCK]
            s_loc = s_scr[g, rows, :nloc] + band_ref[rows, :] + rowbias
            s_ctx = s_scr[g, rows, nloc:]
            m = jnp.maximum(jnp.max(s_loc, axis=-1, keepdims=True), jnp.max(s_ctx, axis=-1, keepdims=True))
            m = jnp.maximum(m, sink)
            p_scr[g, rows, :nloc] = jnp.exp(s_loc - m).astype(BF16)
            p_scr[g, rows, nloc:] = jnp.exp(s_ctx - m).astype(BF16)
            m_scr[g, rows, :] = m

    def stage_c(jb, g):
        o = _dot(p_scr[g], vw[jb, g])
        for hi in range(KV_GROUP):
            h = g * KV_GROUP + hi
            rows = slice(hi * BLOCK, (hi + 1) * BLOCK)
            den = o[rows, HEAD_DIM:HEAD_DIM + 1] + jnp.exp(sink_ref[h] - m_scr[g, rows, :])
            o_ref[rows_of(jb), h * HEAD_DIM:(h + 1) * HEAD_DIM] = o[rows, :HEAD_DIM] / den

    stage_a(0, 0)
    stage_a(0, 1)
    stage_b(0, 0)

    def body(j, carry):
        stage_a(j, 0)
        stage_c(j - 1, 0)
        stage_b(j - 1, 1)
        stage_a(j, 1)
        stage_c(j - 1, 1)
        stage_b(j, 0)
        return carry

    lax.fori_loop(1, nsub, body, 0)
    stage_c(nsub - 1, 0)
    stage_b(nsub - 1, 1)
    stage_c(nsub - 1, 1)


def _band_bias():
    qi = np.arange(ATTN_ROWS)[:, None] % BLOCK
    sj = np.arange(3 * BLOCK)[None, :]
    return jnp.asarray(np.where(np.abs(sj - BLOCK - qi) <= WINDOW, 0.0, NEG_INF), dtype=F32)


def _attention(qkv, kv_ctx, sink, local):
    b, n, _ = qkv.shape
    c = kv_ctx.shape[1]
    tq = min(512, n)
    nsub = tq // BLOCK
    nb = n // BLOCK
    kcol = Q_END // KV_WIDTH
    vcol = K_END // KV_WIDTH
    smem = pl.BlockSpec(memory_space=pltpu.SMEM)
    q_spec = pl.BlockSpec((None, tq, ATTN_WIDTH), lambda bi, i: (bi, i, 0))
    kvc_spec = pl.BlockSpec((None, c, 2 * KV_WIDTH), lambda bi, i: (bi, 0, 0))
    if local:
        def main(colblk):
            return pl.BlockSpec((None, tq, KV_WIDTH), lambda bi, i: (bi, i, colblk))

        def prev(colblk):
            return pl.BlockSpec((None, BLOCK, KV_WIDTH), lambda bi, i: (bi, jnp.maximum(i * nsub - 1, 0), colblk))

        def nxt(colblk):
            return pl.BlockSpec((None, BLOCK, KV_WIDTH), lambda bi, i: (bi, jnp.minimum((i + 1) * nsub, nb - 1), colblk))

        keys = 3 * BLOCK + c
        kern = functools.partial(_win_attn_kernel, tq=tq, nb=nb, nctx=c)
        in_specs = [smem, q_spec, main(kcol), prev(kcol), nxt(kcol), main(vcol), prev(vcol), nxt(vcol), kvc_spec,
                    pl.BlockSpec((ATTN_ROWS, 3 * BLOCK), lambda bi, i: (0, 0))]
        args = [sink, qkv, qkv, qkv, qkv, qkv, qkv, qkv, kv_ctx, _band_bias()]
        scratch = [pltpu.VMEM((nsub, N_KV_HEADS, HEAD_DIM, keys), BF16),
                   pltpu.VMEM((nsub, N_KV_HEADS, keys, 2 * HEAD_DIM), BF16),
                   pltpu.VMEM((2, ATTN_ROWS, keys), F32),
                   pltpu.VMEM((2, ATTN_ROWS, keys), BF16),
                   pltpu.VMEM((2, ATTN_ROWS, 1), F32)]
    else:
        kern = functools.partial(_ctx_attn_kernel, tq=tq)
        in_specs = [smem, q_spec, kvc_spec]
        args = [sink, qkv, kv_ctx]
        scratch = []
    return pl.pallas_call(
        kern,
        out_shape=jax.ShapeDtypeStruct((b, n, ATTN_WIDTH), F32),
        grid=(b, n // tq),
        in_specs=in_specs,
        out_specs=pl.BlockSpec((None, tq, ATTN_WIDTH), lambda bi, i: (bi, i, 0)),
        scratch_shapes=scratch,
        compiler_params=_params(("parallel", "parallel")),
        name="window_attention" if local else "context_attention",
    )(*args)


GROUPS = FFT_N2 // SUBLANES


def _gm_shape(lead, t1, width):
    return (*lead, width // LANES, GROUPS, t1, SUBLANES, LANES)


def _store_group_major(o_ref, val, tl):
    for t1 in range(tl // FFT_N2):
        for jg in range(GROUPS):
            r0 = (t1 * GROUPS + jg) * SUBLANES
            for cc in range(val.shape[1] // LANES):
                o_ref[cc, jg, t1] = val[r0:r0 + SUBLANES, cc * LANES:(cc + 1) * LANES]


def _hyena_prep_kernel(u_ref, p_ref, n_ref, w_ref, b_ref, vx_ref, x0_ref, *, tl, nt, group_major):
    i = pl.program_id(1)
    u = u_ref[...]
    prev_row = jnp.where(i > 0, p_ref[SUBLANES - 1:SUBLANES, :], 0.0)
    next_row = jnp.where(i < nt - 1, n_ref[0:1, :], 0.0)
    row = lax.broadcasted_iota(jnp.int32, u.shape, 0)
    um = jnp.where(row == 0, prev_row, pltpu.roll(u, 1, 0))
    up = jnp.where(row == tl - 1, next_row, pltpu.roll(u, tl - 1, 0))
    z = um * w_ref[0:1, :] + u * w_ref[1:2, :] + up * w_ref[2:3, :] + b_ref[...]
    x0 = z[:, :HYENA_WIDTH]
    vx = z[:, 2 * HYENA_WIDTH:] * z[:, HYENA_WIDTH:2 * HYENA_WIDTH]
    if group_major:
        _store_group_major(vx_ref, vx, tl)
        _store_group_major(x0_ref, x0, tl)
    else:
        vx_ref[...] = vx
        x0_ref[...] = x0


def _hyena_prep(p, conv_w, conv_b, group_major):
    b, n, _ = p.shape
    tl = min(512, n)
    nt = n // tl
    hw = 3 * HYENA_WIDTH
    colblk = 0
    nrow8 = n // SUBLANES
    per = tl // SUBLANES
    if group_major:
        shape = _gm_shape((b,), n // FFT_N2, HYENA_WIDTH)
        out_spec = pl.BlockSpec((None, *_gm_shape((), tl // FFT_N2, HYENA_WIDTH)), lambda bi, i: (bi, 0, 0, i, 0, 0))
    else:
        shape = (b, n, HYENA_WIDTH)
        out_spec = pl.BlockSpec((None, tl, HYENA_WIDTH), lambda bi, i: (bi, i, 0))
    return pl.pallas_call(
        functools.partial(_hyena_prep_kernel, tl=tl, nt=nt, group_major=group_major),
        out_shape=[jax.ShapeDtypeStruct(shape, F32)] * 2,
        grid=(b, nt),
        in_specs=[
            pl.BlockSpec((None, tl, hw), lambda bi, i: (bi, i, colblk)),
            pl.BlockSpec((None, SUBLANES, hw), lambda bi, i: (bi, jnp.maximum(i * per - 1, 0), colblk)),
            pl.BlockSpec((None, SUBLANES, hw), lambda bi, i: (bi, jnp.minimum((i + 1) * per, nrow8 - 1), colblk)),
            pl.BlockSpec((3, hw), lambda bi, i: (0, 0)),
            pl.BlockSpec((1, hw), lambda bi, i: (0, 0)),
        ],
        out_specs=[out_spec, out_spec],
        compiler_params=_params(("parallel", "parallel")),
        name="hyena_prep",
    )(p, p, p, conv_w, conv_b.reshape(1, hw))


def _filter_kernel(ft_ref, w1_ref, b1_ref, f1_ref, w2_ref, b2_ref, f2_ref, w3_ref, dl_ref, h_ref, s_ref,
                   *, tl, n, group_major):
    i = pl.program_id(0)

    def dense(a, w_ref):
        ah, al = _split(a)
        wh, wl = _split(w_ref[...])
        return _dot3(ah, al, wh, wl)

    h = jnp.sin(f1_ref[...] * (dense(ft_ref[...], w1_ref) + b1_ref[...]))
    h = jnp.sin(f2_ref[...] * (dense(h, w2_ref) + b2_ref[...]))
    h = dense(h, w3_ref)
    t = (i * tl + lax.broadcasted_iota(jnp.int32, (tl, HYENA_WIDTH), 0)).astype(F32) / float(n - 1)
    decay = jnp.exp(-t * dl_ref[...])
    h = h * jnp.concatenate([decay, decay], axis=1)
    if group_major:
        _store_group_major(h_ref, h, tl)
    else:
        h_ref[...] = h

    @pl.when(i == 0)
    def _():
        s_ref[...] = jnp.zeros_like(s_ref)

    s_ref[...] += jnp.sum(jnp.abs(h).reshape(tl // SUBLANES, SUBLANES, 2 * HYENA_WIDTH), axis=0)


def _filter_features(n):
    t = jnp.linspace(0.0, 1.0, n, dtype=F32)[:, None]
    bands = (HYENA_EMB_DIM - 1) // 2
    omega = 2.0 * math.pi * jnp.arange(n, dtype=F32)[:, None] / n
    f = jnp.linspace(1e-4, bands - 1, bands, dtype=F32)[None, :]
    feats = jnp.concatenate([t, jnp.cos(f * omega), -jnp.sin(f * omega)], axis=-1)
    return jnp.pad(feats, ((0, 0), (0, LANES - HYENA_EMB_DIM)))


def _filter_deltas():
    max_decay = math.log(HYENA_DECAY_TARGET) / HYENA_FAST_DECAY_PCT
    min_decay = math.log(HYENA_DECAY_TARGET) / HYENA_SLOW_DECAY_PCT
    return jnp.abs(jnp.linspace(min_decay, max_decay, HYENA_WIDTH, dtype=F32)).reshape(1, HYENA_WIDTH)


def _hyena_filter(n, feats, deltas, w1, b1, f1, w2, b2, f2, w3, group_major):
    tl = min(512, n)
    hid = LANES
    pad_h = hid - HYENA_FILTER_HIDDEN
    w1p = jnp.pad(w1, ((0, LANES - HYENA_EMB_DIM), (0, pad_h)))
    w2p = jnp.pad(w2, ((0, pad_h), (0, pad_h)))
    w3p = jnp.pad(w3, ((0, pad_h), (0, 0)))
    vec = lambda v: jnp.pad(v, (0, pad_h)).reshape(1, hid)
    hw2 = 2 * HYENA_WIDTH
    full = lambda shape: pl.BlockSpec(shape, lambda i: (0,) * len(shape))
    if group_major:
        shape = _gm_shape((), n // FFT_N2, hw2)
        out_spec = pl.BlockSpec(_gm_shape((), tl // FFT_N2, hw2), lambda i: (0, 0, i, 0, 0))
    else:
        shape = (n, hw2)
        out_spec = pl.BlockSpec((tl, hw2), lambda i: (i, 0))
    return pl.pallas_call(
        functools.partial(_filter_kernel, tl=tl, n=n, group_major=group_major),
        out_shape=[jax.ShapeDtypeStruct(shape, F32), jax.ShapeDtypeStruct((SUBLANES, hw2), F32)],
        grid=(n // tl,),
        in_specs=[pl.BlockSpec((tl, LANES), lambda i: (i, 0)), full((LANES, hid)), full((1, hid)), full((1, hid)),
                  full((hid, hid)), full((1, hid)), full((1, hid)), full((hid, hw2)), full((1, HYENA_WIDTH))],
        out_specs=[out_spec, full((SUBLANES, hw2))],
        compiler_params=_params(("arbitrary",)),
        name="hyena_filter",
    )(feats, w1p, vec(b1), vec(f1), w2p, vec(b2), vec(f2), w3p, deltas)


def _stack_complex(m):
    return np.block([[m.real, -m.imag], [m.imag, m.real]])


def _hilo(m):
    m = jnp.asarray(m, dtype=F32)
    return _split(m)


@functools.lru_cache(maxsize=None)
def _fft_constants(n):
    m = 2 * n
    n2 = FFT_N2
    n1 = m // n2
    n1h = n1 // 2
    k1 = np.arange(n1)
    t1 = np.arange(n1h)
    f1 = np.exp(-2j * np.pi * np.outer(k1, t1) / n1)
    f3 = np.exp(2j * np.pi * np.outer(t1, k1) / n1) / m
    k2 = np.arange(n2)
    t2 = np.arange(n2)
    w2 = np.exp(-2j * np.pi * np.outer(k2, t2) / n2)
    tw = np.exp(-2j * np.pi * np.outer(k1, t2) / m)
    return dict(
        n1=n1, n1h=n1h,
        f1c=_stack_complex(f1), f1r=np.concatenate([f1.real, f1.imag], axis=0),
        f3c=_stack_complex(f3),
        w2r=w2.real.astype(np.float32), w2i=w2.imag.astype(np.float32),
        twr=tw.real.astype(np.float32), twi=tw.imag.astype(np.float32),
    )


def _stage2_tables(n):
    c = _fft_constants(n)
    w2r, w2i = jnp.asarray(c["w2r"])[None], jnp.asarray(c["w2i"])[None]
    twr, twi = jnp.asarray(c["twr"])[:, None, :], jnp.asarray(c["twi"])[:, None, :]
    gr = w2r * twr - w2i * twi
    gi = w2r * twi + w2i * twr
    g = jnp.concatenate([jnp.concatenate([gr, -gi], axis=2), jnp.concatenate([gi, gr], axis=2)], axis=1)
    gh, gl = _split(g)
    gt = jnp.swapaxes(g, 1, 2)
    gth, gtl = _split(gt)
    return gh, gl, gth, gtl


def _s1_kernel(x_ref, fh_ref, fl_ref, o_ref, *, nparts, ncw, n1, n1h):
    fh = fh_ref[...]
    fl = fl_ref[...]
    for r in range(SUBLANES):
        rows = pl.ds(r, n1h, stride=SUBLANES)
        xs = jnp.concatenate(
            [jnp.concatenate([x_ref[p, cc, rows, :] for p in range(nparts)], axis=0) for cc in range(ncw)], axis=1)
        xh, xl = _split(xs)
        res = _dot3(fh, fl, xh, xl)
        for ri in range(2):
            for cc in range(ncw):
                o_ref[ri, cc, pl.ds(r, n1, stride=SUBLANES), :] = res[ri * n1:(ri + 1) * n1, cc * LANES:(cc + 1) * LANES]


def _fft_stage1(x, fmat, n1, n1h, ncw):
    nparts, ncc, groups = x.shape[:3]
    fh, fl = _hilo(fmat)
    return pl.pallas_call(
        functools.partial(_s1_kernel, nparts=nparts, ncw=ncw, n1=n1, n1h=n1h),
        out_shape=jax.ShapeDtypeStruct((2, ncc, groups, n1 * SUBLANES, LANES), F32),
        grid=(groups, ncc // ncw),
        in_specs=[
            pl.BlockSpec((nparts, ncw, None, n1h * SUBLANES, LANES), lambda j, ci: (0, ci, j, 0, 0)),
            pl.BlockSpec(fh.shape, lambda j, ci: (0, 0)),
            pl.BlockSpec(fl.shape, lambda j, ci: (0, 0)),
        ],
        out_specs=pl.BlockSpec((2, ncw, None, n1 * SUBLANES, LANES), lambda j, ci: (0, ci, j, 0, 0)),
        compiler_params=_params(("parallel", "parallel")),
        name="fft_stage1",
    )(x, fh, fl)


def _load_k1(a_ref, q):
    ncc = a_ref.shape[1]
    return jnp.concatenate([a_ref[:, cc, :, q].reshape(2 * FFT_N2, LANES) for cc in range(ncc)], axis=1)


def _filter_spectrum_kernel(a_ref, gh_ref, gl_ref, s_ref, o_ref, *, kg):
    s = jnp.sum(s_ref[...], axis=0, keepdims=True)
    inv = 1.0 / (s[:, :HYENA_WIDTH] + s[:, HYENA_WIDTH:])
    half = FFT_N2
    for q in range(kg):
        ah, al = _split(_load_k1(a_ref, q))
        h = _dot3(gh_ref[q], gl_ref[q], ah, al)
        hf = h[:, :HYENA_WIDTH]
        hb = h[:, HYENA_WIDTH:]
        o_ref[q, :half] = (hf[:half] + hb[:half]) * inv
        o_ref[q, half:] = (hf[half:] - hb[half:]) * inv


def _k1_spec(kg, width):
    return pl.BlockSpec((2, width // LANES, GROUPS, kg, SUBLANES, LANES), lambda i: (0, 0, 0, i, 0, 0))


def _filter_spectrum(a, gh, gl, sums, n1):
    kg = 4
    a6 = a.reshape(2, 2 * HYENA_WIDTH // LANES, GROUPS, n1, SUBLANES, LANES)
    tspec = pl.BlockSpec((kg, 2 * FFT_N2, 2 * FFT_N2), lambda i: (i, 0, 0))
    return pl.pallas_call(
        functools.partial(_filter_spectrum_kernel, kg=kg),
        out_shape=jax.ShapeDtypeStruct((n1, 2 * FFT_N2, HYENA_WIDTH), F32),
        grid=(n1 // kg,),
        in_specs=[_k1_spec(kg, 2 * HYENA_WIDTH), tspec, tspec,
                  pl.BlockSpec((SUBLANES, 2 * HYENA_WIDTH), lambda i: (0, 0))],
        out_specs=pl.BlockSpec((kg, 2 * FFT_N2, HYENA_WIDTH), lambda i: (i, 0, 0)),
        compiler_params=_params(("parallel",)),
        name="filter_spectrum",
    )(a6, gh, gl, sums)


def _s2_kernel(a_ref, kf_ref, gh_ref, gl_ref, gth_ref, gtl_ref, o_ref, *, kg):
    half = FFT_N2
    for q in range(kg):
        ah, al = _split(_load_k1(a_ref, q))
        x = _dot3(gh_ref[q], gl_ref[q], ah, al)
        xr, xi = x[:half], x[half:]
        kr, ki = kf_ref[q, :half], kf_ref[q, half:]
        y = jnp.concatenate([xr * kr - xi * ki, xr * ki + xi * kr], axis=0)
        yh, yl = _split(y)
        bt = _dot3(gth_ref[q], gtl_ref[q], yh, yl)
        for cc in range(HYENA_WIDTH // LANES):
            o_ref[:, cc, :, q] = bt[:, cc * LANES:(cc + 1) * LANES].reshape(2, GROUPS, SUBLANES, LANES)


def _fft_stage2(a, kf, tables, n1):
    kg = 4
    a6 = a.reshape(2, HYENA_WIDTH // LANES, GROUPS, n1, SUBLANES, LANES)
    tspec = pl.BlockSpec((kg, 2 * FFT_N2, 2 * FFT_N2), lambda i: (i, 0, 0))
    dspec = _k1_spec(kg, HYENA_WIDTH)
    out = pl.pallas_call(
        functools.partial(_s2_kernel, kg=kg),
        out_shape=jax.ShapeDtypeStruct(a6.shape, F32),
        grid=(n1 // kg,),
        in_specs=[dspec, pl.BlockSpec((kg, 2 * FFT_N2, HYENA_WIDTH), lambda i: (i, 0, 0)), tspec, tspec, tspec, tspec],
        out_specs=dspec,
        compiler_params=_params(("parallel",)),
        name="fft_stage2",
    )(a6, kf, *tables)
    return out.reshape(a.shape)


def _s3_kernel(b_ref, vx_ref, x0_ref, bias_ref, fh_ref, fl_ref, o_ref, *, ncw, n1, n1h):
    fh = fh_ref[...]
    fl = fl_ref[...]
    for r in range(SUBLANES):
        krows = pl.ds(r, n1, stride=SUBLANES)
        z = jnp.concatenate(
            [jnp.concatenate([b_ref[ri, cc, krows, :] for ri in range(2)], axis=0) for cc in range(ncw)], axis=1)
        zh, zl = _split(z)
        y = _dot3(fh, fl, zh, zl)
        trows = pl.ds(r, n1h, stride=SUBLANES)
        for p in range(2):
            for cc in range(ncw):
                yy = y[p * n1h:(p + 1) * n1h, cc * LANES:(cc + 1) * LANES]
                o_ref[p, cc, trows, :] = (yy + vx_ref[p, cc, trows, :] * bias_ref[cc]) * x0_ref[p, cc, trows, :]


def _fft_stage3(bt, vx, x0, bias, fmat, n1, n1h, ncw):
    ncc, groups = bt.shape[1:3]
    fh, fl = _hilo(fmat)
    tspec = pl.BlockSpec((2, ncw, None, n1h * SUBLANES, LANES), lambda j, ci: (0, ci, j, 0, 0))
    return pl.pallas_call(
        functools.partial(_s3_kernel, ncw=ncw, n1=n1, n1h=n1h),
        out_shape=jax.ShapeDtypeStruct(vx.shape, F32),
        grid=(groups, ncc // ncw),
        in_specs=[
            pl.BlockSpec((2, ncw, None, n1 * SUBLANES, LANES), lambda j, ci: (0, ci, j, 0, 0)),
            tspec, tspec,
            pl.BlockSpec((ncw, 1, LANES), lambda j, ci: (ci, 0, 0)),
            pl.BlockSpec(fh.shape, lambda j, ci: (0, 0)),
            pl.BlockSpec(fl.shape, lambda j, ci: (0, 0)),
        ],
        out_specs=tspec,
        compiler_params=_params(("parallel", "parallel")),
        name="fft_stage3",
    )(bt, vx, x0, bias, fh, fl)


def _hyena_latent(p, hy, feats, deltas, tables, n):
    conv_w, conv_b, w1, b1, f1, w2, b2, f2, w3, bias = hy
    c = _fft_constants(n)
    n1, n1h = c["n1"], c["n1h"]
    vx, x0 = _hyena_prep(p, conv_w, conv_b, group_major=True)
    gm_shape = vx.shape
    rows = lambda a: a.reshape(*a.shape[:-3], n1h * SUBLANES, LANES)
    vx, x0 = rows(vx), rows(x0)
    taps, sums = _hyena_filter(n, feats, deltas, w1, b1, f1, w2, b2, f2, w3, group_major=True)
    ncw = 2
    a_f = _fft_stage1(rows(taps)[None], c["f1r"], n1, n1h, ncw)
    kf = _filter_spectrum(a_f, tables[0], tables[1], sums, n1)
    a = _fft_stage1(vx, c["f1c"], n1, n1h, ncw)
    bt = _fft_stage2(a, kf, tables, n1)
    out = _fft_stage3(bt, vx, x0, bias.reshape(HYENA_WIDTH // LANES, 1, LANES), c["f3c"], n1, n1h, ncw)
    return out.reshape(gm_shape)


@functools.lru_cache(maxsize=None)
def _dense_dft_constants(n):
    m = 2 * n
    k = np.arange(m)
    t = np.arange(n)
    f = np.exp(-2j * np.pi * np.outer(k, t) / m)
    finv = np.exp(2j * np.pi * np.outer(t, k) / m) / m
    return _stack_complex(f), np.concatenate([f.real, f.imag], axis=0), _stack_complex(finv)


def _dense_conv_kernel(vx_ref, x0_ref, hf_ref, hb_ref, sf_ref, sb_ref, bias_ref,
                       fch, fcl, frh, frl, fih, fil, o_ref, *, n):
    m = 2 * n
    z = jnp.concatenate([vx_ref[0], vx_ref[1]], axis=0)
    zh, zl = _split(z)
    zf = _dot3(fch[...], fcl[...], zh, zl)
    hfh, hfl = _split(hf_ref[...])
    hbh, hbl = _split(hb_ref[...])
    hf = _dot3(frh[...], frl[...], hfh, hfl)
    hb = _dot3(frh[...], frl[...], hbh, hbl)
    inv = 1.0 / (jnp.sum(sf_ref[...], axis=0, keepdims=True) + jnp.sum(sb_ref[...], axis=0, keepdims=True))
    kr = (hf[:m] + hb[:m]) * inv
    ki = (hf[m:] - hb[m:]) * inv
    zr, zi = zf[:m], zf[m:]
    y = jnp.concatenate([zr * kr - zi * ki, zr * ki + zi * kr], axis=0)
    yh, yl = _split(y)
    out = _dot3(fih[...], fil[...], yh, yl)
    bias = bias_ref[...]
    for p in range(2):
        o_ref[p] = (out[p * n:(p + 1) * n] + vx_ref[p] * bias) * x0_ref[p]


def _hyena_context(p, hy, feats, deltas, n):
    conv_w, conv_b, w1, b1, f1, w2, b2, f2, w3, bias = hy
    vx, x0 = _hyena_prep(p, conv_w, conv_b, group_major=False)
    taps, sums = _hyena_filter(n, feats, deltas, w1, b1, f1, w2, b2, f2, w3, group_major=False)
    fc, fr, fi = _dense_dft_constants(n)
    mats = [*_hilo(fc), *_hilo(fr), *_hilo(fi)]
    cw = 256
    nct = HYENA_WIDTH // cw
    dspec = pl.BlockSpec((2, n, cw), lambda ci: (0, 0, ci))
    return pl.pallas_call(
        functools.partial(_dense_conv_kernel, n=n),
        out_shape=jax.ShapeDtypeStruct(vx.shape, F32),
        grid=(nct,),
        in_specs=[dspec, dspec,
                  pl.BlockSpec((n, cw), lambda ci: (0, ci)), pl.BlockSpec((n, cw), lambda ci: (0, nct + ci)),
                  pl.BlockSpec((SUBLANES, cw), lambda ci: (0, ci)), pl.BlockSpec((SUBLANES, cw), lambda ci: (0, nct + ci)),
                  pl.BlockSpec((1, cw), lambda ci: (0, ci))]
                 + [pl.BlockSpec(mt.shape, lambda ci: (0, 0)) for mt in mats],
        out_specs=dspec,
        compiler_params=_params(("parallel",)),
        name="context_long_conv",
    )(vx, x0, taps, taps, sums, sums, bias.reshape(1, HYENA_WIDTH), *mats)


def _pool_kernel(x_ref, p_ref, n_ref, w_ref, sc_ref, o_ref, *, tl, nt, n):
    i = pl.program_id(1)
    x = x_ref[...]
    pv = jnp.where(i > 0, p_ref[...], 0.0)
    nx = jnp.where(i < nt - 1, n_ref[...], 0.0)
    ext = jnp.concatenate([pv, x, nx], axis=0)
    rows = tl + 2 * POOL_HALO
    t = i * tl + lax.broadcasted_iota(jnp.int32, (tl, POOL_GROUP), 0)
    for g, w in enumerate(POOL_WINDOWS):
        lanes = slice(g * POOL_GROUP, (g + 1) * POOL_GROUP)
        a = ext[:, lanes]
        c = a + pltpu.roll(a, 1, 0)
        h = 1
        while 2 * h < w:
            c = pltpu.roll(c, h, 0) + pltpu.roll(c, rows - h, 0)
            h *= 2
        total = c[POOL_HALO:POOL_HALO + tl]
        count = (jnp.minimum(t + h, n) - jnp.maximum(t - h, 0)).astype(F32)
        y = (total / count - x[:, lanes]).astype(BF16)
        o_ref[:, lanes] = _dot(y, w_ref[g].astype(BF16)) * sc_ref[:, lanes]


def _pool_mixer(p, w_pool, scale):
    b, n, _ = p.shape
    tl = min(512, n)
    nt = n // tl
    colblk = 3 * HYENA_WIDTH // POOL_WIDTH
    assert colblk * POOL_WIDTH == 3 * HYENA_WIDTH
    per = tl // POOL_HALO
    nrow = n // POOL_HALO
    return pl.pallas_call(
        functools.partial(_pool_kernel, tl=tl, nt=nt, n=n),
        out_shape=jax.ShapeDtypeStruct((b, n, POOL_WIDTH), F32),
        grid=(b, nt),
        in_specs=[
            pl.BlockSpec((None, tl, POOL_WIDTH), lambda bi, i: (bi, i, colblk)),
            pl.BlockSpec((None, POOL_HALO, POOL_WIDTH), lambda bi, i: (bi, jnp.maximum(i * per - 1, 0), colblk)),
            pl.BlockSpec((None, POOL_HALO, POOL_WIDTH), lambda bi, i: (bi, jnp.minimum((i + 1) * per, nrow - 1), colblk)),
            pl.BlockSpec(w_pool.shape, lambda bi, i: (0, 0, 0)),
            pl.BlockSpec((1, POOL_WIDTH), lambda bi, i: (0, 0)),
        ],
        out_specs=pl.BlockSpec((None, tl, POOL_WIDTH), lambda bi, i: (bi, i, 0)),
        compiler_params=_params(("parallel", "parallel")),
        name="pool_mixer",
    )(p, p, p, w_pool, scale.reshape(1, POOL_WIDTH))


def _outproj_kernel(at_ref, hy_ref, po_ref, x_ref, gb_ref, w_ref, gp_ref, gt_ref, o_ref, m_scr, ox_scr, pg_scr,
                    *, tm, group_major):
    a0, a1 = ATTN_WIDTH, ATTN_WIDTH + HYENA_WIDTH
    _norm_scale_rows(at_ref, m_scr, gb_ref.at[:, :a0], None, tm, cols=slice(0, a0))
    _norm_scale_rows(po_ref, m_scr, gb_ref.at[:, a1:], None, tm, cols=slice(a1, None))
    if group_major:
        ghy = gb_ref[:, a0:a1]
        ncc = HYENA_WIDTH // LANES
        per = BF16_ROWS // SUBLANES
        for t1 in range(tm // FFT_N2):
            for jg in range(0, GROUPS, per):
                y = jnp.concatenate([jnp.concatenate([hy_ref[cc, jg + k, t1] for cc in range(ncc)], axis=1)
                                     for k in range(per)], axis=0)
                r0 = (t1 * GROUPS + jg) * SUBLANES
                m_scr[r0:r0 + BF16_ROWS, a0:a1] = _rms(y, ghy).astype(BF16)
    else:
        _norm_scale_rows(hy_ref, m_scr, gb_ref.at[:, a0:a1], None, tm, cols=slice(a0, a1))
    ox_scr[...] = _dot(m_scr[...], w_ref[...])
    pg_scr[...] = gp_ref[...] * gt_ref[...]
    _residual_norm_rows(x_ref, ox_scr, o_ref, pg_scr, tm)


def _out_projection(attn, hy, po, x, g_branch, w_out, g_post, gate, group_major):
    b, n, d = x.shape
    tm = min(512, n)
    row = lambda width: pl.BlockSpec((None, tm, width), lambda bi, i: (bi, i, 0))
    vec = lambda width: pl.BlockSpec((1, width), lambda bi, i: (0, 0))
    if group_major:
        hy_spec = pl.BlockSpec((None, *_gm_shape((), tm // FFT_N2, HYENA_WIDTH)), lambda bi, i: (bi, 0, 0, i, 0, 0))
    else:
        hy_spec = row(HYENA_WIDTH)
    return pl.pallas_call(
        functools.partial(_outproj_kernel, tm=tm, group_major=group_major),
        out_shape=jax.ShapeDtypeStruct(x.shape, F32),
        grid=(b, n // tm),
        in_specs=[row(ATTN_WIDTH), hy_spec, row(POOL_WIDTH), row(d), vec(d),
                  pl.BlockSpec(w_out.shape, lambda bi, i: (0, 0)), vec(d),
                  pl.BlockSpec((None, 1, d), lambda bi, i: (bi, 0, 0))],
        out_specs=row(d),
        scratch_shapes=[pltpu.VMEM((tm, d), BF16), pltpu.VMEM((tm, d), F32), pltpu.VMEM((1, d), F32)],
        compiler_params=_params(("parallel", "parallel")),
        name="out_projection",
    )(attn, hy, po, x, g_branch.reshape(1, d), w_out, g_post.reshape(1, d), gate)


def _mlp_kernel(x_ref, g_ref, sh_ref, sc_ref, wu_ref, wd_ref, gp_ref, gt_ref, o_ref, h_ref, acc_ref, a_scr, pg_scr,
                *, tm):
    k = pl.program_id(2)

    @pl.when(k == 0)
    def _():
        a_scr[...] = g_ref[...] * (1.0 + sc_ref[...])
        _norm_scale_rows(x_ref, h_ref, a_scr, sh_ref, tm)
        acc_ref[...] = jnp.zeros_like(acc_ref)

    a = jnp.maximum(_dot(h_ref[...], wu_ref[...]), 0.0)
    acc_ref[...] += _dot((a * a).astype(BF16), wd_ref[...])

    @pl.when(k == pl.num_programs(2) - 1)
    def _():
        pg_scr[...] = gp_ref[...] * gt_ref[...]
        _residual_norm_rows(x_ref, acc_ref, o_ref, pg_scr, tm)


def _mlp(x, g_pre, shift, scale, w_up, w_down, g_post, gate):
    b, n, d = x.shape
    hidden = w_up.shape[1]
    tm = min(512, n)
    th = 512
    vec = pl.BlockSpec((1, d), lambda bi, i, k: (0, 0))
    bvec = pl.BlockSpec((None, 1, d), lambda bi, i, k: (bi, 0, 0))
    row = pl.BlockSpec((None, tm, d), lambda bi, i, k: (bi, i, 0))
    return pl.pallas_call(
        functools.partial(_mlp_kernel, tm=tm),
        out_shape=jax.ShapeDtypeStruct(x.shape, F32),
        grid=(b, n // tm, hidden // th),
        in_specs=[row, vec, bvec, bvec,
                  pl.BlockSpec((d, th), lambda bi, i, k: (0, k)),
                  pl.BlockSpec((th, d), lambda bi, i, k: (k, 0)),
                  vec, bvec],
        out_specs=row,
        scratch_shapes=[pltpu.VMEM((tm, d), BF16), pltpu.VMEM((tm, d), F32), pltpu.VMEM((1, d), F32),
                        pltpu.VMEM((1, d), F32)],
        compiler_params=_params(("parallel", "parallel", "arbitrary")),
        name="mlp",
    )(x, g_pre.reshape(1, d), shift, scale, w_up, w_down, g_post.reshape(1, d), gate)


def kernel(x, c, ctx, c_ctx, w_mod, b_mod, g_pre_mix, g_post_mix, g_pre_mlp, g_post_mlp, w_in, w_out, g_branch,
           attn_sink, hy_conv_w, hy_conv_b, hy_w1, hy_b1, hy_freq1, hy_w2, hy_b2, hy_freq2, hy_w3, hy_bias,
           pool_w, pool_scale, w_up, w_down):
    b, n, d = x.shape
    n_ctx = ctx.shape[1]
    depth = w_mod.shape[0]
    assert b == 2 and d == D_MODEL and n % 512 == 0 and n_ctx % BLOCK == 0

    cond = jnp.concatenate([c, c_ctx[None], jnp.zeros((SUBLANES - b - 1, d), F32)], axis=0)
    mods = _modulation(cond, w_mod, b_mod)

    w_in_b = _permute_rope_columns(w_in).astype(BF16)
    w_out_b, w_up_b, w_down_b = (w.astype(BF16) for w in (w_out, w_up, w_down))
    rope = _rope_tables(n)
    feats_x, feats_c = _filter_features(n), _filter_features(n_ctx)
    deltas = _filter_deltas()
    tables = _stage2_tables(n)

    for i in range(depth):
        last = i == depth - 1
        hy = (hy_conv_w[i], hy_conv_b[i], hy_w1[i], hy_b1[i], hy_freq1[i], hy_w2[i], hy_b2[i], hy_freq2[i],
              hy_w3[i], hy_bias[i])
        mx = [m[:, None, :] for m in jnp.split(mods[i, :b], N_MOD, axis=-1)]
        mc = [jnp.broadcast_to(m[None, None, :], (b, 1, d)) for m in jnp.split(mods[i, b], N_MOD, axis=-1)]

        qkv_x, px = _in_projection(x, g_pre_mix[i], mx[0], mx[1], w_in_b[i], rope)
        qkv_c, pc = _in_projection(ctx, g_pre_mix[i], mc[0], mc[1], w_in_b[i])
        kv_ctx = qkv_c[..., Q_END:V_END]

        attn_x = _attention(qkv_x, kv_ctx, attn_sink[i], local=True)
        hy_x = _hyena_latent(px, hy, feats_x, deltas, tables, n)
        po_x = _pool_mixer(px, pool_w[i], pool_scale[i])
        x = _out_projection(attn_x, hy_x, po_x, x, g_branch[i], w_out_b[i], g_post_mix[i], mx[2], group_major=True)
        x = _mlp(x, g_pre_mlp[i], mx[3], mx[4], w_up_b[i], w_down_b[i], g_post_mlp[i], mx[5])

        if not last:
            attn_c = _attention(qkv_c, kv_ctx, attn_sink[i], local=False)
            hy_c = _hyena_context(pc, hy, feats_c, deltas, n_ctx)
            po_c = _pool_mixer(pc, pool_w[i], pool_scale[i])
            ctx = _out_projection(attn_c, hy_c, po_c, ctx, g_branch[i], w_out_b[i], g_post_mix[i], mc[2],
                                  group_major=False)
            ctx = _mlp(ctx, g_pre_mlp[i], mc[3], mc[4], w_up_b[i], w_down_b[i], g_post_mlp[i], mc[5])
    return x
```

```python
import functools
import math

import numpy as np
import jax
import jax.numpy as jnp
from jax import lax
from jax.experimental import pallas as pl
from jax.experimental.pallas import tpu as pltpu

F32 = jnp.float32
BF16 = jnp.bfloat16

D_MODEL = 2048
DEPTH = 4
GRID_W = 64
ATTN_WIDTH = D_MODEL // 2
HYENA_WIDTH = D_MODEL // 4
POOL_WIDTH = D_MODEL - ATTN_WIDTH - HYENA_WIDTH
HEAD_DIM = 128
N_HEADS = ATTN_WIDTH // HEAD_DIM
N_KV_HEADS = 2
KV_GROUP = N_HEADS // N_KV_HEADS
KV_WIDTH = N_KV_HEADS * HEAD_DIM
WINDOW = 128
BLOCK = 128
ROPE_BASE = 10000.0
HYENA_EMB_DIM = 33
HYENA_FILTER_HIDDEN = 64
HYENA_FAST_DECAY_PCT = 0.3
HYENA_SLOW_DECAY_PCT = 1.5
HYENA_DECAY_TARGET = 1e-2
POOL_WINDOWS = (2, 4, 8, 16)
POOL_GROUP = POOL_WIDTH // len(POOL_WINDOWS)
MLP_HIDDEN = 4 * D_MODEL
N_MOD = 6
EPS = 1e-6
NEG_INF = -1e30

Q_END = ATTN_WIDTH
K_END = Q_END + KV_WIDTH
V_END = K_END + KV_WIDTH
HY_END = V_END + 3 * HYENA_WIDTH
IN_WIDTH = HY_END + POOL_WIDTH

LANES = 128
SUBLANES = 8
FFT_N2 = 128
POOL_HALO = 16
VMEM_LIMIT = 56 * 1024 * 1024


def _params(sem, vmem=VMEM_LIMIT):
    return pltpu.CompilerParams(dimension_semantics=sem, vmem_limit_bytes=vmem)


def _split(x):
    hi = x.astype(BF16)
    lo = (x - hi.astype(F32)).astype(BF16)
    return hi, lo


def _dot(a, b):
    return jnp.dot(a, b, preferred_element_type=F32)


def _dot3(ah, al, bh, bl):
    return _dot(ah, bh) + _dot(ah, bl) + _dot(al, bh)


def _rms(x, g):
    return x * lax.rsqrt(jnp.mean(x * x, axis=-1, keepdims=True) + EPS) * g


BF16_ROWS = 2 * SUBLANES


def _norm_scale_rows(x_ref, h_ref, a_ref, s_ref, rows, cols=None):
    cols = slice(None) if cols is None else cols
    nchunks = rows // BF16_ROWS

    def chunk(c):
        return pl.ds(pl.multiple_of(c * BF16_ROWS, BF16_ROWS), BF16_ROWS)

    def inv_rms(c):
        x = x_ref[chunk(c), :]
        return lax.rsqrt(jnp.mean(x * x, axis=-1, keepdims=True) + EPS)

    def body(c, inv):
        inv_next = inv_rms(jnp.minimum(c + 1, nchunks - 1))
        y = x_ref[chunk(c), :] * inv * a_ref[...]
        if s_ref is not None:
            y = y + s_ref[...]
        h_ref[chunk(c), cols] = y.astype(BF16)
        return inv_next

    lax.fori_loop(0, nchunks, body, inv_rms(0), unroll=8)


def _residual_norm_rows(x_ref, y_ref, o_ref, pg_ref, rows):
    nchunks = rows // SUBLANES

    def chunk(c):
        return pl.ds(pl.multiple_of(c * SUBLANES, SUBLANES), SUBLANES)

    def inv_rms(c):
        y = y_ref[chunk(c), :]
        return lax.rsqrt(jnp.mean(y * y, axis=-1, keepdims=True) + EPS)

    def body(c, inv):
        inv_next = inv_rms(jnp.minimum(c + 1, nchunks - 1))
        o_ref[chunk(c), :] = x_ref[chunk(c), :] + y_ref[chunk(c), :] * inv * pg_ref[...]
        return inv_next

    lax.fori_loop(0, nchunks, body, inv_rms(0), unroll=16)


def _mod_kernel(c_ref, w_ref, b_ref, o_ref):
    c = c_ref[...]
    s = c / (1.0 + jnp.exp(-c))
    sh, sl = _split(s)
    wh, wl = _split(w_ref[...])
    o_ref[...] = _dot3(sh, sl, wh, wl) + b_ref[...]


def _modulation(cond, w_mod, b_mod):
    depth, d, width = w_mod.shape
    tn = 1024
    return pl.pallas_call(
        _mod_kernel,
        out_shape=jax.ShapeDtypeStruct((depth, SUBLANES, width), F32),
        grid=(depth, width // tn),
        in_specs=[
            pl.BlockSpec((SUBLANES, d), lambda l, j: (0, 0)),
            pl.BlockSpec((None, d, tn), lambda l, j: (l, 0, j)),
            pl.BlockSpec((None, 1, tn), lambda l, j: (l, 0, j)),
        ],
        out_specs=pl.BlockSpec((None, SUBLANES, tn), lambda l, j: (l, 0, j)),
        compiler_params=_params(("parallel", "parallel")),
        name="modulation",
    )(cond, w_mod, b_mod.reshape(depth, 1, width))


QKV_WIDTH = V_END
REST_WIDTH = IN_WIDTH - V_END
SM_SCALE = HEAD_DIM ** -0.5


def _inproj_kernel(*refs, tm, tn, rope):
    if rope:
        x_ref, g_ref, sh_ref, sc_ref, w_ref, cq_ref, sq_ref, ck_ref, sk_ref, qkv_ref, rest_ref, h_ref, a_scr = refs
    else:
        x_ref, g_ref, sh_ref, sc_ref, w_ref, qkv_ref, rest_ref, h_ref, a_scr = refs
    j = pl.program_id(2)

    @pl.when(j == 0)
    def _():
        a_scr[...] = g_ref[...] * (1.0 + sc_ref[...])
        _norm_scale_rows(x_ref, h_ref, a_scr, sh_ref, tm)

    acc = _dot(h_ref[...], w_ref[...])
    nch = tn // LANES

    def write_qkv(t):
        for ch in range(nch):
            col = (t * nch + ch) * LANES
            a = acc[:, ch * LANES:(ch + 1) * LANES]
            if col < K_END and rope:
                cos, sin = (cq_ref, sq_ref) if col < Q_END else (ck_ref, sk_ref)
                a = a * cos[...] + pltpu.roll(a, HEAD_DIM // 2, 1) * sin[...]
            elif col < Q_END:
                a = a * SM_SCALE
            qkv_ref[:, ch * LANES:(ch + 1) * LANES] = a.astype(BF16)

    for t in range(QKV_WIDTH // tn):
        pl.when(j == t)(functools.partial(write_qkv, t))

    @pl.when(j >= QKV_WIDTH // tn)
    def _():
        rest_ref[...] = acc


def _in_projection(x, g, shift, scale, w, rope_tables=None):
    b, n, d = x.shape
    tm = min(1024, n)
    tn = 512
    nqkv = QKV_WIDTH // tn
    assert nqkv * tn == QKV_WIDTH and w.shape[1] == IN_WIDTH
    rope = rope_tables is not None
    in_specs = [
        pl.BlockSpec((None, tm, d), lambda bi, i, j: (bi, i, 0)),
        pl.BlockSpec((1, d), lambda bi, i, j: (0, 0)),
        pl.BlockSpec((None, 1, d), lambda bi, i, j: (bi, 0, 0)),
        pl.BlockSpec((None, 1, d), lambda bi, i, j: (bi, 0, 0)),
        pl.BlockSpec((d, tn), lambda bi, i, j: (0, j)),
    ]
    args = [x, g.reshape(1, d), shift, scale, w]
    if rope:
        in_specs += [pl.BlockSpec((tm, LANES), lambda bi, i, j: (i, 0))] * 4
        args += list(rope_tables)
    return pl.pallas_call(
        functools.partial(_inproj_kernel, tm=tm, tn=tn, rope=rope),
        out_shape=[jax.ShapeDtypeStruct((b, n, QKV_WIDTH), BF16), jax.ShapeDtypeStruct((b, n, REST_WIDTH), F32)],
        grid=(b, n // tm, IN_WIDTH // tn),
        in_specs=in_specs,
        out_specs=[pl.BlockSpec((None, tm, tn), lambda bi, i, j: (bi, i, jnp.minimum(j, nqkv - 1))),
                   pl.BlockSpec((None, tm, tn), lambda bi, i, j: (bi, i, jnp.maximum(j - nqkv, 0)))],
        scratch_shapes=[pltpu.VMEM((tm, d), BF16), pltpu.VMEM((1, d), F32)],
        compiler_params=_params(("parallel", "parallel", "arbitrary")),
        name="in_projection_rope" if rope else "in_projection",
    )(*args)


def _permute_rope_columns(w_in):
    lead = w_in.shape[:-1]
    quarter = HEAD_DIM // 4
    qk = w_in[..., :K_END].reshape(*lead, K_END // HEAD_DIM, 2, 2, quarter)
    qk = jnp.swapaxes(qk, -3, -2).reshape(*lead, K_END)
    return jnp.concatenate([qk, w_in[..., K_END:]], axis=-1)


def _rope_tables(n):
    quarter = HEAD_DIM // 4
    inv_freq = ROPE_BASE ** (-jnp.arange(quarter, dtype=F32) / quarter)
    t = jnp.arange(n, dtype=jnp.int32)
    row = (t // GRID_W).astype(F32)[:, None] * inv_freq[None, :]
    col = (t % GRID_W).astype(F32)[:, None] * inv_freq[None, :]
    cos = jnp.concatenate([jnp.cos(row), jnp.cos(col), jnp.cos(row), jnp.cos(col)], axis=-1)
    sin = jnp.concatenate([-jnp.sin(row), -jnp.sin(col), jnp.sin(row), jnp.sin(col)], axis=-1)
    return cos * SM_SCALE, sin * SM_SCALE, cos, sin


def _softmax_pv(parts, sink_col):
    m = sink_col
    for s, _ in parts:
        m = jnp.maximum(m, jnp.max(s, axis=-1, keepdims=True))
    den = jnp.exp(sink_col - m)
    out = None
    for s, v in parts:
        p = jnp.exp(s - m)
        den = den + jnp.sum(p, axis=-1, keepdims=True)
        pv = _dot(p.astype(BF16), v)
        out = pv if out is None else out + pv
    return out / den


def _ctx_attn_kernel(sink_ref, q_ref, kvc_ref, o_ref, *, tq):
    for g in range(N_KV_HEADS):
        heads = [g * KV_GROUP + h for h in range(KV_GROUP)]
        qg = jnp.concatenate([q_ref[:, h * HEAD_DIM:(h + 1) * HEAD_DIM] for h in heads], axis=0)
        sink_col = jnp.concatenate([jnp.full((tq, 1), sink_ref[h], F32) for h in heads], axis=0)
        kc = kvc_ref[:, g * HEAD_DIM:(g + 1) * HEAD_DIM]
        vc = kvc_ref[:, KV_WIDTH + g * HEAD_DIM:KV_WIDTH + (g + 1) * HEAD_DIM]
        s = lax.dot_general(qg, kc, (((1,), (1,)), ((), ())), preferred_element_type=F32)
        o = _softmax_pv([(s, vc)], sink_col)
        for hi, h in enumerate(heads):
            o_ref[:, h * HEAD_DIM:(h + 1) * HEAD_DIM] = o[hi * tq:(hi + 1) * tq]


ATTN_ROWS = KV_GROUP * BLOCK
SOFTMAX_CHUNK = 32


def _win_attn_kernel(sink_ref, q_ref, km_ref, kp_ref, kn_ref, vm_ref, vp_ref, vn_ref, kvc_ref, band_ref,
                     o_ref, ktw, vw, s_scr, p_scr, m_scr, *, tq, nb, nctx):
    i = pl.program_id(1)
    nsub = tq // BLOCK
    nloc = 3 * BLOCK

    def transposed(x):
        return x.astype(F32).T.astype(BF16)

    def block_rows(main_ref, prev_ref, next_ref, w, lanes):
        if w == 0:
            return prev_ref[:, lanes]
        if w == nsub + 1:
            return next_ref[:, lanes]
        return main_ref[(w - 1) * BLOCK:w * BLOCK, lanes]

    ones = jnp.ones((nloc + nctx, HEAD_DIM), BF16)
    for g in range(N_KV_HEADS):
        lanes = slice(g * HEAD_DIM, (g + 1) * HEAD_DIM)
        vlanes = slice(KV_WIDTH + g * HEAD_DIM, KV_WIDTH + (g + 1) * HEAD_DIM)
        kts = [transposed(block_rows(km_ref, kp_ref, kn_ref, w, lanes)) for w in range(nsub + 2)]
        kct = [transposed(kvc_ref[cb * BLOCK:(cb + 1) * BLOCK, lanes]) for cb in range(nctx // BLOCK)]
        for jb in range(nsub):
            for w in range(3):
                ktw[jb, g, :, w * BLOCK:(w + 1) * BLOCK] = kts[jb + w]
                vw[jb, g, w * BLOCK:(w + 1) * BLOCK, :HEAD_DIM] = block_rows(vm_ref, vp_ref, vn_ref, jb + w, lanes)
            for cb in range(nctx // BLOCK):
                ktw[jb, g, :, nloc + cb * BLOCK:nloc + (cb + 1) * BLOCK] = kct[cb]
            vw[jb, g, nloc:, :HEAD_DIM] = kvc_ref[:, vlanes]
            vw[jb, g, :, HEAD_DIM:] = ones

    col = lax.broadcasted_iota(jnp.int32, (1, nloc), 1)

    def rows_of(jb):
        start = jb * BLOCK
        return pl.ds(start if isinstance(start, int) else pl.multiple_of(start, BLOCK), BLOCK)

    def stage_a(jb, g):
        qg = jnp.concatenate([q_ref[rows_of(jb), (g * KV_GROUP + h) * HEAD_DIM:(g * KV_GROUP + h + 1) * HEAD_DIM]
                              for h in range(KV_GROUP)], axis=0)
        s_scr[g] = _dot(qg, ktw[jb, g])

    def stage_b(jb, g):
        blk = i * nsub + jb
        pen_prev = jnp.where(blk == 0, NEG_INF, 0.0).astype(F32)
        pen_next = jnp.where(blk == nb - 1, NEG_INF, 0.0).astype(F32)
        rowbias = jnp.where(col < BLOCK, pen_prev, jnp.where(col >= 2 * BLOCK, pen_next, 0.0))
        for c in range(ATTN_ROWS // SOFTMAX_CHUNK):
            rows = slice(c * SOFTMAX_CHUNK, (c + 1) * SOFTMAX_CHUNK)
            sink = sink_ref[g * KV_GROUP + (c * SOFTMAX_CHUNK) // BLOCK]
            s_loc = s_scr[g, rows, :nloc] + band_ref[rows, :] + rowbias
            s_ctx = s_scr[g, rows, nloc:]
            m = jnp.maximum(jnp.max(s_loc, axis=-1, keepdims=True), jnp.max(s_ctx, axis=-1, keepdims=True))
            m = jnp.maximum(m, sink)
            p_scr[g, rows, :nloc] = jnp.exp(s_loc - m).astype(BF16)
            p_scr[g, rows, nloc:] = jnp.exp(s_ctx - m).astype(BF16)
            m_scr[g, rows, :] = m

    def stage_c(jb, g):
        o = _dot(p_scr[g], vw[jb, g])
        for hi in range(KV_GROUP):
            h = g * KV_GROUP + hi
            rows = slice(hi * BLOCK, (hi + 1) * BLOCK)
            den = o[rows, HEAD_DIM:HEAD_DIM + 1] + jnp.exp(sink_ref[h] - m_scr[g, rows, :])
            o_ref[rows_of(jb), h * HEAD_DIM:(h + 1) * HEAD_DIM] = o[rows, :HEAD_DIM] / den

    stage_a(0, 0)
    stage_a(0, 1)
    stage_b(0, 0)

    def body(j, carry):
        stage_a(j, 0)
        stage_c(j - 1, 0)
        stage_b(j - 1, 1)
        stage_a(j, 1)
        stage_c(j - 1, 1)
        stage_b(j, 0)
        return carry

    lax.fori_loop(1, nsub, body, 0)
    stage_c(nsub - 1, 0)
    stage_b(nsub - 1, 1)
    stage_c(nsub - 1, 1)


def _band_bias():
    qi = np.arange(ATTN_ROWS)[:, None] % BLOCK
    sj = np.arange(3 * BLOCK)[None, :]
    return jnp.asarray(np.where(np.abs(sj - BLOCK - qi) <= WINDOW, 0.0, NEG_INF), dtype=F32)


def _attention(qkv, kv_ctx, sink, local):
    b, n, _ = qkv.shape
    c = kv_ctx.shape[1]
    tq = min(1024 if local else 512, n)
    nsub = tq // BLOCK
    nb = n // BLOCK
    kcol = Q_END // KV_WIDTH
    vcol = K_END // KV_WIDTH
    smem = pl.BlockSpec(memory_space=pltpu.SMEM)
    q_spec = pl.BlockSpec((None, tq, ATTN_WIDTH), lambda bi, i: (bi, i, 0))
    kvc_spec = pl.BlockSpec((None, c, 2 * KV_WIDTH), lambda bi, i: (bi, 0, 0))
    if local:
        def main(colblk):
            return pl.BlockSpec((None, tq, KV_WIDTH), lambda bi, i: (bi, i, colblk))

        def prev(colblk):
            return pl.BlockSpec((None, BLOCK, KV_WIDTH), lambda bi, i: (bi, jnp.maximum(i * nsub - 1, 0), colblk))

        def nxt(colblk):
            return pl.BlockSpec((None, BLOCK, KV_WIDTH), lambda bi, i: (bi, jnp.minimum((i + 1) * nsub, nb - 1), colblk))

        keys = 3 * BLOCK + c
        kern = functools.partial(_win_attn_kernel, tq=tq, nb=nb, nctx=c)
        in_specs = [smem, q_spec, main(kcol), prev(kcol), nxt(kcol), main(vcol), prev(vcol), nxt(vcol), kvc_spec,
                    pl.BlockSpec((ATTN_ROWS, 3 * BLOCK), lambda bi, i: (0, 0))]
        args = [sink, qkv, qkv, qkv, qkv, qkv, qkv, qkv, kv_ctx, _band_bias()]
        scratch = [pltpu.VMEM((nsub, N_KV_HEADS, HEAD_DIM, keys), BF16),
                   pltpu.VMEM((nsub, N_KV_HEADS, keys, 2 * HEAD_DIM), BF16),
                   pltpu.VMEM((2, ATTN_ROWS, keys), F32),
                   pltpu.VMEM((2, ATTN_ROWS, keys), BF16),
                   pltpu.VMEM((2, ATTN_ROWS, 1), F32)]
    else:
        kern = functools.partial(_ctx_attn_kernel, tq=tq)
        in_specs = [smem, q_spec, kvc_spec]
        args = [sink, qkv, kv_ctx]
        scratch = []
    return pl.pallas_call(
        kern,
        out_shape=jax.ShapeDtypeStruct((b, n, ATTN_WIDTH), F32),
        grid=(b, n // tq),
        in_specs=in_specs,
        out_specs=pl.BlockSpec((None, tq, ATTN_WIDTH), lambda bi, i: (bi, i, 0)),
        scratch_shapes=scratch,
        compiler_params=_params(("parallel", "parallel")),
        name="window_attention" if local else "context_attention",
    )(*args)


GROUPS = FFT_N2 // SUBLANES


def _gm_shape(lead, t1, width):
    return (*lead, width // LANES, GROUPS, t1, SUBLANES, LANES)


def _store_group_major(o_ref, val, tl):
    for t1 in range(tl // FFT_N2):
        for jg in range(GROUPS):
            r0 = (t1 * GROUPS + jg) * SUBLANES
            for cc in range(val.shape[1] // LANES):
                o_ref[cc, jg, t1] = val[r0:r0 + SUBLANES, cc * LANES:(cc + 1) * LANES]


def _hyena_prep_kernel(u_ref, p_ref, n_ref, w_ref, b_ref, vx_ref, x0_ref, *, tl, nt, group_major):
    i = pl.program_id(1)
    u = u_ref[...]
    prev_row = jnp.where(i > 0, p_ref[SUBLANES - 1:SUBLANES, :], 0.0)
    next_row = jnp.where(i < nt - 1, n_ref[0:1, :], 0.0)
    row = lax.broadcasted_iota(jnp.int32, u.shape, 0)
    um = jnp.where(row == 0, prev_row, pltpu.roll(u, 1, 0))
    up = jnp.where(row == tl - 1, next_row, pltpu.roll(u, tl - 1, 0))
    z = um * w_ref[0:1, :] + u * w_ref[1:2, :] + up * w_ref[2:3, :] + b_ref[...]
    x0 = z[:, :HYENA_WIDTH]
    vx = z[:, 2 * HYENA_WIDTH:] * z[:, HYENA_WIDTH:2 * HYENA_WIDTH]
    if group_major:
        _store_group_major(vx_ref, vx, tl)
        _store_group_major(x0_ref, x0, tl)
    else:
        vx_ref[...] = vx
        x0_ref[...] = x0


def _hyena_prep(p, conv_w, conv_b, group_major):
    b, n, _ = p.shape
    tl = min(512, n)
    nt = n // tl
    hw = 3 * HYENA_WIDTH
    colblk = 0
    nrow8 = n // SUBLANES
    per = tl // SUBLANES
    if group_major:
        shape = _gm_shape((b,), n // FFT_N2, HYENA_WIDTH)
        out_spec = pl.BlockSpec((None, *_gm_shape((), tl // FFT_N2, HYENA_WIDTH)), lambda bi, i: (bi, 0, 0, i, 0, 0))
    else:
        shape = (b, n, HYENA_WIDTH)
        out_spec = pl.BlockSpec((None, tl, HYENA_WIDTH), lambda bi, i: (bi, i, 0))
    return pl.pallas_call(
        functools.partial(_hyena_prep_kernel, tl=tl, nt=nt, group_major=group_major),
        out_shape=[jax.ShapeDtypeStruct(shape, F32)] * 2,
        grid=(b, nt),
        in_specs=[
            pl.BlockSpec((None, tl, hw), lambda bi, i: (bi, i, colblk)),
            pl.BlockSpec((None, SUBLANES, hw), lambda bi, i: (bi, jnp.maximum(i * per - 1, 0), colblk)),
            pl.BlockSpec((None, SUBLANES, hw), lambda bi, i: (bi, jnp.minimum((i + 1) * per, nrow8 - 1), colblk)),
            pl.BlockSpec((3, hw), lambda bi, i: (0, 0)),
            pl.BlockSpec((1, hw), lambda bi, i: (0, 0)),
        ],
        out_specs=[out_spec, out_spec],
        compiler_params=_params(("parallel", "parallel")),
        name="hyena_prep",
    )(p, p, p, conv_w, conv_b.reshape(1, hw))


def _filter_kernel(ft_ref, w1_ref, b1_ref, f1_ref, w2_ref, b2_ref, f2_ref, w3_ref, dl_ref, h_ref, s_ref,
                   *, tl, n, group_major):
    i = pl.program_id(0)

    def dense(a, w_ref):
        ah, al = _split(a)
        wh, wl = _split(w_ref[...])
        return _dot3(ah, al, wh, wl)

    h = jnp.sin(f1_ref[...] * (dense(ft_ref[...], w1_ref) + b1_ref[...]))
    h = jnp.sin(f2_ref[...] * (dense(h, w2_ref) + b2_ref[...]))
    h = dense(h, w3_ref)
    t = (i * tl + lax.broadcasted_iota(jnp.int32, (tl, HYENA_WIDTH), 0)).astype(F32) / float(n - 1)
    decay = jnp.exp(-t * dl_ref[...])
    h = h * jnp.concatenate([decay, decay], axis=1)
    if group_major:
        _store_group_major(h_ref, h, tl)
    else:
        h_ref[...] = h

    @pl.when(i == 0)
    def _():
        s_ref[...] = jnp.zeros_like(s_ref)

    s_ref[...] += jnp.sum(jnp.abs(h).reshape(tl // SUBLANES, SUBLANES, 2 * HYENA_WIDTH), axis=0)


def _filter_features(n):
    t = jnp.linspace(0.0, 1.0, n, dtype=F32)[:, None]
    bands = (HYENA_EMB_DIM - 1) // 2
    omega = 2.0 * math.pi * jnp.arange(n, dtype=F32)[:, None] / n
    f = jnp.linspace(1e-4, bands - 1, bands, dtype=F32)[None, :]
    feats = jnp.concatenate([t, jnp.cos(f * omega), -jnp.sin(f * omega)], axis=-1)
    return jnp.pad(feats, ((0, 0), (0, LANES - HYENA_EMB_DIM)))


def _filter_deltas():
    max_decay = math.log(HYENA_DECAY_TARGET) / HYENA_FAST_DECAY_PCT
    min_decay = math.log(HYENA_DECAY_TARGET) / HYENA_SLOW_DECAY_PCT
    return jnp.abs(jnp.linspace(min_decay, max_decay, HYENA_WIDTH, dtype=F32)).reshape(1, HYENA_WIDTH)


def _hyena_filter(n, feats, deltas, w1, b1, f1, w2, b2, f2, w3, group_major):
    tl = min(512, n)
    hid = LANES
    pad_h = hid - HYENA_FILTER_HIDDEN
    w1p = jnp.pad(w1, ((0, LANES - HYENA_EMB_DIM), (0, pad_h)))
    w2p = jnp.pad(w2, ((0, pad_h), (0, pad_h)))
    w3p = jnp.pad(w3, ((0, pad_h), (0, 0)))
    vec = lambda v: jnp.pad(v, (0, pad_h)).reshape(1, hid)
    hw2 = 2 * HYENA_WIDTH
    full = lambda shape: pl.BlockSpec(shape, lambda i: (0,) * len(shape))
    if group_major:
        shape = _gm_shape((), n // FFT_N2, hw2)
        out_spec = pl.BlockSpec(_gm_shape((), tl // FFT_N2, hw2), lambda i: (0, 0, i, 0, 0))
    else:
        shape = (n, hw2)
        out_spec = pl.BlockSpec((tl, hw2), lambda i: (i, 0))
    return pl.pallas_call(
        functools.partial(_filter_kernel, tl=tl, n=n, group_major=group_major),
        out_shape=[jax.ShapeDtypeStruct(shape, F32), jax.ShapeDtypeStruct((SUBLANES, hw2), F32)],
        grid=(n // tl,),
        in_specs=[pl.BlockSpec((tl, LANES), lambda i: (i, 0)), full((LANES, hid)), full((1, hid)), full((1, hid)),
                  full((hid, hid)), full((1, hid)), full((1, hid)), full((hid, hw2)), full((1, HYENA_WIDTH))],
        out_specs=[out_spec, full((SUBLANES, hw2))],
        compiler_params=_params(("arbitrary",)),
        name="hyena_filter",
    )(feats, w1p, vec(b1), vec(f1), w2p, vec(b2), vec(f2), w3p, deltas)


def _stack_complex(m):
    return np.block([[m.real, -m.imag], [m.imag, m.real]])


def _hilo(m):
    m = jnp.asarray(m, dtype=F32)
    return _split(m)


@functools.lru_cache(maxsize=None)
def _fft_constants(n):
    m = 2 * n
    n2 = FFT_N2
    n1 = m // n2
    n1h = n1 // 2
    k1 = np.arange(n1)
    t1 = np.arange(n1h)
    f1 = np.exp(-2j * np.pi * np.outer(k1, t1) / n1)
    f3 = np.exp(2j * np.pi * np.outer(t1, k1) / n1) / m
    k2 = np.arange(n2)
    t2 = np.arange(n2)
    w2 = np.exp(-2j * np.pi * np.outer(k2, t2) / n2)
    tw = np.exp(-2j * np.pi * np.outer(k1, t2) / m)
    return dict(
        n1=n1, n1h=n1h,
        f1c=_stack_complex(f1), f1r=np.concatenate([f1.real, f1.imag], axis=0),
        f3c=_stack_complex(f3),
        w2r=w2.real.astype(np.float32), w2i=w2.imag.astype(np.float32),
        twr=tw.real.astype(np.float32), twi=tw.imag.astype(np.float32),
    )


def _stage2_tables(n):
    c = _fft_constants(n)
    w2r, w2i = jnp.asarray(c["w2r"])[None], jnp.asarray(c["w2i"])[None]
    twr, twi = jnp.asarray(c["twr"])[:, None, :], jnp.asarray(c["twi"])[:, None, :]
    gr = w2r * twr - w2i * twi
    gi = w2r * twi + w2i * twr
    g = jnp.concatenate([jnp.concatenate([gr, -gi], axis=2), jnp.concatenate([gi, gr], axis=2)], axis=1)
    return _split(g)


def _s1_kernel(x_ref, fh_ref, fl_ref, o_ref, *, nparts, ncw, n1, n1h):
    fh = fh_ref[...]
    fl = fl_ref[...]
    for r in range(SUBLANES):
        rows = pl.ds(r, n1h, stride=SUBLANES)
        xs = jnp.concatenate(
            [jnp.concatenate([x_ref[p, cc, rows, :] for p in range(nparts)], axis=0) for cc in range(ncw)], axis=1)
        xh, xl = _split(xs)
        res = _dot3(fh, fl, xh, xl)
        for ri in range(2):
            for cc in range(ncw):
                o_ref[ri, cc, pl.ds(r, n1, stride=SUBLANES), :] = res[ri * n1:(ri + 1) * n1, cc * LANES:(cc + 1) * LANES]


def _fft_stage1(x, fmat, n1, n1h, ncw):
    nparts, ncc, groups = x.shape[:3]
    fh, fl = _hilo(fmat)
    return pl.pallas_call(
        functools.partial(_s1_kernel, nparts=nparts, ncw=ncw, n1=n1, n1h=n1h),
        out_shape=jax.ShapeDtypeStruct((2, ncc, groups, n1 * SUBLANES, LANES), F32),
        grid=(groups, ncc // ncw),
        in_specs=[
            pl.BlockSpec((nparts, ncw, None, n1h * SUBLANES, LANES), lambda j, ci: (0, ci, j, 0, 0)),
            pl.BlockSpec(fh.shape, lambda j, ci: (0, 0)),
            pl.BlockSpec(fl.shape, lambda j, ci: (0, 0)),
        ],
        out_specs=pl.BlockSpec((2, ncw, None, n1 * SUBLANES, LANES), lambda j, ci: (0, ci, j, 0, 0)),
        compiler_params=_params(("parallel", "parallel")),
        name="fft_stage1",
    )(x, fh, fl)


def _load_k1(a_ref, q):
    ncc = a_ref.shape[1]
    return jnp.concatenate([a_ref[:, cc, :, q].reshape(2 * FFT_N2, LANES) for cc in range(ncc)], axis=1)


def _filter_spectrum_kernel(a_ref, gh_ref, gl_ref, s_ref, o_ref, *, kg):
    s = jnp.sum(s_ref[...], axis=0, keepdims=True)
    inv = 1.0 / (s[:, :HYENA_WIDTH] + s[:, HYENA_WIDTH:])
    half = FFT_N2
    for q in range(kg):
        ah, al = _split(_load_k1(a_ref, q))
        h = _dot3(gh_ref[q], gl_ref[q], ah, al)
        hf = h[:, :HYENA_WIDTH]
        hb = h[:, HYENA_WIDTH:]
        o_ref[q, :half] = (hf[:half] + hb[:half]) * inv
        o_ref[q, half:] = (hf[half:] - hb[half:]) * inv


def _k1_spec(kg, width):
    return pl.BlockSpec((2, width // LANES, GROUPS, kg, SUBLANES, LANES), lambda i: (0, 0, 0, i, 0, 0))


def _filter_spectrum(a, gh, gl, sums, n1):
    kg = 4
    a6 = a.reshape(2, 2 * HYENA_WIDTH // LANES, GROUPS, n1, SUBLANES, LANES)
    tspec = pl.BlockSpec((kg, 2 * FFT_N2, 2 * FFT_N2), lambda i: (i, 0, 0))
    return pl.pallas_call(
        functools.partial(_filter_spectrum_kernel, kg=kg),
        out_shape=jax.ShapeDtypeStruct((n1, 2 * FFT_N2, HYENA_WIDTH), F32),
        grid=(n1 // kg,),
        in_specs=[_k1_spec(kg, 2 * HYENA_WIDTH), tspec, tspec,
                  pl.BlockSpec((SUBLANES, 2 * HYENA_WIDTH), lambda i: (0, 0))],
        out_specs=pl.BlockSpec((kg, 2 * FFT_N2, HYENA_WIDTH), lambda i: (i, 0, 0)),
        compiler_params=_params(("parallel",)),
        name="filter_spectrum",
    )(a6, gh, gl, sums)


def _dot_t(a, b):
    return lax.dot_general(a, b, (((0,), (0,)), ((), ())), preferred_element_type=F32)


def _s2_kernel(a_ref, kf_ref, gh_ref, gl_ref, o_ref, *, kg):
    half = FFT_N2
    for q in range(kg):
        ah, al = _split(_load_k1(a_ref, q))
        gh, gl = gh_ref[q], gl_ref[q]
        x = _dot3(gh, gl, ah, al)
        xr, xi = x[:half], x[half:]
        kr, ki = kf_ref[q, :half], kf_ref[q, half:]
        y = jnp.concatenate([xr * kr - xi * ki, xr * ki + xi * kr], axis=0)
        yh, yl = _split(y)
        bt = _dot_t(gh, yh) + _dot_t(gh, yl) + _dot_t(gl, yh)
        for cc in range(HYENA_WIDTH // LANES):
            o_ref[:, cc, :, q] = bt[:, cc * LANES:(cc + 1) * LANES].reshape(2, GROUPS, SUBLANES, LANES)


def _fft_stage2(a, kf, tables, n1):
    kg = 4
    a6 = a.reshape(2, HYENA_WIDTH // LANES, GROUPS, n1, SUBLANES, LANES)
    tspec = pl.BlockSpec((kg, 2 * FFT_N2, 2 * FFT_N2), lambda i: (i, 0, 0))
    dspec = _k1_spec(kg, HYENA_WIDTH)
    out = pl.pallas_call(
        functools.partial(_s2_kernel, kg=kg),
        out_shape=jax.ShapeDtypeStruct(a6.shape, F32),
        grid=(n1 // kg,),
        in_specs=[dspec, pl.BlockSpec((kg, 2 * FFT_N2, HYENA_WIDTH), lambda i: (i, 0, 0)), tspec, tspec],
        out_specs=dspec,
        compiler_params=_params(("parallel",)),
        name="fft_stage2",
    )(a6, kf, *tables)
    return out.reshape(a.shape)


def _s3_kernel(b_ref, vx_ref, x0_ref, bias_ref, fh_ref, fl_ref, o_ref, *, ncw, n1, n1h):
    fh = fh_ref[...]
    fl = fl_ref[...]
    for r in range(SUBLANES):
        krows = pl.ds(r, n1, stride=SUBLANES)
        z = jnp.concatenate(
            [jnp.concatenate([b_ref[ri, cc, krows, :] for ri in range(2)], axis=0) for cc in range(ncw)], axis=1)
        zh, zl = _split(z)
        y = _dot3(fh, fl, zh, zl)
        trows = pl.ds(r, n1h, stride=SUBLANES)
        for p in range(2):
            for cc in range(ncw):
                yy = y[p * n1h:(p + 1) * n1h, cc * LANES:(cc + 1) * LANES]
                o_ref[p, cc, trows, :] = (yy + vx_ref[p, cc, trows, :] * bias_ref[cc]) * x0_ref[p, cc, trows, :]


def _fft_stage3(bt, vx, x0, bias, fmat, n1, n1h, ncw):
    ncc, groups = bt.shape[1:3]
    fh, fl = _hilo(fmat)
    tspec = pl.BlockSpec((2, ncw, None, n1h * SUBLANES, LANES), lambda j, ci: (0, ci, j, 0, 0))
    return pl.pallas_call(
        functools.partial(_s3_kernel, ncw=ncw, n1=n1, n1h=n1h),
        out_shape=jax.ShapeDtypeStruct(vx.shape, F32),
        grid=(groups, ncc // ncw),
        in_specs=[
            pl.BlockSpec((2, ncw, None, n1 * SUBLANES, LANES), lambda j, ci: (0, ci, j, 0, 0)),
            tspec, tspec,
            pl.BlockSpec((ncw, 1, LANES), lambda j, ci: (ci, 0, 0)),
            pl.BlockSpec(fh.shape, lambda j, ci: (0, 0)),
            pl.BlockSpec(fl.shape, lambda j, ci: (0, 0)),
        ],
        out_specs=tspec,
        compiler_params=_params(("parallel", "parallel")),
        name="fft_stage3",
    )(bt, vx, x0, bias, fh, fl)


def _hyena_latent(p, hy, feats, deltas, tables, n):
    conv_w, conv_b, w1, b1, f1, w2, b2, f2, w3, bias = hy
    c = _fft_constants(n)
    n1, n1h = c["n1"], c["n1h"]
    vx, x0 = _hyena_prep(p, conv_w, conv_b, group_major=True)
    gm_shape = vx.shape
    rows = lambda a: a.reshape(*a.shape[:-3], n1h * SUBLANES, LANES)
    vx, x0 = rows(vx), rows(x0)
    taps, sums = _hyena_filter(n, feats, deltas, w1, b1, f1, w2, b2, f2, w3, group_major=True)
    ncw = 2
    a_f = _fft_stage1(rows(taps)[None], c["f1r"], n1, n1h, ncw)
    kf = _filter_spectrum(a_f, tables[0], tables[1], sums, n1)
    a = _fft_stage1(vx, c["f1c"], n1, n1h, ncw)
    bt = _fft_stage2(a, kf, tables, n1)
    out = _fft_stage3(bt, vx, x0, bias.reshape(HYENA_WIDTH // LANES, 1, LANES), c["f3c"], n1, n1h, ncw)
    return out.reshape(gm_shape)


@functools.lru_cache(maxsize=None)
def _dense_dft_constants(n):
    m = 2 * n
    k = np.arange(m)
    t = np.arange(n)
    f = np.exp(-2j * np.pi * np.outer(k, t) / m)
    finv = np.exp(2j * np.pi * np.outer(t, k) / m) / m
    return _stack_complex(f), np.concatenate([f.real, f.imag], axis=0), _stack_complex(finv)


def _dense_conv_kernel(vx_ref, x0_ref, hf_ref, hb_ref, sf_ref, sb_ref, bias_ref,
                       fch, fcl, frh, frl, fih, fil, o_ref, *, n):
    m = 2 * n
    z = jnp.concatenate([vx_ref[0], vx_ref[1]], axis=0)
    zh, zl = _split(z)
    zf = _dot3(fch[...], fcl[...], zh, zl)
    hfh, hfl = _split(hf_ref[...])
    hbh, hbl = _split(hb_ref[...])
    hf = _dot3(frh[...], frl[...], hfh, hfl)
    hb = _dot3(frh[...], frl[...], hbh, hbl)
    inv = 1.0 / (jnp.sum(sf_ref[...], axis=0, keepdims=True) + jnp.sum(sb_ref[...], axis=0, keepdims=True))
    kr = (hf[:m] + hb[:m]) * inv
    ki = (hf[m:] - hb[m:]) * inv
    zr, zi = zf[:m], zf[m:]
    y = jnp.concatenate([zr * kr - zi * ki, zr * ki + zi * kr], axis=0)
    yh, yl = _split(y)
    out = _dot3(fih[...], fil[...], yh, yl)
    bias = bias_ref[...]
    for p in range(2):
        o_ref[p] = (out[p * n:(p + 1) * n] + vx_ref[p] * bias) * x0_ref[p]


def _hyena_context(p, hy, feats, deltas, n):
    conv_w, conv_b, w1, b1, f1, w2, b2, f2, w3, bias = hy
    vx, x0 = _hyena_prep(p, conv_w, conv_b, group_major=False)
    taps, sums = _hyena_filter(n, feats, deltas, w1, b1, f1, w2, b2, f2, w3, group_major=False)
    fc, fr, fi = _dense_dft_constants(n)
    mats = [*_hilo(fc), *_hilo(fr), *_hilo(fi)]
    cw = 256
    nct = HYENA_WIDTH // cw
    dspec = pl.BlockSpec((2, n, cw), lambda ci: (0, 0, ci))
    return pl.pallas_call(
        functools.partial(_dense_conv_kernel, n=n),
        out_shape=jax.ShapeDtypeStruct(vx.shape, F32),
        grid=(nct,),
        in_specs=[dspec, dspec,
                  pl.BlockSpec((n, cw), lambda ci: (0, ci)), pl.BlockSpec((n, cw), lambda ci: (0, nct + ci)),
                  pl.BlockSpec((SUBLANES, cw), lambda ci: (0, ci)), pl.BlockSpec((SUBLANES, cw), lambda ci: (0, nct + ci)),
                  pl.BlockSpec((1, cw), lambda ci: (0, ci))]
                 + [pl.BlockSpec(mt.shape, lambda ci: (0, 0)) for mt in mats],
        out_specs=dspec,
        compiler_params=_params(("parallel",)),
        name="context_long_conv",
    )(vx, x0, taps, taps, sums, sums, bias.reshape(1, HYENA_WIDTH), *mats)


def _pool_kernel(x_ref, p_ref, n_ref, w_ref, sc_ref, o_ref, *, tl, nt, n):
    i = pl.program_id(1)
    x = x_ref[...]
    pv = jnp.where(i > 0, p_ref[...], 0.0)
    nx = jnp.where(i < nt - 1, n_ref[...], 0.0)
    ext = jnp.concatenate([pv, x, nx], axis=0)
    rows = tl + 2 * POOL_HALO
    t = i * tl + lax.broadcasted_iota(jnp.int32, (tl, POOL_GROUP), 0)
    for g, w in enumerate(POOL_WINDOWS):
        lanes = slice(g * POOL_GROUP, (g + 1) * POOL_GROUP)
        a = ext[:, lanes]
        c = a + pltpu.roll(a, 1, 0)
        h = 1
        while 2 * h < w:
            c = pltpu.roll(c, h, 0) + pltpu.roll(c, rows - h, 0)
            h *= 2
        total = c[POOL_HALO:POOL_HALO + tl]
        count = (jnp.minimum(t + h, n) - jnp.maximum(t - h, 0)).astype(F32)
        y = (total / count - x[:, lanes]).astype(BF16)
        o_ref[:, lanes] = _dot(y, w_ref[g].astype(BF16)) * sc_ref[:, lanes]


def _pool_mixer(p, w_pool, scale):
    b, n, _ = p.shape
    tl = min(512, n)
    nt = n // tl
    colblk = 3 * HYENA_WIDTH // POOL_WIDTH
    assert colblk * POOL_WIDTH == 3 * HYENA_WIDTH
    per = tl // POOL_HALO
    nrow = n // POOL_HALO
    return pl.pallas_call(
        functools.partial(_pool_kernel, tl=tl, nt=nt, n=n),
        out_shape=jax.ShapeDtypeStruct((b, n, POOL_WIDTH), F32),
        grid=(b, nt),
        in_specs=[
            pl.BlockSpec((None, tl, POOL_WIDTH), lambda bi, i: (bi, i, colblk)),
            pl.BlockSpec((None, POOL_HALO, POOL_WIDTH), lambda bi, i: (bi, jnp.maximum(i * per - 1, 0), colblk)),
            pl.BlockSpec((None, POOL_HALO, POOL_WIDTH), lambda bi, i: (bi, jnp.minimum((i + 1) * per, nrow - 1), colblk)),
            pl.BlockSpec(w_pool.shape, lambda bi, i: (0, 0, 0)),
            pl.BlockSpec((1, POOL_WIDTH), lambda bi, i: (0, 0)),
        ],
        out_specs=pl.BlockSpec((None, tl, POOL_WIDTH), lambda bi, i: (bi, i, 0)),
        compiler_params=_params(("parallel", "parallel")),
        name="pool_mixer",
    )(p, p, p, w_pool, scale.reshape(1, POOL_WIDTH))


def _load_group_major(ref, tl):
    ncc = ref.shape[0]
    return jnp.concatenate(
        [jnp.concatenate([ref[cc, jg, t1] for cc in range(ncc)], axis=1)
         for t1 in range(tl // FFT_N2) for jg in range(GROUPS)], axis=0)


def _outproj_kernel(at_ref, hy_ref, po_ref, x_ref, gb_ref, w_ref, gp_ref, gt_ref, o_ref, *, tm, group_major):
    hy = _load_group_major(hy_ref, tm) if group_major else hy_ref[...]
    a0, a1 = ATTN_WIDTH, ATTN_WIDTH + HYENA_WIDTH
    ox = _dot(_rms(at_ref[...], gb_ref[:, :a0]).astype(BF16), w_ref[:a0])
    ox += _dot(_rms(hy, gb_ref[:, a0:a1]).astype(BF16), w_ref[a0:a1])
    ox += _dot(_rms(po_ref[...], gb_ref[:, a1:]).astype(BF16), w_ref[a1:])
    o_ref[...] = x_ref[...] + gt_ref[...] * _rms(ox, gp_ref[...])


def _out_projection(attn, hy, po, x, g_branch, w_out, g_post, gate, group_major):
    b, n, d = x.shape
    tm = min(512, n)
    row = lambda width: pl.BlockSpec((None, tm, width), lambda bi, i: (bi, i, 0))
    vec = lambda width: pl.BlockSpec((1, width), lambda bi, i: (0, 0))
    if group_major:
        hy_spec = pl.BlockSpec((None, *_gm_shape((), tm // FFT_N2, HYENA_WIDTH)), lambda bi, i: (bi, 0, 0, i, 0, 0))
    else:
        hy_spec = row(HYENA_WIDTH)
    return pl.pallas_call(
        functools.partial(_outproj_kernel, tm=tm, group_major=group_major),
        out_shape=jax.ShapeDtypeStruct(x.shape, F32),
        grid=(b, n // tm),
        in_specs=[row(ATTN_WIDTH), hy_spec, row(POOL_WIDTH), row(d), vec(d),
                  pl.BlockSpec(w_out.shape, lambda bi, i: (0, 0)), vec(d),
                  pl.BlockSpec((None, 1, d), lambda bi, i: (bi, 0, 0))],
        out_specs=row(d),
        compiler_params=_params(("parallel", "parallel")),
        name="out_projection",
    )(attn, hy, po, x, g_branch.reshape(1, d), w_out, g_post.reshape(1, d), gate)


def _mlp_kernel(xe_ref, xp_ref, g_ref, sh0_ref, sc0_ref, shn_ref, scn_ref, wu_ref, wd_ref, gp_ref, gtp_ref, o_ref,
                h0_ref, h1_ref, acc0_ref, acc1_ref, *, rc):
    t = pl.program_id(0)
    k = pl.program_id(1)

    @pl.when((t == 0) & (k == 0))
    def _():
        h0_ref[...] = (_rms(xe_ref[...], g_ref[...] * (1.0 + sc0_ref[...])) + sh0_ref[...]).astype(BF16)
        acc0_ref[...] = jnp.zeros_like(acc0_ref)
        acc1_ref[...] = jnp.zeros_like(acc1_ref)

    def step(slot):
        h_cur, h_oth = (h0_ref, h1_ref) if slot == 0 else (h1_ref, h0_ref)
        acc_cur, acc_oth = (acc0_ref, acc1_ref) if slot == 0 else (acc1_ref, acc0_ref)
        rows = pl.ds(pl.multiple_of(k * rc, rc), rc)
        y = acc_oth[rows, :]
        o_ref[rows, :] = xe_ref[rows, :] + _rms(y, gp_ref[...] * gtp_ref[...])
        acc_oth[rows, :] = jnp.zeros_like(y)
        a_next = g_ref[...] * (1.0 + scn_ref[...])
        h_oth[rows, :] = (_rms(xp_ref[rows, :], a_next) + shn_ref[...]).astype(BF16)
        u = jnp.maximum(_dot(h_cur[...], wu_ref[...]), 0.0)
        acc_cur[...] += _dot((u * u).astype(BF16), wd_ref[...])

    pl.when(t % 2 == 0)(functools.partial(step, 0))
    pl.when(t % 2 == 1)(functools.partial(step, 1))


def _mlp(x, g_pre, shift, scale, w_up, w_down, g_post, gate):
    b, n, d = x.shape
    hidden = w_up.shape[1]
    tm = min(512, n)
    th = 512
    nt = n // tm
    tiles = b * nt
    nk = hidden // th
    rc = tm // nk
    assert rc * nk == tm and rc % BF16_ROWS == 0

    def prev(t):
        return jnp.maximum(t - 1, 0)

    def nxt(t):
        return jnp.minimum(t + 1, tiles - 1)

    vec = pl.BlockSpec((1, d), lambda t, k: (0, 0))
    bvec = lambda sel: pl.BlockSpec((None, 1, d), lambda t, k: (sel(t) // nt, 0, 0))
    row = lambda sel: pl.BlockSpec((None, tm, d), lambda t, k: (sel(t) // nt, sel(t) % nt, 0))
    first = lambda t: 0 * t
    return pl.pallas_call(
        functools.partial(_mlp_kernel, rc=rc),
        out_shape=jax.ShapeDtypeStruct(x.shape, F32),
        grid=(tiles + 1, nk),
        in_specs=[row(prev), row(nxt), vec, bvec(first), bvec(first), bvec(nxt), bvec(nxt),
                  pl.BlockSpec((d, th), lambda t, k: (0, k)),
                  pl.BlockSpec((th, d), lambda t, k: (k, 0)),
                  vec, bvec(prev)],
        out_specs=row(prev),
        scratch_shapes=[pltpu.VMEM((tm, d), BF16)] * 2 + [pltpu.VMEM((tm, d), F32)] * 2,
        compiler_params=_params(("arbitrary", "arbitrary")),
        name="mlp",
    )(x, x, g_pre.reshape(1, d), shift, scale, shift, scale, w_up, w_down, g_post.reshape(1, d), gate)


def kernel(x, c, ctx, c_ctx, w_mod, b_mod, g_pre_mix, g_post_mix, g_pre_mlp, g_post_mlp, w_in, w_out, g_branch,
           attn_sink, hy_conv_w, hy_conv_b, hy_w1, hy_b1, hy_freq1, hy_w2, hy_b2, hy_freq2, hy_w3, hy_bias,
           pool_w, pool_scale, w_up, w_down):
    b, n, d = x.shape
    n_ctx = ctx.shape[1]
    depth = w_mod.shape[0]
    assert b == 2 and d == D_MODEL and n % 512 == 0 and n_ctx % BLOCK == 0

    cond = jnp.concatenate([c, c_ctx[None], jnp.zeros((SUBLANES - b - 1, d), F32)], axis=0)
    mods = _modulation(cond, w_mod, b_mod)

    w_in_b = _permute_rope_columns(w_in.astype(BF16))
    w_out_b, w_up_b, w_down_b = (w.astype(BF16) for w in (w_out, w_up, w_down))
    rope = _rope_tables(n)
    feats_x, feats_c = _filter_features(n), _filter_features(n_ctx)
    deltas = _filter_deltas()
    tables = _stage2_tables(n)

    for i in range(depth):
        last = i == depth - 1
        hy = (hy_conv_w[i], hy_conv_b[i], hy_w1[i], hy_b1[i], hy_freq1[i], hy_w2[i], hy_b2[i], hy_freq2[i],
              hy_w3[i], hy_bias[i])
        mx = [m[:, None, :] for m in jnp.split(mods[i, :b], N_MOD, axis=-1)]
        mc = [jnp.broadcast_to(m[None, None, :], (b, 1, d)) for m in jnp.split(mods[i, b], N_MOD, axis=-1)]

        qkv_x, px = _in_projection(x, g_pre_mix[i], mx[0], mx[1], w_in_b[i], rope)
        qkv_c, pc = _in_projection(ctx, g_pre_mix[i], mc[0], mc[1], w_in_b[i])
        kv_ctx = qkv_c[..., Q_END:V_END]

        attn_x = _attention(qkv_x, kv_ctx, attn_sink[i], local=True)
        hy_x = _hyena_latent(px, hy, feats_x, deltas, tables, n)
        po_x = _pool_mixer(px, pool_w[i], pool_scale[i])
        x = _out_projection(attn_x, hy_x, po_x, x, g_branch[i], w_out_b[i], g_post_mix[i], mx[2], group_major=True)
        x = _mlp(x, g_pre_mlp[i], mx[3], mx[4], w_up_b[i], w_down_b[i], g_post_mlp[i], mx[5])

        if not last:
            attn_c = _attention(qkv_c, kv_ctx, attn_sink[i], local=False)
            hy_c = _hyena_context(pc, hy, feats_c, deltas, n_ctx)
            po_c = _pool_mixer(pc, pool_w[i], pool_scale[i])
            ctx = _out_projection(attn_c, hy_c, po_c, ctx, g_branch[i], w_out_b[i], g_post_mix[i], mc[2],
                                  group_major=False)
            ctx = _mlp(ctx, g_pre_mlp[i], mc[3], mc[4], w_up_b[i], w_down_b[i], g_post_mlp[i], mc[5])
    return x
```

```python
import functools
import math

import numpy as np
import jax
import jax.numpy as jnp
from jax import lax
from jax.experimental import pallas as pl
from jax.experimental.pallas import tpu as pltpu

F32 = jnp.float32
BF16 = jnp.bfloat16

D_MODEL = 2048
DEPTH = 4
GRID_W = 64
ATTN_WIDTH = D_MODEL // 2
HYENA_WIDTH = D_MODEL // 4
POOL_WIDTH = D_MODEL - ATTN_WIDTH - HYENA_WIDTH
HEAD_DIM = 128
N_HEADS = ATTN_WIDTH // HEAD_DIM
N_KV_HEADS = 2
KV_GROUP = N_HEADS // N_KV_HEADS
KV_WIDTH = N_KV_HEADS * HEAD_DIM
WINDOW = 128
BLOCK = 128
ROPE_BASE = 10000.0
HYENA_EMB_DIM = 33
HYENA_FILTER_HIDDEN = 64
HYENA_FAST_DECAY_PCT = 0.3
HYENA_SLOW_DECAY_PCT = 1.5
HYENA_DECAY_TARGET = 1e-2
POOL_WINDOWS = (2, 4, 8, 16)
POOL_GROUP = POOL_WIDTH // len(POOL_WINDOWS)
MLP_HIDDEN = 4 * D_MODEL
N_MOD = 6
EPS = 1e-6
NEG_INF = -1e30

Q_END = ATTN_WIDTH
K_END = Q_END + KV_WIDTH
V_END = K_END + KV_WIDTH
HY_END = V_END + 3 * HYENA_WIDTH
IN_WIDTH = HY_END + POOL_WIDTH

LANES = 128
SUBLANES = 8
FFT_N2 = 128
POOL_HALO = 16
VMEM_LIMIT = 56 * 1024 * 1024


def _params(sem, vmem=VMEM_LIMIT):
    return pltpu.CompilerParams(dimension_semantics=sem, vmem_limit_bytes=vmem)


def _split(x):
    hi = x.astype(BF16)
    lo = (x - hi.astype(F32)).astype(BF16)
    return hi, lo


def _dot(a, b):
    return jnp.dot(a, b, preferred_element_type=F32)


def _dot3(ah, al, bh, bl):
    return _dot(ah, bh) + _dot(ah, bl) + _dot(al, bh)


def _rms(x, g):
    return x * lax.rsqrt(jnp.mean(x * x, axis=-1, keepdims=True) + EPS) * g


BF16_ROWS = 2 * SUBLANES


def _norm_scale_rows(x_ref, h_ref, a_ref, s_ref, rows, cols=None):
    cols = slice(None) if cols is None else cols
    nchunks = rows // BF16_ROWS

    def chunk(c):
        return pl.ds(pl.multiple_of(c * BF16_ROWS, BF16_ROWS), BF16_ROWS)

    def inv_rms(c):
        x = x_ref[chunk(c), :]
        return lax.rsqrt(jnp.mean(x * x, axis=-1, keepdims=True) + EPS)

    def body(c, inv):
        inv_next = inv_rms(jnp.minimum(c + 1, nchunks - 1))
        y = x_ref[chunk(c), :] * inv * a_ref[...]
        if s_ref is not None:
            y = y + s_ref[...]
        h_ref[chunk(c), cols] = y.astype(BF16)
        return inv_next

    lax.fori_loop(0, nchunks, body, inv_rms(0), unroll=8)


def _residual_norm_rows(x_ref, y_ref, o_ref, pg_ref, rows):
    nchunks = rows // SUBLANES

    def chunk(c):
        return pl.ds(pl.multiple_of(c * SUBLANES, SUBLANES), SUBLANES)

    def inv_rms(c):
        y = y_ref[chunk(c), :]
        return lax.rsqrt(jnp.mean(y * y, axis=-1, keepdims=True) + EPS)

    def body(c, inv):
        inv_next = inv_rms(jnp.minimum(c + 1, nchunks - 1))
        o_ref[chunk(c), :] = x_ref[chunk(c), :] + y_ref[chunk(c), :] * inv * pg_ref[...]
        return inv_next

    lax.fori_loop(0, nchunks, body, inv_rms(0), unroll=16)


def _mod_kernel(c_ref, w_ref, b_ref, o_ref):
    c = c_ref[...]
    s = c / (1.0 + jnp.exp(-c))
    sh, sl = _split(s)
    wh, wl = _split(w_ref[...])
    o_ref[...] = _dot3(sh, sl, wh, wl) + b_ref[...]


def _modulation(cond, w_mod, b_mod):
    depth, d, width = w_mod.shape
    tn = 1024
    return pl.pallas_call(
        _mod_kernel,
        out_shape=jax.ShapeDtypeStruct((depth, SUBLANES, width), F32),
        grid=(depth, width // tn),
        in_specs=[
            pl.BlockSpec((SUBLANES, d), lambda l, j: (0, 0)),
            pl.BlockSpec((None, d, tn), lambda l, j: (l, 0, j)),
            pl.BlockSpec((None, 1, tn), lambda l, j: (l, 0, j)),
        ],
        out_specs=pl.BlockSpec((None, SUBLANES, tn), lambda l, j: (l, 0, j)),
        compiler_params=_params(("parallel", "parallel")),
        name="modulation",
    )(cond, w_mod, b_mod.reshape(depth, 1, width))


QKV_WIDTH = V_END
REST_WIDTH = IN_WIDTH - V_END
SM_SCALE = HEAD_DIM ** -0.5


def _inproj_kernel(*refs, tm, tn, rope):
    if rope:
        x_ref, g_ref, sh_ref, sc_ref, w_ref, cq_ref, sq_ref, ck_ref, sk_ref, qkv_ref, rest_ref, h_ref, a_scr = refs
    else:
        x_ref, g_ref, sh_ref, sc_ref, w_ref, qkv_ref, rest_ref, h_ref, a_scr = refs
    j = pl.program_id(2)

    @pl.when(j == 0)
    def _():
        a_scr[...] = g_ref[...] * (1.0 + sc_ref[...])
        _norm_scale_rows(x_ref, h_ref, a_scr, sh_ref, tm)

    acc = _dot(h_ref[...], w_ref[...])
    nch = tn // LANES

    def write_qkv(t):
        for ch in range(nch):
            col = (t * nch + ch) * LANES
            a = acc[:, ch * LANES:(ch + 1) * LANES]
            if col < K_END and rope:
                cos, sin = (cq_ref, sq_ref) if col < Q_END else (ck_ref, sk_ref)
                a = a * cos[...] + pltpu.roll(a, HEAD_DIM // 2, 1) * sin[...]
            elif col < Q_END:
                a = a * SM_SCALE
            qkv_ref[:, ch * LANES:(ch + 1) * LANES] = a.astype(BF16)

    for t in range(QKV_WIDTH // tn):
        pl.when(j == t)(functools.partial(write_qkv, t))

    @pl.when(j >= QKV_WIDTH // tn)
    def _():
        rest_ref[...] = acc


def _column_tiles(w, tn):
    *lead, d, width = w.shape
    return jnp.swapaxes(w.reshape(*lead, d, width // tn, tn), -3, -2)


IN_TN = 512


def _in_projection(x, g, shift, scale, w, rope_tables=None):
    b, n, d = x.shape
    tm = min(1024, n)
    tn = w.shape[-1]
    nqkv = QKV_WIDTH // tn
    assert nqkv * tn == QKV_WIDTH and w.shape[0] * tn == IN_WIDTH
    rope = rope_tables is not None
    in_specs = [
        pl.BlockSpec((None, tm, d), lambda bi, i, j: (bi, i, 0)),
        pl.BlockSpec((1, d), lambda bi, i, j: (0, 0)),
        pl.BlockSpec((None, 1, d), lambda bi, i, j: (bi, 0, 0)),
        pl.BlockSpec((None, 1, d), lambda bi, i, j: (bi, 0, 0)),
        pl.BlockSpec((None, d, tn), lambda bi, i, j: (j, 0, 0)),
    ]
    args = [x, g.reshape(1, d), shift, scale, w]
    if rope:
        in_specs += [pl.BlockSpec((tm, LANES), lambda bi, i, j: (i, 0))] * 4
        args += list(rope_tables)
    return pl.pallas_call(
        functools.partial(_inproj_kernel, tm=tm, tn=tn, rope=rope),
        out_shape=[jax.ShapeDtypeStruct((b, n, QKV_WIDTH), BF16), jax.ShapeDtypeStruct((b, n, REST_WIDTH), F32)],
        grid=(b, n // tm, IN_WIDTH // tn),
        in_specs=in_specs,
        out_specs=[pl.BlockSpec((None, tm, tn), lambda bi, i, j: (bi, i, jnp.minimum(j, nqkv - 1))),
                   pl.BlockSpec((None, tm, tn), lambda bi, i, j: (bi, i, jnp.maximum(j - nqkv, 0)))],
        scratch_shapes=[pltpu.VMEM((tm, d), BF16), pltpu.VMEM((1, d), F32)],
        compiler_params=_params(("parallel", "parallel", "arbitrary")),
        name="in_projection_rope" if rope else "in_projection",
    )(*args)


def _permute_rope_columns(w_in):
    lead = w_in.shape[:-1]
    quarter = HEAD_DIM // 4
    qk = w_in[..., :K_END].reshape(*lead, K_END // HEAD_DIM, 2, 2, quarter)
    qk = jnp.swapaxes(qk, -3, -2).reshape(*lead, K_END)
    return jnp.concatenate([qk, w_in[..., K_END:]], axis=-1)


def _rope_tables(n):
    quarter = HEAD_DIM // 4
    inv_freq = ROPE_BASE ** (-jnp.arange(quarter, dtype=F32) / quarter)
    t = jnp.arange(n, dtype=jnp.int32)
    row = (t // GRID_W).astype(F32)[:, None] * inv_freq[None, :]
    col = (t % GRID_W).astype(F32)[:, None] * inv_freq[None, :]
    cos = jnp.concatenate([jnp.cos(row), jnp.cos(col), jnp.cos(row), jnp.cos(col)], axis=-1)
    sin = jnp.concatenate([-jnp.sin(row), -jnp.sin(col), jnp.sin(row), jnp.sin(col)], axis=-1)
    return cos * SM_SCALE, sin * SM_SCALE, cos, sin


def _softmax_pv(parts, sink_col):
    m = sink_col
    for s, _ in parts:
        m = jnp.maximum(m, jnp.max(s, axis=-1, keepdims=True))
    den = jnp.exp(sink_col - m)
    out = None
    for s, v in parts:
        p = jnp.exp(s - m)
        den = den + jnp.sum(p, axis=-1, keepdims=True)
        pv = _dot(p.astype(BF16), v)
        out = pv if out is None else out + pv
    return out / den


def _ctx_attn_kernel(sink_ref, q_ref, kvc_ref, o_ref, *, tq):
    for g in range(N_KV_HEADS):
        heads = [g * KV_GROUP + h for h in range(KV_GROUP)]
        qg = jnp.concatenate([q_ref[:, h * HEAD_DIM:(h + 1) * HEAD_DIM] for h in heads], axis=0)
        sink_col = jnp.concatenate([jnp.full((tq, 1), sink_ref[h], F32) for h in heads], axis=0)
        kc = kvc_ref[:, g * HEAD_DIM:(g + 1) * HEAD_DIM]
        vc = kvc_ref[:, KV_WIDTH + g * HEAD_DIM:KV_WIDTH + (g + 1) * HEAD_DIM]
        s = lax.dot_general(qg, kc, (((1,), (1,)), ((), ())), preferred_element_type=F32)
        o = _softmax_pv([(s, vc)], sink_col)
        for hi, h in enumerate(heads):
            o_ref[:, h * HEAD_DIM:(h + 1) * HEAD_DIM] = o[hi * tq:(hi + 1) * tq]


ATTN_ROWS = KV_GROUP * BLOCK
SOFTMAX_CHUNK = 32


def _win_attn_kernel(sink_ref, q_ref, km_ref, kp_ref, kn_ref, vm_ref, vp_ref, vn_ref, kvc_ref, band_ref,
                     o_ref, ktw, vw, s_scr, p_scr, m_scr, *, tq, nb, nctx):
    i = pl.program_id(1)
    nsub = tq // BLOCK
    nloc = 3 * BLOCK

    def transposed(x):
        return x.astype(F32).T.astype(BF16)

    def block_rows(main_ref, prev_ref, next_ref, w, lanes):
        if w == 0:
            return prev_ref[:, lanes]
        if w == nsub + 1:
            return next_ref[:, lanes]
        return main_ref[(w - 1) * BLOCK:w * BLOCK, lanes]

    ones = jnp.ones((nloc + nctx, HEAD_DIM), BF16)
    for g in range(N_KV_HEADS):
        lanes = slice(g * HEAD_DIM, (g + 1) * HEAD_DIM)
        vlanes = slice(KV_WIDTH + g * HEAD_DIM, KV_WIDTH + (g + 1) * HEAD_DIM)
        kts = [transposed(block_rows(km_ref, kp_ref, kn_ref, w, lanes)) for w in range(nsub + 2)]
        kct = [transposed(kvc_ref[cb * BLOCK:(cb + 1) * BLOCK, lanes]) for cb in range(nctx // BLOCK)]
        for jb in range(nsub):
            for w in range(3):
                ktw[jb, g, :, w * BLOCK:(w + 1) * BLOCK] = kts[jb + w]
                vw[jb, g, w * BLOCK:(w + 1) * BLOCK, :HEAD_DIM] = block_rows(vm_ref, vp_ref, vn_ref, jb + w, lanes)
            for cb in range(nctx // BLOCK):
                ktw[jb, g, :, nloc + cb * BLOCK:nloc + (cb + 1) * BLOCK] = kct[cb]
            vw[jb, g, nloc:, :HEAD_DIM] = kvc_ref[:, vlanes]
            vw[jb, g, :, HEAD_DIM:] = ones

    col = lax.broadcasted_iota(jnp.int32, (1, nloc), 1)

    def rows_of(jb):
        start = jb * BLOCK
        return pl.ds(start if isinstance(start, int) else pl.multiple_of(start, BLOCK), BLOCK)

    def stage_a(jb, g):
        qg = jnp.concatenate([q_ref[rows_of(jb), (g * KV_GROUP + h) * HEAD_DIM:(g * KV_GROUP + h + 1) * HEAD_DIM]
                              for h in range(KV_GROUP)], axis=0)
        s_scr[g] = _dot(qg, ktw[jb, g])

    def stage_b(jb, g):
        blk = i * nsub + jb
        pen_prev = jnp.where(blk == 0, NEG_INF, 0.0).astype(F32)
        pen_next = jnp.where(blk == nb - 1, NEG_INF, 0.0).astype(F32)
        rowbias = jnp.where(col < BLOCK, pen_prev, jnp.where(col >= 2 * BLOCK, pen_next, 0.0))
        for c in range(ATTN_ROWS // SOFTMAX_CHUNK):
            rows = slice(c * SOFTMAX_CHUNK, (c + 1) * SOFTMAX_CHUNK)
            sink = sink_ref[g * KV_GROUP + (c * SOFTMAX_CHUNK) // BLOCK]
            s_loc = s_scr[g, rows, :nloc] + band_ref[rows, :] + rowbias
            s_ctx = s_scr[g, rows, nloc:]
            m = jnp.maximum(jnp.max(s_loc, axis=-1, keepdims=True), jnp.max(s_ctx, axis=-1, keepdims=True))
            m = jnp.maximum(m, sink)
            p_scr[g, rows, :nloc] = jnp.exp(s_loc - m).astype(BF16)
            p_scr[g, rows, nloc:] = jnp.exp(s_ctx - m).astype(BF16)
            m_scr[g, rows, :] = m

    def stage_c(jb, g):
        o = _dot(p_scr[g], vw[jb, g])
        for hi in range(KV_GROUP):
            h = g * KV_GROUP + hi
            rows = slice(hi * BLOCK, (hi + 1) * BLOCK)
            den = o[rows, HEAD_DIM:HEAD_DIM + 1] + jnp.exp(sink_ref[h] - m_scr[g, rows, :])
            o_ref[rows_of(jb), h * HEAD_DIM:(h + 1) * HEAD_DIM] = o[rows, :HEAD_DIM] / den

    stage_a(0, 0)
    stage_a(0, 1)
    stage_b(0, 0)

    def body(j, carry):
        stage_a(j, 0)
        stage_c(j - 1, 0)
        stage_b(j - 1, 1)
        stage_a(j, 1)
        stage_c(j - 1, 1)
        stage_b(j, 0)
        return carry

    lax.fori_loop(1, nsub, body, 0)
    stage_c(nsub - 1, 0)
    stage_b(nsub - 1, 1)
    stage_c(nsub - 1, 1)


def _band_bias():
    qi = np.arange(ATTN_ROWS)[:, None] % BLOCK
    sj = np.arange(3 * BLOCK)[None, :]
    return jnp.asarray(np.where(np.abs(sj - BLOCK - qi) <= WINDOW, 0.0, NEG_INF), dtype=F32)


def _attention(qkv, kv_ctx, sink, local):
    b, n, _ = qkv.shape
    c = kv_ctx.shape[1]
    tq = min(1024 if local else 512, n)
    nsub = tq // BLOCK
    nb = n // BLOCK
    kcol = Q_END // KV_WIDTH
    vcol = K_END // KV_WIDTH
    smem = pl.BlockSpec(memory_space=pltpu.SMEM)
    q_spec = pl.BlockSpec((None, tq, ATTN_WIDTH), lambda bi, i: (bi, i, 0))
    kvc_spec = pl.BlockSpec((None, c, 2 * KV_WIDTH), lambda bi, i: (bi, 0, 0))
    if local:
        def main(colblk):
            return pl.BlockSpec((None, tq, KV_WIDTH), lambda bi, i: (bi, i, colblk))

        def prev(colblk):
            return pl.BlockSpec((None, BLOCK, KV_WIDTH), lambda bi, i: (bi, jnp.maximum(i * nsub - 1, 0), colblk))

        def nxt(colblk):
            return pl.BlockSpec((None, BLOCK, KV_WIDTH), lambda bi, i: (bi, jnp.minimum((i + 1) * nsub, nb - 1), colblk))

        keys = 3 * BLOCK + c
        kern = functools.partial(_win_attn_kernel, tq=tq, nb=nb, nctx=c)
        in_specs = [smem, q_spec, main(kcol), prev(kcol), nxt(kcol), main(vcol), prev(vcol), nxt(vcol), kvc_spec,
                    pl.BlockSpec((ATTN_ROWS, 3 * BLOCK), lambda bi, i: (0, 0))]
        args = [sink, qkv, qkv, qkv, qkv, qkv, qkv, qkv, kv_ctx, _band_bias()]
        scratch = [pltpu.VMEM((nsub, N_KV_HEADS, HEAD_DIM, keys), BF16),
                   pltpu.VMEM((nsub, N_KV_HEADS, keys, 2 * HEAD_DIM), BF16),
                   pltpu.VMEM((2, ATTN_ROWS, keys), F32),
                   pltpu.VMEM((2, ATTN_ROWS, keys), BF16),
                   pltpu.VMEM((2, ATTN_ROWS, 1), F32)]
    else:
        kern = functools.partial(_ctx_attn_kernel, tq=tq)
        in_specs = [smem, q_spec, kvc_spec]
        args = [sink, qkv, kv_ctx]
        scratch = []
    return pl.pallas_call(
        kern,
        out_shape=jax.ShapeDtypeStruct((b, n, ATTN_WIDTH), F32),
        grid=(b, n // tq),
        in_specs=in_specs,
        out_specs=pl.BlockSpec((None, tq, ATTN_WIDTH), lambda bi, i: (bi, i, 0)),
        scratch_shapes=scratch,
        compiler_params=_params(("parallel", "parallel")),
        name="window_attention" if local else "context_attention",
    )(*args)


GROUPS = FFT_N2 // SUBLANES


def _gm_shape(lead, t1, width):
    return (*lead, width // LANES, GROUPS, t1, SUBLANES, LANES)


def _store_group_major(o_ref, val, tl):
    for t1 in range(tl // FFT_N2):
        for jg in range(GROUPS):
            r0 = (t1 * GROUPS + jg) * SUBLANES
            for cc in range(val.shape[1] // LANES):
                o_ref[cc, jg, t1] = val[r0:r0 + SUBLANES, cc * LANES:(cc + 1) * LANES]


def _hyena_prep_kernel(u_ref, p_ref, n_ref, w_ref, b_ref, vx_ref, x0_ref, *, tl, nt, group_major):
    i = pl.program_id(1)
    u = u_ref[...]
    prev_row = jnp.where(i > 0, p_ref[SUBLANES - 1:SUBLANES, :], 0.0)
    next_row = jnp.where(i < nt - 1, n_ref[0:1, :], 0.0)
    row = lax.broadcasted_iota(jnp.int32, u.shape, 0)
    um = jnp.where(row == 0, prev_row, pltpu.roll(u, 1, 0))
    up = jnp.where(row == tl - 1, next_row, pltpu.roll(u, tl - 1, 0))
    z = um * w_ref[0:1, :] + u * w_ref[1:2, :] + up * w_ref[2:3, :] + b_ref[...]
    x0 = z[:, :HYENA_WIDTH]
    vx = z[:, 2 * HYENA_WIDTH:] * z[:, HYENA_WIDTH:2 * HYENA_WIDTH]
    if group_major:
        _store_group_major(vx_ref, vx, tl)
        _store_group_major(x0_ref, x0, tl)
    else:
        vx_ref[...] = vx
        x0_ref[...] = x0


def _hyena_prep(p, conv_w, conv_b, group_major):
    b, n, _ = p.shape
    tl = min(512, n)
    nt = n // tl
    hw = 3 * HYENA_WIDTH
    colblk = 0
    nrow8 = n // SUBLANES
    per = tl // SUBLANES
    if group_major:
        shape = _gm_shape((b,), n // FFT_N2, HYENA_WIDTH)
        out_spec = pl.BlockSpec((None, *_gm_shape((), tl // FFT_N2, HYENA_WIDTH)), lambda bi, i: (bi, 0, 0, i, 0, 0))
    else:
        shape = (b, n, HYENA_WIDTH)
        out_spec = pl.BlockSpec((None, tl, HYENA_WIDTH), lambda bi, i: (bi, i, 0))
    return pl.pallas_call(
        functools.partial(_hyena_prep_kernel, tl=tl, nt=nt, group_major=group_major),
        out_shape=[jax.ShapeDtypeStruct(shape, F32)] * 2,
        grid=(b, nt),
        in_specs=[
            pl.BlockSpec((None, tl, hw), lambda bi, i: (bi, i, colblk)),
            pl.BlockSpec((None, SUBLANES, hw), lambda bi, i: (bi, jnp.maximum(i * per - 1, 0), colblk)),
            pl.BlockSpec((None, SUBLANES, hw), lambda bi, i: (bi, jnp.minimum((i + 1) * per, nrow8 - 1), colblk)),
            pl.BlockSpec((3, hw), lambda bi, i: (0, 0)),
            pl.BlockSpec((1, hw), lambda bi, i: (0, 0)),
        ],
        out_specs=[out_spec, out_spec],
        compiler_params=_params(("parallel", "parallel")),
        name="hyena_prep",
    )(p, p, p, conv_w, conv_b.reshape(1, hw))


def _filter_kernel(ft_ref, w1_ref, b1_ref, f1_ref, w2_ref, b2_ref, f2_ref, w3_ref, dl_ref, h_ref, s_ref,
                   *, tl, n, group_major):
    i = pl.program_id(0)

    def dense(a, w_ref):
        ah, al = _split(a)
        wh, wl = _split(w_ref[...])
        return _dot3(ah, al, wh, wl)

    h = jnp.sin(f1_ref[...] * (dense(ft_ref[...], w1_ref) + b1_ref[...]))
    h = jnp.sin(f2_ref[...] * (dense(h, w2_ref) + b2_ref[...]))
    h = dense(h, w3_ref)
    t = (i * tl + lax.broadcasted_iota(jnp.int32, (tl, HYENA_WIDTH), 0)).astype(F32) / float(n - 1)
    decay = jnp.exp(-t * dl_ref[...])
    h = h * jnp.concatenate([decay, decay], axis=1)
    if group_major:
        _store_group_major(h_ref, h, tl)
    else:
        h_ref[...] = h

    @pl.when(i == 0)
    def _():
        s_ref[...] = jnp.zeros_like(s_ref)

    s_ref[...] += jnp.sum(jnp.abs(h).reshape(tl // SUBLANES, SUBLANES, 2 * HYENA_WIDTH), axis=0)


def _filter_features(n):
    t = jnp.linspace(0.0, 1.0, n, dtype=F32)[:, None]
    bands = (HYENA_EMB_DIM - 1) // 2
    omega = 2.0 * math.pi * jnp.arange(n, dtype=F32)[:, None] / n
    f = jnp.linspace(1e-4, bands - 1, bands, dtype=F32)[None, :]
    feats = jnp.concatenate([t, jnp.cos(f * omega), -jnp.sin(f * omega)], axis=-1)
    return jnp.pad(feats, ((0, 0), (0, LANES - HYENA_EMB_DIM)))


def _filter_deltas():
    max_decay = math.log(HYENA_DECAY_TARGET) / HYENA_FAST_DECAY_PCT
    min_decay = math.log(HYENA_DECAY_TARGET) / HYENA_SLOW_DECAY_PCT
    return jnp.abs(jnp.linspace(min_decay, max_decay, HYENA_WIDTH, dtype=F32)).reshape(1, HYENA_WIDTH)


def _hyena_filter(n, feats, deltas, w1, b1, f1, w2, b2, f2, w3, group_major):
    tl = min(512, n)
    hid = LANES
    pad_h = hid - HYENA_FILTER_HIDDEN
    w1p = jnp.pad(w1, ((0, LANES - HYENA_EMB_DIM), (0, pad_h)))
    w2p = jnp.pad(w2, ((0, pad_h), (0, pad_h)))
    w3p = jnp.pad(w3, ((0, pad_h), (0, 0)))
    vec = lambda v: jnp.pad(v, (0, pad_h)).reshape(1, hid)
    hw2 = 2 * HYENA_WIDTH
    full = lambda shape: pl.BlockSpec(shape, lambda i: (0,) * len(shape))
    if group_major:
        shape = _gm_shape((), n // FFT_N2, hw2)
        out_spec = pl.BlockSpec(_gm_shape((), tl // FFT_N2, hw2), lambda i: (0, 0, i, 0, 0))
    else:
        shape = (n, hw2)
        out_spec = pl.BlockSpec((tl, hw2), lambda i: (i, 0))
    return pl.pallas_call(
        functools.partial(_filter_kernel, tl=tl, n=n, group_major=group_major),
        out_shape=[jax.ShapeDtypeStruct(shape, F32), jax.ShapeDtypeStruct((SUBLANES, hw2), F32)],
        grid=(n // tl,),
        in_specs=[pl.BlockSpec((tl, LANES), lambda i: (i, 0)), full((LANES, hid)), full((1, hid)), full((1, hid)),
                  full((hid, hid)), full((1, hid)), full((1, hid)), full((hid, hw2)), full((1, HYENA_WIDTH))],
        out_specs=[out_spec, full((SUBLANES, hw2))],
        compiler_params=_params(("arbitrary",)),
        name="hyena_filter",
    )(feats, w1p, vec(b1), vec(f1), w2p, vec(b2), vec(f2), w3p, deltas)


def _stack_complex(m):
    return np.block([[m.real, -m.imag], [m.imag, m.real]])


def _hilo(m):
    m = jnp.asarray(m, dtype=F32)
    return _split(m)


@functools.lru_cache(maxsize=None)
def _fft_constants(n):
    m = 2 * n
    n2 = FFT_N2
    n1 = m // n2
    n1h = n1 // 2
    k1 = np.arange(n1)
    t1 = np.arange(n1h)
    f1 = np.exp(-2j * np.pi * np.outer(k1, t1) / n1)
    f3 = np.exp(2j * np.pi * np.outer(t1, k1) / n1) / m
    k2 = np.arange(n2)
    t2 = np.arange(n2)
    w2 = np.exp(-2j * np.pi * np.outer(k2, t2) / n2)
    tw = np.exp(-2j * np.pi * np.outer(k1, t2) / m)
    return dict(
        n1=n1, n1h=n1h,
        f1c=_stack_complex(f1), f1r=np.concatenate([f1.real, f1.imag], axis=0),
        f3c=_stack_complex(f3),
        w2r=w2.real.astype(np.float32), w2i=w2.imag.astype(np.float32),
        twr=tw.real.astype(np.float32), twi=tw.imag.astype(np.float32),
    )


def _stage2_tables(n):
    c = _fft_constants(n)
    w2r, w2i = jnp.asarray(c["w2r"])[None], jnp.asarray(c["w2i"])[None]
    twr, twi = jnp.asarray(c["twr"])[:, None, :], jnp.asarray(c["twi"])[:, None, :]
    gr = w2r * twr - w2i * twi
    gi = w2r * twi + w2i * twr
    g = jnp.concatenate([jnp.concatenate([gr, -gi], axis=2), jnp.concatenate([gi, gr], axis=2)], axis=1)
    return _split(g)


def _s1_kernel(x_ref, fh_ref, fl_ref, o_ref, *, nparts, ncw, n1, n1h):
    fh = fh_ref[...]
    fl = fl_ref[...]
    for r in range(SUBLANES):
        rows = pl.ds(r, n1h, stride=SUBLANES)
        xs = jnp.concatenate(
            [jnp.concatenate([x_ref[p, cc, rows, :] for p in range(nparts)], axis=0) for cc in range(ncw)], axis=1)
        xh, xl = _split(xs)
        res = _dot3(fh, fl, xh, xl)
        for ri in range(2):
            for cc in range(ncw):
                o_ref[ri, cc, pl.ds(r, n1, stride=SUBLANES), :] = res[ri * n1:(ri + 1) * n1, cc * LANES:(cc + 1) * LANES]


def _fft_stage1(x, fmat, n1, n1h, ncw):
    nparts, ncc, groups = x.shape[:3]
    fh, fl = _hilo(fmat)
    return pl.pallas_call(
        functools.partial(_s1_kernel, nparts=nparts, ncw=ncw, n1=n1, n1h=n1h),
        out_shape=jax.ShapeDtypeStruct((2, ncc, groups, n1 * SUBLANES, LANES), F32),
        grid=(groups, ncc // ncw),
        in_specs=[
            pl.BlockSpec((nparts, ncw, None, n1h * SUBLANES, LANES), lambda j, ci: (0, ci, j, 0, 0)),
            pl.BlockSpec(fh.shape, lambda j, ci: (0, 0)),
            pl.BlockSpec(fl.shape, lambda j, ci: (0, 0)),
        ],
        out_specs=pl.BlockSpec((2, ncw, None, n1 * SUBLANES, LANES), lambda j, ci: (0, ci, j, 0, 0)),
        compiler_params=_params(("parallel", "parallel")),
        name="fft_stage1",
    )(x, fh, fl)


def _load_k1(a_ref, q):
    ncc = a_ref.shape[1]
    return jnp.concatenate([a_ref[:, cc, :, q].reshape(2 * FFT_N2, LANES) for cc in range(ncc)], axis=1)


def _filter_spectrum_kernel(a_ref, gh_ref, gl_ref, s_ref, o_ref, *, kg):
    s = jnp.sum(s_ref[...], axis=0, keepdims=True)
    inv = 1.0 / (s[:, :HYENA_WIDTH] + s[:, HYENA_WIDTH:])
    half = FFT_N2
    for q in range(kg):
        ah, al = _split(_load_k1(a_ref, q))
        h = _dot3(gh_ref[q], gl_ref[q], ah, al)
        hf = h[:, :HYENA_WIDTH]
        hb = h[:, HYENA_WIDTH:]
        o_ref[q, :half] = (hf[:half] + hb[:half]) * inv
        o_ref[q, half:] = (hf[half:] - hb[half:]) * inv


def _k1_spec(kg, width):
    return pl.BlockSpec((2, width // LANES, GROUPS, kg, SUBLANES, LANES), lambda i: (0, 0, 0, i, 0, 0))


def _filter_spectrum(a, gh, gl, sums, n1):
    kg = 4
    a6 = a.reshape(2, 2 * HYENA_WIDTH // LANES, GROUPS, n1, SUBLANES, LANES)
    tspec = pl.BlockSpec((kg, 2 * FFT_N2, 2 * FFT_N2), lambda i: (i, 0, 0))
    return pl.pallas_call(
        functools.partial(_filter_spectrum_kernel, kg=kg),
        out_shape=jax.ShapeDtypeStruct((n1, 2 * FFT_N2, HYENA_WIDTH), F32),
        grid=(n1 // kg,),
        in_specs=[_k1_spec(kg, 2 * HYENA_WIDTH), tspec, tspec,
                  pl.BlockSpec((SUBLANES, 2 * HYENA_WIDTH), lambda i: (0, 0))],
        out_specs=pl.BlockSpec((kg, 2 * FFT_N2, HYENA_WIDTH), lambda i: (i, 0, 0)),
        compiler_params=_params(("parallel",)),
        name="filter_spectrum",
    )(a6, gh, gl, sums)


def _dot_t(a, b):
    return lax.dot_general(a, b, (((0,), (0,)), ((), ())), preferred_element_type=F32)


def _s2_kernel(a_ref, kf_ref, gh_ref, gl_ref, o_ref, *, kg):
    half = FFT_N2
    for q in range(kg):
        ah, al = _split(_load_k1(a_ref, q))
        gh, gl = gh_ref[q], gl_ref[q]
        x = _dot3(gh, gl, ah, al)
        xr, xi = x[:half], x[half:]
        kr, ki = kf_ref[q, :half], kf_ref[q, half:]
        y = jnp.concatenate([xr * kr - xi * ki, xr * ki + xi * kr], axis=0)
        yh, yl = _split(y)
        bt = _dot_t(gh, yh) + _dot_t(gh, yl) + _dot_t(gl, yh)
        for cc in range(HYENA_WIDTH // LANES):
            o_ref[:, cc, :, q] = bt[:, cc * LANES:(cc + 1) * LANES].reshape(2, GROUPS, SUBLANES, LANES)


def _fft_stage2(a, kf, tables, n1):
    kg = 4
    a6 = a.reshape(2, HYENA_WIDTH // LANES, GROUPS, n1, SUBLANES, LANES)
    tspec = pl.BlockSpec((kg, 2 * FFT_N2, 2 * FFT_N2), lambda i: (i, 0, 0))
    dspec = _k1_spec(kg, HYENA_WIDTH)
    out = pl.pallas_call(
        functools.partial(_s2_kernel, kg=kg),
        out_shape=jax.ShapeDtypeStruct(a6.shape, F32),
        grid=(n1 // kg,),
        in_specs=[dspec, pl.BlockSpec((kg, 2 * FFT_N2, HYENA_WIDTH), lambda i: (i, 0, 0)), tspec, tspec],
        out_specs=dspec,
        compiler_params=_params(("parallel",)),
        name="fft_stage2",
    )(a6, kf, *tables)
    return out.reshape(a.shape)


def _s3_kernel(b_ref, vx_ref, x0_ref, bias_ref, fh_ref, fl_ref, o_ref, *, ncw, n1, n1h):
    fh = fh_ref[...]
    fl = fl_ref[...]
    for r in range(SUBLANES):
        krows = pl.ds(r, n1, stride=SUBLANES)
        z = jnp.concatenate(
            [jnp.concatenate([b_ref[ri, cc, krows, :] for ri in range(2)], axis=0) for cc in range(ncw)], axis=1)
        zh, zl = _split(z)
        y = _dot3(fh, fl, zh, zl)
        trows = pl.ds(r, n1h, stride=SUBLANES)
        for p in range(2):
            for cc in range(ncw):
                yy = y[p * n1h:(p + 1) * n1h, cc * LANES:(cc + 1) * LANES]
                o_ref[p, cc, trows, :] = (yy + vx_ref[p, cc, trows, :] * bias_ref[cc]) * x0_ref[p, cc, trows, :]


def _fft_stage3(bt, vx, x0, bias, fmat, n1, n1h, ncw):
    ncc, groups = bt.shape[1:3]
    fh, fl = _hilo(fmat)
    tspec = pl.BlockSpec((2, ncw, None, n1h * SUBLANES, LANES), lambda j, ci: (0, ci, j, 0, 0))
    return pl.pallas_call(
        functools.partial(_s3_kernel, ncw=ncw, n1=n1, n1h=n1h),
        out_shape=jax.ShapeDtypeStruct(vx.shape, F32),
        grid=(groups, ncc // ncw),
        in_specs=[
            pl.BlockSpec((2, ncw, None, n1 * SUBLANES, LANES), lambda j, ci: (0, ci, j, 0, 0)),
            tspec, tspec,
            pl.BlockSpec((ncw, 1, LANES), lambda j, ci: (ci, 0, 0)),
            pl.BlockSpec(fh.shape, lambda j, ci: (0, 0)),
            pl.BlockSpec(fl.shape, lambda j, ci: (0, 0)),
        ],
        out_specs=tspec,
        compiler_params=_params(("parallel", "parallel")),
        name="fft_stage3",
    )(bt, vx, x0, bias, fh, fl)


def _hyena_latent(p, hy, feats, deltas, tables, n):
    conv_w, conv_b, w1, b1, f1, w2, b2, f2, w3, bias = hy
    c = _fft_constants(n)
    n1, n1h = c["n1"], c["n1h"]
    vx, x0 = _hyena_prep(p, conv_w, conv_b, group_major=True)
    gm_shape = vx.shape
    rows = lambda a: a.reshape(*a.shape[:-3], n1h * SUBLANES, LANES)
    vx, x0 = rows(vx), rows(x0)
    taps, sums = _hyena_filter(n, feats, deltas, w1, b1, f1, w2, b2, f2, w3, group_major=True)
    ncw = 2
    a_f = _fft_stage1(rows(taps)[None], c["f1r"], n1, n1h, ncw)
    kf = _filter_spectrum(a_f, tables[0], tables[1], sums, n1)
    a = _fft_stage1(vx, c["f1c"], n1, n1h, ncw)
    bt = _fft_stage2(a, kf, tables, n1)
    out = _fft_stage3(bt, vx, x0, bias.reshape(HYENA_WIDTH // LANES, 1, LANES), c["f3c"], n1, n1h, ncw)
    return out.reshape(gm_shape)


@functools.lru_cache(maxsize=None)
def _dense_dft_constants(n):
    m = 2 * n
    k = np.arange(m)
    t = np.arange(n)
    f = np.exp(-2j * np.pi * np.outer(k, t) / m)
    finv = np.exp(2j * np.pi * np.outer(t, k) / m) / m
    return _stack_complex(f), np.concatenate([f.real, f.imag], axis=0), _stack_complex(finv)


def _dense_conv_kernel(vx_ref, x0_ref, hf_ref, hb_ref, sf_ref, sb_ref, bias_ref,
                       fch, fcl, frh, frl, fih, fil, o_ref, *, n):
    m = 2 * n
    z = jnp.concatenate([vx_ref[0], vx_ref[1]], axis=0)
    zh, zl = _split(z)
    zf = _dot3(fch[...], fcl[...], zh, zl)
    hfh, hfl = _split(hf_ref[...])
    hbh, hbl = _split(hb_ref[...])
    hf = _dot3(frh[...], frl[...], hfh, hfl)
    hb = _dot3(frh[...], frl[...], hbh, hbl)
    inv = 1.0 / (jnp.sum(sf_ref[...], axis=0, keepdims=True) + jnp.sum(sb_ref[...], axis=0, keepdims=True))
    kr = (hf[:m] + hb[:m]) * inv
    ki = (hf[m:] - hb[m:]) * inv
    zr, zi = zf[:m], zf[m:]
    y = jnp.concatenate([zr * kr - zi * ki, zr * ki + zi * kr], axis=0)
    yh, yl = _split(y)
    out = _dot3(fih[...], fil[...], yh, yl)
    bias = bias_ref[...]
    for p in range(2):
        o_ref[p] = (out[p * n:(p + 1) * n] + vx_ref[p] * bias) * x0_ref[p]


def _hyena_context(p, hy, feats, deltas, n):
    conv_w, conv_b, w1, b1, f1, w2, b2, f2, w3, bias = hy
    vx, x0 = _hyena_prep(p, conv_w, conv_b, group_major=False)
    taps, sums = _hyena_filter(n, feats, deltas, w1, b1, f1, w2, b2, f2, w3, group_major=False)
    fc, fr, fi = _dense_dft_constants(n)
    mats = [*_hilo(fc), *_hilo(fr), *_hilo(fi)]
    cw = 256
    nct = HYENA_WIDTH // cw
    dspec = pl.BlockSpec((2, n, cw), lambda ci: (0, 0, ci))
    return pl.pallas_call(
        functools.partial(_dense_conv_kernel, n=n),
        out_shape=jax.ShapeDtypeStruct(vx.shape, F32),
        grid=(nct,),
        in_specs=[dspec, dspec,
                  pl.BlockSpec((n, cw), lambda ci: (0, ci)), pl.BlockSpec((n, cw), lambda ci: (0, nct + ci)),
                  pl.BlockSpec((SUBLANES, cw), lambda ci: (0, ci)), pl.BlockSpec((SUBLANES, cw), lambda ci: (0, nct + ci)),
                  pl.BlockSpec((1, cw), lambda ci: (0, ci))]
                 + [pl.BlockSpec(mt.shape, lambda ci: (0, 0)) for mt in mats],
        out_specs=dspec,
        compiler_params=_params(("parallel",)),
        name="context_long_conv",
    )(vx, x0, taps, taps, sums, sums, bias.reshape(1, HYENA_WIDTH), *mats)


def _pool_kernel(x_ref, p_ref, n_ref, w_ref, sc_ref, o_ref, *, tl, nt, n):
    i = pl.program_id(1)
    x = x_ref[...]
    pv = jnp.where(i > 0, p_ref[...], 0.0)
    nx = jnp.where(i < nt - 1, n_ref[...], 0.0)
    ext = jnp.concatenate([pv, x, nx], axis=0)
    rows = tl + 2 * POOL_HALO
    t = i * tl + lax.broadcasted_iota(jnp.int32, (tl, POOL_GROUP), 0)
    for g, w in enumerate(POOL_WINDOWS):
        lanes = slice(g * POOL_GROUP, (g + 1) * POOL_GROUP)
        a = ext[:, lanes]
        c = a + pltpu.roll(a, 1, 0)
        h = 1
        while 2 * h < w:
            c = pltpu.roll(c, h, 0) + pltpu.roll(c, rows - h, 0)
            h *= 2
        total = c[POOL_HALO:POOL_HALO + tl]
        count = (jnp.minimum(t + h, n) - jnp.maximum(t - h, 0)).astype(F32)
        y = (total / count - x[:, lanes]).astype(BF16)
        o_ref[:, lanes] = _dot(y, w_ref[g].astype(BF16)) * sc_ref[:, lanes]


def _pool_mixer(p, w_pool, scale):
    b, n, _ = p.shape
    tl = min(512, n)
    nt = n // tl
    colblk = 3 * HYENA_WIDTH // POOL_WIDTH
    assert colblk * POOL_WIDTH == 3 * HYENA_WIDTH
    per = tl // POOL_HALO
    nrow = n // POOL_HALO
    return pl.pallas_call(
        functools.partial(_pool_kernel, tl=tl, nt=nt, n=n),
        out_shape=jax.ShapeDtypeStruct((b, n, POOL_WIDTH), F32),
        grid=(b, nt),
        in_specs=[
            pl.BlockSpec((None, tl, POOL_WIDTH), lambda bi, i: (bi, i, colblk)),
            pl.BlockSpec((None, POOL_HALO, POOL_WIDTH), lambda bi, i: (bi, jnp.maximum(i * per - 1, 0), colblk)),
            pl.BlockSpec((None, POOL_HALO, POOL_WIDTH), lambda bi, i: (bi, jnp.minimum((i + 1) * per, nrow - 1), colblk)),
            pl.BlockSpec(w_pool.shape, lambda bi, i: (0, 0, 0)),
            pl.BlockSpec((1, POOL_WIDTH), lambda bi, i: (0, 0)),
        ],
        out_specs=pl.BlockSpec((None, tl, POOL_WIDTH), lambda bi, i: (bi, i, 0)),
        compiler_params=_params(("parallel", "parallel")),
        name="pool_mixer",
    )(p, p, p, w_pool, scale.reshape(1, POOL_WIDTH))


def _load_group_major(ref, tl):
    ncc = ref.shape[0]
    return jnp.concatenate(
        [jnp.concatenate([ref[cc, jg, t1] for cc in range(ncc)], axis=1)
         for t1 in range(tl // FFT_N2) for jg in range(GROUPS)], axis=0)


def _outproj_kernel(at_ref, hy_ref, po_ref, x_ref, gb_ref, w_ref, gp_ref, gt_ref, o_ref, *, tm, group_major):
    hy = _load_group_major(hy_ref, tm) if group_major else hy_ref[...]
    a0, a1 = ATTN_WIDTH, ATTN_WIDTH + HYENA_WIDTH
    ox = _dot(_rms(at_ref[...], gb_ref[:, :a0]).astype(BF16), w_ref[:a0])
    ox += _dot(_rms(hy, gb_ref[:, a0:a1]).astype(BF16), w_ref[a0:a1])
    ox += _dot(_rms(po_ref[...], gb_ref[:, a1:]).astype(BF16), w_ref[a1:])
    o_ref[...] = x_ref[...] + gt_ref[...] * _rms(ox, gp_ref[...])


def _out_projection(attn, hy, po, x, g_branch, w_out, g_post, gate, group_major):
    b, n, d = x.shape
    tm = min(512, n)
    row = lambda width: pl.BlockSpec((None, tm, width), lambda bi, i: (bi, i, 0))
    vec = lambda width: pl.BlockSpec((1, width), lambda bi, i: (0, 0))
    if group_major:
        hy_spec = pl.BlockSpec((None, *_gm_shape((), tm // FFT_N2, HYENA_WIDTH)), lambda bi, i: (bi, 0, 0, i, 0, 0))
    else:
        hy_spec = row(HYENA_WIDTH)
    return pl.pallas_call(
        functools.partial(_outproj_kernel, tm=tm, group_major=group_major),
        out_shape=jax.ShapeDtypeStruct(x.shape, F32),
        grid=(b, n // tm),
        in_specs=[row(ATTN_WIDTH), hy_spec, row(POOL_WIDTH), row(d), vec(d),
                  pl.BlockSpec(w_out.shape, lambda bi, i: (0, 0)), vec(d),
                  pl.BlockSpec((None, 1, d), lambda bi, i: (bi, 0, 0))],
        out_specs=row(d),
        compiler_params=_params(("parallel", "parallel")),
        name="out_projection",
    )(attn, hy, po, x, g_branch.reshape(1, d), w_out, g_post.reshape(1, d), gate)


def _mlp_kernel(x0_ref, xe_ref, xp_ref, g_ref, sh0_ref, sc0_ref, shn_ref, scn_ref, wu_ref, wd_ref, gp_ref, gtp_ref, o_ref,
                h0_ref, h1_ref, acc0_ref, acc1_ref, *, rc, tiles):
    t = pl.program_id(0)
    k = pl.program_id(1)

    @pl.when((t == 0) & (k == 0))
    def _():
        h0_ref[...] = (_rms(x0_ref[...], g_ref[...] * (1.0 + sc0_ref[...])) + sh0_ref[...]).astype(BF16)
        acc0_ref[...] = jnp.zeros_like(acc0_ref)
        acc1_ref[...] = jnp.zeros_like(acc1_ref)

    def step(slot, last):
        h_cur, h_oth = (h0_ref, h1_ref) if slot == 0 else (h1_ref, h0_ref)
        acc_cur, acc_oth = (acc0_ref, acc1_ref) if slot == 0 else (acc1_ref, acc0_ref)
        rows = pl.ds(pl.multiple_of(k * rc, rc), rc)
        y = acc_oth[rows, :]
        o_ref[...] = xe_ref[...] + _rms(y, gp_ref[...] * gtp_ref[...])
        if last:
            return
        acc_oth[rows, :] = jnp.zeros_like(y)
        a_next = g_ref[...] * (1.0 + scn_ref[...])
        h_oth[rows, :] = (_rms(xp_ref[...], a_next) + shn_ref[...]).astype(BF16)
        u = jnp.maximum(_dot(h_cur[...], wu_ref[0]), 0.0)
        acc_cur[...] += _dot((u * u).astype(BF16), wd_ref[...])

    pl.when((t % 2 == 0) & (t < tiles))(functools.partial(step, 0, False))
    pl.when((t % 2 == 1) & (t < tiles))(functools.partial(step, 1, False))
    pl.when(t == tiles)(functools.partial(step, tiles % 2, True))


MLP_TH = 512


def _mlp(x, g_pre, shift, scale, w_up_tiles, w_down, g_post, gate):
    b, n, d = x.shape
    nk, _, th = w_up_tiles.shape
    tm = min(1024, n)
    nt = n // tm
    tiles = b * nt
    rc = tm // nk
    assert rc * nk == tm and rc % BF16_ROWS == 0

    def prev(t):
        return jnp.maximum(t - 1, 0)

    def nxt(t):
        return jnp.minimum(t + 1, tiles - 1)

    vec = pl.BlockSpec((1, d), lambda t, k: (0, 0))
    bvec = lambda sel: pl.BlockSpec((None, 1, d), lambda t, k: (sel(t) // nt, 0, 0))
    chunk = lambda sel: pl.BlockSpec((None, rc, d), lambda t, k: (sel(t) // nt, (sel(t) % nt) * nk + k, 0))
    out_chunk = pl.BlockSpec((None, rc, d), lambda t, k: (prev(t) // nt, (prev(t) % nt) * nk + jnp.where(t == 0, 0, k), 0))
    first = lambda t: 0 * t
    return pl.pallas_call(
        functools.partial(_mlp_kernel, rc=rc, tiles=tiles),
        out_shape=jax.ShapeDtypeStruct(x.shape, F32),
        grid=(tiles + 1, nk),
        in_specs=[pl.BlockSpec((None, tm, d), lambda t, k: (0, 0, 0), pipeline_mode=pl.Buffered(1)),
                  chunk(prev), chunk(nxt), vec, bvec(first), bvec(first), bvec(nxt), bvec(nxt),
                  pl.BlockSpec((1, d, th), lambda t, k: (k, 0, 0)),
                  pl.BlockSpec((th, d), lambda t, k: (k, 0)),
                  vec, bvec(prev)],
        out_specs=out_chunk,
        scratch_shapes=[pltpu.VMEM((tm, d), BF16)] * 2 + [pltpu.VMEM((tm, d), F32)] * 2,
        compiler_params=_params(("arbitrary", "arbitrary")),
        name="mlp",
    )(x, x, x, g_pre.reshape(1, d), shift, scale, shift, scale, w_up_tiles, w_down, g_post.reshape(1, d), gate)


def kernel(x, c, ctx, c_ctx, w_mod, b_mod, g_pre_mix, g_post_mix, g_pre_mlp, g_post_mlp, w_in, w_out, g_branch,
           attn_sink, hy_conv_w, hy_conv_b, hy_w1, hy_b1, hy_freq1, hy_w2, hy_b2, hy_freq2, hy_w3, hy_bias,
           pool_w, pool_scale, w_up, w_down):
    b, n, d = x.shape
    n_ctx = ctx.shape[1]
    depth = w_mod.shape[0]
    assert b == 2 and d == D_MODEL and n % 512 == 0 and n_ctx % BLOCK == 0

    cond = jnp.concatenate([c, c_ctx[None], jnp.zeros((SUBLANES - b - 1, d), F32)], axis=0)
    mods = _modulation(cond, w_mod, b_mod)

    w_in_b = _column_tiles(_permute_rope_columns(w_in.astype(BF16)), IN_TN)
    w_up_b = _column_tiles(w_up.astype(BF16), MLP_TH)
    w_out_b, w_down_b = w_out.astype(BF16), w_down.astype(BF16)
    rope = _rope_tables(n)
    feats_x, feats_c = _filter_features(n), _filter_features(n_ctx)
    deltas = _filter_deltas()
    tables = _stage2_tables(n)

    for i in range(depth):
        last = i == depth - 1
        hy = (hy_conv_w[i], hy_conv_b[i], hy_w1[i], hy_b1[i], hy_freq1[i], hy_w2[i], hy_b2[i], hy_freq2[i],
              hy_w3[i], hy_bias[i])
        mx = [m[:, None, :] for m in jnp.split(mods[i, :b], N_MOD, axis=-1)]
        mc = [jnp.broadcast_to(m[None, None, :], (b, 1, d)) for m in jnp.split(mods[i, b], N_MOD, axis=-1)]

        qkv_x, px = _in_projection(x, g_pre_mix[i], mx[0], mx[1], w_in_b[i], rope)
        qkv_c, pc = _in_projection(ctx, g_pre_mix[i], mc[0], mc[1], w_in_b[i])
        kv_ctx = qkv_c[..., Q_END:V_END]

        attn_x = _attention(qkv_x, kv_ctx, attn_sink[i], local=True)
        hy_x = _hyena_latent(px, hy, feats_x, deltas, tables, n)
        po_x = _pool_mixer(px, pool_w[i], pool_scale[i])
        x = _out_projection(attn_x, hy_x, po_x, x, g_branch[i], w_out_b[i], g_post_mix[i], mx[2], group_major=True)
        x = _mlp(x, g_pre_mlp[i], mx[3], mx[4], w_up_b[i], w_down_b[i], g_post_mlp[i], mx[5])

        if not last:
            attn_c = _attention(qkv_c, kv_ctx, attn_sink[i], local=False)
            hy_c = _hyena_context(pc, hy, feats_c, deltas, n_ctx)
            po_c = _pool_mixer(pc, pool_w[i], pool_scale[i])
            ctx = _out_projection(attn_c, hy_c, po_c, ctx, g_branch[i], w_out_b[i], g_post_mix[i], mc[2],
                                  group_major=False)
            ctx = _mlp(ctx, g_pre_mlp[i], mc[3], mc[4], w_up_b[i], w_down_b[i], g_post_mlp[i], mc[5])
    return x
```

```python
import functools
import math

import numpy as np
import jax
import jax.numpy as jnp
from jax import lax
from jax.experimental import pallas as pl
from jax.experimental.pallas import tpu as pltpu

F32 = jnp.float32
BF16 = jnp.bfloat16

D_MODEL = 2048
DEPTH = 4
GRID_W = 64
ATTN_WIDTH = D_MODEL // 2
HYENA_WIDTH = D_MODEL // 4
POOL_WIDTH = D_MODEL - ATTN_WIDTH - HYENA_WIDTH
HEAD_DIM = 128
N_HEADS = ATTN_WIDTH // HEAD_DIM
N_KV_HEADS = 2
KV_GROUP = N_HEADS // N_KV_HEADS
KV_WIDTH = N_KV_HEADS * HEAD_DIM
WINDOW = 128
BLOCK = 128
ROPE_BASE = 10000.0
HYENA_EMB_DIM = 33
HYENA_FILTER_HIDDEN = 64
HYENA_FAST_DECAY_PCT = 0.3
HYENA_SLOW_DECAY_PCT = 1.5
HYENA_DECAY_TARGET = 1e-2
POOL_WINDOWS = (2, 4, 8, 16)
POOL_GROUP = POOL_WIDTH // len(POOL_WINDOWS)
MLP_HIDDEN = 4 * D_MODEL
N_MOD = 6
EPS = 1e-6
NEG_INF = -1e30

Q_END = ATTN_WIDTH
K_END = Q_END + KV_WIDTH
V_END = K_END + KV_WIDTH
HY_END = V_END + 3 * HYENA_WIDTH
IN_WIDTH = HY_END + POOL_WIDTH

LANES = 128
SUBLANES = 8
FFT_N2 = 128
POOL_HALO = 16
VMEM_LIMIT = 56 * 1024 * 1024


def _params(sem, vmem=VMEM_LIMIT):
    return pltpu.CompilerParams(dimension_semantics=sem, vmem_limit_bytes=vmem)


def _split(x):
    hi = x.astype(BF16)
    lo = (x - hi.astype(F32)).astype(BF16)
    return hi, lo


def _dot(a, b):
    return jnp.dot(a, b, preferred_element_type=F32)


def _dot3(ah, al, bh, bl):
    return _dot(ah, bh) + _dot(ah, bl) + _dot(al, bh)


def _rms(x, g):
    return x * lax.rsqrt(jnp.mean(x * x, axis=-1, keepdims=True) + EPS) * g


BF16_ROWS = 2 * SUBLANES


def _norm_scale_rows(x_ref, h_ref, a_ref, s_ref, rows, cols=None):
    cols = slice(None) if cols is None else cols
    nchunks = rows // BF16_ROWS

    def chunk(c):
        return pl.ds(pl.multiple_of(c * BF16_ROWS, BF16_ROWS), BF16_ROWS)

    def inv_rms(c):
        x = x_ref[chunk(c), :]
        return lax.rsqrt(jnp.mean(x * x, axis=-1, keepdims=True) + EPS)

    def body(c, inv):
        inv_next = inv_rms(jnp.minimum(c + 1, nchunks - 1))
        y = x_ref[chunk(c), :] * inv * a_ref[...]
        if s_ref is not None:
            y = y + s_ref[...]
        h_ref[chunk(c), cols] = y.astype(BF16)
        return inv_next

    lax.fori_loop(0, nchunks, body, inv_rms(0), unroll=8)


def _residual_norm_rows(x_ref, y_ref, o_ref, pg_ref, rows):
    nchunks = rows // SUBLANES

    def chunk(c):
        return pl.ds(pl.multiple_of(c * SUBLANES, SUBLANES), SUBLANES)

    def inv_rms(c):
        y = y_ref[chunk(c), :]
        return lax.rsqrt(jnp.mean(y * y, axis=-1, keepdims=True) + EPS)

    def body(c, inv):
        inv_next = inv_rms(jnp.minimum(c + 1, nchunks - 1))
        o_ref[chunk(c), :] = x_ref[chunk(c), :] + y_ref[chunk(c), :] * inv * pg_ref[...]
        return inv_next

    lax.fori_loop(0, nchunks, body, inv_rms(0), unroll=16)


def _mod_kernel(c_ref, w_ref, b_ref, o_ref):
    c = c_ref[...]
    s = c / (1.0 + jnp.exp(-c))
    sh, sl = _split(s)
    wh, wl = _split(w_ref[...])
    o_ref[...] = _dot3(sh, sl, wh, wl) + b_ref[...]


def _modulation(cond, w_mod, b_mod):
    depth, d, width = w_mod.shape
    tn = 1024
    return pl.pallas_call(
        _mod_kernel,
        out_shape=jax.ShapeDtypeStruct((depth, SUBLANES, width), F32),
        grid=(depth, width // tn),
        in_specs=[
            pl.BlockSpec((SUBLANES, d), lambda l, j: (0, 0)),
            pl.BlockSpec((None, d, tn), lambda l, j: (l, 0, j)),
            pl.BlockSpec((None, 1, tn), lambda l, j: (l, 0, j)),
        ],
        out_specs=pl.BlockSpec((None, SUBLANES, tn), lambda l, j: (l, 0, j)),
        compiler_params=_params(("parallel", "parallel")),
        name="modulation",
    )(cond, w_mod, b_mod.reshape(depth, 1, width))


QKV_WIDTH = V_END
REST_WIDTH = IN_WIDTH - V_END
SM_SCALE = HEAD_DIM ** -0.5


TABLE_Q, TABLE_K, TABLE_ID, TABLE_SCALE = range(4)


def _inproj_kernel(x0_ref, xn_ref, g_ref, sh0_ref, sc0_ref, shn_ref, scn_ref, w_ref, ca_ref, sa_ref, cb_ref, sb_ref,
                   qkv_ref, rest_ref, h0_ref, h1_ref, acc_ref, *, tm, tn, rc, tiles):
    t = pl.program_id(0)
    j = pl.program_id(1)

    @pl.when((t == 0) & (j == 0))
    def _():
        h0_ref[...] = (_rms(x0_ref[...], g_ref[...] * (1.0 + sc0_ref[...])) + sh0_ref[...]).astype(BF16)
        acc_ref[...] = jnp.zeros_like(acc_ref)

    def finish_previous():
        acc = acc_ref[...]
        rest_ref[...] = acc
        nch = tn // LANES
        for ch in range(nch):
            a = acc[:, ch * LANES:(ch + 1) * LANES]
            cos, sin = (ca_ref, sa_ref) if ch < nch // 2 else (cb_ref, sb_ref)
            qkv_ref[:, ch * LANES:(ch + 1) * LANES] = (a * cos[...] + pltpu.roll(a, HEAD_DIM // 2, 1) * sin[...]).astype(BF16)

    def step(slot):
        h_cur, h_oth = (h0_ref, h1_ref) if slot == 0 else (h1_ref, h0_ref)
        finish_previous()
        rows = pl.ds(pl.multiple_of(jnp.minimum(j * rc, tm - rc), BF16_ROWS), rc)
        a_next = g_ref[...] * (1.0 + scn_ref[...])
        h_oth[rows, :] = (_rms(xn_ref[rows, :], a_next) + shn_ref[...]).astype(BF16)
        acc_ref[...] = _dot(h_cur[...], w_ref[...])

    pl.when((t % 2 == 0) & (t < tiles))(functools.partial(step, 0))
    pl.when((t % 2 == 1) & (t < tiles))(functools.partial(step, 1))
    pl.when((t == tiles) & (j == 0))(finish_previous)


def _column_tiles(w, tn):
    *lead, d, width = w.shape
    return jnp.swapaxes(w.reshape(*lead, d, width // tn, tn), -3, -2)


IN_TN = 512
QKV_SPARE = QKV_WIDTH // IN_TN
REST_SPARE = REST_WIDTH // IN_TN


def _in_projection(x, g, shift, scale, w, tables, rope):
    b, n, d = x.shape
    tm = min(1024, n)
    tn = w.shape[-1]
    nj = w.shape[0]
    nt = n // tm
    tiles = b * nt
    half = tn // 2
    assert tn == IN_TN and nj * tn == IN_WIDTH and Q_END % tn == 0 and K_END % tn == half and V_END % tn == 0
    rc = -(-tm // nj)
    rc = -(-rc // (2 * BF16_ROWS)) * 2 * BF16_ROWS
    nq, nqkv = Q_END // tn, QKV_WIDTH // tn

    def nxt(t):
        return jnp.minimum(t + 1, tiles - 1)

    def lagged(t, j):
        step = jnp.clip(t * nj + j - 1, 0, tiles * nj - 1)
        return step // nj, step % nj

    def kind_a(j):
        return jnp.where(j < nq, TABLE_Q if rope else TABLE_SCALE, jnp.where(j < nqkv, TABLE_K if rope else TABLE_ID, TABLE_ID))

    def kind_b(j):
        return jnp.where(j < nq, TABLE_Q if rope else TABLE_SCALE, TABLE_ID)

    vec = pl.BlockSpec((1, d), lambda t, j: (0, 0))
    bvec = lambda sel: pl.BlockSpec((None, 1, d), lambda t, j: (sel(t) // nt, 0, 0))
    first = lambda t: 0 * t
    table = lambda kind: pl.BlockSpec((None, tm, LANES),
                                      lambda t, j: (kind(lagged(t, j)[1]), lagged(t, j)[0] % nt, 0))

    def out_spec(col_tile):
        return pl.BlockSpec((None, tm, tn), lambda t, j: (lagged(t, j)[0] // nt, lagged(t, j)[0] % nt,
                                                          col_tile(lagged(t, j)[1])))

    cos, sin = tables
    return pl.pallas_call(
        functools.partial(_inproj_kernel, tm=tm, tn=tn, rc=rc, tiles=tiles),
        out_shape=[jax.ShapeDtypeStruct((b, n, QKV_WIDTH + tn), BF16), jax.ShapeDtypeStruct((b, n, REST_WIDTH + tn), F32)],
        grid=(tiles + 1, nj),
        in_specs=[pl.BlockSpec((None, tm, d), lambda t, j: (0, 0, 0), pipeline_mode=pl.Buffered(1)),
                  pl.BlockSpec((None, tm, d), lambda t, j: (nxt(t) // nt, nxt(t) % nt, 0)),
                  vec, bvec(first), bvec(first), bvec(nxt), bvec(nxt),
                  pl.BlockSpec((None, d, tn), lambda t, j: (jnp.where(t < tiles, j, nj - 1), 0, 0)),
                  table(kind_a), table(kind_a), table(kind_b), table(kind_b)],
        out_specs=[out_spec(lambda jp: jnp.minimum(jp, QKV_SPARE)),
                   out_spec(lambda jp: jnp.where(jp >= nqkv, jp - nqkv, REST_SPARE))],
        scratch_shapes=[pltpu.VMEM((tm, d), BF16)] * 2 + [pltpu.VMEM((tm, tn), F32)],
        compiler_params=_params(("arbitrary", "arbitrary")),
        name="in_projection_rope" if rope else "in_projection",
    )(x, x, g.reshape(1, d), shift, scale, shift, scale, w, cos, sin, cos, sin)


def _permute_rope_columns(w_in):
    lead = w_in.shape[:-1]
    quarter = HEAD_DIM // 4
    qk = w_in[..., :K_END].reshape(*lead, K_END // HEAD_DIM, 2, 2, quarter)
    qk = jnp.swapaxes(qk, -3, -2).reshape(*lead, K_END)
    return jnp.concatenate([qk, w_in[..., K_END:]], axis=-1)


def _rope_tables(n):
    quarter = HEAD_DIM // 4
    inv_freq = ROPE_BASE ** (-jnp.arange(quarter, dtype=F32) / quarter)
    t = jnp.arange(n, dtype=jnp.int32)
    row = (t // GRID_W).astype(F32)[:, None] * inv_freq[None, :]
    col = (t % GRID_W).astype(F32)[:, None] * inv_freq[None, :]
    cos = jnp.concatenate([jnp.cos(row), jnp.cos(col), jnp.cos(row), jnp.cos(col)], axis=-1)
    sin = jnp.concatenate([-jnp.sin(row), -jnp.sin(col), jnp.sin(row), jnp.sin(col)], axis=-1)
    one, zero = jnp.ones_like(cos), jnp.zeros_like(sin)
    return (jnp.stack([cos * SM_SCALE, cos, one, one * SM_SCALE]), jnp.stack([sin * SM_SCALE, sin, zero, zero]))


def _softmax_pv(parts, sink_col):
    m = sink_col
    for s, _ in parts:
        m = jnp.maximum(m, jnp.max(s, axis=-1, keepdims=True))
    den = jnp.exp(sink_col - m)
    out = None
    for s, v in parts:
        p = jnp.exp(s - m)
        den = den + jnp.sum(p, axis=-1, keepdims=True)
        pv = _dot(p.astype(BF16), v)
        out = pv if out is None else out + pv
    return out / den


def _ctx_attn_kernel(sink_ref, q_ref, kvc_ref, o_ref, *, tq):
    for g in range(N_KV_HEADS):
        heads = [g * KV_GROUP + h for h in range(KV_GROUP)]
        qg = jnp.concatenate([q_ref[:, h * HEAD_DIM:(h + 1) * HEAD_DIM] for h in heads], axis=0)
        sink_col = jnp.concatenate([jnp.full((tq, 1), sink_ref[h], F32) for h in heads], axis=0)
        kc = kvc_ref[:, g * HEAD_DIM:(g + 1) * HEAD_DIM]
        vc = kvc_ref[:, KV_WIDTH + g * HEAD_DIM:KV_WIDTH + (g + 1) * HEAD_DIM]
        s = lax.dot_general(qg, kc, (((1,), (1,)), ((), ())), preferred_element_type=F32)
        o = _softmax_pv([(s, vc)], sink_col)
        for hi, h in enumerate(heads):
            o_ref[:, h * HEAD_DIM:(h + 1) * HEAD_DIM] = o[hi * tq:(hi + 1) * tq]


ATTN_ROWS = KV_GROUP * BLOCK
SOFTMAX_CHUNK = 32


def _win_attn_kernel(sink_ref, q_ref, km_ref, kp_ref, kn_ref, vm_ref, vp_ref, vn_ref, kvc_ref, band_ref,
                     o_ref, ktw, vw, s_scr, p_scr, m_scr, *, tq, nb, nctx):
    i = pl.program_id(1)
    nsub = tq // BLOCK
    nloc = 3 * BLOCK

    def transposed(x):
        return x.astype(F32).T.astype(BF16)

    def block_rows(main_ref, prev_ref, next_ref, w, lanes):
        if w == 0:
            return prev_ref[:, lanes]
        if w == nsub + 1:
            return next_ref[:, lanes]
        return main_ref[(w - 1) * BLOCK:w * BLOCK, lanes]

    ones = jnp.ones((nloc + nctx, HEAD_DIM), BF16)
    for g in range(N_KV_HEADS):
        lanes = slice(g * HEAD_DIM, (g + 1) * HEAD_DIM)
        vlanes = slice(KV_WIDTH + g * HEAD_DIM, KV_WIDTH + (g + 1) * HEAD_DIM)
        kts = [transposed(block_rows(km_ref, kp_ref, kn_ref, w, lanes)) for w in range(nsub + 2)]
        kct = [transposed(kvc_ref[cb * BLOCK:(cb + 1) * BLOCK, lanes]) for cb in range(nctx // BLOCK)]
        for jb in range(nsub):
            for w in range(3):
                ktw[jb, g, :, w * BLOCK:(w + 1) * BLOCK] = kts[jb + w]
                vw[jb, g, w * BLOCK:(w + 1) * BLOCK, :HEAD_DIM] = block_rows(vm_ref, vp_ref, vn_ref, jb + w, lanes)
            for cb in range(nctx // BLOCK):
                ktw[jb, g, :, nloc + cb * BLOCK:nloc + (cb + 1) * BLOCK] = kct[cb]
            vw[jb, g, nloc:, :HEAD_DIM] = kvc_ref[:, vlanes]
            vw[jb, g, :, HEAD_DIM:] = ones

    col = lax.broadcasted_iota(jnp.int32, (1, nloc), 1)

    def rows_of(jb):
        start = jb * BLOCK
        return pl.ds(start if isinstance(start, int) else pl.multiple_of(start, BLOCK), BLOCK)

    def stage_a(jb, g):
        qg = jnp.concatenate([q_ref[rows_of(jb), (g * KV_GROUP + h) * HEAD_DIM:(g * KV_GROUP + h + 1) * HEAD_DIM]
                              for h in range(KV_GROUP)], axis=0)
        s_scr[g] = _dot(qg, ktw[jb, g])

    def stage_b(jb, g):
        blk = i * nsub + jb
        pen_prev = jnp.where(blk == 0, NEG_INF, 0.0).astype(F32)
        pen_next = jnp.where(blk == nb - 1, NEG_INF, 0.0).astype(F32)
        rowbias = jnp.where(col < BLOCK, pen_prev, jnp.where(col >= 2 * BLOCK, pen_next, 0.0))
        for c in range(ATTN_ROWS // SOFTMAX_CHUNK):
            rows = slice(c * SOFTMAX_CHUNK, (c + 1) * SOFTMAX_CHUNK)
            sink = sink_ref[g * KV_GROUP + (c * SOFTMAX_CHUNK) // BLOCK]
            s_loc = s_scr[g, rows, :nloc] + band_ref[rows, :] + rowbias
            s_ctx = s_scr[g, rows, nloc:]
            m = jnp.maximum(jnp.max(s_loc, axis=-1, keepdims=True), jnp.max(s_ctx, axis=-1, keepdims=True))
            m = jnp.maximum(m, sink)
            p_scr[g, rows, :nloc] = jnp.exp(s_loc - m).astype(BF16)
            p_scr[g, rows, nloc:] = jnp.exp(s_ctx - m).astype(BF16)
            m_scr[g, rows, :] = m

    def stage_c(jb, g):
        o = _dot(p_scr[g], vw[jb, g])
        for hi in range(KV_GROUP):
            h = g * KV_GROUP + hi
            rows = slice(hi * BLOCK, (hi + 1) * BLOCK)
            den = o[rows, HEAD_DIM:HEAD_DIM + 1] + jnp.exp(sink_ref[h] - m_scr[g, rows, :])
            o_ref[rows_of(jb), h * HEAD_DIM:(h + 1) * HEAD_DIM] = o[rows, :HEAD_DIM] / den

    stage_a(0, 0)
    stage_a(0, 1)
    stage_b(0, 0)

    def body(j, carry):
        stage_a(j, 0)
        stage_c(j - 1, 0)
        stage_b(j - 1, 1)
        stage_a(j, 1)
        stage_c(j - 1, 1)
        stage_b(j, 0)
        return carry

    lax.fori_loop(1, nsub, body, 0)
    stage_c(nsub - 1, 0)
    stage_b(nsub - 1, 1)
    stage_c(nsub - 1, 1)


def _band_bias():
    qi = np.arange(ATTN_ROWS)[:, None] % BLOCK
    sj = np.arange(3 * BLOCK)[None, :]
    return jnp.asarray(np.where(np.abs(sj - BLOCK - qi) <= WINDOW, 0.0, NEG_INF), dtype=F32)


def _attention(qkv, kv_ctx, sink, local):
    b, n, _ = qkv.shape
    c = kv_ctx.shape[1]
    tq = min(1024 if local else 512, n)
    nsub = tq // BLOCK
    nb = n // BLOCK
    kcol = Q_END // KV_WIDTH
    vcol = K_END // KV_WIDTH
    smem = pl.BlockSpec(memory_space=pltpu.SMEM)
    q_spec = pl.BlockSpec((None, tq, ATTN_WIDTH), lambda bi, i: (bi, i, 0))
    kvc_spec = pl.BlockSpec((None, c, 2 * KV_WIDTH), lambda bi, i: (bi, 0, 0))
    if local:
        def main(colblk):
            return pl.BlockSpec((None, tq, KV_WIDTH), lambda bi, i: (bi, i, colblk))

        def prev(colblk):
            return pl.BlockSpec((None, BLOCK, KV_WIDTH), lambda bi, i: (bi, jnp.maximum(i * nsub - 1, 0), colblk))

        def nxt(colblk):
            return pl.BlockSpec((None, BLOCK, KV_WIDTH), lambda bi, i: (bi, jnp.minimum((i + 1) * nsub, nb - 1), colblk))

        keys = 3 * BLOCK + c
        kern = functools.partial(_win_attn_kernel, tq=tq, nb=nb, nctx=c)
        in_specs = [smem, q_spec, main(kcol), prev(kcol), nxt(kcol), main(vcol), prev(vcol), nxt(vcol), kvc_spec,
                    pl.BlockSpec((ATTN_ROWS, 3 * BLOCK), lambda bi, i: (0, 0))]
        args = [sink, qkv, qkv, qkv, qkv, qkv, qkv, qkv, kv_ctx, _band_bias()]
        scratch = [pltpu.VMEM((nsub, N_KV_HEADS, HEAD_DIM, keys), BF16),
                   pltpu.VMEM((nsub, N_KV_HEADS, keys, 2 * HEAD_DIM), BF16),
                   pltpu.VMEM((2, ATTN_ROWS, keys), F32),
                   pltpu.VMEM((2, ATTN_ROWS, keys), BF16),
                   pltpu.VMEM((2, ATTN_ROWS, 1), F32)]
    else:
        kern = functools.partial(_ctx_attn_kernel, tq=tq)
        in_specs = [smem, q_spec, kvc_spec]
        args = [sink, qkv, kv_ctx]
        scratch = []
    return pl.pallas_call(
        kern,
        out_shape=jax.ShapeDtypeStruct((b, n, ATTN_WIDTH), F32),
        grid=(b, n // tq),
        in_specs=in_specs,
        out_specs=pl.BlockSpec((None, tq, ATTN_WIDTH), lambda bi, i: (bi, i, 0)),
        scratch_shapes=scratch,
        compiler_params=_params(("parallel", "parallel")),
        name="window_attention" if local else "context_attention",
    )(*args)


GROUPS = FFT_N2 // SUBLANES


def _gm_shape(lead, t1, width):
    return (*lead, width // LANES, GROUPS, t1, SUBLANES, LANES)


def _store_group_major(o_ref, val, tl):
    for t1 in range(tl // FFT_N2):
        for jg in range(GROUPS):
            r0 = (t1 * GROUPS + jg) * SUBLANES
            for cc in range(val.shape[1] // LANES):
                o_ref[cc, jg, t1] = val[r0:r0 + SUBLANES, cc * LANES:(cc + 1) * LANES]


def _hyena_prep_kernel(u_ref, p_ref, n_ref, w_ref, b_ref, vx_ref, x0_ref, *, tl, nt, group_major):
    i = pl.program_id(1)
    u = u_ref[...]
    prev_row = jnp.where(i > 0, p_ref[SUBLANES - 1:SUBLANES, :], 0.0)
    next_row = jnp.where(i < nt - 1, n_ref[0:1, :], 0.0)
    row = lax.broadcasted_iota(jnp.int32, u.shape, 0)
    um = jnp.where(row == 0, prev_row, pltpu.roll(u, 1, 0))
    up = jnp.where(row == tl - 1, next_row, pltpu.roll(u, tl - 1, 0))
    z = um * w_ref[0:1, :] + u * w_ref[1:2, :] + up * w_ref[2:3, :] + b_ref[...]
    x0 = z[:, :HYENA_WIDTH]
    vx = z[:, 2 * HYENA_WIDTH:] * z[:, HYENA_WIDTH:2 * HYENA_WIDTH]
    if group_major:
        _store_group_major(vx_ref, vx, tl)
        _store_group_major(x0_ref, x0, tl)
    else:
        vx_ref[...] = vx
        x0_ref[...] = x0


def _hyena_prep(p, conv_w, conv_b, group_major):
    b, n, _ = p.shape
    tl = min(512, n)
    nt = n // tl
    hw = 3 * HYENA_WIDTH
    colblk = 0
    nrow8 = n // SUBLANES
    per = tl // SUBLANES
    if group_major:
        shape = _gm_shape((b,), n // FFT_N2, HYENA_WIDTH)
        out_spec = pl.BlockSpec((None, *_gm_shape((), tl // FFT_N2, HYENA_WIDTH)), lambda bi, i: (bi, 0, 0, i, 0, 0))
    else:
        shape = (b, n, HYENA_WIDTH)
        out_spec = pl.BlockSpec((None, tl, HYENA_WIDTH), lambda bi, i: (bi, i, 0))
    return pl.pallas_call(
        functools.partial(_hyena_prep_kernel, tl=tl, nt=nt, group_major=group_major),
        out_shape=[jax.ShapeDtypeStruct(shape, F32)] * 2,
        grid=(b, nt),
        in_specs=[
            pl.BlockSpec((None, tl, hw), lambda bi, i: (bi, i, colblk)),
            pl.BlockSpec((None, SUBLANES, hw), lambda bi, i: (bi, jnp.maximum(i * per - 1, 0), colblk)),
            pl.BlockSpec((None, SUBLANES, hw), lambda bi, i: (bi, jnp.minimum((i + 1) * per, nrow8 - 1), colblk)),
            pl.BlockSpec((3, hw), lambda bi, i: (0, 0)),
            pl.BlockSpec((1, hw), lambda bi, i: (0, 0)),
        ],
        out_specs=[out_spec, out_spec],
        compiler_params=_params(("parallel", "parallel")),
        name="hyena_prep",
    )(p, p, p, conv_w, conv_b.reshape(1, hw))


def _filter_kernel(ft_ref, w1_ref, b1_ref, f1_ref, w2_ref, b2_ref, f2_ref, w3_ref, dl_ref, h_ref, s_ref,
                   *, tl, n, group_major):
    i = pl.program_id(0)

    def dense(a, w_ref):
        ah, al = _split(a)
        wh, wl = _split(w_ref[...])
        return _dot3(ah, al, wh, wl)

    h = jnp.sin(f1_ref[...] * (dense(ft_ref[...], w1_ref) + b1_ref[...]))
    h = jnp.sin(f2_ref[...] * (dense(h, w2_ref) + b2_ref[...]))
    h = dense(h, w3_ref)
    t = (i * tl + lax.broadcasted_iota(jnp.int32, (tl, HYENA_WIDTH), 0)).astype(F32) / float(n - 1)
    decay = jnp.exp(-t * dl_ref[...])
    h = h * jnp.concatenate([decay, decay], axis=1)
    if group_major:
        _store_group_major(h_ref, h, tl)
    else:
        h_ref[...] = h

    @pl.when(i == 0)
    def _():
        s_ref[...] = jnp.zeros_like(s_ref)

    s_ref[...] += jnp.sum(jnp.abs(h).reshape(tl // SUBLANES, SUBLANES, 2 * HYENA_WIDTH), axis=0)


def _filter_features(n):
    t = jnp.linspace(0.0, 1.0, n, dtype=F32)[:, None]
    bands = (HYENA_EMB_DIM - 1) // 2
    omega = 2.0 * math.pi * jnp.arange(n, dtype=F32)[:, None] / n
    f = jnp.linspace(1e-4, bands - 1, bands, dtype=F32)[None, :]
    feats = jnp.concatenate([t, jnp.cos(f * omega), -jnp.sin(f * omega)], axis=-1)
    return jnp.pad(feats, ((0, 0), (0, LANES - HYENA_EMB_DIM)))


def _filter_deltas():
    max_decay = math.log(HYENA_DECAY_TARGET) / HYENA_FAST_DECAY_PCT
    min_decay = math.log(HYENA_DECAY_TARGET) / HYENA_SLOW_DECAY_PCT
    return jnp.abs(jnp.linspace(min_decay, max_decay, HYENA_WIDTH, dtype=F32)).reshape(1, HYENA_WIDTH)


def _hyena_filter(n, feats, deltas, w1, b1, f1, w2, b2, f2, w3, group_major):
    tl = min(512, n)
    hid = LANES
    pad_h = hid - HYENA_FILTER_HIDDEN
    w1p = jnp.pad(w1, ((0, LANES - HYENA_EMB_DIM), (0, pad_h)))
    w2p = jnp.pad(w2, ((0, pad_h), (0, pad_h)))
    w3p = jnp.pad(w3, ((0, pad_h), (0, 0)))
    vec = lambda v: jnp.pad(v, (0, pad_h)).reshape(1, hid)
    hw2 = 2 * HYENA_WIDTH
    full = lambda shape: pl.BlockSpec(shape, lambda i: (0,) * len(shape))
    if group_major:
        shape = _gm_shape((), n // FFT_N2, hw2)
        out_spec = pl.BlockSpec(_gm_shape((), tl // FFT_N2, hw2), lambda i: (0, 0, i, 0, 0))
    else:
        shape = (n, hw2)
        out_spec = pl.BlockSpec((tl, hw2), lambda i: (i, 0))
    return pl.pallas_call(
        functools.partial(_filter_kernel, tl=tl, n=n, group_major=group_major),
        out_shape=[jax.ShapeDtypeStruct(shape, F32), jax.ShapeDtypeStruct((SUBLANES, hw2), F32)],
        grid=(n // tl,),
        in_specs=[pl.BlockSpec((tl, LANES), lambda i: (i, 0)), full((LANES, hid)), full((1, hid)), full((1, hid)),
                  full((hid, hid)), full((1, hid)), full((1, hid)), full((hid, hw2)), full((1, HYENA_WIDTH))],
        out_specs=[out_spec, full((SUBLANES, hw2))],
        compiler_params=_params(("arbitrary",)),
        name="hyena_filter",
    )(feats, w1p, vec(b1), vec(f1), w2p, vec(b2), vec(f2), w3p, deltas)


def _stack_complex(m):
    return np.block([[m.real, -m.imag], [m.imag, m.real]])


def _hilo(m):
    m = jnp.asarray(m, dtype=F32)
    return _split(m)


@functools.lru_cache(maxsize=None)
def _fft_constants(n):
    m = 2 * n
    n2 = FFT_N2
    n1 = m // n2
    n1h = n1 // 2
    k1 = np.arange(n1)
    t1 = np.arange(n1h)
    f1 = np.exp(-2j * np.pi * np.outer(k1, t1) / n1)
    f3 = np.exp(2j * np.pi * np.outer(t1, k1) / n1) / m
    k2 = np.arange(n2)
    t2 = np.arange(n2)
    w2 = np.exp(-2j * np.pi * np.outer(k2, t2) / n2)
    tw = np.exp(-2j * np.pi * np.outer(k1, t2) / m)
    return dict(
        n1=n1, n1h=n1h,
        f1c=_stack_complex(f1), f1r=np.concatenate([f1.real, f1.imag], axis=0),
        f3c=_stack_complex(f3),
        w2r=w2.real.astype(np.float32), w2i=w2.imag.astype(np.float32),
        twr=tw.real.astype(np.float32), twi=tw.imag.astype(np.float32),
    )


def _stage2_tables(n):
    c = _fft_constants(n)
    w2r, w2i = jnp.asarray(c["w2r"])[None], jnp.asarray(c["w2i"])[None]
    twr, twi = jnp.asarray(c["twr"])[:, None, :], jnp.asarray(c["twi"])[:, None, :]
    gr = w2r * twr - w2i * twi
    gi = w2r * twi + w2i * twr
    g = jnp.concatenate([jnp.concatenate([gr, -gi], axis=2), jnp.concatenate([gi, gr], axis=2)], axis=1)
    return _split(g)


def _s1_kernel(x_ref, fh_ref, fl_ref, o_ref, *, nparts, ncw, n1, n1h):
    fh = fh_ref[...]
    fl = fl_ref[...]
    for r in range(SUBLANES):
        rows = pl.ds(r, n1h, stride=SUBLANES)
        xs = jnp.concatenate(
            [jnp.concatenate([x_ref[p, cc, rows, :] for p in range(nparts)], axis=0) for cc in range(ncw)], axis=1)
        xh, xl = _split(xs)
        res = _dot3(fh, fl, xh, xl)
        for ri in range(2):
            for cc in range(ncw):
                o_ref[ri, cc, pl.ds(r, n1, stride=SUBLANES), :] = res[ri * n1:(ri + 1) * n1, cc * LANES:(cc + 1) * LANES]


def _fft_stage1(x, fmat, n1, n1h, ncw):
    nparts, ncc, groups = x.shape[:3]
    fh, fl = _hilo(fmat)
    return pl.pallas_call(
        functools.partial(_s1_kernel, nparts=nparts, ncw=ncw, n1=n1, n1h=n1h),
        out_shape=jax.ShapeDtypeStruct((2, ncc, groups, n1 * SUBLANES, LANES), F32),
        grid=(groups, ncc // ncw),
        in_specs=[
            pl.BlockSpec((nparts, ncw, None, n1h * SUBLANES, LANES), lambda j, ci: (0, ci, j, 0, 0)),
            pl.BlockSpec(fh.shape, lambda j, ci: (0, 0)),
            pl.BlockSpec(fl.shape, lambda j, ci: (0, 0)),
        ],
        out_specs=pl.BlockSpec((2, ncw, None, n1 * SUBLANES, LANES), lambda j, ci: (0, ci, j, 0, 0)),
        compiler_params=_params(("parallel", "parallel")),
        name="fft_stage1",
    )(x, fh, fl)


def _load_k1(a_ref, q):
    ncc = a_ref.shape[1]
    return jnp.concatenate([a_ref[:, cc, :, q].reshape(2 * FFT_N2, LANES) for cc in range(ncc)], axis=1)


def _filter_spectrum_kernel(a_ref, gh_ref, gl_ref, s_ref, o_ref, *, kg):
    s = jnp.sum(s_ref[...], axis=0, keepdims=True)
    inv = 1.0 / (s[:, :HYENA_WIDTH] + s[:, HYENA_WIDTH:])
    half = FFT_N2
    for q in range(kg):
        ah, al = _split(_load_k1(a_ref, q))
        h = _dot3(gh_ref[q], gl_ref[q], ah, al)
        hf = h[:, :HYENA_WIDTH]
        hb = h[:, HYENA_WIDTH:]
        o_ref[q, :half] = (hf[:half] + hb[:half]) * inv
        o_ref[q, half:] = (hf[half:] - hb[half:]) * inv


def _k1_spec(kg, width):
    return pl.BlockSpec((2, width // LANES, GROUPS, kg, SUBLANES, LANES), lambda i: (0, 0, 0, i, 0, 0))


def _filter_spectrum(a, gh, gl, sums, n1):
    kg = 4
    a6 = a.reshape(2, 2 * HYENA_WIDTH // LANES, GROUPS, n1, SUBLANES, LANES)
    tspec = pl.BlockSpec((kg, 2 * FFT_N2, 2 * FFT_N2), lambda i: (i, 0, 0))
    return pl.pallas_call(
        functools.partial(_filter_spectrum_kernel, kg=kg),
        out_shape=jax.ShapeDtypeStruct((n1, 2 * FFT_N2, HYENA_WIDTH), F32),
        grid=(n1 // kg,),
        in_specs=[_k1_spec(kg, 2 * HYENA_WIDTH), tspec, tspec,
                  pl.BlockSpec((SUBLANES, 2 * HYENA_WIDTH), lambda i: (0, 0))],
        out_specs=pl.BlockSpec((kg, 2 * FFT_N2, HYENA_WIDTH), lambda i: (i, 0, 0)),
        compiler_params=_params(("parallel",)),
        name="filter_spectrum",
    )(a6, gh, gl, sums)


def _dot_t(a, b):
    return lax.dot_general(a, b, (((0,), (0,)), ((), ())), preferred_element_type=F32)


def _s2_kernel(a_ref, kf_ref, gh_ref, gl_ref, o_ref, *, kg):
    half = FFT_N2
    for q in range(kg):
        ah, al = _split(_load_k1(a_ref, q))
        gh, gl = gh_ref[q], gl_ref[q]
        x = _dot3(gh, gl, ah, al)
        xr, xi = x[:half], x[half:]
        kr, ki = kf_ref[q, :half], kf_ref[q, half:]
        y = jnp.concatenate([xr * kr - xi * ki, xr * ki + xi * kr], axis=0)
        yh, yl = _split(y)
        bt = _dot_t(gh, yh) + _dot_t(gh, yl) + _dot_t(gl, yh)
        for cc in range(HYENA_WIDTH // LANES):
            o_ref[:, cc, :, q] = bt[:, cc * LANES:(cc + 1) * LANES].reshape(2, GROUPS, SUBLANES, LANES)


def _fft_stage2(a, kf, tables, n1):
    kg = 4
    a6 = a.reshape(2, HYENA_WIDTH // LANES, GROUPS, n1, SUBLANES, LANES)
    tspec = pl.BlockSpec((kg, 2 * FFT_N2, 2 * FFT_N2), lambda i: (i, 0, 0))
    dspec = _k1_spec(kg, HYENA_WIDTH)
    out = pl.pallas_call(
        functools.partial(_s2_kernel, kg=kg),
        out_shape=jax.ShapeDtypeStruct(a6.shape, F32),
        grid=(n1 // kg,),
        in_specs=[dspec, pl.BlockSpec((kg, 2 * FFT_N2, HYENA_WIDTH), lambda i: (i, 0, 0)), tspec, tspec],
        out_specs=dspec,
        compiler_params=_params(("parallel",)),
        name="fft_stage2",
    )(a6, kf, *tables)
    return out.reshape(a.shape)


def _s3_kernel(b_ref, vx_ref, x0_ref, bias_ref, fh_ref, fl_ref, o_ref, *, ncw, n1, n1h):
    fh = fh_ref[...]
    fl = fl_ref[...]
    for r in range(SUBLANES):
        krows = pl.ds(r, n1, stride=SUBLANES)
        z = jnp.concatenate(
            [jnp.concatenate([b_ref[ri, cc, krows, :] for ri in range(2)], axis=0) for cc in range(ncw)], axis=1)
        zh, zl = _split(z)
        y = _dot3(fh, fl, zh, zl)
        trows = pl.ds(r, n1h, stride=SUBLANES)
        for p in range(2):
            for cc in range(ncw):
                yy = y[p * n1h:(p + 1) * n1h, cc * LANES:(cc + 1) * LANES]
                o_ref[p, cc, trows, :] = (yy + vx_ref[p, cc, trows, :] * bias_ref[cc]) * x0_ref[p, cc, trows, :]


def _fft_stage3(bt, vx, x0, bias, fmat, n1, n1h, ncw):
    ncc, groups = bt.shape[1:3]
    fh, fl = _hilo(fmat)
    tspec = pl.BlockSpec((2, ncw, None, n1h * SUBLANES, LANES), lambda j, ci: (0, ci, j, 0, 0))
    return pl.pallas_call(
        functools.partial(_s3_kernel, ncw=ncw, n1=n1, n1h=n1h),
        out_shape=jax.ShapeDtypeStruct(vx.shape, F32),
        grid=(groups, ncc // ncw),
        in_specs=[
            pl.BlockSpec((2, ncw, None, n1 * SUBLANES, LANES), lambda j, ci: (0, ci, j, 0, 0)),
            tspec, tspec,
            pl.BlockSpec((ncw, 1, LANES), lambda j, ci: (ci, 0, 0)),
            pl.BlockSpec(fh.shape, lambda j, ci: (0, 0)),
            pl.BlockSpec(fl.shape, lambda j, ci: (0, 0)),
        ],
        out_specs=tspec,
        compiler_params=_params(("parallel", "parallel")),
        name="fft_stage3",
    )(bt, vx, x0, bias, fh, fl)


def _hyena_latent(p, hy, feats, deltas, tables, n):
    conv_w, conv_b, w1, b1, f1, w2, b2, f2, w3, bias = hy
    c = _fft_constants(n)
    n1, n1h = c["n1"], c["n1h"]
    vx, x0 = _hyena_prep(p, conv_w, conv_b, group_major=True)
    gm_shape = vx.shape
    rows = lambda a: a.reshape(*a.shape[:-3], n1h * SUBLANES, LANES)
    vx, x0 = rows(vx), rows(x0)
    taps, sums = _hyena_filter(n, feats, deltas, w1, b1, f1, w2, b2, f2, w3, group_major=True)
    ncw = 2
    a_f = _fft_stage1(rows(taps)[None], c["f1r"], n1, n1h, ncw)
    kf = _filter_spectrum(a_f, tables[0], tables[1], sums, n1)
    a = _fft_stage1(vx, c["f1c"], n1, n1h, ncw)
    bt = _fft_stage2(a, kf, tables, n1)
    out = _fft_stage3(bt, vx, x0, bias.reshape(HYENA_WIDTH // LANES, 1, LANES), c["f3c"], n1, n1h, ncw)
    return out.reshape(gm_shape)


@functools.lru_cache(maxsize=None)
def _dense_dft_constants(n):
    m = 2 * n
    k = np.arange(m)
    t = np.arange(n)
    f = np.exp(-2j * np.pi * np.outer(k, t) / m)
    finv = np.exp(2j * np.pi * np.outer(t, k) / m) / m
    return _stack_complex(f), np.concatenate([f.real, f.imag], axis=0), _stack_complex(finv)


def _dense_conv_kernel(vx_ref, x0_ref, hf_ref, hb_ref, sf_ref, sb_ref, bias_ref,
                       fch, fcl, frh, frl, fih, fil, o_ref, *, n):
    m = 2 * n
    z = jnp.concatenate([vx_ref[0], vx_ref[1]], axis=0)
    zh, zl = _split(z)
    zf = _dot3(fch[...], fcl[...], zh, zl)
    hfh, hfl = _split(hf_ref[...])
    hbh, hbl = _split(hb_ref[...])
    hf = _dot3(frh[...], frl[...], hfh, hfl)
    hb = _dot3(frh[...], frl[...], hbh, hbl)
    inv = 1.0 / (jnp.sum(sf_ref[...], axis=0, keepdims=True) + jnp.sum(sb_ref[...], axis=0, keepdims=True))
    kr = (hf[:m] + hb[:m]) * inv
    ki = (hf[m:] - hb[m:]) * inv
    zr, zi = zf[:m], zf[m:]
    y = jnp.concatenate([zr * kr - zi * ki, zr * ki + zi * kr], axis=0)
    yh, yl = _split(y)
    out = _dot3(fih[...], fil[...], yh, yl)
    bias = bias_ref[...]
    for p in range(2):
        o_ref[p] = (out[p * n:(p + 1) * n] + vx_ref[p] * bias) * x0_ref[p]


def _hyena_context(p, hy, feats, deltas, n):
    conv_w, conv_b, w1, b1, f1, w2, b2, f2, w3, bias = hy
    vx, x0 = _hyena_prep(p, conv_w, conv_b, group_major=False)
    taps, sums = _hyena_filter(n, feats, deltas, w1, b1, f1, w2, b2, f2, w3, group_major=False)
    fc, fr, fi = _dense_dft_constants(n)
    mats = [*_hilo(fc), *_hilo(fr), *_hilo(fi)]
    cw = 256
    nct = HYENA_WIDTH // cw
    dspec = pl.BlockSpec((2, n, cw), lambda ci: (0, 0, ci))
    return pl.pallas_call(
        functools.partial(_dense_conv_kernel, n=n),
        out_shape=jax.ShapeDtypeStruct(vx.shape, F32),
        grid=(nct,),
        in_specs=[dspec, dspec,
                  pl.BlockSpec((n, cw), lambda ci: (0, ci)), pl.BlockSpec((n, cw), lambda ci: (0, nct + ci)),
                  pl.BlockSpec((SUBLANES, cw), lambda ci: (0, ci)), pl.BlockSpec((SUBLANES, cw), lambda ci: (0, nct + ci)),
                  pl.BlockSpec((1, cw), lambda ci: (0, ci))]
                 + [pl.BlockSpec(mt.shape, lambda ci: (0, 0)) for mt in mats],
        out_specs=dspec,
        compiler_params=_params(("parallel",)),
        name="context_long_conv",
    )(vx, x0, taps, taps, sums, sums, bias.reshape(1, HYENA_WIDTH), *mats)


def _pool_kernel(x_ref, p_ref, n_ref, w_ref, sc_ref, o_ref, *, tl, nt, n):
    i = pl.program_id(1)
    x = x_ref[...]
    pv = jnp.where(i > 0, p_ref[...], 0.0)
    nx = jnp.where(i < nt - 1, n_ref[...], 0.0)
    ext = jnp.concatenate([pv, x, nx], axis=0)
    rows = tl + 2 * POOL_HALO
    t = i * tl + lax.broadcasted_iota(jnp.int32, (tl, POOL_GROUP), 0)
    for g, w in enumerate(POOL_WINDOWS):
        lanes = slice(g * POOL_GROUP, (g + 1) * POOL_GROUP)
        a = ext[:, lanes]
        c = a + pltpu.roll(a, 1, 0)
        h = 1
        while 2 * h < w:
            c = pltpu.roll(c, h, 0) + pltpu.roll(c, rows - h, 0)
            h *= 2
        total = c[POOL_HALO:POOL_HALO + tl]
        count = (jnp.minimum(t + h, n) - jnp.maximum(t - h, 0)).astype(F32)
        y = (total / count - x[:, lanes]).astype(BF16)
        o_ref[:, lanes] = _dot(y, w_ref[g].astype(BF16)) * sc_ref[:, lanes]


def _pool_mixer(p, w_pool, scale):
    b, n, _ = p.shape
    tl = min(512, n)
    nt = n // tl
    colblk = 3 * HYENA_WIDTH // POOL_WIDTH
    assert colblk * POOL_WIDTH == 3 * HYENA_WIDTH
    per = tl // POOL_HALO
    nrow = n // POOL_HALO
    return pl.pallas_call(
        functools.partial(_pool_kernel, tl=tl, nt=nt, n=n),
        out_shape=jax.ShapeDtypeStruct((b, n, POOL_WIDTH), F32),
        grid=(b, nt),
        in_specs=[
            pl.BlockSpec((None, tl, POOL_WIDTH), lambda bi, i: (bi, i, colblk)),
            pl.BlockSpec((None, POOL_HALO, POOL_WIDTH), lambda bi, i: (bi, jnp.maximum(i * per - 1, 0), colblk)),
            pl.BlockSpec((None, POOL_HALO, POOL_WIDTH), lambda bi, i: (bi, jnp.minimum((i + 1) * per, nrow - 1), colblk)),
            pl.BlockSpec(w_pool.shape, lambda bi, i: (0, 0, 0)),
            pl.BlockSpec((1, POOL_WIDTH), lambda bi, i: (0, 0)),
        ],
        out_specs=pl.BlockSpec((None, tl, POOL_WIDTH), lambda bi, i: (bi, i, 0)),
        compiler_params=_params(("parallel", "parallel")),
        name="pool_mixer",
    )(p, p, p, w_pool, scale.reshape(1, POOL_WIDTH))


def _load_group_major(ref, tl):
    ncc = ref.shape[0]
    return jnp.concatenate(
        [jnp.concatenate([ref[cc, jg, t1] for cc in range(ncc)], axis=1)
         for t1 in range(tl // FFT_N2) for jg in range(GROUPS)], axis=0)


def _outproj_kernel(at_ref, hy_ref, po_ref, x_ref, gb_ref, w_ref, gp_ref, gt_ref, o_ref, *, tm, group_major):
    hy = _load_group_major(hy_ref, tm) if group_major else hy_ref[...]
    a0, a1 = ATTN_WIDTH, ATTN_WIDTH + HYENA_WIDTH
    ox = _dot(_rms(at_ref[...], gb_ref[:, :a0]).astype(BF16), w_ref[:a0])
    ox += _dot(_rms(hy, gb_ref[:, a0:a1]).astype(BF16), w_ref[a0:a1])
    ox += _dot(_rms(po_ref[...], gb_ref[:, a1:]).astype(BF16), w_ref[a1:])
    o_ref[...] = x_ref[...] + gt_ref[...] * _rms(ox, gp_ref[...])


def _out_projection(attn, hy, po, x, g_branch, w_out, g_post, gate, group_major):
    b, n, d = x.shape
    tm = min(512, n)
    row = lambda width: pl.BlockSpec((None, tm, width), lambda bi, i: (bi, i, 0))
    vec = lambda width: pl.BlockSpec((1, width), lambda bi, i: (0, 0))
    if group_major:
        hy_spec = pl.BlockSpec((None, *_gm_shape((), tm // FFT_N2, HYENA_WIDTH)), lambda bi, i: (bi, 0, 0, i, 0, 0))
    else:
        hy_spec = row(HYENA_WIDTH)
    return pl.pallas_call(
        functools.partial(_outproj_kernel, tm=tm, group_major=group_major),
        out_shape=jax.ShapeDtypeStruct(x.shape, F32),
        grid=(b, n // tm),
        in_specs=[row(ATTN_WIDTH), hy_spec, row(POOL_WIDTH), row(d), vec(d),
                  pl.BlockSpec(w_out.shape, lambda bi, i: (0, 0)), vec(d),
                  pl.BlockSpec((None, 1, d), lambda bi, i: (bi, 0, 0))],
        out_specs=row(d),
        compiler_params=_params(("parallel", "parallel")),
        name="out_projection",
    )(attn, hy, po, x, g_branch.reshape(1, d), w_out, g_post.reshape(1, d), gate)


def _mlp_kernel(x0_ref, xe_ref, xp_ref, g_ref, sh0_ref, sc0_ref, shn_ref, scn_ref, wu_ref, wd_ref, gp_ref, gtp_ref, o_ref,
                h0_ref, h1_ref, acc0_ref, acc1_ref, *, rc, tiles):
    t = pl.program_id(0)
    k = pl.program_id(1)

    @pl.when((t == 0) & (k == 0))
    def _():
        h0_ref[...] = (_rms(x0_ref[...], g_ref[...] * (1.0 + sc0_ref[...])) + sh0_ref[...]).astype(BF16)
        acc0_ref[...] = jnp.zeros_like(acc0_ref)
        acc1_ref[...] = jnp.zeros_like(acc1_ref)

    def step(slot, last):
        h_cur, h_oth = (h0_ref, h1_ref) if slot == 0 else (h1_ref, h0_ref)
        acc_cur, acc_oth = (acc0_ref, acc1_ref) if slot == 0 else (acc1_ref, acc0_ref)
        rows = pl.ds(pl.multiple_of(k * rc, rc), rc)
        y = acc_oth[rows, :]
        o_ref[...] = xe_ref[...] + _rms(y, gp_ref[...] * gtp_ref[...])
        if last:
            return
        acc_oth[rows, :] = jnp.zeros_like(y)
        a_next = g_ref[...] * (1.0 + scn_ref[...])
        h_oth[rows, :] = (_rms(xp_ref[...], a_next) + shn_ref[...]).astype(BF16)
        u = jnp.maximum(_dot(h_cur[...], wu_ref[...]), 0.0)
        acc_cur[...] += _dot((u * u).astype(BF16), wd_ref[...])

    pl.when((t % 2 == 0) & (t < tiles))(functools.partial(step, 0, False))
    pl.when((t % 2 == 1) & (t < tiles))(functools.partial(step, 1, False))
    pl.when(t == tiles)(functools.partial(step, tiles % 2, True))


MLP_TH = 512


def _mlp(x, g_pre, shift, scale, w_up, w_down, g_post, gate):
    b, n, d = x.shape
    th = MLP_TH
    nk = w_up.shape[1] // th
    tm = min(1024, n)
    nt = n // tm
    tiles = b * nt
    rc = tm // nk
    assert rc * nk == tm and rc % BF16_ROWS == 0

    def prev(t):
        return jnp.maximum(t - 1, 0)

    def nxt(t):
        return jnp.minimum(t + 1, tiles - 1)

    vec = pl.BlockSpec((1, d), lambda t, k: (0, 0))
    bvec = lambda sel: pl.BlockSpec((None, 1, d), lambda t, k: (sel(t) // nt, 0, 0))
    chunk = lambda sel: pl.BlockSpec((None, rc, d), lambda t, k: (sel(t) // nt, (sel(t) % nt) * nk + k, 0))
    out_chunk = pl.BlockSpec((None, rc, d), lambda t, k: (prev(t) // nt, (prev(t) % nt) * nk + jnp.where(t == 0, 0, k), 0))
    first = lambda t: 0 * t
    return pl.pallas_call(
        functools.partial(_mlp_kernel, rc=rc, tiles=tiles),
        out_shape=jax.ShapeDtypeStruct(x.shape, F32),
        grid=(tiles + 1, nk),
        in_specs=[pl.BlockSpec((None, tm, d), lambda t, k: (0, 0, 0), pipeline_mode=pl.Buffered(1)),
                  chunk(prev), chunk(nxt), vec, bvec(first), bvec(first), bvec(nxt), bvec(nxt),
                  pl.BlockSpec((d, th), lambda t, k: (0, k)),
                  pl.BlockSpec((th, d), lambda t, k: (k, 0)),
                  vec, bvec(prev)],
        out_specs=out_chunk,
        scratch_shapes=[pltpu.VMEM((tm, d), BF16)] * 2 + [pltpu.VMEM((tm, d), F32)] * 2,
        compiler_params=_params(("arbitrary", "arbitrary")),
        name="mlp",
    )(x, x, x, g_pre.reshape(1, d), shift, scale, shift, scale, w_up, w_down, g_post.reshape(1, d), gate)


def kernel(x, c, ctx, c_ctx, w_mod, b_mod, g_pre_mix, g_post_mix, g_pre_mlp, g_post_mlp, w_in, w_out, g_branch,
           attn_sink, hy_conv_w, hy_conv_b, hy_w1, hy_b1, hy_freq1, hy_w2, hy_b2, hy_freq2, hy_w3, hy_bias,
           pool_w, pool_scale, w_up, w_down):
    b, n, d = x.shape
    n_ctx = ctx.shape[1]
    depth = w_mod.shape[0]
    assert b == 2 and d == D_MODEL and n % 512 == 0 and n_ctx % BLOCK == 0

    cond = jnp.concatenate([c, c_ctx[None], jnp.zeros((SUBLANES - b - 1, d), F32)], axis=0)
    mods = _modulation(cond, w_mod, b_mod)

    w_in_b = _column_tiles(_permute_rope_columns(w_in.astype(BF16)), IN_TN)
    w_up_b = w_up.astype(BF16)
    w_out_b, w_down_b = w_out.astype(BF16), w_down.astype(BF16)
    tables_x, tables_c = _rope_tables(n), _rope_tables(n_ctx)
    feats_x, feats_c = _filter_features(n), _filter_features(n_ctx)
    deltas = _filter_deltas()
    tables = _stage2_tables(n)

    for i in range(depth):
        last = i == depth - 1
        hy = (hy_conv_w[i], hy_conv_b[i], hy_w1[i], hy_b1[i], hy_freq1[i], hy_w2[i], hy_b2[i], hy_freq2[i],
              hy_w3[i], hy_bias[i])
        mx = [m[:, None, :] for m in jnp.split(mods[i, :b], N_MOD, axis=-1)]
        mc = [jnp.broadcast_to(m[None, None, :], (b, 1, d)) for m in jnp.split(mods[i, b], N_MOD, axis=-1)]

        qkv_x, px = _in_projection(x, g_pre_mix[i], mx[0], mx[1], w_in_b[i], tables_x, rope=True)
        qkv_c, pc = _in_projection(ctx, g_pre_mix[i], mc[0], mc[1], w_in_b[i], tables_c, rope=False)
        kv_ctx = qkv_c[..., Q_END:V_END]

        attn_x = _attention(qkv_x, kv_ctx, attn_sink[i], local=True)
        hy_x = _hyena_latent(px, hy, feats_x, deltas, tables, n)
        po_x = _pool_mixer(px, pool_w[i], pool_scale[i])
        x = _out_projection(attn_x, hy_x, po_x, x, g_branch[i], w_out_b[i], g_post_mix[i], mx[2], group_major=True)
        x = _mlp(x, g_pre_mlp[i], mx[3], mx[4], w_up_b[i], w_down_b[i], g_post_mlp[i], mx[5])

        if not last:
            attn_c = _attention(qkv_c, kv_ctx, attn_sink[i], local=False)
            hy_c = _hyena_context(pc, hy, feats_c, deltas, n_ctx)
            po_c = _pool_mixer(pc, pool_w[i], pool_scale[i])
            ctx = _out_projection(attn_c, hy_c, po_c, ctx, g_branch[i], w_out_b[i], g_post_mix[i], mc[2],
                                  group_major=False)
            ctx = _mlp(ctx, g_pre_mlp[i], mc[3], mc[4], w_up_b[i], w_down_b[i], g_post_mlp[i], mc[5])
    return x
```

```python
import functools
import math

import numpy as np
import jax
import jax.numpy as jnp
from jax import lax
from jax.experimental import pallas as pl
from jax.experimental.pallas import tpu as pltpu

F32 = jnp.float32
BF16 = jnp.bfloat16

D_MODEL = 2048
DEPTH = 4
GRID_W = 64
ATTN_WIDTH = D_MODEL // 2
HYENA_WIDTH = D_MODEL // 4
POOL_WIDTH = D_MODEL - ATTN_WIDTH - HYENA_WIDTH
HEAD_DIM = 128
N_HEADS = ATTN_WIDTH // HEAD_DIM
N_KV_HEADS = 2
KV_GROUP = N_HEADS // N_KV_HEADS
KV_WIDTH = N_KV_HEADS * HEAD_DIM
WINDOW = 128
BLOCK = 128
ROPE_BASE = 10000.0
HYENA_EMB_DIM = 33
HYENA_FILTER_HIDDEN = 64
HYENA_FAST_DECAY_PCT = 0.3
HYENA_SLOW_DECAY_PCT = 1.5
HYENA_DECAY_TARGET = 1e-2
POOL_WINDOWS = (2, 4, 8, 16)
POOL_GROUP = POOL_WIDTH // len(POOL_WINDOWS)
MLP_HIDDEN = 4 * D_MODEL
N_MOD = 6
EPS = 1e-6
NEG_INF = -1e30

Q_END = ATTN_WIDTH
K_END = Q_END + KV_WIDTH
V_END = K_END + KV_WIDTH
HY_END = V_END + 3 * HYENA_WIDTH
IN_WIDTH = HY_END + POOL_WIDTH

LANES = 128
SUBLANES = 8
FFT_N2 = 128
POOL_HALO = 16
VMEM_LIMIT = 56 * 1024 * 1024


def _params(sem, vmem=VMEM_LIMIT):
    return pltpu.CompilerParams(dimension_semantics=sem, vmem_limit_bytes=vmem)


def _split(x):
    hi = x.astype(BF16)
    lo = (x - hi.astype(F32)).astype(BF16)
    return hi, lo


def _dot(a, b):
    return jnp.dot(a, b, preferred_element_type=F32)


def _dot3(ah, al, bh, bl):
    return _dot(ah, bh) + _dot(ah, bl) + _dot(al, bh)


def _rms(x, g):
    return x * lax.rsqrt(jnp.mean(x * x, axis=-1, keepdims=True) + EPS) * g


BF16_ROWS = 2 * SUBLANES


def _norm_scale_rows(x_ref, h_ref, a_ref, s_ref, rows, cols=None):
    cols = slice(None) if cols is None else cols
    nchunks = rows // BF16_ROWS

    def chunk(c):
        return pl.ds(pl.multiple_of(c * BF16_ROWS, BF16_ROWS), BF16_ROWS)

    def inv_rms(c):
        x = x_ref[chunk(c), :]
        return lax.rsqrt(jnp.mean(x * x, axis=-1, keepdims=True) + EPS)

    def body(c, inv):
        inv_next = inv_rms(jnp.minimum(c + 1, nchunks - 1))
        y = x_ref[chunk(c), :] * inv * a_ref[...]
        if s_ref is not None:
            y = y + s_ref[...]
        h_ref[chunk(c), cols] = y.astype(BF16)
        return inv_next

    lax.fori_loop(0, nchunks, body, inv_rms(0), unroll=8)


def _residual_norm_rows(x_ref, y_ref, o_ref, pg_ref, rows):
    nchunks = rows // SUBLANES

    def chunk(c):
        return pl.ds(pl.multiple_of(c * SUBLANES, SUBLANES), SUBLANES)

    def inv_rms(c):
        y = y_ref[chunk(c), :]
        return lax.rsqrt(jnp.mean(y * y, axis=-1, keepdims=True) + EPS)

    def body(c, inv):
        inv_next = inv_rms(jnp.minimum(c + 1, nchunks - 1))
        o_ref[chunk(c), :] = x_ref[chunk(c), :] + y_ref[chunk(c), :] * inv * pg_ref[...]
        return inv_next

    lax.fori_loop(0, nchunks, body, inv_rms(0), unroll=16)


def _mod_kernel(c_ref, w_ref, b_ref, o_ref):
    c = c_ref[...]
    s = c / (1.0 + jnp.exp(-c))
    sh, sl = _split(s)
    wh, wl = _split(w_ref[...])
    o_ref[...] = _dot3(sh, sl, wh, wl) + b_ref[...]


def _modulation(cond, w_mod, b_mod):
    depth, d, width = w_mod.shape
    tn = 1024
    return pl.pallas_call(
        _mod_kernel,
        out_shape=jax.ShapeDtypeStruct((depth, SUBLANES, width), F32),
        grid=(depth, width // tn),
        in_specs=[
            pl.BlockSpec((SUBLANES, d), lambda l, j: (0, 0)),
            pl.BlockSpec((None, d, tn), lambda l, j: (l, 0, j)),
            pl.BlockSpec((None, 1, tn), lambda l, j: (l, 0, j)),
        ],
        out_specs=pl.BlockSpec((None, SUBLANES, tn), lambda l, j: (l, 0, j)),
        compiler_params=_params(("parallel", "parallel")),
        name="modulation",
    )(cond, w_mod, b_mod.reshape(depth, 1, width))


QKV_WIDTH = V_END
REST_WIDTH = IN_WIDTH - V_END
SM_SCALE = HEAD_DIM ** -0.5


TABLE_Q, TABLE_K, TABLE_ID, TABLE_SCALE = range(4)


def _inproj_kernel(x0_ref, xn_ref, g_ref, sh0_ref, sc0_ref, shn_ref, scn_ref, w_ref, ca_ref, sa_ref, cb_ref, sb_ref,
                   qkv_ref, rest_ref, h0_ref, h1_ref, acc_ref, *, tm, tn, rc, tiles):
    t = pl.program_id(0)
    j = pl.program_id(1)

    @pl.when((t == 0) & (j == 0))
    def _():
        h0_ref[...] = (_rms(x0_ref[...], g_ref[...] * (1.0 + sc0_ref[...])) + sh0_ref[...]).astype(BF16)
        acc_ref[...] = jnp.zeros_like(acc_ref)

    def finish_previous():
        acc = acc_ref[...]
        rest_ref[...] = acc
        nch = tn // LANES
        for ch in range(nch):
            a = acc[:, ch * LANES:(ch + 1) * LANES]
            cos, sin = (ca_ref, sa_ref) if ch < nch // 2 else (cb_ref, sb_ref)
            qkv_ref[:, ch * LANES:(ch + 1) * LANES] = (a * cos[...] + pltpu.roll(a, HEAD_DIM // 2, 1) * sin[...]).astype(BF16)

    def step(slot):
        h_cur, h_oth = (h0_ref, h1_ref) if slot == 0 else (h1_ref, h0_ref)
        finish_previous()
        rows = pl.ds(pl.multiple_of(jnp.minimum(j * rc, tm - rc), BF16_ROWS), rc)
        a_next = g_ref[...] * (1.0 + scn_ref[...])
        h_oth[rows, :] = (_rms(xn_ref[rows, :], a_next) + shn_ref[...]).astype(BF16)
        acc_ref[...] = _dot(h_cur[...], w_ref[...])

    pl.when((t % 2 == 0) & (t < tiles))(functools.partial(step, 0))
    pl.when((t % 2 == 1) & (t < tiles))(functools.partial(step, 1))
    pl.when((t == tiles) & (j == 0))(finish_previous)


def _column_tiles(w, tn):
    *lead, d, width = w.shape
    return jnp.swapaxes(w.reshape(*lead, d, width // tn, tn), -3, -2)


IN_TN = 512
QKV_SPARE = QKV_WIDTH // IN_TN
REST_SPARE = REST_WIDTH // IN_TN


def _in_projection(x, g, shift, scale, w, tables, rope):
    b, n, d = x.shape
    tm = min(1024, n)
    tn = w.shape[-1]
    nj = w.shape[0]
    nt = n // tm
    tiles = b * nt
    half = tn // 2
    assert tn == IN_TN and nj * tn == IN_WIDTH and Q_END % tn == 0 and K_END % tn == half and V_END % tn == 0
    rc = -(-tm // nj)
    rc = -(-rc // (2 * BF16_ROWS)) * 2 * BF16_ROWS
    nq, nqkv = Q_END // tn, QKV_WIDTH // tn

    def nxt(t):
        return jnp.minimum(t + 1, tiles - 1)

    def lagged(t, j):
        step = jnp.clip(t * nj + j - 1, 0, tiles * nj - 1)
        return step // nj, step % nj

    def kind_a(j):
        return jnp.where(j < nq, TABLE_Q if rope else TABLE_SCALE, jnp.where(j < nqkv, TABLE_K if rope else TABLE_ID, TABLE_ID))

    def kind_b(j):
        return jnp.where(j < nq, TABLE_Q if rope else TABLE_SCALE, TABLE_ID)

    vec = pl.BlockSpec((1, d), lambda t, j: (0, 0))
    bvec = lambda sel: pl.BlockSpec((None, 1, d), lambda t, j: (sel(t) // nt, 0, 0))
    first = lambda t: 0 * t
    def table(kind):
        def index(t, j):
            tp, jp = lagged(t, j)
            k = kind(jp)
            return k, jnp.where(k == TABLE_ID, 0, tp % nt), 0
        return pl.BlockSpec((None, tm, LANES), index)

    def out_spec(col_tile):
        return pl.BlockSpec((None, tm, tn), lambda t, j: (lagged(t, j)[0] // nt, lagged(t, j)[0] % nt,
                                                          col_tile(lagged(t, j)[1])))

    cos, sin = tables
    return pl.pallas_call(
        functools.partial(_inproj_kernel, tm=tm, tn=tn, rc=rc, tiles=tiles),
        out_shape=[jax.ShapeDtypeStruct((b, n, QKV_WIDTH + tn), BF16), jax.ShapeDtypeStruct((b, n, REST_WIDTH + tn), F32)],
        grid=(tiles + 1, nj),
        in_specs=[pl.BlockSpec((None, tm, d), lambda t, j: (0, 0, 0), pipeline_mode=pl.Buffered(1)),
                  pl.BlockSpec((None, tm, d), lambda t, j: (nxt(t) // nt, nxt(t) % nt, 0)),
                  vec, bvec(first), bvec(first), bvec(nxt), bvec(nxt),
                  pl.BlockSpec((None, d, tn), lambda t, j: (jnp.where(t < tiles, j, nj - 1), 0, 0)),
                  table(kind_a), table(kind_a), table(kind_b), table(kind_b)],
        out_specs=[out_spec(lambda jp: jnp.minimum(jp, QKV_SPARE)),
                   out_spec(lambda jp: jnp.where(jp >= nqkv, jp - nqkv, REST_SPARE))],
        scratch_shapes=[pltpu.VMEM((tm, d), BF16)] * 2 + [pltpu.VMEM((tm, tn), F32)],
        compiler_params=_params(("arbitrary", "arbitrary")),
        name="in_projection_rope" if rope else "in_projection",
    )(x, x, g.reshape(1, d), shift, scale, shift, scale, w, cos, sin, cos, sin)


def _permute_rope_columns(w_in):
    lead = w_in.shape[:-1]
    quarter = HEAD_DIM // 4
    qk = w_in[..., :K_END].reshape(*lead, K_END // HEAD_DIM, 2, 2, quarter)
    qk = jnp.swapaxes(qk, -3, -2).reshape(*lead, K_END)
    return jnp.concatenate([qk, w_in[..., K_END:]], axis=-1)


def _rope_tables(n):
    quarter = HEAD_DIM // 4
    inv_freq = ROPE_BASE ** (-jnp.arange(quarter, dtype=F32) / quarter)
    t = jnp.arange(n, dtype=jnp.int32)
    row = (t // GRID_W).astype(F32)[:, None] * inv_freq[None, :]
    col = (t % GRID_W).astype(F32)[:, None] * inv_freq[None, :]
    cos = jnp.concatenate([jnp.cos(row), jnp.cos(col), jnp.cos(row), jnp.cos(col)], axis=-1)
    sin = jnp.concatenate([-jnp.sin(row), -jnp.sin(col), jnp.sin(row), jnp.sin(col)], axis=-1)
    one, zero = jnp.ones_like(cos), jnp.zeros_like(sin)
    return (jnp.stack([cos * SM_SCALE, cos, one, one * SM_SCALE]), jnp.stack([sin * SM_SCALE, sin, zero, zero]))


def _softmax_pv(parts, sink_col):
    m = sink_col
    for s, _ in parts:
        m = jnp.maximum(m, jnp.max(s, axis=-1, keepdims=True))
    den = jnp.exp(sink_col - m)
    out = None
    for s, v in parts:
        p = jnp.exp(s - m)
        den = den + jnp.sum(p, axis=-1, keepdims=True)
        pv = _dot(p.astype(BF16), v)
        out = pv if out is None else out + pv
    return out / den


def _ctx_attn_kernel(sink_ref, q_ref, kvc_ref, o_ref, *, tq):
    for g in range(N_KV_HEADS):
        heads = [g * KV_GROUP + h for h in range(KV_GROUP)]
        qg = jnp.concatenate([q_ref[:, h * HEAD_DIM:(h + 1) * HEAD_DIM] for h in heads], axis=0)
        sink_col = jnp.concatenate([jnp.full((tq, 1), sink_ref[h], F32) for h in heads], axis=0)
        kc = kvc_ref[:, g * HEAD_DIM:(g + 1) * HEAD_DIM]
        vc = kvc_ref[:, KV_WIDTH + g * HEAD_DIM:KV_WIDTH + (g + 1) * HEAD_DIM]
        s = lax.dot_general(qg, kc, (((1,), (1,)), ((), ())), preferred_element_type=F32)
        o = _softmax_pv([(s, vc)], sink_col)
        for hi, h in enumerate(heads):
            o_ref[:, h * HEAD_DIM:(h + 1) * HEAD_DIM] = o[hi * tq:(hi + 1) * tq]


ATTN_ROWS = KV_GROUP * BLOCK
SOFTMAX_CHUNK = 32


def _win_attn_kernel(sink_ref, q_ref, km_ref, kp_ref, kn_ref, vm_ref, vp_ref, vn_ref, kvc_ref, band_ref,
                     o_ref, ktw, vw, s_scr, p_scr, m_scr, *, tq, nb, nctx):
    i = pl.program_id(1)
    nsub = tq // BLOCK
    nloc = 3 * BLOCK

    def transposed(x):
        return x.astype(F32).T.astype(BF16)

    def block_rows(main_ref, prev_ref, next_ref, w, lanes):
        if w == 0:
            return prev_ref[:, lanes]
        if w == nsub + 1:
            return next_ref[:, lanes]
        return main_ref[(w - 1) * BLOCK:w * BLOCK, lanes]

    ones = jnp.ones((nloc + nctx, HEAD_DIM), BF16)
    for g in range(N_KV_HEADS):
        lanes = slice(g * HEAD_DIM, (g + 1) * HEAD_DIM)
        vlanes = slice(KV_WIDTH + g * HEAD_DIM, KV_WIDTH + (g + 1) * HEAD_DIM)
        kts = [transposed(block_rows(km_ref, kp_ref, kn_ref, w, lanes)) for w in range(nsub + 2)]
        kct = [transposed(kvc_ref[cb * BLOCK:(cb + 1) * BLOCK, lanes]) for cb in range(nctx // BLOCK)]
        for jb in range(nsub):
            for w in range(3):
                ktw[jb, g, :, w * BLOCK:(w + 1) * BLOCK] = kts[jb + w]
                vw[jb, g, w * BLOCK:(w + 1) * BLOCK, :HEAD_DIM] = block_rows(vm_ref, vp_ref, vn_ref, jb + w, lanes)
            for cb in range(nctx // BLOCK):
                ktw[jb, g, :, nloc + cb * BLOCK:nloc + (cb + 1) * BLOCK] = kct[cb]
            vw[jb, g, nloc:, :HEAD_DIM] = kvc_ref[:, vlanes]
            vw[jb, g, :, HEAD_DIM:] = ones

    col = lax.broadcasted_iota(jnp.int32, (1, nloc), 1)

    def rows_of(jb):
        start = jb * BLOCK
        return pl.ds(start if isinstance(start, int) else pl.multiple_of(start, BLOCK), BLOCK)

    def stage_a(jb, g):
        qg = jnp.concatenate([q_ref[rows_of(jb), (g * KV_GROUP + h) * HEAD_DIM:(g * KV_GROUP + h + 1) * HEAD_DIM]
                              for h in range(KV_GROUP)], axis=0)
        s_scr[g] = _dot(qg, ktw[jb, g])

    def stage_b(jb, g):
        blk = i * nsub + jb
        pen_prev = jnp.where(blk == 0, NEG_INF, 0.0).astype(F32)
        pen_next = jnp.where(blk == nb - 1, NEG_INF, 0.0).astype(F32)
        rowbias = jnp.where(col < BLOCK, pen_prev, jnp.where(col >= 2 * BLOCK, pen_next, 0.0))
        for c in range(ATTN_ROWS // SOFTMAX_CHUNK):
            rows = slice(c * SOFTMAX_CHUNK, (c + 1) * SOFTMAX_CHUNK)
            sink = sink_ref[g * KV_GROUP + (c * SOFTMAX_CHUNK) // BLOCK]
            s_loc = s_scr[g, rows, :nloc] + band_ref[rows, :] + rowbias
            s_ctx = s_scr[g, rows, nloc:]
            m = jnp.maximum(jnp.max(s_loc, axis=-1, keepdims=True), jnp.max(s_ctx, axis=-1, keepdims=True))
            m = jnp.maximum(m, sink)
            p_scr[g, rows, :nloc] = jnp.exp(s_loc - m).astype(BF16)
            p_scr[g, rows, nloc:] = jnp.exp(s_ctx - m).astype(BF16)
            m_scr[g, rows, :] = m

    def stage_c(jb, g):
        o = _dot(p_scr[g], vw[jb, g])
        for hi in range(KV_GROUP):
            h = g * KV_GROUP + hi
            rows = slice(hi * BLOCK, (hi + 1) * BLOCK)
            den = o[rows, HEAD_DIM:HEAD_DIM + 1] + jnp.exp(sink_ref[h] - m_scr[g, rows, :])
            o_ref[rows_of(jb), h * HEAD_DIM:(h + 1) * HEAD_DIM] = o[rows, :HEAD_DIM] / den

    stage_a(0, 0)
    stage_a(0, 1)
    stage_b(0, 0)

    def body(j, carry):
        stage_a(j, 0)
        stage_c(j - 1, 0)
        stage_b(j - 1, 1)
        stage_a(j, 1)
        stage_c(j - 1, 1)
        stage_b(j, 0)
        return carry

    lax.fori_loop(1, nsub, body, 0)
    stage_c(nsub - 1, 0)
    stage_b(nsub - 1, 1)
    stage_c(nsub - 1, 1)


def _band_bias():
    qi = np.arange(ATTN_ROWS)[:, None] % BLOCK
    sj = np.arange(3 * BLOCK)[None, :]
    return jnp.asarray(np.where(np.abs(sj - BLOCK - qi) <= WINDOW, 0.0, NEG_INF), dtype=F32)


def _attention(qkv, kv_ctx, sink, local):
    b, n, _ = qkv.shape
    c = kv_ctx.shape[1]
    tq = min(1024 if local else 512, n)
    nsub = tq // BLOCK
    nb = n // BLOCK
    kcol = Q_END // KV_WIDTH
    vcol = K_END // KV_WIDTH
    smem = pl.BlockSpec(memory_space=pltpu.SMEM)
    q_spec = pl.BlockSpec((None, tq, ATTN_WIDTH), lambda bi, i: (bi, i, 0))
    kvc_spec = pl.BlockSpec((None, c, 2 * KV_WIDTH), lambda bi, i: (bi, 0, 0))
    if local:
        def main(colblk):
            return pl.BlockSpec((None, tq, KV_WIDTH), lambda bi, i: (bi, i, colblk))

        def prev(colblk):
            return pl.BlockSpec((None, BLOCK, KV_WIDTH), lambda bi, i: (bi, jnp.maximum(i * nsub - 1, 0), colblk))

        def nxt(colblk):
            return pl.BlockSpec((None, BLOCK, KV_WIDTH), lambda bi, i: (bi, jnp.minimum((i + 1) * nsub, nb - 1), colblk))

        keys = 3 * BLOCK + c
        kern = functools.partial(_win_attn_kernel, tq=tq, nb=nb, nctx=c)
        in_specs = [smem, q_spec, main(kcol), prev(kcol), nxt(kcol), main(vcol), prev(vcol), nxt(vcol), kvc_spec,
                    pl.BlockSpec((ATTN_ROWS, 3 * BLOCK), lambda bi, i: (0, 0))]
        args = [sink, qkv, qkv, qkv, qkv, qkv, qkv, qkv, kv_ctx, _band_bias()]
        scratch = [pltpu.VMEM((nsub, N_KV_HEADS, HEAD_DIM, keys), BF16),
                   pltpu.VMEM((nsub, N_KV_HEADS, keys, 2 * HEAD_DIM), BF16),
                   pltpu.VMEM((2, ATTN_ROWS, keys), F32),
                   pltpu.VMEM((2, ATTN_ROWS, keys), BF16),
                   pltpu.VMEM((2, ATTN_ROWS, 1), F32)]
    else:
        kern = functools.partial(_ctx_attn_kernel, tq=tq)
        in_specs = [smem, q_spec, kvc_spec]
        args = [sink, qkv, kv_ctx]
        scratch = []
    return pl.pallas_call(
        kern,
        out_shape=jax.ShapeDtypeStruct((b, n, ATTN_WIDTH), F32),
        grid=(b, n // tq),
        in_specs=in_specs,
        out_specs=pl.BlockSpec((None, tq, ATTN_WIDTH), lambda bi, i: (bi, i, 0)),
        scratch_shapes=scratch,
        compiler_params=_params(("parallel", "parallel")),
        name="window_attention" if local else "context_attention",
    )(*args)


GROUPS = FFT_N2 // SUBLANES


def _gm_shape(lead, t1, width):
    return (*lead, width // LANES, GROUPS, t1, SUBLANES, LANES)


def _store_group_major(o_ref, val, tl):
    for t1 in range(tl // FFT_N2):
        for jg in range(GROUPS):
            r0 = (t1 * GROUPS + jg) * SUBLANES
            for cc in range(val.shape[1] // LANES):
                o_ref[cc, jg, t1] = val[r0:r0 + SUBLANES, cc * LANES:(cc + 1) * LANES]


def _hyena_prep_kernel(u_ref, p_ref, n_ref, w_ref, b_ref, vx_ref, x0_ref, *, tl, nt, group_major):
    i = pl.program_id(1)
    u = u_ref[...]
    prev_row = jnp.where(i > 0, p_ref[SUBLANES - 1:SUBLANES, :], 0.0)
    next_row = jnp.where(i < nt - 1, n_ref[0:1, :], 0.0)
    row = lax.broadcasted_iota(jnp.int32, u.shape, 0)
    um = jnp.where(row == 0, prev_row, pltpu.roll(u, 1, 0))
    up = jnp.where(row == tl - 1, next_row, pltpu.roll(u, tl - 1, 0))
    z = um * w_ref[0:1, :] + u * w_ref[1:2, :] + up * w_ref[2:3, :] + b_ref[...]
    x0 = z[:, :HYENA_WIDTH]
    vx = z[:, 2 * HYENA_WIDTH:] * z[:, HYENA_WIDTH:2 * HYENA_WIDTH]
    if group_major:
        _store_group_major(vx_ref, vx, tl)
        _store_group_major(x0_ref, x0, tl)
    else:
        vx_ref[...] = vx
        x0_ref[...] = x0


def _hyena_prep(p, conv_w, conv_b, group_major):
    b, n, _ = p.shape
    tl = min(512, n)
    nt = n // tl
    hw = 3 * HYENA_WIDTH
    colblk = 0
    nrow8 = n // SUBLANES
    per = tl // SUBLANES
    if group_major:
        shape = _gm_shape((b,), n // FFT_N2, HYENA_WIDTH)
        out_spec = pl.BlockSpec((None, *_gm_shape((), tl // FFT_N2, HYENA_WIDTH)), lambda bi, i: (bi, 0, 0, i, 0, 0))
    else:
        shape = (b, n, HYENA_WIDTH)
        out_spec = pl.BlockSpec((None, tl, HYENA_WIDTH), lambda bi, i: (bi, i, 0))
    return pl.pallas_call(
        functools.partial(_hyena_prep_kernel, tl=tl, nt=nt, group_major=group_major),
        out_shape=[jax.ShapeDtypeStruct(shape, F32)] * 2,
        grid=(b, nt),
        in_specs=[
            pl.BlockSpec((None, tl, hw), lambda bi, i: (bi, i, colblk)),
            pl.BlockSpec((None, SUBLANES, hw), lambda bi, i: (bi, jnp.maximum(i * per - 1, 0), colblk)),
            pl.BlockSpec((None, SUBLANES, hw), lambda bi, i: (bi, jnp.minimum((i + 1) * per, nrow8 - 1), colblk)),
            pl.BlockSpec((3, hw), lambda bi, i: (0, 0)),
            pl.BlockSpec((1, hw), lambda bi, i: (0, 0)),
        ],
        out_specs=[out_spec, out_spec],
        compiler_params=_params(("parallel", "parallel")),
        name="hyena_prep",
    )(p, p, p, conv_w, conv_b.reshape(1, hw))


def _filter_kernel(ft_ref, w1_ref, b1_ref, f1_ref, w2_ref, b2_ref, f2_ref, w3_ref, dl_ref, h_ref, s_ref,
                   *, tl, n, group_major):
    i = pl.program_id(0)

    def dense(a, w_ref):
        ah, al = _split(a)
        wh, wl = _split(w_ref[...])
        return _dot3(ah, al, wh, wl)

    h = jnp.sin(f1_ref[...] * (dense(ft_ref[...], w1_ref) + b1_ref[...]))
    h = jnp.sin(f2_ref[...] * (dense(h, w2_ref) + b2_ref[...]))
    h = dense(h, w3_ref)
    t = (i * tl + lax.broadcasted_iota(jnp.int32, (tl, HYENA_WIDTH), 0)).astype(F32) / float(n - 1)
    decay = jnp.exp(-t * dl_ref[...])
    h = h * jnp.concatenate([decay, decay], axis=1)
    if group_major:
        _store_group_major(h_ref, h, tl)
    else:
        h_ref[...] = h

    @pl.when(i == 0)
    def _():
        s_ref[...] = jnp.zeros_like(s_ref)

    s_ref[...] += jnp.sum(jnp.abs(h).reshape(tl // SUBLANES, SUBLANES, 2 * HYENA_WIDTH), axis=0)


def _filter_features(n):
    t = jnp.linspace(0.0, 1.0, n, dtype=F32)[:, None]
    bands = (HYENA_EMB_DIM - 1) // 2
    omega = 2.0 * math.pi * jnp.arange(n, dtype=F32)[:, None] / n
    f = jnp.linspace(1e-4, bands - 1, bands, dtype=F32)[None, :]
    feats = jnp.concatenate([t, jnp.cos(f * omega), -jnp.sin(f * omega)], axis=-1)
    return jnp.pad(feats, ((0, 0), (0, LANES - HYENA_EMB_DIM)))


def _filter_deltas():
    max_decay = math.log(HYENA_DECAY_TARGET) / HYENA_FAST_DECAY_PCT
    min_decay = math.log(HYENA_DECAY_TARGET) / HYENA_SLOW_DECAY_PCT
    return jnp.abs(jnp.linspace(min_decay, max_decay, HYENA_WIDTH, dtype=F32)).reshape(1, HYENA_WIDTH)


def _hyena_filter(n, feats, deltas, w1, b1, f1, w2, b2, f2, w3, group_major):
    tl = min(512, n)
    hid = LANES
    pad_h = hid - HYENA_FILTER_HIDDEN
    w1p = jnp.pad(w1, ((0, LANES - HYENA_EMB_DIM), (0, pad_h)))
    w2p = jnp.pad(w2, ((0, pad_h), (0, pad_h)))
    w3p = jnp.pad(w3, ((0, pad_h), (0, 0)))
    vec = lambda v: jnp.pad(v, (0, pad_h)).reshape(1, hid)
    hw2 = 2 * HYENA_WIDTH
    full = lambda shape: pl.BlockSpec(shape, lambda i: (0,) * len(shape))
    if group_major:
        shape = _gm_shape((), n // FFT_N2, hw2)
        out_spec = pl.BlockSpec(_gm_shape((), tl // FFT_N2, hw2), lambda i: (0, 0, i, 0, 0))
    else:
        shape = (n, hw2)
        out_spec = pl.BlockSpec((tl, hw2), lambda i: (i, 0))
    return pl.pallas_call(
        functools.partial(_filter_kernel, tl=tl, n=n, group_major=group_major),
        out_shape=[jax.ShapeDtypeStruct(shape, F32), jax.ShapeDtypeStruct((SUBLANES, hw2), F32)],
        grid=(n // tl,),
        in_specs=[pl.BlockSpec((tl, LANES), lambda i: (i, 0)), full((LANES, hid)), full((1, hid)), full((1, hid)),
                  full((hid, hid)), full((1, hid)), full((1, hid)), full((hid, hw2)), full((1, HYENA_WIDTH))],
        out_specs=[out_spec, full((SUBLANES, hw2))],
        compiler_params=_params(("arbitrary",)),
        name="hyena_filter",
    )(feats, w1p, vec(b1), vec(f1), w2p, vec(b2), vec(f2), w3p, deltas)


def _stack_complex(m):
    return np.block([[m.real, -m.imag], [m.imag, m.real]])


def _hilo(m):
    m = jnp.asarray(m, dtype=F32)
    return _split(m)


@functools.lru_cache(maxsize=None)
def _fft_constants(n):
    m = 2 * n
    n2 = FFT_N2
    n1 = m // n2
    n1h = n1 // 2
    k1 = np.arange(n1)
    t1 = np.arange(n1h)
    f1 = np.exp(-2j * np.pi * np.outer(k1, t1) / n1)
    f3 = np.exp(2j * np.pi * np.outer(t1, k1) / n1) / m
    k2 = np.arange(n2)
    t2 = np.arange(n2)
    w2 = np.exp(-2j * np.pi * np.outer(k2, t2) / n2)
    tw = np.exp(-2j * np.pi * np.outer(k1, t2) / m)
    return dict(
        n1=n1, n1h=n1h,
        f1c=_stack_complex(f1), f1r=np.concatenate([f1.real, f1.imag], axis=0),
        f3c=_stack_complex(f3),
        w2r=w2.real.astype(np.float32), w2i=w2.imag.astype(np.float32),
        twr=tw.real.astype(np.float32), twi=tw.imag.astype(np.float32),
    )


def _stage2_tables(n):
    c = _fft_constants(n)
    w2r, w2i = jnp.asarray(c["w2r"])[None], jnp.asarray(c["w2i"])[None]
    twr, twi = jnp.asarray(c["twr"])[:, None, :], jnp.asarray(c["twi"])[:, None, :]
    gr = w2r * twr - w2i * twi
    gi = w2r * twi + w2i * twr
    g = jnp.concatenate([jnp.concatenate([gr, -gi], axis=2), jnp.concatenate([gi, gr], axis=2)], axis=1)
    return _split(g)


def _s1_kernel(x_ref, fh_ref, fl_ref, o_ref, *, nparts, ncw, n1, n1h):
    fh = fh_ref[...]
    fl = fl_ref[...]
    for r in range(SUBLANES):
        rows = pl.ds(r, n1h, stride=SUBLANES)
        xs = jnp.concatenate(
            [jnp.concatenate([x_ref[p, cc, rows, :] for p in range(nparts)], axis=0) for cc in range(ncw)], axis=1)
        xh, xl = _split(xs)
        res = _dot3(fh, fl, xh, xl)
        for ri in range(2):
            for cc in range(ncw):
                o_ref[ri, cc, pl.ds(r, n1, stride=SUBLANES), :] = res[ri * n1:(ri + 1) * n1, cc * LANES:(cc + 1) * LANES]


def _fft_stage1(x, fmat, n1, n1h, ncw):
    nparts, ncc, groups = x.shape[:3]
    fh, fl = _hilo(fmat)
    return pl.pallas_call(
        functools.partial(_s1_kernel, nparts=nparts, ncw=ncw, n1=n1, n1h=n1h),
        out_shape=jax.ShapeDtypeStruct((2, ncc, groups, n1 * SUBLANES, LANES), F32),
        grid=(groups, ncc // ncw),
        in_specs=[
            pl.BlockSpec((nparts, ncw, None, n1h * SUBLANES, LANES), lambda j, ci: (0, ci, j, 0, 0)),
            pl.BlockSpec(fh.shape, lambda j, ci: (0, 0)),
            pl.BlockSpec(fl.shape, lambda j, ci: (0, 0)),
        ],
        out_specs=pl.BlockSpec((2, ncw, None, n1 * SUBLANES, LANES), lambda j, ci: (0, ci, j, 0, 0)),
        compiler_params=_params(("parallel", "parallel")),
        name="fft_stage1",
    )(x, fh, fl)


def _load_k1(a_ref, q):
    ncc = a_ref.shape[1]
    return jnp.concatenate([a_ref[:, cc, :, q].reshape(2 * FFT_N2, LANES) for cc in range(ncc)], axis=1)


def _filter_spectrum_kernel(a_ref, gh_ref, gl_ref, s_ref, o_ref, *, kg):
    s = jnp.sum(s_ref[...], axis=0, keepdims=True)
    inv = 1.0 / (s[:, :HYENA_WIDTH] + s[:, HYENA_WIDTH:])
    half = FFT_N2
    for q in range(kg):
        ah, al = _split(_load_k1(a_ref, q))
        h = _dot3(gh_ref[q], gl_ref[q], ah, al)
        hf = h[:, :HYENA_WIDTH]
        hb = h[:, HYENA_WIDTH:]
        o_ref[q, :half] = (hf[:half] + hb[:half]) * inv
        o_ref[q, half:] = (hf[half:] - hb[half:]) * inv


def _k1_spec(kg, width):
    return pl.BlockSpec((2, width // LANES, GROUPS, kg, SUBLANES, LANES), lambda i: (0, 0, 0, i, 0, 0))


def _filter_spectrum(a, gh, gl, sums, n1):
    kg = 4
    a6 = a.reshape(2, 2 * HYENA_WIDTH // LANES, GROUPS, n1, SUBLANES, LANES)
    tspec = pl.BlockSpec((kg, 2 * FFT_N2, 2 * FFT_N2), lambda i: (i, 0, 0))
    return pl.pallas_call(
        functools.partial(_filter_spectrum_kernel, kg=kg),
        out_shape=jax.ShapeDtypeStruct((n1, 2 * FFT_N2, HYENA_WIDTH), F32),
        grid=(n1 // kg,),
        in_specs=[_k1_spec(kg, 2 * HYENA_WIDTH), tspec, tspec,
                  pl.BlockSpec((SUBLANES, 2 * HYENA_WIDTH), lambda i: (0, 0))],
        out_specs=pl.BlockSpec((kg, 2 * FFT_N2, HYENA_WIDTH), lambda i: (i, 0, 0)),
        compiler_params=_params(("parallel",)),
        name="filter_spectrum",
    )(a6, gh, gl, sums)


def _dot_t(a, b):
    return lax.dot_general(a, b, (((0,), (0,)), ((), ())), preferred_element_type=F32)


def _s2_kernel(a_ref, kf_ref, gh_ref, gl_ref, o_ref, *, kg):
    half = FFT_N2
    for q in range(kg):
        ah, al = _split(_load_k1(a_ref, q))
        gh, gl = gh_ref[q], gl_ref[q]
        x = _dot3(gh, gl, ah, al)
        xr, xi = x[:half], x[half:]
        kr, ki = kf_ref[q, :half], kf_ref[q, half:]
        y = jnp.concatenate([xr * kr - xi * ki, xr * ki + xi * kr], axis=0)
        yh, yl = _split(y)
        bt = _dot_t(gh, yh) + _dot_t(gh, yl) + _dot_t(gl, yh)
        for cc in range(HYENA_WIDTH // LANES):
            o_ref[:, cc, :, q] = bt[:, cc * LANES:(cc + 1) * LANES].reshape(2, GROUPS, SUBLANES, LANES)


def _fft_stage2(a, kf, tables, n1):
    kg = 4
    a6 = a.reshape(2, HYENA_WIDTH // LANES, GROUPS, n1, SUBLANES, LANES)
    tspec = pl.BlockSpec((kg, 2 * FFT_N2, 2 * FFT_N2), lambda i: (i, 0, 0))
    dspec = _k1_spec(kg, HYENA_WIDTH)
    out = pl.pallas_call(
        functools.partial(_s2_kernel, kg=kg),
        out_shape=jax.ShapeDtypeStruct(a6.shape, F32),
        grid=(n1 // kg,),
        in_specs=[dspec, pl.BlockSpec((kg, 2 * FFT_N2, HYENA_WIDTH), lambda i: (i, 0, 0)), tspec, tspec],
        out_specs=dspec,
        compiler_params=_params(("parallel",)),
        name="fft_stage2",
    )(a6, kf, *tables)
    return out.reshape(a.shape)


def _s3_kernel(b_ref, vx_ref, x0_ref, bias_ref, fh_ref, fl_ref, o_ref, *, ncw, n1, n1h):
    fh = fh_ref[...]
    fl = fl_ref[...]
    for r in range(SUBLANES):
        krows = pl.ds(r, n1, stride=SUBLANES)
        z = jnp.concatenate(
            [jnp.concatenate([b_ref[ri, cc, krows, :] for ri in range(2)], axis=0) for cc in range(ncw)], axis=1)
        zh, zl = _split(z)
        y = _dot3(fh, fl, zh, zl)
        trows = pl.ds(r, n1h, stride=SUBLANES)
        for p in range(2):
            for cc in range(ncw):
                yy = y[p * n1h:(p + 1) * n1h, cc * LANES:(cc + 1) * LANES]
                o_ref[p, cc, trows, :] = (yy + vx_ref[p, cc, trows, :] * bias_ref[cc]) * x0_ref[p, cc, trows, :]


def _fft_stage3(bt, vx, x0, bias, fmat, n1, n1h, ncw):
    ncc, groups = bt.shape[1:3]
    fh, fl = _hilo(fmat)
    tspec = pl.BlockSpec((2, ncw, None, n1h * SUBLANES, LANES), lambda j, ci: (0, ci, j, 0, 0))
    return pl.pallas_call(
        functools.partial(_s3_kernel, ncw=ncw, n1=n1, n1h=n1h),
        out_shape=jax.ShapeDtypeStruct(vx.shape, F32),
        grid=(groups, ncc // ncw),
        in_specs=[
            pl.BlockSpec((2, ncw, None, n1 * SUBLANES, LANES), lambda j, ci: (0, ci, j, 0, 0)),
            tspec, tspec,
            pl.BlockSpec((ncw, 1, LANES), lambda j, ci: (ci, 0, 0)),
            pl.BlockSpec(fh.shape, lambda j, ci: (0, 0)),
            pl.BlockSpec(fl.shape, lambda j, ci: (0, 0)),
        ],
        out_specs=tspec,
        compiler_params=_params(("parallel", "parallel")),
        name="fft_stage3",
    )(bt, vx, x0, bias, fh, fl)


def _hyena_latent(p, hy, feats, deltas, tables, n):
    conv_w, conv_b, w1, b1, f1, w2, b2, f2, w3, bias = hy
    c = _fft_constants(n)
    n1, n1h = c["n1"], c["n1h"]
    vx, x0 = _hyena_prep(p, conv_w, conv_b, group_major=True)
    gm_shape = vx.shape
    rows = lambda a: a.reshape(*a.shape[:-3], n1h * SUBLANES, LANES)
    vx, x0 = rows(vx), rows(x0)
    taps, sums = _hyena_filter(n, feats, deltas, w1, b1, f1, w2, b2, f2, w3, group_major=True)
    ncw = 2
    a_f = _fft_stage1(rows(taps)[None], c["f1r"], n1, n1h, ncw)
    kf = _filter_spectrum(a_f, tables[0], tables[1], sums, n1)
    a = _fft_stage1(vx, c["f1c"], n1, n1h, ncw)
    bt = _fft_stage2(a, kf, tables, n1)
    out = _fft_stage3(bt, vx, x0, bias.reshape(HYENA_WIDTH // LANES, 1, LANES), c["f3c"], n1, n1h, ncw)
    return out.reshape(gm_shape)


@functools.lru_cache(maxsize=None)
def _dense_dft_constants(n):
    m = 2 * n
    k = np.arange(m)
    t = np.arange(n)
    f = np.exp(-2j * np.pi * np.outer(k, t) / m)
    finv = np.exp(2j * np.pi * np.outer(t, k) / m) / m
    return _stack_complex(f), np.concatenate([f.real, f.imag], axis=0), _stack_complex(finv)


def _dense_conv_kernel(vx_ref, x0_ref, hf_ref, hb_ref, sf_ref, sb_ref, bias_ref,
                       fch, fcl, frh, frl, fih, fil, o_ref, *, n):
    m = 2 * n
    z = jnp.concatenate([vx_ref[0], vx_ref[1]], axis=0)
    zh, zl = _split(z)
    zf = _dot3(fch[...], fcl[...], zh, zl)
    hfh, hfl = _split(hf_ref[...])
    hbh, hbl = _split(hb_ref[...])
    hf = _dot3(frh[...], frl[...], hfh, hfl)
    hb = _dot3(frh[...], frl[...], hbh, hbl)
    inv = 1.0 / (jnp.sum(sf_ref[...], axis=0, keepdims=True) + jnp.sum(sb_ref[...], axis=0, keepdims=True))
    kr = (hf[:m] + hb[:m]) * inv
    ki = (hf[m:] - hb[m:]) * inv
    zr, zi = zf[:m], zf[m:]
    y = jnp.concatenate([zr * kr - zi * ki, zr * ki + zi * kr], axis=0)
    yh, yl = _split(y)
    out = _dot3(fih[...], fil[...], yh, yl)
    bias = bias_ref[...]
    for p in range(2):
        o_ref[p] = (out[p * n:(p + 1) * n] + vx_ref[p] * bias) * x0_ref[p]


def _hyena_context(p, hy, feats, deltas, n):
    conv_w, conv_b, w1, b1, f1, w2, b2, f2, w3, bias = hy
    vx, x0 = _hyena_prep(p, conv_w, conv_b, group_major=False)
    taps, sums = _hyena_filter(n, feats, deltas, w1, b1, f1, w2, b2, f2, w3, group_major=False)
    fc, fr, fi = _dense_dft_constants(n)
    mats = [*_hilo(fc), *_hilo(fr), *_hilo(fi)]
    cw = 256
    nct = HYENA_WIDTH // cw
    dspec = pl.BlockSpec((2, n, cw), lambda ci: (0, 0, ci))
    return pl.pallas_call(
        functools.partial(_dense_conv_kernel, n=n),
        out_shape=jax.ShapeDtypeStruct(vx.shape, F32),
        grid=(nct,),
        in_specs=[dspec, dspec,
                  pl.BlockSpec((n, cw), lambda ci: (0, ci)), pl.BlockSpec((n, cw), lambda ci: (0, nct + ci)),
                  pl.BlockSpec((SUBLANES, cw), lambda ci: (0, ci)), pl.BlockSpec((SUBLANES, cw), lambda ci: (0, nct + ci)),
                  pl.BlockSpec((1, cw), lambda ci: (0, ci))]
                 + [pl.BlockSpec(mt.shape, lambda ci: (0, 0)) for mt in mats],
        out_specs=dspec,
        compiler_params=_params(("parallel",)),
        name="context_long_conv",
    )(vx, x0, taps, taps, sums, sums, bias.reshape(1, HYENA_WIDTH), *mats)


def _pool_kernel(x_ref, p_ref, n_ref, w_ref, sc_ref, o_ref, *, tl, nt, n):
    i = pl.program_id(1)
    x = x_ref[...]
    pv = jnp.where(i > 0, p_ref[...], 0.0)
    nx = jnp.where(i < nt - 1, n_ref[...], 0.0)
    ext = jnp.concatenate([pv, x, nx], axis=0)
    rows = tl + 2 * POOL_HALO
    t = i * tl + lax.broadcasted_iota(jnp.int32, (tl, POOL_GROUP), 0)
    for g, w in enumerate(POOL_WINDOWS):
        lanes = slice(g * POOL_GROUP, (g + 1) * POOL_GROUP)
        a = ext[:, lanes]
        c = a + pltpu.roll(a, 1, 0)
        h = 1
        while 2 * h < w:
            c = pltpu.roll(c, h, 0) + pltpu.roll(c, rows - h, 0)
            h *= 2
        total = c[POOL_HALO:POOL_HALO + tl]
        count = (jnp.minimum(t + h, n) - jnp.maximum(t - h, 0)).astype(F32)
        y = (total / count - x[:, lanes]).astype(BF16)
        o_ref[:, lanes] = _dot(y, w_ref[g].astype(BF16)) * sc_ref[:, lanes]


def _pool_mixer(p, w_pool, scale):
    b, n, _ = p.shape
    tl = min(512, n)
    nt = n // tl
    colblk = 3 * HYENA_WIDTH // POOL_WIDTH
    assert colblk * POOL_WIDTH == 3 * HYENA_WIDTH
    per = tl // POOL_HALO
    nrow = n // POOL_HALO
    return pl.pallas_call(
        functools.partial(_pool_kernel, tl=tl, nt=nt, n=n),
        out_shape=jax.ShapeDtypeStruct((b, n, POOL_WIDTH), F32),
        grid=(b, nt),
        in_specs=[
            pl.BlockSpec((None, tl, POOL_WIDTH), lambda bi, i: (bi, i, colblk)),
            pl.BlockSpec((None, POOL_HALO, POOL_WIDTH), lambda bi, i: (bi, jnp.maximum(i * per - 1, 0), colblk)),
            pl.BlockSpec((None, POOL_HALO, POOL_WIDTH), lambda bi, i: (bi, jnp.minimum((i + 1) * per, nrow - 1), colblk)),
            pl.BlockSpec(w_pool.shape, lambda bi, i: (0, 0, 0)),
            pl.BlockSpec((1, POOL_WIDTH), lambda bi, i: (0, 0)),
        ],
        out_specs=pl.BlockSpec((None, tl, POOL_WIDTH), lambda bi, i: (bi, i, 0)),
        compiler_params=_params(("parallel", "parallel")),
        name="pool_mixer",
    )(p, p, p, w_pool, scale.reshape(1, POOL_WIDTH))


def _load_group_major(ref, tl):
    ncc = ref.shape[0]
    return jnp.concatenate(
        [jnp.concatenate([ref[cc, jg, t1] for cc in range(ncc)], axis=1)
         for t1 in range(tl // FFT_N2) for jg in range(GROUPS)], axis=0)


def _outproj_kernel(at_ref, hy_ref, po_ref, x_ref, gb_ref, w_ref, gp_ref, gt_ref, o_ref, ox_ref, m_ref,
                    *, tm, tiles, group_major):
    t = pl.program_id(0)

    @pl.when(t == 0)
    def _():
        ox_ref[...] = jnp.zeros_like(ox_ref)

    def finish_previous():
        o_ref[...] = x_ref[...] + gt_ref[...] * _rms(ox_ref[...], gp_ref[...])

    @pl.when(t < tiles)
    def _():
        finish_previous()
        hy = _load_group_major(hy_ref, tm) if group_major else hy_ref[...]
        a0, a1 = ATTN_WIDTH, ATTN_WIDTH + HYENA_WIDTH
        m_ref[:, :a0] = _rms(at_ref[...], gb_ref[:, :a0]).astype(BF16)
        m_ref[:, a0:a1] = _rms(hy, gb_ref[:, a0:a1]).astype(BF16)
        m_ref[:, a1:] = _rms(po_ref[...], gb_ref[:, a1:]).astype(BF16)
        ox_ref[...] = _dot(m_ref[...], w_ref[...])

    pl.when(t == tiles)(finish_previous)


def _out_projection(attn, hy, po, x, g_branch, w_out, g_post, gate, group_major):
    b, n, d = x.shape
    tm = min(512, n)
    nt = n // tm
    tiles = b * nt

    def cur(t):
        return jnp.minimum(t, tiles - 1)

    def prev(t):
        return jnp.maximum(t - 1, 0)

    row = lambda width, sel: pl.BlockSpec((None, tm, width), lambda t: (sel(t) // nt, sel(t) % nt, 0))
    vec = lambda width: pl.BlockSpec((1, width), lambda t: (0, 0))
    if group_major:
        hy_spec = pl.BlockSpec((None, *_gm_shape((), tm // FFT_N2, HYENA_WIDTH)),
                               lambda t: (cur(t) // nt, 0, 0, cur(t) % nt, 0, 0))
    else:
        hy_spec = row(HYENA_WIDTH, cur)
    return pl.pallas_call(
        functools.partial(_outproj_kernel, tm=tm, tiles=tiles, group_major=group_major),
        out_shape=jax.ShapeDtypeStruct(x.shape, F32),
        grid=(tiles + 1,),
        in_specs=[row(ATTN_WIDTH, cur), hy_spec, row(POOL_WIDTH, cur), row(d, prev), vec(d),
                  pl.BlockSpec(w_out.shape, lambda t: (0, 0), pipeline_mode=pl.Buffered(1)), vec(d),
                  pl.BlockSpec((None, 1, d), lambda t: (prev(t) // nt, 0, 0))],
        out_specs=row(d, prev),
        scratch_shapes=[pltpu.VMEM((tm, d), F32), pltpu.VMEM((tm, d), BF16)],
        compiler_params=_params(("arbitrary",)),
        name="out_projection",
    )(attn, hy, po, x, g_branch.reshape(1, d), w_out, g_post.reshape(1, d), gate)


def _mlp_kernel(xe_ref, xp_ref, g_ref, shn_ref, scn_ref, wu_ref, wd_ref, gp_ref, gtp_ref, o_ref,
                h0_ref, h1_ref, acc0_ref, acc1_ref, *, rc, tiles):
    t = pl.program_id(0)
    k = pl.program_id(1)
    rows = pl.ds(pl.multiple_of(k * rc, rc), rc)

    def normalise(h_ref):
        a = g_ref[...] * (1.0 + scn_ref[...])
        h_ref[rows, :] = (_rms(xp_ref[...], a) + shn_ref[...]).astype(BF16)

    def finish(acc_ref):
        o_ref[...] = xe_ref[...] + _rms(acc_ref[rows, :], gp_ref[...] * gtp_ref[...])

    @pl.when(t == 0)
    def _():
        normalise(h0_ref)
        acc0_ref[rows, :] = jnp.zeros((rc, acc0_ref.shape[1]), F32)
        acc1_ref[rows, :] = jnp.zeros((rc, acc1_ref.shape[1]), F32)

    def step(slot):
        h_cur, h_oth = (h0_ref, h1_ref) if slot == 0 else (h1_ref, h0_ref)
        acc_cur, acc_oth = (acc0_ref, acc1_ref) if slot == 0 else (acc1_ref, acc0_ref)
        finish(acc_oth)
        acc_oth[rows, :] = jnp.zeros((rc, acc_oth.shape[1]), F32)
        normalise(h_oth)
        u = jnp.maximum(_dot(h_cur[...], wu_ref[...]), 0.0)
        acc_cur[...] += _dot((u * u).astype(BF16), wd_ref[...])

    work = (t >= 1) & (t <= tiles)
    pl.when(work & (t % 2 == 1))(functools.partial(step, 0))
    pl.when(work & (t % 2 == 0))(functools.partial(step, 1))
    pl.when(t == tiles + 1)(functools.partial(finish, acc0_ref if (tiles - 1) % 2 == 0 else acc1_ref))


MLP_TH = 1024


def _mlp(x, g_pre, shift, scale, w_up, w_down, g_post, gate):
    b, n, d = x.shape
    th = MLP_TH
    nk = w_up.shape[1] // th
    tm = min(1024, n)
    nt = n // tm
    tiles = b * nt
    rc = tm // nk
    assert rc * nk == tm and rc % BF16_ROWS == 0

    def norm_tile(t):
        return jnp.minimum(t, tiles - 1)

    def done_tile(t):
        return jnp.clip(t - 2, 0, tiles - 1)

    vec = pl.BlockSpec((1, d), lambda t, k: (0, 0))
    bvec = lambda sel: pl.BlockSpec((None, 1, d), lambda t, k: (sel(t) // nt, 0, 0))
    chunk = lambda sel, first: pl.BlockSpec(
        (None, rc, d), lambda t, k: (sel(t) // nt, (sel(t) % nt) * nk + jnp.where(t < first, 0, k), 0))
    busy = lambda t: (t >= 1) & (t <= tiles)
    return pl.pallas_call(
        functools.partial(_mlp_kernel, rc=rc, tiles=tiles),
        out_shape=jax.ShapeDtypeStruct(x.shape, F32),
        grid=(tiles + 2, nk),
        in_specs=[chunk(done_tile, 2), chunk(norm_tile, 0), vec, bvec(norm_tile), bvec(norm_tile),
                  pl.BlockSpec((d, th), lambda t, k: (0, jnp.where(busy(t), k, 0))),
                  pl.BlockSpec((th, d), lambda t, k: (jnp.where(busy(t), k, 0), 0)),
                  vec, bvec(done_tile)],
        out_specs=chunk(done_tile, 2),
        scratch_shapes=[pltpu.VMEM((tm, d), BF16)] * 2 + [pltpu.VMEM((tm, d), F32)] * 2,
        compiler_params=_params(("arbitrary", "arbitrary")),
        name="mlp",
    )(x, x, g_pre.reshape(1, d), shift, scale, w_up, w_down, g_post.reshape(1, d), gate)


def kernel(x, c, ctx, c_ctx, w_mod, b_mod, g_pre_mix, g_post_mix, g_pre_mlp, g_post_mlp, w_in, w_out, g_branch,
           attn_sink, hy_conv_w, hy_conv_b, hy_w1, hy_b1, hy_freq1, hy_w2, hy_b2, hy_freq2, hy_w3, hy_bias,
           pool_w, pool_scale, w_up, w_down):
    b, n, d = x.shape
    n_ctx = ctx.shape[1]
    depth = w_mod.shape[0]
    assert b == 2 and d == D_MODEL and n % 512 == 0 and n_ctx % BLOCK == 0

    cond = jnp.concatenate([c, c_ctx[None], jnp.zeros((SUBLANES - b - 1, d), F32)], axis=0)
    mods = _modulation(cond, w_mod, b_mod)

    w_in_b = _column_tiles(_permute_rope_columns(w_in.astype(BF16)), IN_TN)
    w_up_b = w_up.astype(BF16)
    w_out_b, w_down_b = w_out.astype(BF16), w_down.astype(BF16)
    tables_x, tables_c = _rope_tables(n), _rope_tables(n_ctx)
    feats_x, feats_c = _filter_features(n), _filter_features(n_ctx)
    deltas = _filter_deltas()
    tables = _stage2_tables(n)

    for i in range(depth):
        last = i == depth - 1
        hy = (hy_conv_w[i], hy_conv_b[i], hy_w1[i], hy_b1[i], hy_freq1[i], hy_w2[i], hy_b2[i], hy_freq2[i],
              hy_w3[i], hy_bias[i])
        mx = [m[:, None, :] for m in jnp.split(mods[i, :b], N_MOD, axis=-1)]
        mc = [jnp.broadcast_to(m[None, None, :], (b, 1, d)) for m in jnp.split(mods[i, b], N_MOD, axis=-1)]

        qkv_x, px = _in_projection(x, g_pre_mix[i], mx[0], mx[1], w_in_b[i], tables_x, rope=True)
        qkv_c, pc = _in_projection(ctx, g_pre_mix[i], mc[0], mc[1], w_in_b[i], tables_c, rope=False)
        kv_ctx = qkv_c[..., Q_END:V_END]

        attn_x = _attention(qkv_x, kv_ctx, attn_sink[i], local=True)
        hy_x = _hyena_latent(px, hy, feats_x, deltas, tables, n)
        po_x = _pool_mixer(px, pool_w[i], pool_scale[i])
        x = _out_projection(attn_x, hy_x, po_x, x, g_branch[i], w_out_b[i], g_post_mix[i], mx[2], group_major=True)
        x = _mlp(x, g_pre_mlp[i], mx[3], mx[4], w_up_b[i], w_down_b[i], g_post_mlp[i], mx[5])

        if not last:
            attn_c = _attention(qkv_c, kv_ctx, attn_sink[i], local=False)
            hy_c = _hyena_context(pc, hy, feats_c, deltas, n_ctx)
            po_c = _pool_mixer(pc, pool_w[i], pool_scale[i])
            ctx = _out_projection(attn_c, hy_c, po_c, ctx, g_branch[i], w_out_b[i], g_post_mix[i], mc[2],
                                  group_major=False)
            ctx = _mlp(ctx, g_pre_mlp[i], mc[3], mc[4], w_up_b[i], w_down_b[i], g_post_mlp[i], mc[5])
    return x
```

```python
import functools
import math

import numpy as np
import jax
import jax.numpy as jnp
from jax import lax
from jax.experimental import pallas as pl
from jax.experimental.pallas import tpu as pltpu

F32 = jnp.float32
BF16 = jnp.bfloat16

D_MODEL = 2048
DEPTH = 4
GRID_W = 64
ATTN_WIDTH = D_MODEL // 2
HYENA_WIDTH = D_MODEL // 4
POOL_WIDTH = D_MODEL - ATTN_WIDTH - HYENA_WIDTH
HEAD_DIM = 128
N_HEADS = ATTN_WIDTH // HEAD_DIM
N_KV_HEADS = 2
KV_GROUP = N_HEADS // N_KV_HEADS
KV_WIDTH = N_KV_HEADS * HEAD_DIM
WINDOW = 128
BLOCK = 128
ROPE_BASE = 10000.0
HYENA_EMB_DIM = 33
HYENA_FILTER_HIDDEN = 64
HYENA_FAST_DECAY_PCT = 0.3
HYENA_SLOW_DECAY_PCT = 1.5
HYENA_DECAY_TARGET = 1e-2
POOL_WINDOWS = (2, 4, 8, 16)
POOL_GROUP = POOL_WIDTH // len(POOL_WINDOWS)
MLP_HIDDEN = 4 * D_MODEL
N_MOD = 6
EPS = 1e-6
NEG_INF = -1e30

Q_END = ATTN_WIDTH
K_END = Q_END + KV_WIDTH
V_END = K_END + KV_WIDTH
HY_END = V_END + 3 * HYENA_WIDTH
IN_WIDTH = HY_END + POOL_WIDTH

LANES = 128
SUBLANES = 8
FFT_N2 = 128
POOL_HALO = 16
VMEM_LIMIT = 56 * 1024 * 1024


def _params(sem, vmem=VMEM_LIMIT):
    return pltpu.CompilerParams(dimension_semantics=sem, vmem_limit_bytes=vmem)


def _split(x):
    hi = x.astype(BF16)
    lo = (x - hi.astype(F32)).astype(BF16)
    return hi, lo


def _dot(a, b):
    return jnp.dot(a, b, preferred_element_type=F32)


def _dot3(ah, al, bh, bl):
    return _dot(ah, bh) + _dot(ah, bl) + _dot(al, bh)


def _rms(x, g):
    return x * lax.rsqrt(jnp.mean(x * x, axis=-1, keepdims=True) + EPS) * g


BF16_ROWS = 2 * SUBLANES


def _norm_scale_rows(x_ref, h_ref, a_ref, s_ref, rows, cols=None):
    cols = slice(None) if cols is None else cols
    nchunks = rows // BF16_ROWS

    def chunk(c):
        return pl.ds(pl.multiple_of(c * BF16_ROWS, BF16_ROWS), BF16_ROWS)

    def inv_rms(c):
        x = x_ref[chunk(c), :]
        return lax.rsqrt(jnp.mean(x * x, axis=-1, keepdims=True) + EPS)

    def body(c, inv):
        inv_next = inv_rms(jnp.minimum(c + 1, nchunks - 1))
        y = x_ref[chunk(c), :] * inv * a_ref[...]
        if s_ref is not None:
            y = y + s_ref[...]
        h_ref[chunk(c), cols] = y.astype(BF16)
        return inv_next

    lax.fori_loop(0, nchunks, body, inv_rms(0), unroll=8)


def _residual_norm_rows(x_ref, y_ref, o_ref, pg_ref, rows):
    nchunks = rows // SUBLANES

    def chunk(c):
        return pl.ds(pl.multiple_of(c * SUBLANES, SUBLANES), SUBLANES)

    def inv_rms(c):
        y = y_ref[chunk(c), :]
        return lax.rsqrt(jnp.mean(y * y, axis=-1, keepdims=True) + EPS)

    def body(c, inv):
        inv_next = inv_rms(jnp.minimum(c + 1, nchunks - 1))
        o_ref[chunk(c), :] = x_ref[chunk(c), :] + y_ref[chunk(c), :] * inv * pg_ref[...]
        return inv_next

    lax.fori_loop(0, nchunks, body, inv_rms(0), unroll=16)


def _mod_kernel(c_ref, w_ref, b_ref, o_ref):
    c = c_ref[...]
    s = c / (1.0 + jnp.exp(-c))
    sh, sl = _split(s)
    wh, wl = _split(w_ref[...])
    o_ref[...] = _dot3(sh, sl, wh, wl) + b_ref[...]


def _modulation(cond, w_mod, b_mod):
    depth, d, width = w_mod.shape
    tn = 1024
    return pl.pallas_call(
        _mod_kernel,
        out_shape=jax.ShapeDtypeStruct((depth, SUBLANES, width), F32),
        grid=(depth, width // tn),
        in_specs=[
            pl.BlockSpec((SUBLANES, d), lambda l, j: (0, 0)),
            pl.BlockSpec((None, d, tn), lambda l, j: (l, 0, j)),
            pl.BlockSpec((None, 1, tn), lambda l, j: (l, 0, j)),
        ],
        out_specs=pl.BlockSpec((None, SUBLANES, tn), lambda l, j: (l, 0, j)),
        compiler_params=_params(("parallel", "parallel")),
        name="modulation",
    )(cond, w_mod, b_mod.reshape(depth, 1, width))


QKV_WIDTH = V_END
REST_WIDTH = IN_WIDTH - V_END
SM_SCALE = HEAD_DIM ** -0.5


TABLE_Q, TABLE_K, TABLE_ID, TABLE_SCALE = range(4)


def _inproj_kernel(xn_ref, g_ref, shn_ref, scn_ref, w_ref, ca_ref, sa_ref, cb_ref, sb_ref,
                   qkv_ref, rest_ref, h0_ref, h1_ref, acc_ref, *, tm, tn, rc, tiles):
    t = pl.program_id(0)
    j = pl.program_id(1)

    def normalise(h_ref):
        rows = pl.ds(pl.multiple_of(jnp.minimum(j * rc, tm - rc), BF16_ROWS), rc)
        a = g_ref[...] * (1.0 + scn_ref[...])
        h_ref[rows, :] = (_rms(xn_ref[rows, :], a) + shn_ref[...]).astype(BF16)

    def finish_previous():
        acc = acc_ref[...]
        rest_ref[...] = acc
        nch = tn // LANES
        for ch in range(nch):
            a = acc[:, ch * LANES:(ch + 1) * LANES]
            cos, sin = (ca_ref, sa_ref) if ch < nch // 2 else (cb_ref, sb_ref)
            qkv_ref[:, ch * LANES:(ch + 1) * LANES] = (a * cos[...] + pltpu.roll(a, HEAD_DIM // 2, 1) * sin[...]).astype(BF16)

    @pl.when(t == 0)
    def _():
        normalise(h0_ref)

    @pl.when((t == 0) & (j == 0))
    def _():
        acc_ref[...] = jnp.zeros_like(acc_ref)

    def step(slot):
        h_cur, h_oth = (h0_ref, h1_ref) if slot == 0 else (h1_ref, h0_ref)
        finish_previous()
        normalise(h_oth)
        acc_ref[...] = _dot(h_cur[...], w_ref[j])

    work = (t >= 1) & (t <= tiles)
    pl.when(work & (t % 2 == 1))(functools.partial(step, 0))
    pl.when(work & (t % 2 == 0))(functools.partial(step, 1))
    pl.when((t == tiles + 1) & (j == 0))(finish_previous)


def _column_tiles(w, tn):
    *lead, d, width = w.shape
    return jnp.swapaxes(w.reshape(*lead, d, width // tn, tn), -3, -2)


IN_TN = 512
QKV_SPARE = QKV_WIDTH // IN_TN
REST_SPARE = REST_WIDTH // IN_TN


def _in_projection(x, g, shift, scale, w, tables, rope):
    b, n, d = x.shape
    tm = min(1024, n)
    tn = w.shape[-1]
    nj = w.shape[0]
    nt = n // tm
    tiles = b * nt
    half = tn // 2
    assert tn == IN_TN and nj * tn == IN_WIDTH and Q_END % tn == 0 and K_END % tn == half and V_END % tn == 0
    rc = -(-tm // nj)
    rc = -(-rc // (2 * BF16_ROWS)) * 2 * BF16_ROWS
    nq, nqkv = Q_END // tn, QKV_WIDTH // tn

    def norm_tile(t):
        return jnp.minimum(t, tiles - 1)

    def lagged(t, j):
        step = jnp.clip((t - 1) * nj + j - 1, 0, tiles * nj - 1)
        return step // nj, step % nj

    def kind_a(j):
        return jnp.where(j < nq, TABLE_Q if rope else TABLE_SCALE, jnp.where(j < nqkv, TABLE_K if rope else TABLE_ID, TABLE_ID))

    def kind_b(j):
        return jnp.where(j < nq, TABLE_Q if rope else TABLE_SCALE, TABLE_ID)

    vec = pl.BlockSpec((1, d), lambda t, j: (0, 0))
    bvec = lambda sel: pl.BlockSpec((None, 1, d), lambda t, j: (sel(t) // nt, 0, 0))
    def table(kind):
        def index(t, j):
            tp, jp = lagged(t, j)
            k = kind(jp)
            return k, jnp.where(k == TABLE_ID, 0, tp % nt), 0
        return pl.BlockSpec((None, tm, LANES), index)

    def out_spec(col_tile):
        return pl.BlockSpec((None, tm, tn), lambda t, j: (lagged(t, j)[0] // nt, lagged(t, j)[0] % nt,
                                                          col_tile(lagged(t, j)[1])))

    cos, sin = tables
    return pl.pallas_call(
        functools.partial(_inproj_kernel, tm=tm, tn=tn, rc=rc, tiles=tiles),
        out_shape=[jax.ShapeDtypeStruct((b, n, QKV_WIDTH + tn), BF16), jax.ShapeDtypeStruct((b, n, REST_WIDTH + tn), F32)],
        grid=(tiles + 2, nj),
        in_specs=[pl.BlockSpec((None, tm, d), lambda t, j: (norm_tile(t) // nt, norm_tile(t) % nt, 0)),
                  vec, bvec(norm_tile), bvec(norm_tile),
                  pl.BlockSpec(w.shape, lambda t, j: (0, 0, 0), pipeline_mode=pl.Buffered(1)),
                  table(kind_a), table(kind_a), table(kind_b), table(kind_b)],
        out_specs=[out_spec(lambda jp: jnp.minimum(jp, QKV_SPARE)),
                   out_spec(lambda jp: jnp.where(jp >= nqkv, jp - nqkv, REST_SPARE))],
        scratch_shapes=[pltpu.VMEM((tm, d), BF16)] * 2 + [pltpu.VMEM((tm, tn), F32)],
        compiler_params=_params(("arbitrary", "arbitrary")),
        name="in_projection_rope" if rope else "in_projection",
    )(x, g.reshape(1, d), shift, scale, w, cos, sin, cos, sin)


def _permute_rope_columns(w_in):
    lead = w_in.shape[:-1]
    quarter = HEAD_DIM // 4
    qk = w_in[..., :K_END].reshape(*lead, K_END // HEAD_DIM, 2, 2, quarter)
    qk = jnp.swapaxes(qk, -3, -2).reshape(*lead, K_END)
    return jnp.concatenate([qk, w_in[..., K_END:]], axis=-1)


def _rope_tables(n):
    quarter = HEAD_DIM // 4
    inv_freq = ROPE_BASE ** (-jnp.arange(quarter, dtype=F32) / quarter)
    t = jnp.arange(n, dtype=jnp.int32)
    row = (t // GRID_W).astype(F32)[:, None] * inv_freq[None, :]
    col = (t % GRID_W).astype(F32)[:, None] * inv_freq[None, :]
    cos = jnp.concatenate([jnp.cos(row), jnp.cos(col), jnp.cos(row), jnp.cos(col)], axis=-1)
    sin = jnp.concatenate([-jnp.sin(row), -jnp.sin(col), jnp.sin(row), jnp.sin(col)], axis=-1)
    one, zero = jnp.ones_like(cos), jnp.zeros_like(sin)
    return (jnp.stack([cos * SM_SCALE, cos, one, one * SM_SCALE]), jnp.stack([sin * SM_SCALE, sin, zero, zero]))


def _softmax_pv(parts, sink_col):
    m = sink_col
    for s, _ in parts:
        m = jnp.maximum(m, jnp.max(s, axis=-1, keepdims=True))
    den = jnp.exp(sink_col - m)
    out = None
    for s, v in parts:
        p = jnp.exp(s - m)
        den = den + jnp.sum(p, axis=-1, keepdims=True)
        pv = _dot(p.astype(BF16), v)
        out = pv if out is None else out + pv
    return out / den


def _ctx_attn_kernel(sink_ref, q_ref, kvc_ref, o_ref, *, tq):
    for g in range(N_KV_HEADS):
        heads = [g * KV_GROUP + h for h in range(KV_GROUP)]
        qg = jnp.concatenate([q_ref[:, h * HEAD_DIM:(h + 1) * HEAD_DIM] for h in heads], axis=0)
        sink_col = jnp.concatenate([jnp.full((tq, 1), sink_ref[h], F32) for h in heads], axis=0)
        kc = kvc_ref[:, g * HEAD_DIM:(g + 1) * HEAD_DIM]
        vc = kvc_ref[:, KV_WIDTH + g * HEAD_DIM:KV_WIDTH + (g + 1) * HEAD_DIM]
        s = lax.dot_general(qg, kc, (((1,), (1,)), ((), ())), preferred_element_type=F32)
        o = _softmax_pv([(s, vc)], sink_col)
        for hi, h in enumerate(heads):
            o_ref[:, h * HEAD_DIM:(h + 1) * HEAD_DIM] = o[hi * tq:(hi + 1) * tq]


ATTN_ROWS = KV_GROUP * BLOCK
SOFTMAX_CHUNK = 32


def _win_attn_kernel(sink_ref, q_ref, km_ref, kp_ref, kn_ref, vm_ref, vp_ref, vn_ref, kvc_ref, band_ref,
                     o_ref, ktw, vw, s_scr, p_scr, m_scr, *, tq, nb, nctx):
    i = pl.program_id(1)
    nsub = tq // BLOCK
    nloc = 3 * BLOCK

    def transposed(x):
        return x.astype(F32).T.astype(BF16)

    def block_rows(main_ref, prev_ref, next_ref, w, lanes):
        if w == 0:
            return prev_ref[:, lanes]
        if w == nsub + 1:
            return next_ref[:, lanes]
        return main_ref[(w - 1) * BLOCK:w * BLOCK, lanes]

    ones = jnp.ones((nloc + nctx, HEAD_DIM), BF16)
    for g in range(N_KV_HEADS):
        lanes = slice(g * HEAD_DIM, (g + 1) * HEAD_DIM)
        vlanes = slice(KV_WIDTH + g * HEAD_DIM, KV_WIDTH + (g + 1) * HEAD_DIM)
        kts = [transposed(block_rows(km_ref, kp_ref, kn_ref, w, lanes)) for w in range(nsub + 2)]
        kct = [transposed(kvc_ref[cb * BLOCK:(cb + 1) * BLOCK, lanes]) for cb in range(nctx // BLOCK)]
        for jb in range(nsub):
            for w in range(3):
                ktw[jb, g, :, w * BLOCK:(w + 1) * BLOCK] = kts[jb + w]
                vw[jb, g, w * BLOCK:(w + 1) * BLOCK, :HEAD_DIM] = block_rows(vm_ref, vp_ref, vn_ref, jb + w, lanes)
            for cb in range(nctx // BLOCK):
                ktw[jb, g, :, nloc + cb * BLOCK:nloc + (cb + 1) * BLOCK] = kct[cb]
            vw[jb, g, nloc:, :HEAD_DIM] = kvc_ref[:, vlanes]
            vw[jb, g, :, HEAD_DIM:] = ones

    col = lax.broadcasted_iota(jnp.int32, (1, nloc), 1)

    def rows_of(jb):
        start = jb * BLOCK
        return pl.ds(start if isinstance(start, int) else pl.multiple_of(start, BLOCK), BLOCK)

    def stage_a(jb, g):
        qg = jnp.concatenate([q_ref[rows_of(jb), (g * KV_GROUP + h) * HEAD_DIM:(g * KV_GROUP + h + 1) * HEAD_DIM]
                              for h in range(KV_GROUP)], axis=0)
        s_scr[g] = _dot(qg, ktw[jb, g])

    def stage_b(jb, g):
        blk = i * nsub + jb
        pen_prev = jnp.where(blk == 0, NEG_INF, 0.0).astype(F32)
        pen_next = jnp.where(blk == nb - 1, NEG_INF, 0.0).astype(F32)
        rowbias = jnp.where(col < BLOCK, pen_prev, jnp.where(col >= 2 * BLOCK, pen_next, 0.0))
        for c in range(ATTN_ROWS // SOFTMAX_CHUNK):
            rows = slice(c * SOFTMAX_CHUNK, (c + 1) * SOFTMAX_CHUNK)
            sink = sink_ref[g * KV_GROUP + (c * SOFTMAX_CHUNK) // BLOCK]
            s_loc = s_scr[g, rows, :nloc] + band_ref[rows, :] + rowbias
            s_ctx = s_scr[g, rows, nloc:]
            m = jnp.maximum(jnp.max(s_loc, axis=-1, keepdims=True), jnp.max(s_ctx, axis=-1, keepdims=True))
            m = jnp.maximum(m, sink)
            p_scr[g, rows, :nloc] = jnp.exp(s_loc - m).astype(BF16)
            p_scr[g, rows, nloc:] = jnp.exp(s_ctx - m).astype(BF16)
            m_scr[g, rows, :] = m

    def stage_c(jb, g):
        o = _dot(p_scr[g], vw[jb, g])
        for hi in range(KV_GROUP):
            h = g * KV_GROUP + hi
            rows = slice(hi * BLOCK, (hi + 1) * BLOCK)
            den = o[rows, HEAD_DIM:HEAD_DIM + 1] + jnp.exp(sink_ref[h] - m_scr[g, rows, :])
            o_ref[rows_of(jb), h * HEAD_DIM:(h + 1) * HEAD_DIM] = o[rows, :HEAD_DIM] / den

    stage_a(0, 0)
    stage_a(0, 1)
    stage_b(0, 0)

    def body(j, carry):
        stage_a(j, 0)
        stage_c(j - 1, 0)
        stage_b(j - 1, 1)
        stage_a(j, 1)
        stage_c(j - 1, 1)
        stage_b(j, 0)
        return carry

    lax.fori_loop(1, nsub, body, 0)
    stage_c(nsub - 1, 0)
    stage_b(nsub - 1, 1)
    stage_c(nsub - 1, 1)


def _band_bias():
    qi = np.arange(ATTN_ROWS)[:, None] % BLOCK
    sj = np.arange(3 * BLOCK)[None, :]
    return jnp.asarray(np.where(np.abs(sj - BLOCK - qi) <= WINDOW, 0.0, NEG_INF), dtype=F32)


def _attention(qkv, kv_ctx, sink, local):
    b, n, _ = qkv.shape
    c = kv_ctx.shape[1]
    tq = min(1024 if local else 512, n)
    nsub = tq // BLOCK
    nb = n // BLOCK
    kcol = Q_END // KV_WIDTH
    vcol = K_END // KV_WIDTH
    smem = pl.BlockSpec(memory_space=pltpu.SMEM)
    q_spec = pl.BlockSpec((None, tq, ATTN_WIDTH), lambda bi, i: (bi, i, 0))
    kvc_spec = pl.BlockSpec((None, c, 2 * KV_WIDTH), lambda bi, i: (bi, 0, 0))
    if local:
        def main(colblk):
            return pl.BlockSpec((None, tq, KV_WIDTH), lambda bi, i: (bi, i, colblk))

        def prev(colblk):
            return pl.BlockSpec((None, BLOCK, KV_WIDTH), lambda bi, i: (bi, jnp.maximum(i * nsub - 1, 0), colblk))

        def nxt(colblk):
            return pl.BlockSpec((None, BLOCK, KV_WIDTH), lambda bi, i: (bi, jnp.minimum((i + 1) * nsub, nb - 1), colblk))

        keys = 3 * BLOCK + c
        kern = functools.partial(_win_attn_kernel, tq=tq, nb=nb, nctx=c)
        in_specs = [smem, q_spec, main(kcol), prev(kcol), nxt(kcol), main(vcol), prev(vcol), nxt(vcol), kvc_spec,
                    pl.BlockSpec((ATTN_ROWS, 3 * BLOCK), lambda bi, i: (0, 0))]
        args = [sink, qkv, qkv, qkv, qkv, qkv, qkv, qkv, kv_ctx, _band_bias()]
        scratch = [pltpu.VMEM((nsub, N_KV_HEADS, HEAD_DIM, keys), BF16),
                   pltpu.VMEM((nsub, N_KV_HEADS, keys, 2 * HEAD_DIM), BF16),
                   pltpu.VMEM((2, ATTN_ROWS, keys), F32),
                   pltpu.VMEM((2, ATTN_ROWS, keys), BF16),
                   pltpu.VMEM((2, ATTN_ROWS, 1), F32)]
    else:
        kern = functools.partial(_ctx_attn_kernel, tq=tq)
        in_specs = [smem, q_spec, kvc_spec]
        args = [sink, qkv, kv_ctx]
        scratch = []
    return pl.pallas_call(
        kern,
        out_shape=jax.ShapeDtypeStruct((b, n, ATTN_WIDTH), F32),
        grid=(b, n // tq),
        in_specs=in_specs,
        out_specs=pl.BlockSpec((None, tq, ATTN_WIDTH), lambda bi, i: (bi, i, 0)),
        scratch_shapes=scratch,
        compiler_params=_params(("parallel", "parallel")),
        name="window_attention" if local else "context_attention",
    )(*args)


GROUPS = FFT_N2 // SUBLANES


def _gm_shape(lead, t1, width):
    return (*lead, width // LANES, GROUPS, t1, SUBLANES, LANES)


def _store_group_major(o_ref, val, tl):
    for t1 in range(tl // FFT_N2):
        for jg in range(GROUPS):
            r0 = (t1 * GROUPS + jg) * SUBLANES
            for cc in range(val.shape[1] // LANES):
                o_ref[cc, jg, t1] = val[r0:r0 + SUBLANES, cc * LANES:(cc + 1) * LANES]


def _hyena_prep_kernel(u_ref, p_ref, n_ref, w_ref, b_ref, vx_ref, x0_ref, *, tl, nt, group_major):
    i = pl.program_id(1)
    u = u_ref[...]
    prev_row = jnp.where(i > 0, p_ref[SUBLANES - 1:SUBLANES, :], 0.0)
    next_row = jnp.where(i < nt - 1, n_ref[0:1, :], 0.0)
    row = lax.broadcasted_iota(jnp.int32, u.shape, 0)
    um = jnp.where(row == 0, prev_row, pltpu.roll(u, 1, 0))
    up = jnp.where(row == tl - 1, next_row, pltpu.roll(u, tl - 1, 0))
    z = um * w_ref[0:1, :] + u * w_ref[1:2, :] + up * w_ref[2:3, :] + b_ref[...]
    x0 = z[:, :HYENA_WIDTH]
    vx = z[:, 2 * HYENA_WIDTH:] * z[:, HYENA_WIDTH:2 * HYENA_WIDTH]
    if group_major:
        _store_group_major(vx_ref, vx, tl)
        _store_group_major(x0_ref, x0, tl)
    else:
        vx_ref[...] = vx
        x0_ref[...] = x0


def _hyena_prep(p, conv_w, conv_b, group_major):
    b, n, _ = p.shape
    tl = min(512, n)
    nt = n // tl
    hw = 3 * HYENA_WIDTH
    colblk = 0
    nrow8 = n // SUBLANES
    per = tl // SUBLANES
    if group_major:
        shape = _gm_shape((b,), n // FFT_N2, HYENA_WIDTH)
        out_spec = pl.BlockSpec((None, *_gm_shape((), tl // FFT_N2, HYENA_WIDTH)), lambda bi, i: (bi, 0, 0, i, 0, 0))
    else:
        shape = (b, n, HYENA_WIDTH)
        out_spec = pl.BlockSpec((None, tl, HYENA_WIDTH), lambda bi, i: (bi, i, 0))
    return pl.pallas_call(
        functools.partial(_hyena_prep_kernel, tl=tl, nt=nt, group_major=group_major),
        out_shape=[jax.ShapeDtypeStruct(shape, F32)] * 2,
        grid=(b, nt),
        in_specs=[
            pl.BlockSpec((None, tl, hw), lambda bi, i: (bi, i, colblk)),
            pl.BlockSpec((None, SUBLANES, hw), lambda bi, i: (bi, jnp.maximum(i * per - 1, 0), colblk)),
            pl.BlockSpec((None, SUBLANES, hw), lambda bi, i: (bi, jnp.minimum((i + 1) * per, nrow8 - 1), colblk)),
            pl.BlockSpec((3, hw), lambda bi, i: (0, 0)),
            pl.BlockSpec((1, hw), lambda bi, i: (0, 0)),
        ],
        out_specs=[out_spec, out_spec],
        compiler_params=_params(("parallel", "parallel")),
        name="hyena_prep",
    )(p, p, p, conv_w, conv_b.reshape(1, hw))


def _filter_kernel(ft_ref, w1_ref, b1_ref, f1_ref, w2_ref, b2_ref, f2_ref, w3_ref, dl_ref, h_ref, s_ref,
                   *, tl, n, group_major):
    i = pl.program_id(0)

    def dense(a, w_ref):
        ah, al = _split(a)
        wh, wl = _split(w_ref[...])
        return _dot3(ah, al, wh, wl)

    h = jnp.sin(f1_ref[...] * (dense(ft_ref[...], w1_ref) + b1_ref[...]))
    h = jnp.sin(f2_ref[...] * (dense(h, w2_ref) + b2_ref[...]))
    h = dense(h, w3_ref)
    t = (i * tl + lax.broadcasted_iota(jnp.int32, (tl, HYENA_WIDTH), 0)).astype(F32) / float(n - 1)
    decay = jnp.exp(-t * dl_ref[...])
    h = h * jnp.concatenate([decay, decay], axis=1)
    if group_major:
        _store_group_major(h_ref, h, tl)
    else:
        h_ref[...] = h

    @pl.when(i == 0)
    def _():
        s_ref[...] = jnp.zeros_like(s_ref)

    s_ref[...] += jnp.sum(jnp.abs(h).reshape(tl // SUBLANES, SUBLANES, 2 * HYENA_WIDTH), axis=0)


def _filter_features(n):
    t = jnp.linspace(0.0, 1.0, n, dtype=F32)[:, None]
    bands = (HYENA_EMB_DIM - 1) // 2
    omega = 2.0 * math.pi * jnp.arange(n, dtype=F32)[:, None] / n
    f = jnp.linspace(1e-4, bands - 1, bands, dtype=F32)[None, :]
    feats = jnp.concatenate([t, jnp.cos(f * omega), -jnp.sin(f * omega)], axis=-1)
    return jnp.pad(feats, ((0, 0), (0, LANES - HYENA_EMB_DIM)))


def _filter_deltas():
    max_decay = math.log(HYENA_DECAY_TARGET) / HYENA_FAST_DECAY_PCT
    min_decay = math.log(HYENA_DECAY_TARGET) / HYENA_SLOW_DECAY_PCT
    return jnp.abs(jnp.linspace(min_decay, max_decay, HYENA_WIDTH, dtype=F32)).reshape(1, HYENA_WIDTH)


def _hyena_filter(n, feats, deltas, w1, b1, f1, w2, b2, f2, w3, group_major):
    tl = min(512, n)
    hid = LANES
    pad_h = hid - HYENA_FILTER_HIDDEN
    w1p = jnp.pad(w1, ((0, LANES - HYENA_EMB_DIM), (0, pad_h)))
    w2p = jnp.pad(w2, ((0, pad_h), (0, pad_h)))
    w3p = jnp.pad(w3, ((0, pad_h), (0, 0)))
    vec = lambda v: jnp.pad(v, (0, pad_h)).reshape(1, hid)
    hw2 = 2 * HYENA_WIDTH
    full = lambda shape: pl.BlockSpec(shape, lambda i: (0,) * len(shape))
    if group_major:
        shape = _gm_shape((), n // FFT_N2, hw2)
        out_spec = pl.BlockSpec(_gm_shape((), tl // FFT_N2, hw2), lambda i: (0, 0, i, 0, 0))
    else:
        shape = (n, hw2)
        out_spec = pl.BlockSpec((tl, hw2), lambda i: (i, 0))
    return pl.pallas_call(
        functools.partial(_filter_kernel, tl=tl, n=n, group_major=group_major),
        out_shape=[jax.ShapeDtypeStruct(shape, F32), jax.ShapeDtypeStruct((SUBLANES, hw2), F32)],
        grid=(n // tl,),
        in_specs=[pl.BlockSpec((tl, LANES), lambda i: (i, 0)), full((LANES, hid)), full((1, hid)), full((1, hid)),
                  full((hid, hid)), full((1, hid)), full((1, hid)), full((hid, hw2)), full((1, HYENA_WIDTH))],
        out_specs=[out_spec, full((SUBLANES, hw2))],
        compiler_params=_params(("arbitrary",)),
        name="hyena_filter",
    )(feats, w1p, vec(b1), vec(f1), w2p, vec(b2), vec(f2), w3p, deltas)


def _stack_complex(m):
    return np.block([[m.real, -m.imag], [m.imag, m.real]])


def _hilo(m):
    m = jnp.asarray(m, dtype=F32)
    return _split(m)


@functools.lru_cache(maxsize=None)
def _fft_constants(n):
    m = 2 * n
    n2 = FFT_N2
    n1 = m // n2
    n1h = n1 // 2
    k1 = np.arange(n1)
    t1 = np.arange(n1h)
    f1 = np.exp(-2j * np.pi * np.outer(k1, t1) / n1)
    f3 = np.exp(2j * np.pi * np.outer(t1, k1) / n1) / m
    k2 = np.arange(n2)
    t2 = np.arange(n2)
    w2 = np.exp(-2j * np.pi * np.outer(k2, t2) / n2)
    tw = np.exp(-2j * np.pi * np.outer(k1, t2) / m)
    return dict(
        n1=n1, n1h=n1h,
        f1c=_stack_complex(f1), f1r=np.concatenate([f1.real, f1.imag], axis=0),
        f3c=_stack_complex(f3),
        w2r=w2.real.astype(np.float32), w2i=w2.imag.astype(np.float32),
        twr=tw.real.astype(np.float32), twi=tw.imag.astype(np.float32),
    )


def _stage2_tables(n):
    c = _fft_constants(n)
    w2r, w2i = jnp.asarray(c["w2r"])[None], jnp.asarray(c["w2i"])[None]
    twr, twi = jnp.asarray(c["twr"])[:, None, :], jnp.asarray(c["twi"])[:, None, :]
    gr = w2r * twr - w2i * twi
    gi = w2r * twi + w2i * twr
    g = jnp.concatenate([jnp.concatenate([gr, -gi], axis=2), jnp.concatenate([gi, gr], axis=2)], axis=1)
    return _split(g)


def _s1_kernel(x_ref, fh_ref, fl_ref, o_ref, *, nparts, ncw, n1, n1h):
    fh = fh_ref[...]
    fl = fl_ref[...]
    for r in range(SUBLANES):
        rows = pl.ds(r, n1h, stride=SUBLANES)
        xs = jnp.concatenate(
            [jnp.concatenate([x_ref[p, cc, rows, :] for p in range(nparts)], axis=0) for cc in range(ncw)], axis=1)
        xh, xl = _split(xs)
        res = _dot3(fh, fl, xh, xl)
        for ri in range(2):
            for cc in range(ncw):
                o_ref[ri, cc, pl.ds(r, n1, stride=SUBLANES), :] = res[ri * n1:(ri + 1) * n1, cc * LANES:(cc + 1) * LANES]


def _fft_stage1(x, fmat, n1, n1h, ncw):
    nparts, ncc, groups = x.shape[:3]
    fh, fl = _hilo(fmat)
    return pl.pallas_call(
        functools.partial(_s1_kernel, nparts=nparts, ncw=ncw, n1=n1, n1h=n1h),
        out_shape=jax.ShapeDtypeStruct((2, ncc, groups, n1 * SUBLANES, LANES), F32),
        grid=(groups, ncc // ncw),
        in_specs=[
            pl.BlockSpec((nparts, ncw, None, n1h * SUBLANES, LANES), lambda j, ci: (0, ci, j, 0, 0)),
            pl.BlockSpec(fh.shape, lambda j, ci: (0, 0)),
            pl.BlockSpec(fl.shape, lambda j, ci: (0, 0)),
        ],
        out_specs=pl.BlockSpec((2, ncw, None, n1 * SUBLANES, LANES), lambda j, ci: (0, ci, j, 0, 0)),
        compiler_params=_params(("parallel", "parallel")),
        name="fft_stage1",
    )(x, fh, fl)


def _load_k1(a_ref, q):
    ncc = a_ref.shape[1]
    return jnp.concatenate([a_ref[:, cc, :, q].reshape(2 * FFT_N2, LANES) for cc in range(ncc)], axis=1)


def _filter_spectrum_kernel(a_ref, gh_ref, gl_ref, s_ref, o_ref, *, kg):
    s = jnp.sum(s_ref[...], axis=0, keepdims=True)
    inv = 1.0 / (s[:, :HYENA_WIDTH] + s[:, HYENA_WIDTH:])
    half = FFT_N2
    for q in range(kg):
        ah, al = _split(_load_k1(a_ref, q))
        h = _dot3(gh_ref[q], gl_ref[q], ah, al)
        hf = h[:, :HYENA_WIDTH]
        hb = h[:, HYENA_WIDTH:]
        o_ref[q, :half] = (hf[:half] + hb[:half]) * inv
        o_ref[q, half:] = (hf[half:] - hb[half:]) * inv


def _k1_spec(kg, width):
    return pl.BlockSpec((2, width // LANES, GROUPS, kg, SUBLANES, LANES), lambda i: (0, 0, 0, i, 0, 0))


def _filter_spectrum(a, gh, gl, sums, n1):
    kg = 4
    a6 = a.reshape(2, 2 * HYENA_WIDTH // LANES, GROUPS, n1, SUBLANES, LANES)
    tspec = pl.BlockSpec((kg, 2 * FFT_N2, 2 * FFT_N2), lambda i: (i, 0, 0))
    return pl.pallas_call(
        functools.partial(_filter_spectrum_kernel, kg=kg),
        out_shape=jax.ShapeDtypeStruct((n1, 2 * FFT_N2, HYENA_WIDTH), F32),
        grid=(n1 // kg,),
        in_specs=[_k1_spec(kg, 2 * HYENA_WIDTH), tspec, tspec,
                  pl.BlockSpec((SUBLANES, 2 * HYENA_WIDTH), lambda i: (0, 0))],
        out_specs=pl.BlockSpec((kg, 2 * FFT_N2, HYENA_WIDTH), lambda i: (i, 0, 0)),
        compiler_params=_params(("parallel",)),
        name="filter_spectrum",
    )(a6, gh, gl, sums)


def _dot_t(a, b):
    return lax.dot_general(a, b, (((0,), (0,)), ((), ())), preferred_element_type=F32)


def _s2_kernel(a_ref, kf_ref, gh_ref, gl_ref, o_ref, *, kg):
    half = FFT_N2
    for q in range(kg):
        ah, al = _split(_load_k1(a_ref, q))
        gh, gl = gh_ref[q], gl_ref[q]
        x = _dot3(gh, gl, ah, al)
        xr, xi = x[:half], x[half:]
        kr, ki = kf_ref[q, :half], kf_ref[q, half:]
        y = jnp.concatenate([xr * kr - xi * ki, xr * ki + xi * kr], axis=0)
        yh, yl = _split(y)
        bt = _dot_t(gh, yh) + _dot_t(gh, yl) + _dot_t(gl, yh)
        for cc in range(HYENA_WIDTH // LANES):
            o_ref[:, cc, :, q] = bt[:, cc * LANES:(cc + 1) * LANES].reshape(2, GROUPS, SUBLANES, LANES)


def _fft_stage2(a, kf, tables, n1):
    kg = 4
    a6 = a.reshape(2, HYENA_WIDTH // LANES, GROUPS, n1, SUBLANES, LANES)
    tspec = pl.BlockSpec((kg, 2 * FFT_N2, 2 * FFT_N2), lambda i: (i, 0, 0))
    dspec = _k1_spec(kg, HYENA_WIDTH)
    out = pl.pallas_call(
        functools.partial(_s2_kernel, kg=kg),
        out_shape=jax.ShapeDtypeStruct(a6.shape, F32),
        grid=(n1 // kg,),
        in_specs=[dspec, pl.BlockSpec((kg, 2 * FFT_N2, HYENA_WIDTH), lambda i: (i, 0, 0)), tspec, tspec],
        out_specs=dspec,
        compiler_params=_params(("parallel",)),
        name="fft_stage2",
    )(a6, kf, *tables)
    return out.reshape(a.shape)


def _s3_kernel(b_ref, vx_ref, x0_ref, bias_ref, fh_ref, fl_ref, o_ref, *, ncw, n1, n1h):
    fh = fh_ref[...]
    fl = fl_ref[...]
    for r in range(SUBLANES):
        krows = pl.ds(r, n1, stride=SUBLANES)
        z = jnp.concatenate(
            [jnp.concatenate([b_ref[ri, cc, krows, :] for ri in range(2)], axis=0) for cc in range(ncw)], axis=1)
        zh, zl = _split(z)
        y = _dot3(fh, fl, zh, zl)
        trows = pl.ds(r, n1h, stride=SUBLANES)
        for p in range(2):
            for cc in range(ncw):
                yy = y[p * n1h:(p + 1) * n1h, cc * LANES:(cc + 1) * LANES]
                o_ref[p, cc, trows, :] = (yy + vx_ref[p, cc, trows, :] * bias_ref[cc]) * x0_ref[p, cc, trows, :]


def _fft_stage3(bt, vx, x0, bias, fmat, n1, n1h, ncw):
    ncc, groups = bt.shape[1:3]
    fh, fl = _hilo(fmat)
    tspec = pl.BlockSpec((2, ncw, None, n1h * SUBLANES, LANES), lambda j, ci: (0, ci, j, 0, 0))
    return pl.pallas_call(
        functools.partial(_s3_kernel, ncw=ncw, n1=n1, n1h=n1h),
        out_shape=jax.ShapeDtypeStruct(vx.shape, F32),
        grid=(groups, ncc // ncw),
        in_specs=[
            pl.BlockSpec((2, ncw, None, n1 * SUBLANES, LANES), lambda j, ci: (0, ci, j, 0, 0)),
            tspec, tspec,
            pl.BlockSpec((ncw, 1, LANES), lambda j, ci: (ci, 0, 0)),
            pl.BlockSpec(fh.shape, lambda j, ci: (0, 0)),
            pl.BlockSpec(fl.shape, lambda j, ci: (0, 0)),
        ],
        out_specs=tspec,
        compiler_params=_params(("parallel", "parallel")),
        name="fft_stage3",
    )(bt, vx, x0, bias, fh, fl)


def _hyena_latent(p, hy, feats, deltas, tables, n):
    conv_w, conv_b, w1, b1, f1, w2, b2, f2, w3, bias = hy
    c = _fft_constants(n)
    n1, n1h = c["n1"], c["n1h"]
    vx, x0 = _hyena_prep(p, conv_w, conv_b, group_major=True)
    gm_shape = vx.shape
    rows = lambda a: a.reshape(*a.shape[:-3], n1h * SUBLANES, LANES)
    vx, x0 = rows(vx), rows(x0)
    taps, sums = _hyena_filter(n, feats, deltas, w1, b1, f1, w2, b2, f2, w3, group_major=True)
    ncw = 2
    a_f = _fft_stage1(rows(taps)[None], c["f1r"], n1, n1h, ncw)
    kf = _filter_spectrum(a_f, tables[0], tables[1], sums, n1)
    a = _fft_stage1(vx, c["f1c"], n1, n1h, ncw)
    bt = _fft_stage2(a, kf, tables, n1)
    out = _fft_stage3(bt, vx, x0, bias.reshape(HYENA_WIDTH // LANES, 1, LANES), c["f3c"], n1, n1h, ncw)
    return out.reshape(gm_shape)


@functools.lru_cache(maxsize=None)
def _dense_dft_constants(n):
    m = 2 * n
    k = np.arange(m)
    t = np.arange(n)
    f = np.exp(-2j * np.pi * np.outer(k, t) / m)
    finv = np.exp(2j * np.pi * np.outer(t, k) / m) / m
    return _stack_complex(f), np.concatenate([f.real, f.imag], axis=0), _stack_complex(finv)


def _dense_conv_kernel(vx_ref, x0_ref, hf_ref, hb_ref, sf_ref, sb_ref, bias_ref,
                       fch, fcl, frh, frl, fih, fil, o_ref, *, n):
    m = 2 * n
    z = jnp.concatenate([vx_ref[0], vx_ref[1]], axis=0)
    zh, zl = _split(z)
    zf = _dot3(fch[...], fcl[...], zh, zl)
    hfh, hfl = _split(hf_ref[...])
    hbh, hbl = _split(hb_ref[...])
    hf = _dot3(frh[...], frl[...], hfh, hfl)
    hb = _dot3(frh[...], frl[...], hbh, hbl)
    inv = 1.0 / (jnp.sum(sf_ref[...], axis=0, keepdims=True) + jnp.sum(sb_ref[...], axis=0, keepdims=True))
    kr = (hf[:m] + hb[:m]) * inv
    ki = (hf[m:] - hb[m:]) * inv
    zr, zi = zf[:m], zf[m:]
    y = jnp.concatenate([zr * kr - zi * ki, zr * ki + zi * kr], axis=0)
    yh, yl = _split(y)
    out = _dot3(fih[...], fil[...], yh, yl)
    bias = bias_ref[...]
    for p in range(2):
        o_ref[p] = (out[p * n:(p + 1) * n] + vx_ref[p] * bias) * x0_ref[p]


def _hyena_context(p, hy, feats, deltas, n):
    conv_w, conv_b, w1, b1, f1, w2, b2, f2, w3, bias = hy
    vx, x0 = _hyena_prep(p, conv_w, conv_b, group_major=False)
    taps, sums = _hyena_filter(n, feats, deltas, w1, b1, f1, w2, b2, f2, w3, group_major=False)
    fc, fr, fi = _dense_dft_constants(n)
    mats = [*_hilo(fc), *_hilo(fr), *_hilo(fi)]
    cw = 256
    nct = HYENA_WIDTH // cw
    dspec = pl.BlockSpec((2, n, cw), lambda ci: (0, 0, ci))
    return pl.pallas_call(
        functools.partial(_dense_conv_kernel, n=n),
        out_shape=jax.ShapeDtypeStruct(vx.shape, F32),
        grid=(nct,),
        in_specs=[dspec, dspec,
                  pl.BlockSpec((n, cw), lambda ci: (0, ci)), pl.BlockSpec((n, cw), lambda ci: (0, nct + ci)),
                  pl.BlockSpec((SUBLANES, cw), lambda ci: (0, ci)), pl.BlockSpec((SUBLANES, cw), lambda ci: (0, nct + ci)),
                  pl.BlockSpec((1, cw), lambda ci: (0, ci))]
                 + [pl.BlockSpec(mt.shape, lambda ci: (0, 0)) for mt in mats],
        out_specs=dspec,
        compiler_params=_params(("parallel",)),
        name="context_long_conv",
    )(vx, x0, taps, taps, sums, sums, bias.reshape(1, HYENA_WIDTH), *mats)


def _pool_kernel(x_ref, p_ref, n_ref, w_ref, sc_ref, o_ref, *, tl, nt, n):
    i = pl.program_id(1)
    x = x_ref[...]
    pv = jnp.where(i > 0, p_ref[...], 0.0)
    nx = jnp.where(i < nt - 1, n_ref[...], 0.0)
    ext = jnp.concatenate([pv, x, nx], axis=0)
    rows = tl + 2 * POOL_HALO
    t = i * tl + lax.broadcasted_iota(jnp.int32, (tl, POOL_GROUP), 0)
    for g, w in enumerate(POOL_WINDOWS):
        lanes = slice(g * POOL_GROUP, (g + 1) * POOL_GROUP)
        a = ext[:, lanes]
        c = a + pltpu.roll(a, 1, 0)
        h = 1
        while 2 * h < w:
            c = pltpu.roll(c, h, 0) + pltpu.roll(c, rows - h, 0)
            h *= 2
        total = c[POOL_HALO:POOL_HALO + tl]
        count = (jnp.minimum(t + h, n) - jnp.maximum(t - h, 0)).astype(F32)
        y = (total / count - x[:, lanes]).astype(BF16)
        o_ref[:, lanes] = _dot(y, w_ref[g].astype(BF16)) * sc_ref[:, lanes]


def _pool_mixer(p, w_pool, scale):
    b, n, _ = p.shape
    tl = min(512, n)
    nt = n // tl
    colblk = 3 * HYENA_WIDTH // POOL_WIDTH
    assert colblk * POOL_WIDTH == 3 * HYENA_WIDTH
    per = tl // POOL_HALO
    nrow = n // POOL_HALO
    return pl.pallas_call(
        functools.partial(_pool_kernel, tl=tl, nt=nt, n=n),
        out_shape=jax.ShapeDtypeStruct((b, n, POOL_WIDTH), F32),
        grid=(b, nt),
        in_specs=[
            pl.BlockSpec((None, tl, POOL_WIDTH), lambda bi, i: (bi, i, colblk)),
            pl.BlockSpec((None, POOL_HALO, POOL_WIDTH), lambda bi, i: (bi, jnp.maximum(i * per - 1, 0), colblk)),
            pl.BlockSpec((None, POOL_HALO, POOL_WIDTH), lambda bi, i: (bi, jnp.minimum((i + 1) * per, nrow - 1), colblk)),
            pl.BlockSpec(w_pool.shape, lambda bi, i: (0, 0, 0)),
            pl.BlockSpec((1, POOL_WIDTH), lambda bi, i: (0, 0)),
        ],
        out_specs=pl.BlockSpec((None, tl, POOL_WIDTH), lambda bi, i: (bi, i, 0)),
        compiler_params=_params(("parallel", "parallel")),
        name="pool_mixer",
    )(p, p, p, w_pool, scale.reshape(1, POOL_WIDTH))


def _load_group_major(ref, tl):
    ncc = ref.shape[0]
    return jnp.concatenate(
        [jnp.concatenate([ref[cc, jg, t1] for cc in range(ncc)], axis=1)
         for t1 in range(tl // FFT_N2) for jg in range(GROUPS)], axis=0)


def _outproj_kernel(at_ref, hy_ref, po_ref, x_ref, gb_ref, w_ref, gp_ref, gt_ref, o_ref, ox_ref, m_ref,
                    *, tm, tiles, group_major):
    t = pl.program_id(0)

    @pl.when(t == 0)
    def _():
        ox_ref[...] = jnp.zeros_like(ox_ref)

    def finish_previous():
        o_ref[...] = x_ref[...] + gt_ref[...] * _rms(ox_ref[...], gp_ref[...])

    @pl.when(t < tiles)
    def _():
        finish_previous()
        hy = _load_group_major(hy_ref, tm) if group_major else hy_ref[...]
        a0, a1 = ATTN_WIDTH, ATTN_WIDTH + HYENA_WIDTH
        m_ref[:, :a0] = _rms(at_ref[...], gb_ref[:, :a0]).astype(BF16)
        m_ref[:, a0:a1] = _rms(hy, gb_ref[:, a0:a1]).astype(BF16)
        m_ref[:, a1:] = _rms(po_ref[...], gb_ref[:, a1:]).astype(BF16)
        ox_ref[...] = _dot(m_ref[...], w_ref[...])

    pl.when(t == tiles)(finish_previous)


def _out_projection(attn, hy, po, x, g_branch, w_out, g_post, gate, group_major):
    b, n, d = x.shape
    tm = min(512, n)
    nt = n // tm
    tiles = b * nt

    def cur(t):
        return jnp.minimum(t, tiles - 1)

    def prev(t):
        return jnp.maximum(t - 1, 0)

    row = lambda width, sel: pl.BlockSpec((None, tm, width), lambda t: (sel(t) // nt, sel(t) % nt, 0))
    vec = lambda width: pl.BlockSpec((1, width), lambda t: (0, 0))
    if group_major:
        hy_spec = pl.BlockSpec((None, *_gm_shape((), tm // FFT_N2, HYENA_WIDTH)),
                               lambda t: (cur(t) // nt, 0, 0, cur(t) % nt, 0, 0))
    else:
        hy_spec = row(HYENA_WIDTH, cur)
    return pl.pallas_call(
        functools.partial(_outproj_kernel, tm=tm, tiles=tiles, group_major=group_major),
        out_shape=jax.ShapeDtypeStruct(x.shape, F32),
        grid=(tiles + 1,),
        in_specs=[row(ATTN_WIDTH, cur), hy_spec, row(POOL_WIDTH, cur), row(d, prev), vec(d),
                  pl.BlockSpec(w_out.shape, lambda t: (0, 0), pipeline_mode=pl.Buffered(1)), vec(d),
                  pl.BlockSpec((None, 1, d), lambda t: (prev(t) // nt, 0, 0))],
        out_specs=row(d, prev),
        scratch_shapes=[pltpu.VMEM((tm, d), F32), pltpu.VMEM((tm, d), BF16)],
        compiler_params=_params(("arbitrary",)),
        name="out_projection",
    )(attn, hy, po, x, g_branch.reshape(1, d), w_out, g_post.reshape(1, d), gate)


def _mlp_kernel(xe_ref, xp_ref, g_ref, shn_ref, scn_ref, wu_ref, wd_ref, gp_ref, gtp_ref, o_ref,
                h0_ref, h1_ref, acc0_ref, acc1_ref, *, rc, tiles):
    t = pl.program_id(0)
    k = pl.program_id(1)
    rows = pl.ds(pl.multiple_of(k * rc, rc), rc)

    def normalise(h_ref):
        a = g_ref[...] * (1.0 + scn_ref[...])
        h_ref[rows, :] = (_rms(xp_ref[...], a) + shn_ref[...]).astype(BF16)

    def finish(acc_ref):
        o_ref[...] = xe_ref[...] + _rms(acc_ref[rows, :], gp_ref[...] * gtp_ref[...])

    @pl.when(t == 0)
    def _():
        normalise(h0_ref)
        acc0_ref[rows, :] = jnp.zeros((rc, acc0_ref.shape[1]), F32)
        acc1_ref[rows, :] = jnp.zeros((rc, acc1_ref.shape[1]), F32)

    def step(slot):
        h_cur, h_oth = (h0_ref, h1_ref) if slot == 0 else (h1_ref, h0_ref)
        acc_cur, acc_oth = (acc0_ref, acc1_ref) if slot == 0 else (acc1_ref, acc0_ref)
        finish(acc_oth)
        acc_oth[rows, :] = jnp.zeros((rc, acc_oth.shape[1]), F32)
        normalise(h_oth)
        u = jnp.maximum(_dot(h_cur[...], wu_ref[...]), 0.0)
        acc_cur[...] += _dot((u * u).astype(BF16), wd_ref[...])

    work = (t >= 1) & (t <= tiles)
    pl.when(work & (t % 2 == 1))(functools.partial(step, 0))
    pl.when(work & (t % 2 == 0))(functools.partial(step, 1))
    pl.when(t == tiles + 1)(functools.partial(finish, acc0_ref if (tiles - 1) % 2 == 0 else acc1_ref))


MLP_TH = 1024


def _mlp(x, g_pre, shift, scale, w_up, w_down, g_post, gate):
    b, n, d = x.shape
    th = MLP_TH
    nk = w_up.shape[1] // th
    tm = min(1024, n)
    nt = n // tm
    tiles = b * nt
    rc = tm // nk
    assert rc * nk == tm and rc % BF16_ROWS == 0

    def norm_tile(t):
        return jnp.minimum(t, tiles - 1)

    def done_tile(t):
        return jnp.clip(t - 2, 0, tiles - 1)

    vec = pl.BlockSpec((1, d), lambda t, k: (0, 0))
    bvec = lambda sel: pl.BlockSpec((None, 1, d), lambda t, k: (sel(t) // nt, 0, 0))
    chunk = lambda sel, first: pl.BlockSpec(
        (None, rc, d), lambda t, k: (sel(t) // nt, (sel(t) % nt) * nk + jnp.where(t < first, 0, k), 0))
    busy = lambda t: (t >= 1) & (t <= tiles)
    return pl.pallas_call(
        functools.partial(_mlp_kernel, rc=rc, tiles=tiles),
        out_shape=jax.ShapeDtypeStruct(x.shape, F32),
        grid=(tiles + 2, nk),
        in_specs=[chunk(done_tile, 2), chunk(norm_tile, 0), vec, bvec(norm_tile), bvec(norm_tile),
                  pl.BlockSpec((d, th), lambda t, k: (0, jnp.where(busy(t), k, 0))),
                  pl.BlockSpec((th, d), lambda t, k: (jnp.where(busy(t), k, 0), 0)),
                  vec, bvec(done_tile)],
        out_specs=chunk(done_tile, 2),
        scratch_shapes=[pltpu.VMEM((tm, d), BF16)] * 2 + [pltpu.VMEM((tm, d), F32)] * 2,
        compiler_params=_params(("arbitrary", "arbitrary")),
        name="mlp",
    )(x, x, g_pre.reshape(1, d), shift, scale, w_up, w_down, g_post.reshape(1, d), gate)


def kernel(x, c, ctx, c_ctx, w_mod, b_mod, g_pre_mix, g_post_mix, g_pre_mlp, g_post_mlp, w_in, w_out, g_branch,
           attn_sink, hy_conv_w, hy_conv_b, hy_w1, hy_b1, hy_freq1, hy_w2, hy_b2, hy_freq2, hy_w3, hy_bias,
           pool_w, pool_scale, w_up, w_down):
    b, n, d = x.shape
    n_ctx = ctx.shape[1]
    depth = w_mod.shape[0]
    assert b == 2 and d == D_MODEL and n % 512 == 0 and n_ctx % BLOCK == 0

    cond = jnp.concatenate([c, c_ctx[None], jnp.zeros((SUBLANES - b - 1, d), F32)], axis=0)
    mods = _modulation(cond, w_mod, b_mod)

    w_in_b = _column_tiles(_permute_rope_columns(w_in.astype(BF16)), IN_TN)
    w_up_b = w_up.astype(BF16)
    w_out_b, w_down_b = w_out.astype(BF16), w_down.astype(BF16)
    tables_x, tables_c = _rope_tables(n), _rope_tables(n_ctx)
    feats_x, feats_c = _filter_features(n), _filter_features(n_ctx)
    deltas = _filter_deltas()
    tables = _stage2_tables(n)

    for i in range(depth):
        last = i == depth - 1
        hy = (hy_conv_w[i], hy_conv_b[i], hy_w1[i], hy_b1[i], hy_freq1[i], hy_w2[i], hy_b2[i], hy_freq2[i],
              hy_w3[i], hy_bias[i])
        mx = [m[:, None, :] for m in jnp.split(mods[i, :b], N_MOD, axis=-1)]
        mc = [jnp.broadcast_to(m[None, None, :], (b, 1, d)) for m in jnp.split(mods[i, b], N_MOD, axis=-1)]

        qkv_x, px = _in_projection(x, g_pre_mix[i], mx[0], mx[1], w_in_b[i], tables_x, rope=True)
        qkv_c, pc = _in_projection(ctx, g_pre_mix[i], mc[0], mc[1], w_in_b[i], tables_c, rope=False)
        kv_ctx = qkv_c[..., Q_END:V_END]

        attn_x = _attention(qkv_x, kv_ctx, attn_sink[i], local=True)
        hy_x = _hyena_latent(px, hy, feats_x, deltas, tables, n)
        po_x = _pool_mixer(px, pool_w[i], pool_scale[i])
        x = _out_projection(attn_x, hy_x, po_x, x, g_branch[i], w_out_b[i], g_post_mix[i], mx[2], group_major=True)
        x = _mlp(x, g_pre_mlp[i], mx[3], mx[4], w_up_b[i], w_down_b[i], g_post_mlp[i], mx[5])

        if not last:
            attn_c = _attention(qkv_c, kv_ctx, attn_sink[i], local=False)
            hy_c = _hyena_context(pc, hy, feats_c, deltas, n_ctx)
            po_c = _pool_mixer(pc, pool_w[i], pool_scale[i])
            ctx = _out_projection(attn_c, hy_c, po_c, ctx, g_branch[i], w_out_b[i], g_post_mix[i], mc[2],
                                  group_major=False)
            ctx = _mlp(ctx, g_pre_mlp[i], mc[3], mc[4], w_up_b[i], w_down_b[i], g_post_mlp[i], mc[5])
    return x
```

```python
import functools
import math

import numpy as np
import jax
import jax.numpy as jnp
from jax import lax
from jax.experimental import pallas as pl
from jax.experimental.pallas import tpu as pltpu

F32 = jnp.float32
BF16 = jnp.bfloat16

D_MODEL = 2048
DEPTH = 4
GRID_W = 64
ATTN_WIDTH = D_MODEL // 2
HYENA_WIDTH = D_MODEL // 4
POOL_WIDTH = D_MODEL - ATTN_WIDTH - HYENA_WIDTH
HEAD_DIM = 128
N_HEADS = ATTN_WIDTH // HEAD_DIM
N_KV_HEADS = 2
KV_GROUP = N_HEADS // N_KV_HEADS
KV_WIDTH = N_KV_HEADS * HEAD_DIM
WINDOW = 128
BLOCK = 128
ROPE_BASE = 10000.0
HYENA_EMB_DIM = 33
HYENA_FILTER_HIDDEN = 64
HYENA_FAST_DECAY_PCT = 0.3
HYENA_SLOW_DECAY_PCT = 1.5
HYENA_DECAY_TARGET = 1e-2
POOL_WINDOWS = (2, 4, 8, 16)
POOL_GROUP = POOL_WIDTH // len(POOL_WINDOWS)
MLP_HIDDEN = 4 * D_MODEL
N_MOD = 6
EPS = 1e-6
NEG_INF = -1e30

Q_END = ATTN_WIDTH
K_END = Q_END + KV_WIDTH
V_END = K_END + KV_WIDTH
HY_END = V_END + 3 * HYENA_WIDTH
IN_WIDTH = HY_END + POOL_WIDTH

LANES = 128
SUBLANES = 8
FFT_N2 = 128
POOL_HALO = 16
VMEM_LIMIT = 56 * 1024 * 1024


def _params(sem, vmem=VMEM_LIMIT):
    return pltpu.CompilerParams(dimension_semantics=sem, vmem_limit_bytes=vmem)


def _split(x):
    hi = x.astype(BF16)
    lo = (x - hi.astype(F32)).astype(BF16)
    return hi, lo


def _dot(a, b):
    return jnp.dot(a, b, preferred_element_type=F32)


def _dot3(ah, al, bh, bl):
    return _dot(ah, bh) + _dot(ah, bl) + _dot(al, bh)


def _rms(x, g):
    return x * lax.rsqrt(jnp.mean(x * x, axis=-1, keepdims=True) + EPS) * g


BF16_ROWS = 2 * SUBLANES


def _norm_scale_rows(x_ref, h_ref, a_ref, s_ref, rows, cols=None):
    cols = slice(None) if cols is None else cols
    nchunks = rows // BF16_ROWS

    def chunk(c):
        return pl.ds(pl.multiple_of(c * BF16_ROWS, BF16_ROWS), BF16_ROWS)

    def inv_rms(c):
        x = x_ref[chunk(c), :]
        return lax.rsqrt(jnp.mean(x * x, axis=-1, keepdims=True) + EPS)

    def body(c, inv):
        inv_next = inv_rms(jnp.minimum(c + 1, nchunks - 1))
        y = x_ref[chunk(c), :] * inv * a_ref[...]
        if s_ref is not None:
            y = y + s_ref[...]
        h_ref[chunk(c), cols] = y.astype(BF16)
        return inv_next

    lax.fori_loop(0, nchunks, body, inv_rms(0), unroll=8)


def _residual_norm_rows(x_ref, y_ref, o_ref, pg_ref, rows):
    nchunks = rows // SUBLANES

    def chunk(c):
        return pl.ds(pl.multiple_of(c * SUBLANES, SUBLANES), SUBLANES)

    def inv_rms(c):
        y = y_ref[chunk(c), :]
        return lax.rsqrt(jnp.mean(y * y, axis=-1, keepdims=True) + EPS)

    def body(c, inv):
        inv_next = inv_rms(jnp.minimum(c + 1, nchunks - 1))
        o_ref[chunk(c), :] = x_ref[chunk(c), :] + y_ref[chunk(c), :] * inv * pg_ref[...]
        return inv_next

    lax.fori_loop(0, nchunks, body, inv_rms(0), unroll=16)


def _mod_kernel(c_ref, w_ref, b_ref, o_ref):
    c = c_ref[...]
    s = c / (1.0 + jnp.exp(-c))
    sh, sl = _split(s)
    wh, wl = _split(w_ref[...])
    o_ref[...] = _dot3(sh, sl, wh, wl) + b_ref[...]


def _modulation(cond, w_mod, b_mod):
    depth, d, width = w_mod.shape
    tn = 1024
    return pl.pallas_call(
        _mod_kernel,
        out_shape=jax.ShapeDtypeStruct((depth, SUBLANES, width), F32),
        grid=(depth, width // tn),
        in_specs=[
            pl.BlockSpec((SUBLANES, d), lambda l, j: (0, 0)),
            pl.BlockSpec((None, d, tn), lambda l, j: (l, 0, j)),
            pl.BlockSpec((None, 1, tn), lambda l, j: (l, 0, j)),
        ],
        out_specs=pl.BlockSpec((None, SUBLANES, tn), lambda l, j: (l, 0, j)),
        compiler_params=_params(("parallel", "parallel")),
        name="modulation",
    )(cond, w_mod, b_mod.reshape(depth, 1, width))


QKV_WIDTH = V_END
REST_WIDTH = IN_WIDTH - V_END
SM_SCALE = HEAD_DIM ** -0.5


TABLE_Q, TABLE_K, TABLE_ID, TABLE_SCALE = range(4)


def _inproj_kernel(xn_ref, g_ref, shn_ref, scn_ref, w_ref, ca_ref, sa_ref, cb_ref, sb_ref,
                   qkv_ref, rest_ref, h0_ref, h1_ref, acc_ref, *, tm, tn, rc, tiles):
    t = pl.program_id(0)
    j = pl.program_id(1)

    def normalise(h_ref):
        rows = pl.ds(pl.multiple_of(jnp.minimum(j * rc, tm - rc), BF16_ROWS), rc)
        a = g_ref[...] * (1.0 + scn_ref[...])
        h_ref[rows, :] = (_rms(xn_ref[rows, :], a) + shn_ref[...]).astype(BF16)

    def finish_previous():
        acc = acc_ref[...]
        rest_ref[...] = acc
        nch = tn // LANES
        for ch in range(nch):
            a = acc[:, ch * LANES:(ch + 1) * LANES]
            cos, sin = (ca_ref, sa_ref) if ch < nch // 2 else (cb_ref, sb_ref)
            qkv_ref[:, ch * LANES:(ch + 1) * LANES] = (a * cos[...] + pltpu.roll(a, HEAD_DIM // 2, 1) * sin[...]).astype(BF16)

    @pl.when(t == 0)
    def _():
        normalise(h0_ref)

    @pl.when((t == 0) & (j == 0))
    def _():
        acc_ref[...] = jnp.zeros_like(acc_ref)

    def step(slot):
        h_cur, h_oth = (h0_ref, h1_ref) if slot == 0 else (h1_ref, h0_ref)
        finish_previous()
        normalise(h_oth)
        acc_ref[...] = _dot(h_cur[...], w_ref[j])

    work = (t >= 1) & (t <= tiles)
    pl.when(work & (t % 2 == 1))(functools.partial(step, 0))
    pl.when(work & (t % 2 == 0))(functools.partial(step, 1))
    pl.when((t == tiles + 1) & (j == 0))(finish_previous)


def _column_tiles(w, tn):
    *lead, d, width = w.shape
    return jnp.swapaxes(w.reshape(*lead, d, width // tn, tn), -3, -2)


IN_TN = 512
QKV_SPARE = QKV_WIDTH // IN_TN
REST_SPARE = REST_WIDTH // IN_TN


def _in_projection(x, g, shift, scale, w, tables, rope):
    b, n, d = x.shape
    tm = min(1024, n)
    tn = w.shape[-1]
    nj = w.shape[0]
    nt = n // tm
    tiles = b * nt
    half = tn // 2
    assert tn == IN_TN and nj * tn == IN_WIDTH and Q_END % tn == 0 and K_END % tn == half and V_END % tn == 0
    rc = -(-tm // nj)
    rc = -(-rc // (2 * BF16_ROWS)) * 2 * BF16_ROWS
    nq, nqkv = Q_END // tn, QKV_WIDTH // tn

    def norm_tile(t):
        return jnp.minimum(t, tiles - 1)

    def lagged(t, j):
        step = jnp.clip((t - 1) * nj + j - 1, 0, tiles * nj - 1)
        return step // nj, step % nj

    def kind_a(j):
        return jnp.where(j < nq, TABLE_Q if rope else TABLE_SCALE, jnp.where(j < nqkv, TABLE_K if rope else TABLE_ID, TABLE_ID))

    def kind_b(j):
        return jnp.where(j < nq, TABLE_Q if rope else TABLE_SCALE, TABLE_ID)

    vec = pl.BlockSpec((1, d), lambda t, j: (0, 0))
    bvec = lambda sel: pl.BlockSpec((None, 1, d), lambda t, j: (sel(t) // nt, 0, 0))
    def table(kind):
        def index(t, j):
            tp, jp = lagged(t, j)
            k = kind(jp)
            return k, jnp.where(k == TABLE_ID, 0, tp % nt), 0
        return pl.BlockSpec((None, tm, LANES), index)

    def out_spec(col_tile):
        return pl.BlockSpec((None, tm, tn), lambda t, j: (lagged(t, j)[0] // nt, lagged(t, j)[0] % nt,
                                                          col_tile(lagged(t, j)[1])))

    cos, sin = tables
    return pl.pallas_call(
        functools.partial(_inproj_kernel, tm=tm, tn=tn, rc=rc, tiles=tiles),
        out_shape=[jax.ShapeDtypeStruct((b, n, QKV_WIDTH + tn), BF16), jax.ShapeDtypeStruct((b, n, REST_WIDTH + tn), F32)],
        grid=(tiles + 2, nj),
        in_specs=[pl.BlockSpec((None, tm, d), lambda t, j: (norm_tile(t) // nt, norm_tile(t) % nt, 0)),
                  vec, bvec(norm_tile), bvec(norm_tile),
                  pl.BlockSpec(w.shape, lambda t, j: (0, 0, 0), pipeline_mode=pl.Buffered(1)),
                  table(kind_a), table(kind_a), table(kind_b), table(kind_b)],
        out_specs=[out_spec(lambda jp: jnp.minimum(jp, QKV_SPARE)),
                   out_spec(lambda jp: jnp.where(jp >= nqkv, jp - nqkv, REST_SPARE))],
        scratch_shapes=[pltpu.VMEM((tm, d), BF16)] * 2 + [pltpu.VMEM((tm, tn), F32)],
        compiler_params=_params(("arbitrary", "arbitrary")),
        name="in_projection_rope" if rope else "in_projection",
    )(x, g.reshape(1, d), shift, scale, w, cos, sin, cos, sin)


def _permute_rope_columns(w_in):
    lead = w_in.shape[:-1]
    quarter = HEAD_DIM // 4
    qk = w_in[..., :K_END].reshape(*lead, K_END // HEAD_DIM, 2, 2, quarter)
    qk = jnp.swapaxes(qk, -3, -2).reshape(*lead, K_END)
    return jnp.concatenate([qk, w_in[..., K_END:]], axis=-1)


def _rope_tables(n):
    quarter = HEAD_DIM // 4
    inv_freq = ROPE_BASE ** (-jnp.arange(quarter, dtype=F32) / quarter)
    t = jnp.arange(n, dtype=jnp.int32)
    row = (t // GRID_W).astype(F32)[:, None] * inv_freq[None, :]
    col = (t % GRID_W).astype(F32)[:, None] * inv_freq[None, :]
    cos = jnp.concatenate([jnp.cos(row), jnp.cos(col), jnp.cos(row), jnp.cos(col)], axis=-1)
    sin = jnp.concatenate([-jnp.sin(row), -jnp.sin(col), jnp.sin(row), jnp.sin(col)], axis=-1)
    one, zero = jnp.ones_like(cos), jnp.zeros_like(sin)
    return (jnp.stack([cos * SM_SCALE, cos, one, one * SM_SCALE]), jnp.stack([sin * SM_SCALE, sin, zero, zero]))


def _softmax_pv(parts, sink_col):
    m = sink_col
    for s, _ in parts:
        m = jnp.maximum(m, jnp.max(s, axis=-1, keepdims=True))
    den = jnp.exp(sink_col - m)
    out = None
    for s, v in parts:
        p = jnp.exp(s - m)
        den = den + jnp.sum(p, axis=-1, keepdims=True)
        pv = _dot(p.astype(BF16), v)
        out = pv if out is None else out + pv
    return out / den


def _ctx_attn_kernel(sink_ref, q_ref, kvc_ref, o_ref, *, tq):
    for g in range(N_KV_HEADS):
        heads = [g * KV_GROUP + h for h in range(KV_GROUP)]
        qg = jnp.concatenate([q_ref[:, h * HEAD_DIM:(h + 1) * HEAD_DIM] for h in heads], axis=0)
        sink_col = jnp.concatenate([jnp.full((tq, 1), sink_ref[h], F32) for h in heads], axis=0)
        kc = kvc_ref[:, g * HEAD_DIM:(g + 1) * HEAD_DIM]
        vc = kvc_ref[:, KV_WIDTH + g * HEAD_DIM:KV_WIDTH + (g + 1) * HEAD_DIM]
        s = lax.dot_general(qg, kc, (((1,), (1,)), ((), ())), preferred_element_type=F32)
        o = _softmax_pv([(s, vc)], sink_col)
        for hi, h in enumerate(heads):
            o_ref[:, h * HEAD_DIM:(h + 1) * HEAD_DIM] = o[hi * tq:(hi + 1) * tq]


ATTN_ROWS = KV_GROUP * BLOCK
SOFTMAX_CHUNK = 32


def _win_attn_kernel(sink_ref, q_ref, km_ref, kp_ref, kn_ref, vm_ref, vp_ref, vn_ref, kvc_ref, band_ref,
                     o_ref, ktw, vw, s_scr, p_scr, m_scr, *, tq, nb, nctx):
    i = pl.program_id(1)
    nsub = tq // BLOCK
    nloc = 3 * BLOCK

    def transposed(x):
        return x.astype(F32).T.astype(BF16)

    def block_rows(main_ref, prev_ref, next_ref, w, lanes):
        if w == 0:
            return prev_ref[:, lanes]
        if w == nsub + 1:
            return next_ref[:, lanes]
        return main_ref[(w - 1) * BLOCK:w * BLOCK, lanes]

    ones = jnp.ones((nloc + nctx, HEAD_DIM), BF16)
    for g in range(N_KV_HEADS):
        lanes = slice(g * HEAD_DIM, (g + 1) * HEAD_DIM)
        vlanes = slice(KV_WIDTH + g * HEAD_DIM, KV_WIDTH + (g + 1) * HEAD_DIM)
        kts = [transposed(block_rows(km_ref, kp_ref, kn_ref, w, lanes)) for w in range(nsub + 2)]
        kct = [transposed(kvc_ref[cb * BLOCK:(cb + 1) * BLOCK, lanes]) for cb in range(nctx // BLOCK)]
        for jb in range(nsub):
            for w in range(3):
                ktw[jb, g, :, w * BLOCK:(w + 1) * BLOCK] = kts[jb + w]
                vw[jb, g, w * BLOCK:(w + 1) * BLOCK, :HEAD_DIM] = block_rows(vm_ref, vp_ref, vn_ref, jb + w, lanes)
            for cb in range(nctx // BLOCK):
                ktw[jb, g, :, nloc + cb * BLOCK:nloc + (cb + 1) * BLOCK] = kct[cb]
            vw[jb, g, nloc:, :HEAD_DIM] = kvc_ref[:, vlanes]
            vw[jb, g, :, HEAD_DIM:] = ones

    col = lax.broadcasted_iota(jnp.int32, (1, nloc), 1)

    def rows_of(jb):
        start = jb * BLOCK
        return pl.ds(start if isinstance(start, int) else pl.multiple_of(start, BLOCK), BLOCK)

    def stage_a(jb, g):
        qg = jnp.concatenate([q_ref[rows_of(jb), (g * KV_GROUP + h) * HEAD_DIM:(g * KV_GROUP + h + 1) * HEAD_DIM]
                              for h in range(KV_GROUP)], axis=0)
        s_scr[g] = _dot(qg, ktw[jb, g])

    def stage_b(jb, g):
        blk = i * nsub + jb
        pen_prev = jnp.where(blk == 0, NEG_INF, 0.0).astype(F32)
        pen_next = jnp.where(blk == nb - 1, NEG_INF, 0.0).astype(F32)
        rowbias = jnp.where(col < BLOCK, pen_prev, jnp.where(col >= 2 * BLOCK, pen_next, 0.0))
        for c in range(ATTN_ROWS // SOFTMAX_CHUNK):
            rows = slice(c * SOFTMAX_CHUNK, (c + 1) * SOFTMAX_CHUNK)
            sink = sink_ref[g * KV_GROUP + (c * SOFTMAX_CHUNK) // BLOCK]
            s_loc = s_scr[g, rows, :nloc] + band_ref[rows, :] + rowbias
            s_ctx = s_scr[g, rows, nloc:]
            m = jnp.maximum(jnp.max(s_loc, axis=-1, keepdims=True), jnp.max(s_ctx, axis=-1, keepdims=True))
            m = jnp.maximum(m, sink)
            p_scr[g, rows, :nloc] = jnp.exp(s_loc - m).astype(BF16)
            p_scr[g, rows, nloc:] = jnp.exp(s_ctx - m).astype(BF16)
            m_scr[g, rows, :] = m

    def stage_c(jb, g):
        o = _dot(p_scr[g], vw[jb, g])
        for hi in range(KV_GROUP):
            h = g * KV_GROUP + hi
            rows = slice(hi * BLOCK, (hi + 1) * BLOCK)
            den = o[rows, HEAD_DIM:HEAD_DIM + 1] + jnp.exp(sink_ref[h] - m_scr[g, rows, :])
            o_ref[rows_of(jb), h * HEAD_DIM:(h + 1) * HEAD_DIM] = o[rows, :HEAD_DIM] / den

    stage_a(0, 0)
    stage_a(0, 1)
    stage_b(0, 0)

    def body(j, carry):
        stage_a(j, 0)
        stage_c(j - 1, 0)
        stage_b(j - 1, 1)
        stage_a(j, 1)
        stage_c(j - 1, 1)
        stage_b(j, 0)
        return carry

    lax.fori_loop(1, nsub, body, 0)
    stage_c(nsub - 1, 0)
    stage_b(nsub - 1, 1)
    stage_c(nsub - 1, 1)


def _band_bias():
    qi = np.arange(ATTN_ROWS)[:, None] % BLOCK
    sj = np.arange(3 * BLOCK)[None, :]
    return jnp.asarray(np.where(np.abs(sj - BLOCK - qi) <= WINDOW, 0.0, NEG_INF), dtype=F32)


def _attention(qkv, kv_ctx, sink, local):
    b, n, _ = qkv.shape
    c = kv_ctx.shape[1]
    tq = min(1024 if local else 512, n)
    nsub = tq // BLOCK
    nb = n // BLOCK
    kcol = Q_END // KV_WIDTH
    vcol = K_END // KV_WIDTH
    smem = pl.BlockSpec(memory_space=pltpu.SMEM)
    q_spec = pl.BlockSpec((None, tq, ATTN_WIDTH), lambda bi, i: (bi, i, 0))
    kvc_spec = pl.BlockSpec((None, c, 2 * KV_WIDTH), lambda bi, i: (bi, 0, 0))
    if local:
        def main(colblk):
            return pl.BlockSpec((None, tq, KV_WIDTH), lambda bi, i: (bi, i, colblk))

        def prev(colblk):
            return pl.BlockSpec((None, BLOCK, KV_WIDTH), lambda bi, i: (bi, jnp.maximum(i * nsub - 1, 0), colblk))

        def nxt(colblk):
            return pl.BlockSpec((None, BLOCK, KV_WIDTH), lambda bi, i: (bi, jnp.minimum((i + 1) * nsub, nb - 1), colblk))

        keys = 3 * BLOCK + c
        kern = functools.partial(_win_attn_kernel, tq=tq, nb=nb, nctx=c)
        in_specs = [smem, q_spec, main(kcol), prev(kcol), nxt(kcol), main(vcol), prev(vcol), nxt(vcol), kvc_spec,
                    pl.BlockSpec((ATTN_ROWS, 3 * BLOCK), lambda bi, i: (0, 0))]
        args = [sink, qkv, qkv, qkv, qkv, qkv, qkv, qkv, kv_ctx, _band_bias()]
        scratch = [pltpu.VMEM((nsub, N_KV_HEADS, HEAD_DIM, keys), BF16),
                   pltpu.VMEM((nsub, N_KV_HEADS, keys, 2 * HEAD_DIM), BF16),
                   pltpu.VMEM((2, ATTN_ROWS, keys), F32),
                   pltpu.VMEM((2, ATTN_ROWS, keys), BF16),
                   pltpu.VMEM((2, ATTN_ROWS, 1), F32)]
    else:
        kern = functools.partial(_ctx_attn_kernel, tq=tq)
        in_specs = [smem, q_spec, kvc_spec]
        args = [sink, qkv, kv_ctx]
        scratch = []
    return pl.pallas_call(
        kern,
        out_shape=jax.ShapeDtypeStruct((b, n, ATTN_WIDTH), F32),
        grid=(b, n // tq),
        in_specs=in_specs,
        out_specs=pl.BlockSpec((None, tq, ATTN_WIDTH), lambda bi, i: (bi, i, 0)),
        scratch_shapes=scratch,
        compiler_params=_params(("parallel", "parallel")),
        name="window_attention" if local else "context_attention",
    )(*args)


GROUPS = FFT_N2 // SUBLANES


def _gm_shape(lead, t1, width):
    return (*lead, width // LANES, GROUPS, t1, SUBLANES, LANES)


def _store_group_major(o_ref, val, tl):
    for t1 in range(tl // FFT_N2):
        for jg in range(GROUPS):
            r0 = (t1 * GROUPS + jg) * SUBLANES
            for cc in range(val.shape[1] // LANES):
                o_ref[cc, jg, t1] = val[r0:r0 + SUBLANES, cc * LANES:(cc + 1) * LANES]


def _hyena_prep_kernel(u_ref, p_ref, n_ref, w_ref, b_ref, vx_ref, x0_ref, *, tl, nt, group_major):
    i = pl.program_id(1)
    u = u_ref[...]
    prev_row = jnp.where(i > 0, p_ref[SUBLANES - 1:SUBLANES, :], 0.0)
    next_row = jnp.where(i < nt - 1, n_ref[0:1, :], 0.0)
    row = lax.broadcasted_iota(jnp.int32, u.shape, 0)
    um = jnp.where(row == 0, prev_row, pltpu.roll(u, 1, 0))
    up = jnp.where(row == tl - 1, next_row, pltpu.roll(u, tl - 1, 0))
    z = um * w_ref[0:1, :] + u * w_ref[1:2, :] + up * w_ref[2:3, :] + b_ref[...]
    x0 = z[:, :HYENA_WIDTH]
    vx = z[:, 2 * HYENA_WIDTH:] * z[:, HYENA_WIDTH:2 * HYENA_WIDTH]
    if group_major:
        _store_group_major(vx_ref, vx, tl)
        _store_group_major(x0_ref, x0, tl)
    else:
        vx_ref[...] = vx
        x0_ref[...] = x0


def _hyena_prep(p, conv_w, conv_b, group_major):
    b, n, _ = p.shape
    tl = min(512, n)
    nt = n // tl
    hw = 3 * HYENA_WIDTH
    colblk = 0
    nrow8 = n // SUBLANES
    per = tl // SUBLANES
    if group_major:
        shape = _gm_shape((b,), n // FFT_N2, HYENA_WIDTH)
        out_spec = pl.BlockSpec((None, *_gm_shape((), tl // FFT_N2, HYENA_WIDTH)), lambda bi, i: (bi, 0, 0, i, 0, 0))
    else:
        shape = (b, n, HYENA_WIDTH)
        out_spec = pl.BlockSpec((None, tl, HYENA_WIDTH), lambda bi, i: (bi, i, 0))
    return pl.pallas_call(
        functools.partial(_hyena_prep_kernel, tl=tl, nt=nt, group_major=group_major),
        out_shape=[jax.ShapeDtypeStruct(shape, F32)] * 2,
        grid=(b, nt),
        in_specs=[
            pl.BlockSpec((None, tl, hw), lambda bi, i: (bi, i, colblk)),
            pl.BlockSpec((None, SUBLANES, hw), lambda bi, i: (bi, jnp.maximum(i * per - 1, 0), colblk)),
            pl.BlockSpec((None, SUBLANES, hw), lambda bi, i: (bi, jnp.minimum((i + 1) * per, nrow8 - 1), colblk)),
            pl.BlockSpec((3, hw), lambda bi, i: (0, 0)),
            pl.BlockSpec((1, hw), lambda bi, i: (0, 0)),
        ],
        out_specs=[out_spec, out_spec],
        compiler_params=_params(("parallel", "parallel")),
        name="hyena_prep",
    )(p, p, p, conv_w, conv_b.reshape(1, hw))


def _filter_kernel(ft_ref, w1_ref, b1_ref, f1_ref, w2_ref, b2_ref, f2_ref, w3_ref, dl_ref, h_ref, s_ref,
                   *, tl, n, group_major):
    i = pl.program_id(0)

    def dense(a, w_ref):
        ah, al = _split(a)
        wh, wl = _split(w_ref[...])
        return _dot3(ah, al, wh, wl)

    h = jnp.sin(f1_ref[...] * (dense(ft_ref[...], w1_ref) + b1_ref[...]))
    h = jnp.sin(f2_ref[...] * (dense(h, w2_ref) + b2_ref[...]))
    h = dense(h, w3_ref)
    t = (i * tl + lax.broadcasted_iota(jnp.int32, (tl, HYENA_WIDTH), 0)).astype(F32) / float(n - 1)
    decay = jnp.exp(-t * dl_ref[...])
    h = h * jnp.concatenate([decay, decay], axis=1)
    if group_major:
        _store_group_major(h_ref, h, tl)
    else:
        h_ref[...] = h

    @pl.when(i == 0)
    def _():
        s_ref[...] = jnp.zeros_like(s_ref)

    s_ref[...] += jnp.sum(jnp.abs(h).reshape(tl // SUBLANES, SUBLANES, 2 * HYENA_WIDTH), axis=0)


def _filter_features(n):
    t = jnp.linspace(0.0, 1.0, n, dtype=F32)[:, None]
    bands = (HYENA_EMB_DIM - 1) // 2
    omega = 2.0 * math.pi * jnp.arange(n, dtype=F32)[:, None] / n
    f = jnp.linspace(1e-4, bands - 1, bands, dtype=F32)[None, :]
    feats = jnp.concatenate([t, jnp.cos(f * omega), -jnp.sin(f * omega)], axis=-1)
    return jnp.pad(feats, ((0, 0), (0, LANES - HYENA_EMB_DIM)))


def _filter_deltas():
    max_decay = math.log(HYENA_DECAY_TARGET) / HYENA_FAST_DECAY_PCT
    min_decay = math.log(HYENA_DECAY_TARGET) / HYENA_SLOW_DECAY_PCT
    return jnp.abs(jnp.linspace(min_decay, max_decay, HYENA_WIDTH, dtype=F32)).reshape(1, HYENA_WIDTH)


def _hyena_filter(n, feats, deltas, w1, b1, f1, w2, b2, f2, w3, group_major):
    tl = min(512, n)
    hid = LANES
    pad_h = hid - HYENA_FILTER_HIDDEN
    w1p = jnp.pad(w1, ((0, LANES - HYENA_EMB_DIM), (0, pad_h)))
    w2p = jnp.pad(w2, ((0, pad_h), (0, pad_h)))
    w3p = jnp.pad(w3, ((0, pad_h), (0, 0)))
    vec = lambda v: jnp.pad(v, (0, pad_h)).reshape(1, hid)
    hw2 = 2 * HYENA_WIDTH
    full = lambda shape: pl.BlockSpec(shape, lambda i: (0,) * len(shape))
    if group_major:
        shape = _gm_shape((), n // FFT_N2, hw2)
        out_spec = pl.BlockSpec(_gm_shape((), tl // FFT_N2, hw2), lambda i: (0, 0, i, 0, 0))
    else:
        shape = (n, hw2)
        out_spec = pl.BlockSpec((tl, hw2), lambda i: (i, 0))
    return pl.pallas_call(
        functools.partial(_filter_kernel, tl=tl, n=n, group_major=group_major),
        out_shape=[jax.ShapeDtypeStruct(shape, F32), jax.ShapeDtypeStruct((SUBLANES, hw2), F32)],
        grid=(n // tl,),
        in_specs=[pl.BlockSpec((tl, LANES), lambda i: (i, 0)), full((LANES, hid)), full((1, hid)), full((1, hid)),
                  full((hid, hid)), full((1, hid)), full((1, hid)), full((hid, hw2)), full((1, HYENA_WIDTH))],
        out_specs=[out_spec, full((SUBLANES, hw2))],
        compiler_params=_params(("arbitrary",)),
        name="hyena_filter",
    )(feats, w1p, vec(b1), vec(f1), w2p, vec(b2), vec(f2), w3p, deltas)


def _stack_complex(m):
    return np.block([[m.real, -m.imag], [m.imag, m.real]])


def _hilo(m):
    m = jnp.asarray(m, dtype=F32)
    return _split(m)


@functools.lru_cache(maxsize=None)
def _fft_constants(n):
    m = 2 * n
    n2 = FFT_N2
    n1 = m // n2
    n1h = n1 // 2
    k1 = np.arange(n1)
    t1 = np.arange(n1h)
    f1 = np.exp(-2j * np.pi * np.outer(k1, t1) / n1)
    f3 = np.exp(2j * np.pi * np.outer(t1, k1) / n1) / m
    k2 = np.arange(n2)
    t2 = np.arange(n2)
    w2 = np.exp(-2j * np.pi * np.outer(k2, t2) / n2)
    tw = np.exp(-2j * np.pi * np.outer(k1, t2) / m)
    return dict(
        n1=n1, n1h=n1h,
        f1c=_stack_complex(f1), f1r=np.concatenate([f1.real, f1.imag], axis=0),
        f3c=_stack_complex(f3),
        w2r=w2.real.astype(np.float32), w2i=w2.imag.astype(np.float32),
        twr=tw.real.astype(np.float32), twi=tw.imag.astype(np.float32),
    )


def _stage2_tables(n):
    c = _fft_constants(n)
    w2r, w2i = jnp.asarray(c["w2r"])[None], jnp.asarray(c["w2i"])[None]
    twr, twi = jnp.asarray(c["twr"])[:, None, :], jnp.asarray(c["twi"])[:, None, :]
    return (*_split(w2r * twr - w2i * twi), *_split(w2r * twi + w2i * twr))


def _stacked_tables(grh_ref, grl_ref, gih_ref, gil_ref, q):
    def stack(gr, gi):
        return jnp.concatenate([jnp.concatenate([gr, -gi], axis=1), jnp.concatenate([gi, gr], axis=1)], axis=0)
    return stack(grh_ref[q], gih_ref[q]), stack(grl_ref[q], gil_ref[q])


def _s1_kernel(x_ref, fh_ref, fl_ref, o_ref, *, nparts, ncw, n1, n1h):
    fh = fh_ref[...]
    fl = fl_ref[...]
    for r in range(SUBLANES):
        rows = pl.ds(r, n1h, stride=SUBLANES)
        xs = jnp.concatenate(
            [jnp.concatenate([x_ref[p, cc, rows, :] for p in range(nparts)], axis=0) for cc in range(ncw)], axis=1)
        xh, xl = _split(xs)
        res = _dot3(fh, fl, xh, xl)
        for ri in range(2):
            for cc in range(ncw):
                o_ref[ri, cc, pl.ds(r, n1, stride=SUBLANES), :] = res[ri * n1:(ri + 1) * n1, cc * LANES:(cc + 1) * LANES]


def _fft_stage1(x, fmat, n1, n1h, ncw):
    nparts, ncc, groups = x.shape[:3]
    fh, fl = _hilo(fmat)
    return pl.pallas_call(
        functools.partial(_s1_kernel, nparts=nparts, ncw=ncw, n1=n1, n1h=n1h),
        out_shape=jax.ShapeDtypeStruct((2, ncc, groups, n1 * SUBLANES, LANES), F32),
        grid=(groups, ncc // ncw),
        in_specs=[
            pl.BlockSpec((nparts, ncw, None, n1h * SUBLANES, LANES), lambda j, ci: (0, ci, j, 0, 0)),
            pl.BlockSpec(fh.shape, lambda j, ci: (0, 0)),
            pl.BlockSpec(fl.shape, lambda j, ci: (0, 0)),
        ],
        out_specs=pl.BlockSpec((2, ncw, None, n1 * SUBLANES, LANES), lambda j, ci: (0, ci, j, 0, 0)),
        compiler_params=_params(("parallel", "parallel")),
        name="fft_stage1",
    )(x, fh, fl)


def _load_k1(a_ref, q):
    ncc = a_ref.shape[1]
    return jnp.concatenate([a_ref[:, cc, :, q].reshape(2 * FFT_N2, LANES) for cc in range(ncc)], axis=1)


def _filter_spectrum_kernel(a_ref, grh_ref, grl_ref, gih_ref, gil_ref, s_ref, o_ref, *, kg):
    s = jnp.sum(s_ref[...], axis=0, keepdims=True)
    inv = 1.0 / (s[:, :HYENA_WIDTH] + s[:, HYENA_WIDTH:])
    half = FFT_N2
    for q in range(kg):
        ah, al = _split(_load_k1(a_ref, q))
        gh, gl = _stacked_tables(grh_ref, grl_ref, gih_ref, gil_ref, q)
        h = _dot3(gh, gl, ah, al)
        hf = h[:, :HYENA_WIDTH]
        hb = h[:, HYENA_WIDTH:]
        o_ref[q, :half] = (hf[:half] + hb[:half]) * inv
        o_ref[q, half:] = (hf[half:] - hb[half:]) * inv


FFT_K1_GROUP = 8


def _k1_spec(kg, width):
    return pl.BlockSpec((2, width // LANES, GROUPS, kg, SUBLANES, LANES), lambda i: (0, 0, 0, i, 0, 0))


def _filter_spectrum(a, tables, sums, n1):
    kg = FFT_K1_GROUP
    a6 = a.reshape(2, 2 * HYENA_WIDTH // LANES, GROUPS, n1, SUBLANES, LANES)
    tspec = pl.BlockSpec((kg, FFT_N2, FFT_N2), lambda i: (i, 0, 0))
    return pl.pallas_call(
        functools.partial(_filter_spectrum_kernel, kg=kg),
        out_shape=jax.ShapeDtypeStruct((n1, 2 * FFT_N2, HYENA_WIDTH), F32),
        grid=(n1 // kg,),
        in_specs=[_k1_spec(kg, 2 * HYENA_WIDTH), tspec, tspec, tspec, tspec,
                  pl.BlockSpec((SUBLANES, 2 * HYENA_WIDTH), lambda i: (0, 0))],
        out_specs=pl.BlockSpec((kg, 2 * FFT_N2, HYENA_WIDTH), lambda i: (i, 0, 0)),
        compiler_params=_params(("parallel",)),
        name="filter_spectrum",
    )(a6, *tables, sums)


def _dot_t(a, b):
    return lax.dot_general(a, b, (((0,), (0,)), ((), ())), preferred_element_type=F32)


def _s2_kernel(a_ref, kf_ref, grh_ref, grl_ref, gih_ref, gil_ref, o_ref, *, kg):
    half = FFT_N2
    for q in range(kg):
        ah, al = _split(_load_k1(a_ref, q))
        gh, gl = _stacked_tables(grh_ref, grl_ref, gih_ref, gil_ref, q)
        x = _dot3(gh, gl, ah, al)
        xr, xi = x[:half], x[half:]
        kr, ki = kf_ref[q, :half], kf_ref[q, half:]
        y = jnp.concatenate([xr * kr - xi * ki, xr * ki + xi * kr], axis=0)
        yh, yl = _split(y)
        bt = _dot_t(gh, yh) + _dot_t(gh, yl) + _dot_t(gl, yh)
        for cc in range(HYENA_WIDTH // LANES):
            o_ref[:, cc, :, q] = bt[:, cc * LANES:(cc + 1) * LANES].reshape(2, GROUPS, SUBLANES, LANES)


def _fft_stage2(a, kf, tables, n1):
    kg = FFT_K1_GROUP
    a6 = a.reshape(2, HYENA_WIDTH // LANES, GROUPS, n1, SUBLANES, LANES)
    tspec = pl.BlockSpec((kg, FFT_N2, FFT_N2), lambda i: (i, 0, 0))
    dspec = _k1_spec(kg, HYENA_WIDTH)
    out = pl.pallas_call(
        functools.partial(_s2_kernel, kg=kg),
        out_shape=jax.ShapeDtypeStruct(a6.shape, F32),
        grid=(n1 // kg,),
        in_specs=[dspec, pl.BlockSpec((kg, 2 * FFT_N2, HYENA_WIDTH), lambda i: (i, 0, 0)), tspec, tspec, tspec, tspec],
        out_specs=dspec,
        compiler_params=_params(("parallel",)),
        name="fft_stage2",
    )(a6, kf, *tables)
    return out.reshape(a.shape)


def _s3_kernel(b_ref, vx_ref, x0_ref, bias_ref, fh_ref, fl_ref, o_ref, *, ncw, n1, n1h):
    fh = fh_ref[...]
    fl = fl_ref[...]
    for r in range(SUBLANES):
        krows = pl.ds(r, n1, stride=SUBLANES)
        z = jnp.concatenate(
            [jnp.concatenate([b_ref[ri, cc, krows, :] for ri in range(2)], axis=0) for cc in range(ncw)], axis=1)
        zh, zl = _split(z)
        y = _dot3(fh, fl, zh, zl)
        trows = pl.ds(r, n1h, stride=SUBLANES)
        for p in range(2):
            for cc in range(ncw):
                yy = y[p * n1h:(p + 1) * n1h, cc * LANES:(cc + 1) * LANES]
                o_ref[p, cc, trows, :] = (yy + vx_ref[p, cc, trows, :] * bias_ref[cc]) * x0_ref[p, cc, trows, :]


def _fft_stage3(bt, vx, x0, bias, fmat, n1, n1h, ncw):
    ncc, groups = bt.shape[1:3]
    fh, fl = _hilo(fmat)
    tspec = pl.BlockSpec((2, ncw, None, n1h * SUBLANES, LANES), lambda j, ci: (0, ci, j, 0, 0))
    return pl.pallas_call(
        functools.partial(_s3_kernel, ncw=ncw, n1=n1, n1h=n1h),
        out_shape=jax.ShapeDtypeStruct(vx.shape, F32),
        grid=(groups, ncc // ncw),
        in_specs=[
            pl.BlockSpec((2, ncw, None, n1 * SUBLANES, LANES), lambda j, ci: (0, ci, j, 0, 0)),
            tspec, tspec,
            pl.BlockSpec((ncw, 1, LANES), lambda j, ci: (ci, 0, 0)),
            pl.BlockSpec(fh.shape, lambda j, ci: (0, 0)),
            pl.BlockSpec(fl.shape, lambda j, ci: (0, 0)),
        ],
        out_specs=tspec,
        compiler_params=_params(("parallel", "parallel")),
        name="fft_stage3",
    )(bt, vx, x0, bias, fh, fl)


def _hyena_latent(p, hy, feats, deltas, tables, n):
    conv_w, conv_b, w1, b1, f1, w2, b2, f2, w3, bias = hy
    c = _fft_constants(n)
    n1, n1h = c["n1"], c["n1h"]
    vx, x0 = _hyena_prep(p, conv_w, conv_b, group_major=True)
    gm_shape = vx.shape
    rows = lambda a: a.reshape(*a.shape[:-3], n1h * SUBLANES, LANES)
    vx, x0 = rows(vx), rows(x0)
    taps, sums = _hyena_filter(n, feats, deltas, w1, b1, f1, w2, b2, f2, w3, group_major=True)
    ncw = 2
    a_f = _fft_stage1(rows(taps)[None], c["f1r"], n1, n1h, ncw)
    kf = _filter_spectrum(a_f, tables, sums, n1)
    a = _fft_stage1(vx, c["f1c"], n1, n1h, ncw)
    bt = _fft_stage2(a, kf, tables, n1)
    out = _fft_stage3(bt, vx, x0, bias.reshape(HYENA_WIDTH // LANES, 1, LANES), c["f3c"], n1, n1h, ncw)
    return out.reshape(gm_shape)


@functools.lru_cache(maxsize=None)
def _dense_dft_constants(n):
    m = 2 * n
    k = np.arange(m)
    t = np.arange(n)
    f = np.exp(-2j * np.pi * np.outer(k, t) / m)
    finv = np.exp(2j * np.pi * np.outer(t, k) / m) / m
    return _stack_complex(f), np.concatenate([f.real, f.imag], axis=0), _stack_complex(finv)


def _dense_conv_kernel(vx_ref, x0_ref, hf_ref, hb_ref, sf_ref, sb_ref, bias_ref,
                       fch, fcl, frh, frl, fih, fil, o_ref, *, n):
    m = 2 * n
    z = jnp.concatenate([vx_ref[0], vx_ref[1]], axis=0)
    zh, zl = _split(z)
    zf = _dot3(fch[...], fcl[...], zh, zl)
    hfh, hfl = _split(hf_ref[...])
    hbh, hbl = _split(hb_ref[...])
    hf = _dot3(frh[...], frl[...], hfh, hfl)
    hb = _dot3(frh[...], frl[...], hbh, hbl)
    inv = 1.0 / (jnp.sum(sf_ref[...], axis=0, keepdims=True) + jnp.sum(sb_ref[...], axis=0, keepdims=True))
    kr = (hf[:m] + hb[:m]) * inv
    ki = (hf[m:] - hb[m:]) * inv
    zr, zi = zf[:m], zf[m:]
    y = jnp.concatenate([zr * kr - zi * ki, zr * ki + zi * kr], axis=0)
    yh, yl = _split(y)
    out = _dot3(fih[...], fil[...], yh, yl)
    bias = bias_ref[...]
    for p in range(2):
        o_ref[p] = (out[p * n:(p + 1) * n] + vx_ref[p] * bias) * x0_ref[p]


def _hyena_context(p, hy, feats, deltas, n):
    conv_w, conv_b, w1, b1, f1, w2, b2, f2, w3, bias = hy
    vx, x0 = _hyena_prep(p, conv_w, conv_b, group_major=False)
    taps, sums = _hyena_filter(n, feats, deltas, w1, b1, f1, w2, b2, f2, w3, group_major=False)
    fc, fr, fi = _dense_dft_constants(n)
    mats = [*_hilo(fc), *_hilo(fr), *_hilo(fi)]
    cw = 256
    nct = HYENA_WIDTH // cw
    dspec = pl.BlockSpec((2, n, cw), lambda ci: (0, 0, ci))
    return pl.pallas_call(
        functools.partial(_dense_conv_kernel, n=n),
        out_shape=jax.ShapeDtypeStruct(vx.shape, F32),
        grid=(nct,),
        in_specs=[dspec, dspec,
                  pl.BlockSpec((n, cw), lambda ci: (0, ci)), pl.BlockSpec((n, cw), lambda ci: (0, nct + ci)),
                  pl.BlockSpec((SUBLANES, cw), lambda ci: (0, ci)), pl.BlockSpec((SUBLANES, cw), lambda ci: (0, nct + ci)),
                  pl.BlockSpec((1, cw), lambda ci: (0, ci))]
                 + [pl.BlockSpec(mt.shape, lambda ci: (0, 0)) for mt in mats],
        out_specs=dspec,
        compiler_params=_params(("parallel",)),
        name="context_long_conv",
    )(vx, x0, taps, taps, sums, sums, bias.reshape(1, HYENA_WIDTH), *mats)


def _pool_kernel(x_ref, p_ref, n_ref, w_ref, sc_ref, o_ref, *, tl, nt, n):
    i = pl.program_id(1)
    x = x_ref[...]
    pv = jnp.where(i > 0, p_ref[...], 0.0)
    nx = jnp.where(i < nt - 1, n_ref[...], 0.0)
    ext = jnp.concatenate([pv, x, nx], axis=0)
    rows = tl + 2 * POOL_HALO
    t = i * tl + lax.broadcasted_iota(jnp.int32, (tl, POOL_GROUP), 0)
    for g, w in enumerate(POOL_WINDOWS):
        lanes = slice(g * POOL_GROUP, (g + 1) * POOL_GROUP)
        a = ext[:, lanes]
        c = a + pltpu.roll(a, 1, 0)
        h = 1
        while 2 * h < w:
            c = pltpu.roll(c, h, 0) + pltpu.roll(c, rows - h, 0)
            h *= 2
        total = c[POOL_HALO:POOL_HALO + tl]
        count = (jnp.minimum(t + h, n) - jnp.maximum(t - h, 0)).astype(F32)
        y = (total / count - x[:, lanes]).astype(BF16)
        o_ref[:, lanes] = _dot(y, w_ref[g].astype(BF16)) * sc_ref[:, lanes]


def _pool_mixer(p, w_pool, scale):
    b, n, _ = p.shape
    tl = min(512, n)
    nt = n // tl
    colblk = 3 * HYENA_WIDTH // POOL_WIDTH
    assert colblk * POOL_WIDTH == 3 * HYENA_WIDTH
    per = tl // POOL_HALO
    nrow = n // POOL_HALO
    return pl.pallas_call(
        functools.partial(_pool_kernel, tl=tl, nt=nt, n=n),
        out_shape=jax.ShapeDtypeStruct((b, n, POOL_WIDTH), F32),
        grid=(b, nt),
        in_specs=[
            pl.BlockSpec((None, tl, POOL_WIDTH), lambda bi, i: (bi, i, colblk)),
            pl.BlockSpec((None, POOL_HALO, POOL_WIDTH), lambda bi, i: (bi, jnp.maximum(i * per - 1, 0), colblk)),
            pl.BlockSpec((None, POOL_HALO, POOL_WIDTH), lambda bi, i: (bi, jnp.minimum((i + 1) * per, nrow - 1), colblk)),
            pl.BlockSpec(w_pool.shape, lambda bi, i: (0, 0, 0)),
            pl.BlockSpec((1, POOL_WIDTH), lambda bi, i: (0, 0)),
        ],
        out_specs=pl.BlockSpec((None, tl, POOL_WIDTH), lambda bi, i: (bi, i, 0)),
        compiler_params=_params(("parallel", "parallel")),
        name="pool_mixer",
    )(p, p, p, w_pool, scale.reshape(1, POOL_WIDTH))


def _load_group_major(ref, tl):
    ncc = ref.shape[0]
    return jnp.concatenate(
        [jnp.concatenate([ref[cc, jg, t1] for cc in range(ncc)], axis=1)
         for t1 in range(tl // FFT_N2) for jg in range(GROUPS)], axis=0)


def _outproj_kernel(at_ref, hy_ref, po_ref, x_ref, gb_ref, w_ref, gp_ref, gt_ref, o_ref, ox_ref, m_ref,
                    *, tm, tiles, group_major):
    t = pl.program_id(0)

    @pl.when(t == 0)
    def _():
        ox_ref[...] = jnp.zeros_like(ox_ref)

    def finish_previous():
        o_ref[...] = x_ref[...] + gt_ref[...] * _rms(ox_ref[...], gp_ref[...])

    @pl.when(t < tiles)
    def _():
        finish_previous()
        hy = _load_group_major(hy_ref, tm) if group_major else hy_ref[...]
        a0, a1 = ATTN_WIDTH, ATTN_WIDTH + HYENA_WIDTH
        m_ref[:, :a0] = _rms(at_ref[...], gb_ref[:, :a0]).astype(BF16)
        m_ref[:, a0:a1] = _rms(hy, gb_ref[:, a0:a1]).astype(BF16)
        m_ref[:, a1:] = _rms(po_ref[...], gb_ref[:, a1:]).astype(BF16)
        ox_ref[...] = _dot(m_ref[...], w_ref[...])

    pl.when(t == tiles)(finish_previous)


def _out_projection(attn, hy, po, x, g_branch, w_out, g_post, gate, group_major):
    b, n, d = x.shape
    tm = min(512, n)
    nt = n // tm
    tiles = b * nt

    def cur(t):
        return jnp.minimum(t, tiles - 1)

    def prev(t):
        return jnp.maximum(t - 1, 0)

    row = lambda width, sel: pl.BlockSpec((None, tm, width), lambda t: (sel(t) // nt, sel(t) % nt, 0))
    vec = lambda width: pl.BlockSpec((1, width), lambda t: (0, 0))
    if group_major:
        hy_spec = pl.BlockSpec((None, *_gm_shape((), tm // FFT_N2, HYENA_WIDTH)),
                               lambda t: (cur(t) // nt, 0, 0, cur(t) % nt, 0, 0))
    else:
        hy_spec = row(HYENA_WIDTH, cur)
    return pl.pallas_call(
        functools.partial(_outproj_kernel, tm=tm, tiles=tiles, group_major=group_major),
        out_shape=jax.ShapeDtypeStruct(x.shape, F32),
        grid=(tiles + 1,),
        in_specs=[row(ATTN_WIDTH, cur), hy_spec, row(POOL_WIDTH, cur), row(d, prev), vec(d),
                  pl.BlockSpec(w_out.shape, lambda t: (0, 0), pipeline_mode=pl.Buffered(1)), vec(d),
                  pl.BlockSpec((None, 1, d), lambda t: (prev(t) // nt, 0, 0))],
        out_specs=row(d, prev),
        scratch_shapes=[pltpu.VMEM((tm, d), F32), pltpu.VMEM((tm, d), BF16)],
        compiler_params=_params(("arbitrary",)),
        name="out_projection",
    )(attn, hy, po, x, g_branch.reshape(1, d), w_out, g_post.reshape(1, d), gate)


def _mlp_kernel(xe_ref, xp_ref, g_ref, shn_ref, scn_ref, wu_ref, wd_ref, gp_ref, gtp_ref, o_ref,
                h0_ref, h1_ref, acc0_ref, acc1_ref, *, rc, tiles):
    t = pl.program_id(0)
    k = pl.program_id(1)
    rows = pl.ds(pl.multiple_of(k * rc, rc), rc)

    def normalise(h_ref):
        a = g_ref[...] * (1.0 + scn_ref[...])
        h_ref[rows, :] = (_rms(xp_ref[...], a) + shn_ref[...]).astype(BF16)

    def finish(acc_ref):
        o_ref[...] = xe_ref[...] + _rms(acc_ref[rows, :], gp_ref[...] * gtp_ref[...])

    @pl.when(t == 0)
    def _():
        normalise(h0_ref)
        acc0_ref[rows, :] = jnp.zeros((rc, acc0_ref.shape[1]), F32)
        acc1_ref[rows, :] = jnp.zeros((rc, acc1_ref.shape[1]), F32)

    def step(slot):
        h_cur, h_oth = (h0_ref, h1_ref) if slot == 0 else (h1_ref, h0_ref)
        acc_cur, acc_oth = (acc0_ref, acc1_ref) if slot == 0 else (acc1_ref, acc0_ref)
        finish(acc_oth)
        acc_oth[rows, :] = jnp.zeros((rc, acc_oth.shape[1]), F32)
        normalise(h_oth)
        u = jnp.maximum(_dot(h_cur[...], wu_ref[...]), 0.0)
        acc_cur[...] += _dot((u * u).astype(BF16), wd_ref[...])

    work = (t >= 1) & (t <= tiles)
    pl.when(work & (t % 2 == 1))(functools.partial(step, 0))
    pl.when(work & (t % 2 == 0))(functools.partial(step, 1))
    pl.when(t == tiles + 1)(functools.partial(finish, acc0_ref if (tiles - 1) % 2 == 0 else acc1_ref))


MLP_TH = 1024


def _mlp(x, g_pre, shift, scale, w_up, w_down, g_post, gate):
    b, n, d = x.shape
    th = MLP_TH
    nk = w_up.shape[1] // th
    tm = min(1024, n)
    nt = n // tm
    tiles = b * nt
    rc = tm // nk
    assert rc * nk == tm and rc % BF16_ROWS == 0

    def norm_tile(t):
        return jnp.minimum(t, tiles - 1)

    def done_tile(t):
        return jnp.clip(t - 2, 0, tiles - 1)

    vec = pl.BlockSpec((1, d), lambda t, k: (0, 0))
    bvec = lambda sel: pl.BlockSpec((None, 1, d), lambda t, k: (sel(t) // nt, 0, 0))
    chunk = lambda sel, first: pl.BlockSpec(
        (None, rc, d), lambda t, k: (sel(t) // nt, (sel(t) % nt) * nk + jnp.where(t < first, 0, k), 0))
    busy = lambda t: (t >= 1) & (t <= tiles)
    return pl.pallas_call(
        functools.partial(_mlp_kernel, rc=rc, tiles=tiles),
        out_shape=jax.ShapeDtypeStruct(x.shape, F32),
        grid=(tiles + 2, nk),
        in_specs=[chunk(done_tile, 2), chunk(norm_tile, 0), vec, bvec(norm_tile), bvec(norm_tile),
                  pl.BlockSpec((d, th), lambda t, k: (0, jnp.where(busy(t), k, 0))),
                  pl.BlockSpec((th, d), lambda t, k: (jnp.where(busy(t), k, 0), 0)),
                  vec, bvec(done_tile)],
        out_specs=chunk(done_tile, 2),
        scratch_shapes=[pltpu.VMEM((tm, d), BF16)] * 2 + [pltpu.VMEM((tm, d), F32)] * 2,
        compiler_params=_params(("arbitrary", "arbitrary")),
        name="mlp",
    )(x, x, g_pre.reshape(1, d), shift, scale, w_up, w_down, g_post.reshape(1, d), gate)


def kernel(x, c, ctx, c_ctx, w_mod, b_mod, g_pre_mix, g_post_mix, g_pre_mlp, g_post_mlp, w_in, w_out, g_branch,
           attn_sink, hy_conv_w, hy_conv_b, hy_w1, hy_b1, hy_freq1, hy_w2, hy_b2, hy_freq2, hy_w3, hy_bias,
           pool_w, pool_scale, w_up, w_down):
    b, n, d = x.shape
    n_ctx = ctx.shape[1]
    depth = w_mod.shape[0]
    assert b == 2 and d == D_MODEL and n % 512 == 0 and n_ctx % BLOCK == 0

    cond = jnp.concatenate([c, c_ctx[None], jnp.zeros((SUBLANES - b - 1, d), F32)], axis=0)
    mods = _modulation(cond, w_mod, b_mod)

    w_in_b = _column_tiles(_permute_rope_columns(w_in.astype(BF16)), IN_TN)
    w_up_b = w_up.astype(BF16)
    w_out_b, w_down_b = w_out.astype(BF16), w_down.astype(BF16)
    tables_x, tables_c = _rope_tables(n), _rope_tables(n_ctx)
    feats_x, feats_c = _filter_features(n), _filter_features(n_ctx)
    deltas = _filter_deltas()
    tables = _stage2_tables(n)

    for i in range(depth):
        last = i == depth - 1
        hy = (hy_conv_w[i], hy_conv_b[i], hy_w1[i], hy_b1[i], hy_freq1[i], hy_w2[i], hy_b2[i], hy_freq2[i],
              hy_w3[i], hy_bias[i])
        mx = [m[:, None, :] for m in jnp.split(mods[i, :b], N_MOD, axis=-1)]
        mc = [jnp.broadcast_to(m[None, None, :], (b, 1, d)) for m in jnp.split(mods[i, b], N_MOD, axis=-1)]

        qkv_x, px = _in_projection(x, g_pre_mix[i], mx[0], mx[1], w_in_b[i], tables_x, rope=True)
        qkv_c, pc = _in_projection(ctx, g_pre_mix[i], mc[0], mc[1], w_in_b[i], tables_c, rope=False)
        kv_ctx = qkv_c[..., Q_END:V_END]

        attn_x = _attention(qkv_x, kv_ctx, attn_sink[i], local=True)
        hy_x = _hyena_latent(px, hy, feats_x, deltas, tables, n)
        po_x = _pool_mixer(px, pool_w[i], pool_scale[i])
        x = _out_projection(attn_x, hy_x, po_x, x, g_branch[i], w_out_b[i], g_post_mix[i], mx[2], group_major=True)
        x = _mlp(x, g_pre_mlp[i], mx[3], mx[4], w_up_b[i], w_down_b[i], g_post_mlp[i], mx[5])

        if not last:
            attn_c = _attention(qkv_c, kv_ctx, attn_sink[i], local=False)
            hy_c = _hyena_context(pc, hy, feats_c, deltas, n_ctx)
            po_c = _pool_mixer(pc, pool_w[i], pool_scale[i])
            ctx = _out_projection(attn_c, hy_c, po_c, ctx, g_branch[i], w_out_b[i], g_post_mix[i], mc[2],
                                  group_major=False)
            ctx = _mlp(ctx, g_pre_mlp[i], mc[3], mc[4], w_up_b[i], w_down_b[i], g_post_mlp[i], mc[5])
    return x
```

```python
import functools
import math

import numpy as np
import jax
import jax.numpy as jnp
from jax import lax
from jax.experimental import pallas as pl
from jax.experimental.pallas import tpu as pltpu

F32 = jnp.float32
BF16 = jnp.bfloat16

D_MODEL = 2048
DEPTH = 4
GRID_W = 64
ATTN_WIDTH = D_MODEL // 2
HYENA_WIDTH = D_MODEL // 4
POOL_WIDTH = D_MODEL - ATTN_WIDTH - HYENA_WIDTH
HEAD_DIM = 128
N_HEADS = ATTN_WIDTH // HEAD_DIM
N_KV_HEADS = 2
KV_GROUP = N_HEADS // N_KV_HEADS
KV_WIDTH = N_KV_HEADS * HEAD_DIM
WINDOW = 128
BLOCK = 128
ROPE_BASE = 10000.0
HYENA_EMB_DIM = 33
HYENA_FILTER_HIDDEN = 64
HYENA_FAST_DECAY_PCT = 0.3
HYENA_SLOW_DECAY_PCT = 1.5
HYENA_DECAY_TARGET = 1e-2
POOL_WINDOWS = (2, 4, 8, 16)
POOL_GROUP = POOL_WIDTH // len(POOL_WINDOWS)
MLP_HIDDEN = 4 * D_MODEL
N_MOD = 6
EPS = 1e-6
NEG_INF = -1e30

Q_END = ATTN_WIDTH
K_END = Q_END + KV_WIDTH
V_END = K_END + KV_WIDTH
HY_END = V_END + 3 * HYENA_WIDTH
IN_WIDTH = HY_END + POOL_WIDTH

LANES = 128
SUBLANES = 8
FFT_N2 = 128
POOL_HALO = 16
VMEM_LIMIT = 56 * 1024 * 1024


def _params(sem, vmem=VMEM_LIMIT):
    return pltpu.CompilerParams(dimension_semantics=sem, vmem_limit_bytes=vmem)


def _split(x):
    hi = x.astype(BF16)
    lo = (x - hi.astype(F32)).astype(BF16)
    return hi, lo


def _dot(a, b):
    return jnp.dot(a, b, preferred_element_type=F32)


def _dot3(ah, al, bh, bl):
    return _dot(ah, bh) + _dot(ah, bl) + _dot(al, bh)


def _rms(x, g):
    return x * lax.rsqrt(jnp.mean(x * x, axis=-1, keepdims=True) + EPS) * g


BF16_ROWS = 2 * SUBLANES


def _norm_scale_rows(x_ref, h_ref, a_ref, s_ref, rows, cols=None):
    cols = slice(None) if cols is None else cols
    nchunks = rows // BF16_ROWS

    def chunk(c):
        return pl.ds(pl.multiple_of(c * BF16_ROWS, BF16_ROWS), BF16_ROWS)

    def inv_rms(c):
        x = x_ref[chunk(c), :]
        return lax.rsqrt(jnp.mean(x * x, axis=-1, keepdims=True) + EPS)

    def body(c, inv):
        inv_next = inv_rms(jnp.minimum(c + 1, nchunks - 1))
        y = x_ref[chunk(c), :] * inv * a_ref[...]
        if s_ref is not None:
            y = y + s_ref[...]
        h_ref[chunk(c), cols] = y.astype(BF16)
        return inv_next

    lax.fori_loop(0, nchunks, body, inv_rms(0), unroll=8)


def _residual_norm_rows(x_ref, y_ref, o_ref, pg_ref, rows):
    nchunks = rows // SUBLANES

    def chunk(c):
        return pl.ds(pl.multiple_of(c * SUBLANES, SUBLANES), SUBLANES)

    def inv_rms(c):
        y = y_ref[chunk(c), :]
        return lax.rsqrt(jnp.mean(y * y, axis=-1, keepdims=True) + EPS)

    def body(c, inv):
        inv_next = inv_rms(jnp.minimum(c + 1, nchunks - 1))
        o_ref[chunk(c), :] = x_ref[chunk(c), :] + y_ref[chunk(c), :] * inv * pg_ref[...]
        return inv_next

    lax.fori_loop(0, nchunks, body, inv_rms(0), unroll=16)


def _mod_kernel(c_ref, w_ref, b_ref, o_ref):
    c = c_ref[...]
    s = c / (1.0 + jnp.exp(-c))
    sh, sl = _split(s)
    wh, wl = _split(w_ref[...])
    o_ref[...] = _dot3(sh, sl, wh, wl) + b_ref[...]


def _modulation(cond, w_mod, b_mod):
    depth, d, width = w_mod.shape
    tn = 1024
    return pl.pallas_call(
        _mod_kernel,
        out_shape=jax.ShapeDtypeStruct((depth, SUBLANES, width), F32),
        grid=(depth, width // tn),
        in_specs=[
            pl.BlockSpec((SUBLANES, d), lambda l, j: (0, 0)),
            pl.BlockSpec((None, d, tn), lambda l, j: (l, 0, j)),
            pl.BlockSpec((None, 1, tn), lambda l, j: (l, 0, j)),
        ],
        out_specs=pl.BlockSpec((None, SUBLANES, tn), lambda l, j: (l, 0, j)),
        compiler_params=_params(("parallel", "parallel")),
        name="modulation",
    )(cond, w_mod, b_mod.reshape(depth, 1, width))


QKV_WIDTH = V_END
REST_WIDTH = IN_WIDTH - V_END
SM_SCALE = HEAD_DIM ** -0.5


TABLE_Q, TABLE_K, TABLE_ID, TABLE_SCALE = range(4)


def _inproj_kernel(xn_ref, g_ref, shn_ref, scn_ref, w_ref, ca_ref, sa_ref, cb_ref, sb_ref,
                   qkv_ref, rest_ref, h0_ref, h1_ref, acc_ref, *, tm, tn, rc, tiles):
    t = pl.program_id(0)
    j = pl.program_id(1)

    def normalise(h_ref):
        rows = pl.ds(pl.multiple_of(jnp.minimum(j * rc, tm - rc), BF16_ROWS), rc)
        a = g_ref[...] * (1.0 + scn_ref[...])
        h_ref[rows, :] = (_rms(xn_ref[rows, :], a) + shn_ref[...]).astype(BF16)

    def finish_previous():
        acc = acc_ref[...]
        rest_ref[...] = acc
        nch = tn // LANES
        for ch in range(nch):
            a = acc[:, ch * LANES:(ch + 1) * LANES]
            cos, sin = (ca_ref, sa_ref) if ch < nch // 2 else (cb_ref, sb_ref)
            qkv_ref[:, ch * LANES:(ch + 1) * LANES] = (a * cos[...] + pltpu.roll(a, HEAD_DIM // 2, 1) * sin[...]).astype(BF16)

    @pl.when(t == 0)
    def _():
        normalise(h0_ref)

    @pl.when((t == 0) & (j == 0))
    def _():
        acc_ref[...] = jnp.zeros_like(acc_ref)

    def step(slot):
        h_cur, h_oth = (h0_ref, h1_ref) if slot == 0 else (h1_ref, h0_ref)
        finish_previous()
        normalise(h_oth)
        acc_ref[...] = _dot(h_cur[...], w_ref[j])

    work = (t >= 1) & (t <= tiles)
    pl.when(work & (t % 2 == 1))(functools.partial(step, 0))
    pl.when(work & (t % 2 == 0))(functools.partial(step, 1))
    pl.when((t == tiles + 1) & (j == 0))(finish_previous)


IN_TN = 512
QKV_SPARE = QKV_WIDTH // IN_TN
REST_SPARE = REST_WIDTH // IN_TN


def _in_projection(x, g, shift, scale, w, tables, rope):
    b, n, d = x.shape
    tm = min(1024, n)
    tn = w.shape[-1]
    nj = w.shape[0]
    nt = n // tm
    tiles = b * nt
    half = tn // 2
    assert tn == IN_TN and nj * tn == IN_WIDTH and Q_END % tn == 0 and K_END % tn == half and V_END % tn == 0
    rc = -(-tm // nj)
    rc = -(-rc // (2 * BF16_ROWS)) * 2 * BF16_ROWS
    nq, nqkv = Q_END // tn, QKV_WIDTH // tn

    def norm_tile(t):
        return jnp.minimum(t, tiles - 1)

    def lagged(t, j):
        step = jnp.clip((t - 1) * nj + j - 1, 0, tiles * nj - 1)
        return step // nj, step % nj

    def kind_a(j):
        return jnp.where(j < nq, TABLE_Q if rope else TABLE_SCALE, jnp.where(j < nqkv, TABLE_K if rope else TABLE_ID, TABLE_ID))

    def kind_b(j):
        return jnp.where(j < nq, TABLE_Q if rope else TABLE_SCALE, TABLE_ID)

    vec = pl.BlockSpec((1, d), lambda t, j: (0, 0))
    bvec = lambda sel: pl.BlockSpec((None, 1, d), lambda t, j: (sel(t) // nt, 0, 0))
    def table(kind):
        def index(t, j):
            tp, jp = lagged(t, j)
            k = kind(jp)
            return k, jnp.where(k == TABLE_ID, 0, tp % nt), 0
        return pl.BlockSpec((None, tm, LANES), index)

    def out_spec(col_tile):
        return pl.BlockSpec((None, tm, tn), lambda t, j: (lagged(t, j)[0] // nt, lagged(t, j)[0] % nt,
                                                          col_tile(lagged(t, j)[1])))

    cos, sin = tables
    return pl.pallas_call(
        functools.partial(_inproj_kernel, tm=tm, tn=tn, rc=rc, tiles=tiles),
        out_shape=[jax.ShapeDtypeStruct((b, n, QKV_WIDTH + tn), BF16), jax.ShapeDtypeStruct((b, n, REST_WIDTH + tn), F32)],
        grid=(tiles + 2, nj),
        in_specs=[pl.BlockSpec((None, tm, d), lambda t, j: (norm_tile(t) // nt, norm_tile(t) % nt, 0)),
                  vec, bvec(norm_tile), bvec(norm_tile),
                  pl.BlockSpec(w.shape, lambda t, j: (0, 0, 0), pipeline_mode=pl.Buffered(1)),
                  table(kind_a), table(kind_a), table(kind_b), table(kind_b)],
        out_specs=[out_spec(lambda jp: jnp.minimum(jp, QKV_SPARE)),
                   out_spec(lambda jp: jnp.where(jp >= nqkv, jp - nqkv, REST_SPARE))],
        scratch_shapes=[pltpu.VMEM((tm, d), BF16)] * 2 + [pltpu.VMEM((tm, tn), F32)],
        compiler_params=_params(("arbitrary", "arbitrary")),
        name="in_projection_rope" if rope else "in_projection",
    )(x, g.reshape(1, d), shift, scale, w, cos, sin, cos, sin)


def _prep_w_in_kernel(w_ref, o_ref, *, tn):
    j = pl.program_id(1)
    quarter = HEAD_DIM // 4
    nch = tn // LANES

    def write(n_swapped):
        for ch in range(nch):
            a = w_ref[:, ch * LANES:(ch + 1) * LANES]
            if ch < n_swapped:
                q = lax.broadcasted_iota(jnp.int32, a.shape, 1) // quarter
                a = jnp.where(q == 1, pltpu.roll(a, LANES - quarter, 1), jnp.where(q == 2, pltpu.roll(a, quarter, 1), a))
            o_ref[:, ch * LANES:(ch + 1) * LANES] = a.astype(BF16)

    nfull, nrem = divmod(K_END // LANES, nch)
    pl.when(j < nfull)(functools.partial(write, nch))
    pl.when(j == nfull)(functools.partial(write, nrem))
    pl.when(j > nfull)(functools.partial(write, 0))


def _prepare_w_in(w_in, tn):
    depth, d, width = w_in.shape
    return pl.pallas_call(
        functools.partial(_prep_w_in_kernel, tn=tn),
        out_shape=jax.ShapeDtypeStruct((depth, width // tn, d, tn), BF16),
        grid=(depth, width // tn),
        in_specs=[pl.BlockSpec((None, d, tn), lambda l, j: (l, 0, j))],
        out_specs=pl.BlockSpec((None, None, d, tn), lambda l, j: (l, j, 0, 0)),
        compiler_params=_params(("parallel", "parallel")),
        name="prepare_w_in",
    )(w_in)


def _rope_tables(n):
    quarter = HEAD_DIM // 4
    inv_freq = ROPE_BASE ** (-jnp.arange(quarter, dtype=F32) / quarter)
    t = jnp.arange(n, dtype=jnp.int32)
    row = (t // GRID_W).astype(F32)[:, None] * inv_freq[None, :]
    col = (t % GRID_W).astype(F32)[:, None] * inv_freq[None, :]
    cos = jnp.concatenate([jnp.cos(row), jnp.cos(col), jnp.cos(row), jnp.cos(col)], axis=-1)
    sin = jnp.concatenate([-jnp.sin(row), -jnp.sin(col), jnp.sin(row), jnp.sin(col)], axis=-1)
    one, zero = jnp.ones_like(cos), jnp.zeros_like(sin)
    return (jnp.stack([cos * SM_SCALE, cos, one, one * SM_SCALE]), jnp.stack([sin * SM_SCALE, sin, zero, zero]))


def _softmax_pv(parts, sink_col):
    m = sink_col
    for s, _ in parts:
        m = jnp.maximum(m, jnp.max(s, axis=-1, keepdims=True))
    den = jnp.exp(sink_col - m)
    out = None
    for s, v in parts:
        p = jnp.exp(s - m)
        den = den + jnp.sum(p, axis=-1, keepdims=True)
        pv = _dot(p.astype(BF16), v)
        out = pv if out is None else out + pv
    return out / den


def _ctx_attn_kernel(sink_ref, q_ref, kvc_ref, o_ref, *, tq):
    for g in range(N_KV_HEADS):
        heads = [g * KV_GROUP + h for h in range(KV_GROUP)]
        qg = jnp.concatenate([q_ref[:, h * HEAD_DIM:(h + 1) * HEAD_DIM] for h in heads], axis=0)
        sink_col = jnp.concatenate([jnp.full((tq, 1), sink_ref[h], F32) for h in heads], axis=0)
        kc = kvc_ref[:, g * HEAD_DIM:(g + 1) * HEAD_DIM]
        vc = kvc_ref[:, KV_WIDTH + g * HEAD_DIM:KV_WIDTH + (g + 1) * HEAD_DIM]
        s = lax.dot_general(qg, kc, (((1,), (1,)), ((), ())), preferred_element_type=F32)
        o = _softmax_pv([(s, vc)], sink_col)
        for hi, h in enumerate(heads):
            o_ref[:, h * HEAD_DIM:(h + 1) * HEAD_DIM] = o[hi * tq:(hi + 1) * tq]


ATTN_ROWS = KV_GROUP * BLOCK
SOFTMAX_CHUNK = 32


def _win_attn_kernel(sink_ref, q_ref, km_ref, kp_ref, kn_ref, vm_ref, vp_ref, vn_ref, kvc_ref, band_ref,
                     o_ref, ktw, vw, s_scr, p_scr, m_scr, *, tq, nb, nctx):
    i = pl.program_id(1)
    nsub = tq // BLOCK
    nloc = 3 * BLOCK

    def transposed(x):
        return x.astype(F32).T.astype(BF16)

    def block_rows(main_ref, prev_ref, next_ref, w, lanes):
        if w == 0:
            return prev_ref[:, lanes]
        if w == nsub + 1:
            return next_ref[:, lanes]
        return main_ref[(w - 1) * BLOCK:w * BLOCK, lanes]

    ones = jnp.ones((nloc + nctx, HEAD_DIM), BF16)
    for g in range(N_KV_HEADS):
        lanes = slice(g * HEAD_DIM, (g + 1) * HEAD_DIM)
        vlanes = slice(KV_WIDTH + g * HEAD_DIM, KV_WIDTH + (g + 1) * HEAD_DIM)
        kts = [transposed(block_rows(km_ref, kp_ref, kn_ref, w, lanes)) for w in range(nsub + 2)]
        kct = [transposed(kvc_ref[cb * BLOCK:(cb + 1) * BLOCK, lanes]) for cb in range(nctx // BLOCK)]
        for jb in range(nsub):
            for w in range(3):
                ktw[jb, g, :, w * BLOCK:(w + 1) * BLOCK] = kts[jb + w]
                vw[jb, g, w * BLOCK:(w + 1) * BLOCK, :HEAD_DIM] = block_rows(vm_ref, vp_ref, vn_ref, jb + w, lanes)
            for cb in range(nctx // BLOCK):
                ktw[jb, g, :, nloc + cb * BLOCK:nloc + (cb + 1) * BLOCK] = kct[cb]
            vw[jb, g, nloc:, :HEAD_DIM] = kvc_ref[:, vlanes]
            vw[jb, g, :, HEAD_DIM:] = ones

    col = lax.broadcasted_iota(jnp.int32, (1, nloc), 1)

    def rows_of(jb):
        start = jb * BLOCK
        return pl.ds(start if isinstance(start, int) else pl.multiple_of(start, BLOCK), BLOCK)

    def stage_a(jb, g):
        qg = jnp.concatenate([q_ref[rows_of(jb), (g * KV_GROUP + h) * HEAD_DIM:(g * KV_GROUP + h + 1) * HEAD_DIM]
                              for h in range(KV_GROUP)], axis=0)
        s_scr[g] = _dot(qg, ktw[jb, g])

    def stage_b(jb, g):
        blk = i * nsub + jb
        pen_prev = jnp.where(blk == 0, NEG_INF, 0.0).astype(F32)
        pen_next = jnp.where(blk == nb - 1, NEG_INF, 0.0).astype(F32)
        rowbias = jnp.where(col < BLOCK, pen_prev, jnp.where(col >= 2 * BLOCK, pen_next, 0.0))
        for c in range(ATTN_ROWS // SOFTMAX_CHUNK):
            rows = slice(c * SOFTMAX_CHUNK, (c + 1) * SOFTMAX_CHUNK)
            sink = sink_ref[g * KV_GROUP + (c * SOFTMAX_CHUNK) // BLOCK]
            s_loc = s_scr[g, rows, :nloc] + band_ref[rows, :] + rowbias
            s_ctx = s_scr[g, rows, nloc:]
            m = jnp.maximum(jnp.max(s_loc, axis=-1, keepdims=True), jnp.max(s_ctx, axis=-1, keepdims=True))
            m = jnp.maximum(m, sink)
            p_scr[g, rows, :nloc] = jnp.exp(s_loc - m).astype(BF16)
            p_scr[g, rows, nloc:] = jnp.exp(s_ctx - m).astype(BF16)
            m_scr[g, rows, :] = m

    def stage_c(jb, g):
        o = _dot(p_scr[g], vw[jb, g])
        for hi in range(KV_GROUP):
            h = g * KV_GROUP + hi
            rows = slice(hi * BLOCK, (hi + 1) * BLOCK)
            den = o[rows, HEAD_DIM:HEAD_DIM + 1] + jnp.exp(sink_ref[h] - m_scr[g, rows, :])
            o_ref[rows_of(jb), h * HEAD_DIM:(h + 1) * HEAD_DIM] = o[rows, :HEAD_DIM] / den

    stage_a(0, 0)
    stage_a(0, 1)
    stage_b(0, 0)

    def body(j, carry):
        stage_a(j, 0)
        stage_c(j - 1, 0)
        stage_b(j - 1, 1)
        stage_a(j, 1)
        stage_c(j - 1, 1)
        stage_b(j, 0)
        return carry

    lax.fori_loop(1, nsub, body, 0)
    stage_c(nsub - 1, 0)
    stage_b(nsub - 1, 1)
    stage_c(nsub - 1, 1)


def _band_bias():
    qi = np.arange(ATTN_ROWS)[:, None] % BLOCK
    sj = np.arange(3 * BLOCK)[None, :]
    return jnp.asarray(np.where(np.abs(sj - BLOCK - qi) <= WINDOW, 0.0, NEG_INF), dtype=F32)


def _attention(qkv, kv_ctx, sink, local):
    b, n, _ = qkv.shape
    c = kv_ctx.shape[1]
    tq = min(1024 if local else 512, n)
    nsub = tq // BLOCK
    nb = n // BLOCK
    kcol = Q_END // KV_WIDTH
    vcol = K_END // KV_WIDTH
    smem = pl.BlockSpec(memory_space=pltpu.SMEM)
    q_spec = pl.BlockSpec((None, tq, ATTN_WIDTH), lambda bi, i: (bi, i, 0))
    kvc_spec = pl.BlockSpec((None, c, 2 * KV_WIDTH), lambda bi, i: (bi, 0, 0))
    if local:
        def main(colblk):
            return pl.BlockSpec((None, tq, KV_WIDTH), lambda bi, i: (bi, i, colblk))

        def prev(colblk):
            return pl.BlockSpec((None, BLOCK, KV_WIDTH), lambda bi, i: (bi, jnp.maximum(i * nsub - 1, 0), colblk))

        def nxt(colblk):
            return pl.BlockSpec((None, BLOCK, KV_WIDTH), lambda bi, i: (bi, jnp.minimum((i + 1) * nsub, nb - 1), colblk))

        keys = 3 * BLOCK + c
        kern = functools.partial(_win_attn_kernel, tq=tq, nb=nb, nctx=c)
        in_specs = [smem, q_spec, main(kcol), prev(kcol), nxt(kcol), main(vcol), prev(vcol), nxt(vcol), kvc_spec,
                    pl.BlockSpec((ATTN_ROWS, 3 * BLOCK), lambda bi, i: (0, 0))]
        args = [sink, qkv, qkv, qkv, qkv, qkv, qkv, qkv, kv_ctx, _band_bias()]
        scratch = [pltpu.VMEM((nsub, N_KV_HEADS, HEAD_DIM, keys), BF16),
                   pltpu.VMEM((nsub, N_KV_HEADS, keys, 2 * HEAD_DIM), BF16),
                   pltpu.VMEM((2, ATTN_ROWS, keys), F32),
                   pltpu.VMEM((2, ATTN_ROWS, keys), BF16),
                   pltpu.VMEM((2, ATTN_ROWS, 1), F32)]
    else:
        kern = functools.partial(_ctx_attn_kernel, tq=tq)
        in_specs = [smem, q_spec, kvc_spec]
        args = [sink, qkv, kv_ctx]
        scratch = []
    return pl.pallas_call(
        kern,
        out_shape=jax.ShapeDtypeStruct((b, n, ATTN_WIDTH), F32),
        grid=(b, n // tq),
        in_specs=in_specs,
        out_specs=pl.BlockSpec((None, tq, ATTN_WIDTH), lambda bi, i: (bi, i, 0)),
        scratch_shapes=scratch,
        compiler_params=_params(("parallel", "parallel")),
        name="window_attention" if local else "context_attention",
    )(*args)


GROUPS = FFT_N2 // SUBLANES


def _gm_shape(lead, t1, width):
    return (*lead, width // LANES, GROUPS, t1, SUBLANES, LANES)


def _store_group_major(o_ref, val, tl):
    for t1 in range(tl // FFT_N2):
        for jg in range(GROUPS):
            r0 = (t1 * GROUPS + jg) * SUBLANES
            for cc in range(val.shape[1] // LANES):
                o_ref[cc, jg, t1] = val[r0:r0 + SUBLANES, cc * LANES:(cc + 1) * LANES]


def _hyena_prep_kernel(u_ref, p_ref, n_ref, w_ref, b_ref, vx_ref, x0_ref, *, tl, nt, group_major):
    i = pl.program_id(1)
    u = u_ref[...]
    prev_row = jnp.where(i > 0, p_ref[SUBLANES - 1:SUBLANES, :], 0.0)
    next_row = jnp.where(i < nt - 1, n_ref[0:1, :], 0.0)
    row = lax.broadcasted_iota(jnp.int32, u.shape, 0)
    um = jnp.where(row == 0, prev_row, pltpu.roll(u, 1, 0))
    up = jnp.where(row == tl - 1, next_row, pltpu.roll(u, tl - 1, 0))
    z = um * w_ref[0:1, :] + u * w_ref[1:2, :] + up * w_ref[2:3, :] + b_ref[...]
    x0 = z[:, :HYENA_WIDTH]
    vx = z[:, 2 * HYENA_WIDTH:] * z[:, HYENA_WIDTH:2 * HYENA_WIDTH]
    if group_major:
        _store_group_major(vx_ref, vx, tl)
        _store_group_major(x0_ref, x0, tl)
    else:
        vx_ref[...] = vx
        x0_ref[...] = x0


def _hyena_prep(p, conv_w, conv_b, group_major):
    b, n, _ = p.shape
    tl = min(512, n)
    nt = n // tl
    hw = 3 * HYENA_WIDTH
    colblk = 0
    nrow8 = n // SUBLANES
    per = tl // SUBLANES
    if group_major:
        shape = _gm_shape((b,), n // FFT_N2, HYENA_WIDTH)
        out_spec = pl.BlockSpec((None, *_gm_shape((), tl // FFT_N2, HYENA_WIDTH)), lambda bi, i: (bi, 0, 0, i, 0, 0))
    else:
        shape = (b, n, HYENA_WIDTH)
        out_spec = pl.BlockSpec((None, tl, HYENA_WIDTH), lambda bi, i: (bi, i, 0))
    return pl.pallas_call(
        functools.partial(_hyena_prep_kernel, tl=tl, nt=nt, group_major=group_major),
        out_shape=[jax.ShapeDtypeStruct(shape, F32)] * 2,
        grid=(b, nt),
        in_specs=[
            pl.BlockSpec((None, tl, hw), lambda bi, i: (bi, i, colblk)),
            pl.BlockSpec((None, SUBLANES, hw), lambda bi, i: (bi, jnp.maximum(i * per - 1, 0), colblk)),
            pl.BlockSpec((None, SUBLANES, hw), lambda bi, i: (bi, jnp.minimum((i + 1) * per, nrow8 - 1), colblk)),
            pl.BlockSpec((3, hw), lambda bi, i: (0, 0)),
            pl.BlockSpec((1, hw), lambda bi, i: (0, 0)),
        ],
        out_specs=[out_spec, out_spec],
        compiler_params=_params(("parallel", "parallel")),
        name="hyena_prep",
    )(p, p, p, conv_w, conv_b.reshape(1, hw))


def _filter_kernel(ft_ref, w1_ref, b1_ref, f1_ref, w2_ref, b2_ref, f2_ref, w3_ref, dl_ref, h_ref, s_ref,
                   *, tl, n, group_major):
    i = pl.program_id(0)

    def dense(a, w_ref):
        ah, al = _split(a)
        wh, wl = _split(w_ref[...])
        return _dot3(ah, al, wh, wl)

    h = jnp.sin(f1_ref[...] * (dense(ft_ref[...], w1_ref) + b1_ref[...]))
    h = jnp.sin(f2_ref[...] * (dense(h, w2_ref) + b2_ref[...]))
    h = dense(h, w3_ref)
    t = (i * tl + lax.broadcasted_iota(jnp.int32, (tl, HYENA_WIDTH), 0)).astype(F32) / float(n - 1)
    decay = jnp.exp(-t * dl_ref[...])
    h = h * jnp.concatenate([decay, decay], axis=1)
    if group_major:
        _store_group_major(h_ref, h, tl)
    else:
        h_ref[...] = h

    @pl.when(i == 0)
    def _():
        s_ref[...] = jnp.zeros_like(s_ref)

    s_ref[...] += jnp.sum(jnp.abs(h).reshape(tl // SUBLANES, SUBLANES, 2 * HYENA_WIDTH), axis=0)


def _filter_features(n):
    t = jnp.linspace(0.0, 1.0, n, dtype=F32)[:, None]
    bands = (HYENA_EMB_DIM - 1) // 2
    omega = 2.0 * math.pi * jnp.arange(n, dtype=F32)[:, None] / n
    f = jnp.linspace(1e-4, bands - 1, bands, dtype=F32)[None, :]
    feats = jnp.concatenate([t, jnp.cos(f * omega), -jnp.sin(f * omega)], axis=-1)
    return jnp.pad(feats, ((0, 0), (0, LANES - HYENA_EMB_DIM)))


def _filter_deltas():
    max_decay = math.log(HYENA_DECAY_TARGET) / HYENA_FAST_DECAY_PCT
    min_decay = math.log(HYENA_DECAY_TARGET) / HYENA_SLOW_DECAY_PCT
    return jnp.abs(jnp.linspace(min_decay, max_decay, HYENA_WIDTH, dtype=F32)).reshape(1, HYENA_WIDTH)


def _hyena_filter(n, feats, deltas, w1, b1, f1, w2, b2, f2, w3, group_major):
    tl = min(512, n)
    hid = LANES
    pad_h = hid - HYENA_FILTER_HIDDEN
    w1p = jnp.pad(w1, ((0, LANES - HYENA_EMB_DIM), (0, pad_h)))
    w2p = jnp.pad(w2, ((0, pad_h), (0, pad_h)))
    w3p = jnp.pad(w3, ((0, pad_h), (0, 0)))
    vec = lambda v: jnp.pad(v, (0, pad_h)).reshape(1, hid)
    hw2 = 2 * HYENA_WIDTH
    full = lambda shape: pl.BlockSpec(shape, lambda i: (0,) * len(shape))
    if group_major:
        shape = _gm_shape((), n // FFT_N2, hw2)
        out_spec = pl.BlockSpec(_gm_shape((), tl // FFT_N2, hw2), lambda i: (0, 0, i, 0, 0))
    else:
        shape = (n, hw2)
        out_spec = pl.BlockSpec((tl, hw2), lambda i: (i, 0))
    return pl.pallas_call(
        functools.partial(_filter_kernel, tl=tl, n=n, group_major=group_major),
        out_shape=[jax.ShapeDtypeStruct(shape, F32), jax.ShapeDtypeStruct((SUBLANES, hw2), F32)],
        grid=(n // tl,),
        in_specs=[pl.BlockSpec((tl, LANES), lambda i: (i, 0)), full((LANES, hid)), full((1, hid)), full((1, hid)),
                  full((hid, hid)), full((1, hid)), full((1, hid)), full((hid, hw2)), full((1, HYENA_WIDTH))],
        out_specs=[out_spec, full((SUBLANES, hw2))],
        compiler_params=_params(("arbitrary",)),
        name="hyena_filter",
    )(feats, w1p, vec(b1), vec(f1), w2p, vec(b2), vec(f2), w3p, deltas)


def _stack_complex(m):
    return np.block([[m.real, -m.imag], [m.imag, m.real]])


def _hilo(m):
    m = jnp.asarray(m, dtype=F32)
    return _split(m)


@functools.lru_cache(maxsize=None)
def _fft_constants(n):
    m = 2 * n
    n2 = FFT_N2
    n1 = m // n2
    n1h = n1 // 2
    k1 = np.arange(n1)
    t1 = np.arange(n1h)
    f1 = np.exp(-2j * np.pi * np.outer(k1, t1) / n1)
    f3 = np.exp(2j * np.pi * np.outer(t1, k1) / n1) / m
    k2 = np.arange(n2)
    t2 = np.arange(n2)
    w2 = np.exp(-2j * np.pi * np.outer(k2, t2) / n2)
    tw = np.exp(-2j * np.pi * np.outer(k1, t2) / m)
    return dict(
        n1=n1, n1h=n1h,
        f1c=_stack_complex(f1), f1r=np.concatenate([f1.real, f1.imag], axis=0),
        f3c=_stack_complex(f3),
        w2r=w2.real.astype(np.float32), w2i=w2.imag.astype(np.float32),
        twr=tw.real.astype(np.float32), twi=tw.imag.astype(np.float32),
    )


def _stage2_tables(n):
    c = _fft_constants(n)
    w2r, w2i = jnp.asarray(c["w2r"])[None], jnp.asarray(c["w2i"])[None]
    twr, twi = jnp.asarray(c["twr"])[:, None, :], jnp.asarray(c["twi"])[:, None, :]
    return (*_split(w2r * twr - w2i * twi), *_split(w2r * twi + w2i * twr))


def _stacked_tables(grh_ref, grl_ref, gih_ref, gil_ref, q):
    def stack(gr, gi):
        return jnp.concatenate([jnp.concatenate([gr, -gi], axis=1), jnp.concatenate([gi, gr], axis=1)], axis=0)
    return stack(grh_ref[q], gih_ref[q]), stack(grl_ref[q], gil_ref[q])


def _s1_kernel(x_ref, fh_ref, fl_ref, o_ref, *, nparts, ncw, n1, n1h):
    fh = fh_ref[...]
    fl = fl_ref[...]
    for r in range(SUBLANES):
        rows = pl.ds(r, n1h, stride=SUBLANES)
        xs = jnp.concatenate(
            [jnp.concatenate([x_ref[p, cc, rows, :] for p in range(nparts)], axis=0) for cc in range(ncw)], axis=1)
        xh, xl = _split(xs)
        res = _dot3(fh, fl, xh, xl)
        for ri in range(2):
            for cc in range(ncw):
                o_ref[ri, cc, pl.ds(r, n1, stride=SUBLANES), :] = res[ri * n1:(ri + 1) * n1, cc * LANES:(cc + 1) * LANES]


def _fft_stage1(x, fmat, n1, n1h, ncw):
    nparts, ncc, groups = x.shape[:3]
    fh, fl = _hilo(fmat)
    return pl.pallas_call(
        functools.partial(_s1_kernel, nparts=nparts, ncw=ncw, n1=n1, n1h=n1h),
        out_shape=jax.ShapeDtypeStruct((2, ncc, groups, n1 * SUBLANES, LANES), F32),
        grid=(groups, ncc // ncw),
        in_specs=[
            pl.BlockSpec((nparts, ncw, None, n1h * SUBLANES, LANES), lambda j, ci: (0, ci, j, 0, 0)),
            pl.BlockSpec(fh.shape, lambda j, ci: (0, 0)),
            pl.BlockSpec(fl.shape, lambda j, ci: (0, 0)),
        ],
        out_specs=pl.BlockSpec((2, ncw, None, n1 * SUBLANES, LANES), lambda j, ci: (0, ci, j, 0, 0)),
        compiler_params=_params(("parallel", "parallel")),
        name="fft_stage1",
    )(x, fh, fl)


def _load_k1(a_ref, q):
    ncc = a_ref.shape[1]
    return jnp.concatenate([a_ref[:, cc, :, q].reshape(2 * FFT_N2, LANES) for cc in range(ncc)], axis=1)


def _filter_spectrum_kernel(a_ref, grh_ref, grl_ref, gih_ref, gil_ref, s_ref, o_ref, *, kg):
    s = jnp.sum(s_ref[...], axis=0, keepdims=True)
    inv = 1.0 / (s[:, :HYENA_WIDTH] + s[:, HYENA_WIDTH:])
    half = FFT_N2
    for q in range(kg):
        ah, al = _split(_load_k1(a_ref, q))
        gh, gl = _stacked_tables(grh_ref, grl_ref, gih_ref, gil_ref, q)
        h = _dot3(gh, gl, ah, al)
        hf = h[:, :HYENA_WIDTH]
        hb = h[:, HYENA_WIDTH:]
        o_ref[q, :half] = (hf[:half] + hb[:half]) * inv
        o_ref[q, half:] = (hf[half:] - hb[half:]) * inv


FFT_K1_GROUP = 8


def _k1_spec(kg, width):
    return pl.BlockSpec((2, width // LANES, GROUPS, kg, SUBLANES, LANES), lambda i: (0, 0, 0, i, 0, 0))


def _filter_spectrum(a, tables, sums, n1):
    kg = FFT_K1_GROUP
    a6 = a.reshape(2, 2 * HYENA_WIDTH // LANES, GROUPS, n1, SUBLANES, LANES)
    tspec = pl.BlockSpec((kg, FFT_N2, FFT_N2), lambda i: (i, 0, 0))
    return pl.pallas_call(
        functools.partial(_filter_spectrum_kernel, kg=kg),
        out_shape=jax.ShapeDtypeStruct((n1, 2 * FFT_N2, HYENA_WIDTH), F32),
        grid=(n1 // kg,),
        in_specs=[_k1_spec(kg, 2 * HYENA_WIDTH), tspec, tspec, tspec, tspec,
                  pl.BlockSpec((SUBLANES, 2 * HYENA_WIDTH), lambda i: (0, 0))],
        out_specs=pl.BlockSpec((kg, 2 * FFT_N2, HYENA_WIDTH), lambda i: (i, 0, 0)),
        compiler_params=_params(("parallel",)),
        name="filter_spectrum",
    )(a6, *tables, sums)


def _dot_t(a, b):
    return lax.dot_general(a, b, (((0,), (0,)), ((), ())), preferred_element_type=F32)


def _s2_kernel(a_ref, kf_ref, grh_ref, grl_ref, gih_ref, gil_ref, o_ref, *, kg):
    half = FFT_N2
    for q in range(kg):
        ah, al = _split(_load_k1(a_ref, q))
        gh, gl = _stacked_tables(grh_ref, grl_ref, gih_ref, gil_ref, q)
        x = _dot3(gh, gl, ah, al)
        xr, xi = x[:half], x[half:]
        kr, ki = kf_ref[q, :half], kf_ref[q, half:]
        y = jnp.concatenate([xr * kr - xi * ki, xr * ki + xi * kr], axis=0)
        yh, yl = _split(y)
        bt = _dot_t(gh, yh) + _dot_t(gh, yl) + _dot_t(gl, yh)
        for cc in range(HYENA_WIDTH // LANES):
            o_ref[:, cc, :, q] = bt[:, cc * LANES:(cc + 1) * LANES].reshape(2, GROUPS, SUBLANES, LANES)


def _fft_stage2(a, kf, tables, n1):
    kg = FFT_K1_GROUP
    a6 = a.reshape(2, HYENA_WIDTH // LANES, GROUPS, n1, SUBLANES, LANES)
    tspec = pl.BlockSpec((kg, FFT_N2, FFT_N2), lambda i: (i, 0, 0))
    dspec = _k1_spec(kg, HYENA_WIDTH)
    out = pl.pallas_call(
        functools.partial(_s2_kernel, kg=kg),
        out_shape=jax.ShapeDtypeStruct(a6.shape, F32),
        grid=(n1 // kg,),
        in_specs=[dspec, pl.BlockSpec((kg, 2 * FFT_N2, HYENA_WIDTH), lambda i: (i, 0, 0)), tspec, tspec, tspec, tspec],
        out_specs=dspec,
        compiler_params=_params(("parallel",)),
        name="fft_stage2",
    )(a6, kf, *tables)
    return out.reshape(a.shape)


def _s3_kernel(b_ref, vx_ref, x0_ref, bias_ref, fh_ref, fl_ref, o_ref, *, ncw, n1, n1h):
    fh = fh_ref[...]
    fl = fl_ref[...]
    for r in range(SUBLANES):
        krows = pl.ds(r, n1, stride=SUBLANES)
        z = jnp.concatenate(
            [jnp.concatenate([b_ref[ri, cc, krows, :] for ri in range(2)], axis=0) for cc in range(ncw)], axis=1)
        zh, zl = _split(z)
        y = _dot3(fh, fl, zh, zl)
        trows = pl.ds(r, n1h, stride=SUBLANES)
        for p in range(2):
            for cc in range(ncw):
                yy = y[p * n1h:(p + 1) * n1h, cc * LANES:(cc + 1) * LANES]
                o_ref[p, cc, trows, :] = (yy + vx_ref[p, cc, trows, :] * bias_ref[cc]) * x0_ref[p, cc, trows, :]


def _fft_stage3(bt, vx, x0, bias, fmat, n1, n1h, ncw):
    ncc, groups = bt.shape[1:3]
    fh, fl = _hilo(fmat)
    tspec = pl.BlockSpec((2, ncw, None, n1h * SUBLANES, LANES), lambda j, ci: (0, ci, j, 0, 0))
    return pl.pallas_call(
        functools.partial(_s3_kernel, ncw=ncw, n1=n1, n1h=n1h),
        out_shape=jax.ShapeDtypeStruct(vx.shape, F32),
        grid=(groups, ncc // ncw),
        in_specs=[
            pl.BlockSpec((2, ncw, None, n1 * SUBLANES, LANES), lambda j, ci: (0, ci, j, 0, 0)),
            tspec, tspec,
            pl.BlockSpec((ncw, 1, LANES), lambda j, ci: (ci, 0, 0)),
            pl.BlockSpec(fh.shape, lambda j, ci: (0, 0)),
            pl.BlockSpec(fl.shape, lambda j, ci: (0, 0)),
        ],
        out_specs=tspec,
        compiler_params=_params(("parallel", "parallel")),
        name="fft_stage3",
    )(bt, vx, x0, bias, fh, fl)


def _hyena_latent(p, hy, feats, deltas, tables, n):
    conv_w, conv_b, w1, b1, f1, w2, b2, f2, w3, bias = hy
    c = _fft_constants(n)
    n1, n1h = c["n1"], c["n1h"]
    vx, x0 = _hyena_prep(p, conv_w, conv_b, group_major=True)
    gm_shape = vx.shape
    rows = lambda a: a.reshape(*a.shape[:-3], n1h * SUBLANES, LANES)
    vx, x0 = rows(vx), rows(x0)
    taps, sums = _hyena_filter(n, feats, deltas, w1, b1, f1, w2, b2, f2, w3, group_major=True)
    a_f = _fft_stage1(rows(taps)[None], c["f1r"], n1, n1h, ncw=4)
    kf = _filter_spectrum(a_f, tables, sums, n1)
    a = _fft_stage1(vx, c["f1c"], n1, n1h, ncw=4)
    bt = _fft_stage2(a, kf, tables, n1)
    out = _fft_stage3(bt, vx, x0, bias.reshape(HYENA_WIDTH // LANES, 1, LANES), c["f3c"], n1, n1h, ncw=2)
    return out.reshape(gm_shape)


@functools.lru_cache(maxsize=None)
def _dense_dft_constants(n):
    m = 2 * n
    k = np.arange(m)
    t = np.arange(n)
    f = np.exp(-2j * np.pi * np.outer(k, t) / m)
    finv = np.exp(2j * np.pi * np.outer(t, k) / m) / m
    return _stack_complex(f), np.concatenate([f.real, f.imag], axis=0), _stack_complex(finv)


def _dense_conv_kernel(vx_ref, x0_ref, hf_ref, hb_ref, sf_ref, sb_ref, bias_ref,
                       fch, fcl, frh, frl, fih, fil, o_ref, *, n):
    m = 2 * n
    z = jnp.concatenate([vx_ref[0], vx_ref[1]], axis=0)
    zh, zl = _split(z)
    zf = _dot3(fch[...], fcl[...], zh, zl)
    hfh, hfl = _split(hf_ref[...])
    hbh, hbl = _split(hb_ref[...])
    hf = _dot3(frh[...], frl[...], hfh, hfl)
    hb = _dot3(frh[...], frl[...], hbh, hbl)
    inv = 1.0 / (jnp.sum(sf_ref[...], axis=0, keepdims=True) + jnp.sum(sb_ref[...], axis=0, keepdims=True))
    kr = (hf[:m] + hb[:m]) * inv
    ki = (hf[m:] - hb[m:]) * inv
    zr, zi = zf[:m], zf[m:]
    y = jnp.concatenate([zr * kr - zi * ki, zr * ki + zi * kr], axis=0)
    yh, yl = _split(y)
    out = _dot3(fih[...], fil[...], yh, yl)
    bias = bias_ref[...]
    for p in range(2):
        o_ref[p] = (out[p * n:(p + 1) * n] + vx_ref[p] * bias) * x0_ref[p]


def _hyena_context(p, hy, feats, deltas, n):
    conv_w, conv_b, w1, b1, f1, w2, b2, f2, w3, bias = hy
    vx, x0 = _hyena_prep(p, conv_w, conv_b, group_major=False)
    taps, sums = _hyena_filter(n, feats, deltas, w1, b1, f1, w2, b2, f2, w3, group_major=False)
    fc, fr, fi = _dense_dft_constants(n)
    mats = [*_hilo(fc), *_hilo(fr), *_hilo(fi)]
    cw = 256
    nct = HYENA_WIDTH // cw
    dspec = pl.BlockSpec((2, n, cw), lambda ci: (0, 0, ci))
    return pl.pallas_call(
        functools.partial(_dense_conv_kernel, n=n),
        out_shape=jax.ShapeDtypeStruct(vx.shape, F32),
        grid=(nct,),
        in_specs=[dspec, dspec,
                  pl.BlockSpec((n, cw), lambda ci: (0, ci)), pl.BlockSpec((n, cw), lambda ci: (0, nct + ci)),
                  pl.BlockSpec((SUBLANES, cw), lambda ci: (0, ci)), pl.BlockSpec((SUBLANES, cw), lambda ci: (0, nct + ci)),
                  pl.BlockSpec((1, cw), lambda ci: (0, ci))]
                 + [pl.BlockSpec(mt.shape, lambda ci: (0, 0)) for mt in mats],
        out_specs=dspec,
        compiler_params=_params(("parallel",)),
        name="context_long_conv",
    )(vx, x0, taps, taps, sums, sums, bias.reshape(1, HYENA_WIDTH), *mats)


def _pool_kernel(x_ref, p_ref, n_ref, w_ref, sc_ref, o_ref, *, tl, nt, n):
    i = pl.program_id(1)
    x = x_ref[...]
    pv = jnp.where(i > 0, p_ref[...], 0.0)
    nx = jnp.where(i < nt - 1, n_ref[...], 0.0)
    ext = jnp.concatenate([pv, x, nx], axis=0)
    rows = tl + 2 * POOL_HALO
    t = i * tl + lax.broadcasted_iota(jnp.int32, (tl, POOL_GROUP), 0)
    for g, w in enumerate(POOL_WINDOWS):
        lanes = slice(g * POOL_GROUP, (g + 1) * POOL_GROUP)
        a = ext[:, lanes]
        c = a + pltpu.roll(a, 1, 0)
        h = 1
        while 2 * h < w:
            c = pltpu.roll(c, h, 0) + pltpu.roll(c, rows - h, 0)
            h *= 2
        total = c[POOL_HALO:POOL_HALO + tl]
        count = (jnp.minimum(t + h, n) - jnp.maximum(t - h, 0)).astype(F32)
        y = (total / count - x[:, lanes]).astype(BF16)
        o_ref[:, lanes] = _dot(y, w_ref[g].astype(BF16)) * sc_ref[:, lanes]


def _pool_mixer(p, w_pool, scale):
    b, n, _ = p.shape
    tl = min(512, n)
    nt = n // tl
    colblk = 3 * HYENA_WIDTH // POOL_WIDTH
    assert colblk * POOL_WIDTH == 3 * HYENA_WIDTH
    per = tl // POOL_HALO
    nrow = n // POOL_HALO
    return pl.pallas_call(
        functools.partial(_pool_kernel, tl=tl, nt=nt, n=n),
        out_shape=jax.ShapeDtypeStruct((b, n, POOL_WIDTH), F32),
        grid=(b, nt),
        in_specs=[
            pl.BlockSpec((None, tl, POOL_WIDTH), lambda bi, i: (bi, i, colblk)),
            pl.BlockSpec((None, POOL_HALO, POOL_WIDTH), lambda bi, i: (bi, jnp.maximum(i * per - 1, 0), colblk)),
            pl.BlockSpec((None, POOL_HALO, POOL_WIDTH), lambda bi, i: (bi, jnp.minimum((i + 1) * per, nrow - 1), colblk)),
            pl.BlockSpec(w_pool.shape, lambda bi, i: (0, 0, 0)),
            pl.BlockSpec((1, POOL_WIDTH), lambda bi, i: (0, 0)),
        ],
        out_specs=pl.BlockSpec((None, tl, POOL_WIDTH), lambda bi, i: (bi, i, 0)),
        compiler_params=_params(("parallel", "parallel")),
        name="pool_mixer",
    )(p, p, p, w_pool, scale.reshape(1, POOL_WIDTH))


def _load_group_major(ref, tl):
    ncc = ref.shape[0]
    return jnp.concatenate(
        [jnp.concatenate([ref[cc, jg, t1] for cc in range(ncc)], axis=1)
         for t1 in range(tl // FFT_N2) for jg in range(GROUPS)], axis=0)


def _outproj_kernel(at_ref, hy_ref, po_ref, x_ref, gb_ref, w_ref, gp_ref, gt_ref, o_ref, ox_ref, m_ref,
                    *, tm, tiles, group_major):
    t = pl.program_id(0)

    @pl.when(t == 0)
    def _():
        ox_ref[...] = jnp.zeros_like(ox_ref)

    def finish_previous():
        o_ref[...] = x_ref[...] + gt_ref[...] * _rms(ox_ref[...], gp_ref[...])

    @pl.when(t < tiles)
    def _():
        finish_previous()
        hy = _load_group_major(hy_ref, tm) if group_major else hy_ref[...]
        a0, a1 = ATTN_WIDTH, ATTN_WIDTH + HYENA_WIDTH
        m_ref[:, :a0] = _rms(at_ref[...], gb_ref[:, :a0]).astype(BF16)
        m_ref[:, a0:a1] = _rms(hy, gb_ref[:, a0:a1]).astype(BF16)
        m_ref[:, a1:] = _rms(po_ref[...], gb_ref[:, a1:]).astype(BF16)
        ox_ref[...] = _dot(m_ref[...], w_ref[...])

    pl.when(t == tiles)(finish_previous)


def _out_projection(attn, hy, po, x, g_branch, w_out, g_post, gate, group_major):
    b, n, d = x.shape
    tm = min(512, n)
    nt = n // tm
    tiles = b * nt

    def cur(t):
        return jnp.minimum(t, tiles - 1)

    def prev(t):
        return jnp.maximum(t - 1, 0)

    row = lambda width, sel: pl.BlockSpec((None, tm, width), lambda t: (sel(t) // nt, sel(t) % nt, 0))
    vec = lambda width: pl.BlockSpec((1, width), lambda t: (0, 0))
    if group_major:
        hy_spec = pl.BlockSpec((None, *_gm_shape((), tm // FFT_N2, HYENA_WIDTH)),
                               lambda t: (cur(t) // nt, 0, 0, cur(t) % nt, 0, 0))
    else:
        hy_spec = row(HYENA_WIDTH, cur)
    return pl.pallas_call(
        functools.partial(_outproj_kernel, tm=tm, tiles=tiles, group_major=group_major),
        out_shape=jax.ShapeDtypeStruct(x.shape, F32),
        grid=(tiles + 1,),
        in_specs=[row(ATTN_WIDTH, cur), hy_spec, row(POOL_WIDTH, cur), row(d, prev), vec(d),
                  pl.BlockSpec(w_out.shape, lambda t: (0, 0), pipeline_mode=pl.Buffered(1)), vec(d),
                  pl.BlockSpec((None, 1, d), lambda t: (prev(t) // nt, 0, 0))],
        out_specs=row(d, prev),
        scratch_shapes=[pltpu.VMEM((tm, d), F32), pltpu.VMEM((tm, d), BF16)],
        compiler_params=_params(("arbitrary",)),
        name="out_projection",
    )(attn, hy, po, x, g_branch.reshape(1, d), w_out, g_post.reshape(1, d), gate)


def _mlp_kernel(xe_ref, xp_ref, g_ref, shn_ref, scn_ref, wu_ref, wd_ref, gp_ref, gtp_ref, o_ref,
                h0_ref, h1_ref, acc0_ref, acc1_ref, *, rc, tiles):
    t = pl.program_id(0)
    k = pl.program_id(1)
    rows = pl.ds(pl.multiple_of(k * rc, rc), rc)

    def normalise(h_ref):
        a = g_ref[...] * (1.0 + scn_ref[...])
        h_ref[rows, :] = (_rms(xp_ref[...], a) + shn_ref[...]).astype(BF16)

    def finish(acc_ref):
        o_ref[...] = xe_ref[...] + _rms(acc_ref[rows, :], gp_ref[...] * gtp_ref[...])

    @pl.when(t == 0)
    def _():
        normalise(h0_ref)
        acc0_ref[rows, :] = jnp.zeros((rc, acc0_ref.shape[1]), F32)
        acc1_ref[rows, :] = jnp.zeros((rc, acc1_ref.shape[1]), F32)

    def step(slot):
        h_cur, h_oth = (h0_ref, h1_ref) if slot == 0 else (h1_ref, h0_ref)
        acc_cur, acc_oth = (acc0_ref, acc1_ref) if slot == 0 else (acc1_ref, acc0_ref)
        finish(acc_oth)
        acc_oth[rows, :] = jnp.zeros((rc, acc_oth.shape[1]), F32)
        normalise(h_oth)
        u = jnp.maximum(_dot(h_cur[...], wu_ref[...]), 0.0)
        acc_cur[...] += _dot((u * u).astype(BF16), wd_ref[...])

    work = (t >= 1) & (t <= tiles)
    pl.when(work & (t % 2 == 1))(functools.partial(step, 0))
    pl.when(work & (t % 2 == 0))(functools.partial(step, 1))
    pl.when(t == tiles + 1)(functools.partial(finish, acc0_ref if (tiles - 1) % 2 == 0 else acc1_ref))


MLP_TH = 1024


def _mlp(x, g_pre, shift, scale, w_up, w_down, g_post, gate):
    b, n, d = x.shape
    th = MLP_TH
    nk = w_up.shape[1] // th
    tm = min(1024, n)
    nt = n // tm
    tiles = b * nt
    rc = tm // nk
    assert rc * nk == tm and rc % BF16_ROWS == 0

    def norm_tile(t):
        return jnp.minimum(t, tiles - 1)

    def done_tile(t):
        return jnp.clip(t - 2, 0, tiles - 1)

    vec = pl.BlockSpec((1, d), lambda t, k: (0, 0))
    bvec = lambda sel: pl.BlockSpec((None, 1, d), lambda t, k: (sel(t) // nt, 0, 0))
    chunk = lambda sel, first: pl.BlockSpec(
        (None, rc, d), lambda t, k: (sel(t) // nt, (sel(t) % nt) * nk + jnp.where(t < first, 0, k), 0))
    busy = lambda t: (t >= 1) & (t <= tiles)
    return pl.pallas_call(
        functools.partial(_mlp_kernel, rc=rc, tiles=tiles),
        out_shape=jax.ShapeDtypeStruct(x.shape, F32),
        grid=(tiles + 2, nk),
        in_specs=[chunk(done_tile, 2), chunk(norm_tile, 0), vec, bvec(norm_tile), bvec(norm_tile),
                  pl.BlockSpec((d, th), lambda t, k: (0, jnp.where(busy(t), k, 0))),
                  pl.BlockSpec((th, d), lambda t, k: (jnp.where(busy(t), k, 0), 0)),
                  vec, bvec(done_tile)],
        out_specs=chunk(done_tile, 2),
        scratch_shapes=[pltpu.VMEM((tm, d), BF16)] * 2 + [pltpu.VMEM((tm, d), F32)] * 2,
        compiler_params=_params(("arbitrary", "arbitrary")),
        name="mlp",
    )(x, x, g_pre.reshape(1, d), shift, scale, w_up, w_down, g_post.reshape(1, d), gate)


def kernel(x, c, ctx, c_ctx, w_mod, b_mod, g_pre_mix, g_post_mix, g_pre_mlp, g_post_mlp, w_in, w_out, g_branch,
           attn_sink, hy_conv_w, hy_conv_b, hy_w1, hy_b1, hy_freq1, hy_w2, hy_b2, hy_freq2, hy_w3, hy_bias,
           pool_w, pool_scale, w_up, w_down):
    b, n, d = x.shape
    n_ctx = ctx.shape[1]
    depth = w_mod.shape[0]
    assert b == 2 and d == D_MODEL and n % 512 == 0 and n_ctx % BLOCK == 0

    cond = jnp.concatenate([c, c_ctx[None], jnp.zeros((SUBLANES - b - 1, d), F32)], axis=0)
    mods = _modulation(cond, w_mod, b_mod)

    w_in_b = _prepare_w_in(w_in, IN_TN)
    w_up_b = w_up.astype(BF16)
    w_out_b, w_down_b = w_out.astype(BF16), w_down.astype(BF16)
    tables_x, tables_c = _rope_tables(n), _rope_tables(n_ctx)
    feats_x, feats_c = _filter_features(n), _filter_features(n_ctx)
    deltas = _filter_deltas()
    tables = _stage2_tables(n)

    for i in range(depth):
        last = i == depth - 1
        hy = (hy_conv_w[i], hy_conv_b[i], hy_w1[i], hy_b1[i], hy_freq1[i], hy_w2[i], hy_b2[i], hy_freq2[i],
              hy_w3[i], hy_bias[i])
        mx = [m[:, None, :] for m in jnp.split(mods[i, :b], N_MOD, axis=-1)]
        mc = [jnp.broadcast_to(m[None, None, :], (b, 1, d)) for m in jnp.split(mods[i, b], N_MOD, axis=-1)]

        qkv_x, px = _in_projection(x, g_pre_mix[i], mx[0], mx[1], w_in_b[i], tables_x, rope=True)
        qkv_c, pc = _in_projection(ctx, g_pre_mix[i], mc[0], mc[1], w_in_b[i], tables_c, rope=False)
        kv_ctx = qkv_c[..., Q_END:V_END]

        attn_x = _attention(qkv_x, kv_ctx, attn_sink[i], local=True)
        hy_x = _hyena_latent(px, hy, feats_x, deltas, tables, n)
        po_x = _pool_mixer(px, pool_w[i], pool_scale[i])
        x = _out_projection(attn_x, hy_x, po_x, x, g_branch[i], w_out_b[i], g_post_mix[i], mx[2], group_major=True)
        x = _mlp(x, g_pre_mlp[i], mx[3], mx[4], w_up_b[i], w_down_b[i], g_post_mlp[i], mx[5])

        if not last:
            attn_c = _attention(qkv_c, kv_ctx, attn_sink[i], local=False)
            hy_c = _hyena_context(pc, hy, feats_c, deltas, n_ctx)
            po_c = _pool_mixer(pc, pool_w[i], pool_scale[i])
            ctx = _out_projection(attn_c, hy_c, po_c, ctx, g_branch[i], w_out_b[i], g_post_mix[i], mc[2],
                                  group_major=False)
            ctx = _mlp(ctx, g_pre_mlp[i], mc[3], mc[4], w_up_b[i], w_down_b[i], g_post_mlp[i], mc[5])
    return x
```

```python
import functools
import math

import numpy as np
import jax
import jax.numpy as jnp
from jax import lax
from jax.experimental import pallas as pl
from jax.experimental.pallas import tpu as pltpu

F32 = jnp.float32
BF16 = jnp.bfloat16

D_MODEL = 2048
DEPTH = 4
GRID_W = 64
ATTN_WIDTH = D_MODEL // 2
HYENA_WIDTH = D_MODEL // 4
POOL_WIDTH = D_MODEL - ATTN_WIDTH - HYENA_WIDTH
HEAD_DIM = 128
N_HEADS = ATTN_WIDTH // HEAD_DIM
N_KV_HEADS = 2
KV_GROUP = N_HEADS // N_KV_HEADS
KV_WIDTH = N_KV_HEADS * HEAD_DIM
WINDOW = 128
BLOCK = 128
ROPE_BASE = 10000.0
HYENA_EMB_DIM = 33
HYENA_FILTER_HIDDEN = 64
HYENA_FAST_DECAY_PCT = 0.3
HYENA_SLOW_DECAY_PCT = 1.5
HYENA_DECAY_TARGET = 1e-2
POOL_WINDOWS = (2, 4, 8, 16)
POOL_GROUP = POOL_WIDTH // len(POOL_WINDOWS)
MLP_HIDDEN = 4 * D_MODEL
N_MOD = 6
EPS = 1e-6
NEG_INF = -1e30

Q_END = ATTN_WIDTH
K_END = Q_END + KV_WIDTH
V_END = K_END + KV_WIDTH
HY_END = V_END + 3 * HYENA_WIDTH
IN_WIDTH = HY_END + POOL_WIDTH

LANES = 128
SUBLANES = 8
FFT_N2 = 128
POOL_HALO = 16
VMEM_LIMIT = 56 * 1024 * 1024


def _params(sem, vmem=VMEM_LIMIT):
    return pltpu.CompilerParams(dimension_semantics=sem, vmem_limit_bytes=vmem)


def _split(x):
    hi = x.astype(BF16)
    lo = (x - hi.astype(F32)).astype(BF16)
    return hi, lo


def _dot(a, b):
    return jnp.dot(a, b, preferred_element_type=F32)


def _dot3(ah, al, bh, bl):
    return _dot(ah, bh) + _dot(ah, bl) + _dot(al, bh)


def _rms(x, g):
    return x * lax.rsqrt(jnp.mean(x * x, axis=-1, keepdims=True) + EPS) * g


BF16_ROWS = 2 * SUBLANES


def _norm_scale_rows(x_ref, h_ref, a_ref, s_ref, rows, cols=None):
    cols = slice(None) if cols is None else cols
    nchunks = rows // BF16_ROWS

    def chunk(c):
        return pl.ds(pl.multiple_of(c * BF16_ROWS, BF16_ROWS), BF16_ROWS)

    def inv_rms(c):
        x = x_ref[chunk(c), :]
        return lax.rsqrt(jnp.mean(x * x, axis=-1, keepdims=True) + EPS)

    def body(c, inv):
        inv_next = inv_rms(jnp.minimum(c + 1, nchunks - 1))
        y = x_ref[chunk(c), :] * inv * a_ref[...]
        if s_ref is not None:
            y = y + s_ref[...]
        h_ref[chunk(c), cols] = y.astype(BF16)
        return inv_next

    lax.fori_loop(0, nchunks, body, inv_rms(0), unroll=8)


def _residual_norm_rows(x_ref, y_ref, o_ref, pg_ref, rows):
    nchunks = rows // SUBLANES

    def chunk(c):
        return pl.ds(pl.multiple_of(c * SUBLANES, SUBLANES), SUBLANES)

    def inv_rms(c):
        y = y_ref[chunk(c), :]
        return lax.rsqrt(jnp.mean(y * y, axis=-1, keepdims=True) + EPS)

    def body(c, inv):
        inv_next = inv_rms(jnp.minimum(c + 1, nchunks - 1))
        o_ref[chunk(c), :] = x_ref[chunk(c), :] + y_ref[chunk(c), :] * inv * pg_ref[...]
        return inv_next

    lax.fori_loop(0, nchunks, body, inv_rms(0), unroll=16)


def _mod_kernel(c_ref, w_ref, b_ref, o_ref):
    c = c_ref[...]
    s = c / (1.0 + jnp.exp(-c))
    sh, sl = _split(s)
    wh, wl = _split(w_ref[...])
    o_ref[...] = _dot3(sh, sl, wh, wl) + b_ref[...]


def _modulation(cond, w_mod, b_mod):
    depth, d, width = w_mod.shape
    tn = 1024
    return pl.pallas_call(
        _mod_kernel,
        out_shape=jax.ShapeDtypeStruct((depth, SUBLANES, width), F32),
        grid=(depth, width // tn),
        in_specs=[
            pl.BlockSpec((SUBLANES, d), lambda l, j: (0, 0)),
            pl.BlockSpec((None, d, tn), lambda l, j: (l, 0, j)),
            pl.BlockSpec((None, 1, tn), lambda l, j: (l, 0, j)),
        ],
        out_specs=pl.BlockSpec((None, SUBLANES, tn), lambda l, j: (l, 0, j)),
        compiler_params=_params(("parallel", "parallel")),
        name="modulation",
    )(cond, w_mod, b_mod.reshape(depth, 1, width))


QKV_WIDTH = V_END
REST_WIDTH = IN_WIDTH - V_END
SM_SCALE = HEAD_DIM ** -0.5


TABLE_Q, TABLE_K, TABLE_ID, TABLE_SCALE = range(4)


def _inproj_kernel(xn_ref, g_ref, shn_ref, scn_ref, w_ref, ca_ref, sa_ref, cb_ref, sb_ref,
                   qkv_ref, rest_ref, h0_ref, h1_ref, acc_ref, *, tm, tn, rc, tiles):
    t = pl.program_id(0)
    j = pl.program_id(1)

    def normalise(h_ref):
        rows = pl.ds(pl.multiple_of(jnp.minimum(j * rc, tm - rc), BF16_ROWS), rc)
        a = g_ref[...] * (1.0 + scn_ref[...])
        h_ref[rows, :] = (_rms(xn_ref[rows, :], a) + shn_ref[...]).astype(BF16)

    def finish_previous():
        acc = acc_ref[...]
        rest_ref[...] = acc
        nch = tn // LANES
        for ch in range(nch):
            a = acc[:, ch * LANES:(ch + 1) * LANES]
            cos, sin = (ca_ref, sa_ref) if ch < nch // 2 else (cb_ref, sb_ref)
            qkv_ref[:, ch * LANES:(ch + 1) * LANES] = (a * cos[...] + pltpu.roll(a, HEAD_DIM // 2, 1) * sin[...]).astype(BF16)

    @pl.when(t == 0)
    def _():
        normalise(h0_ref)

    @pl.when((t == 0) & (j == 0))
    def _():
        acc_ref[...] = jnp.zeros_like(acc_ref)

    def step(slot):
        h_cur, h_oth = (h0_ref, h1_ref) if slot == 0 else (h1_ref, h0_ref)
        finish_previous()
        normalise(h_oth)
        acc_ref[...] = _dot(h_cur[...], w_ref[j])

    work = (t >= 1) & (t <= tiles)
    pl.when(work & (t % 2 == 1))(functools.partial(step, 0))
    pl.when(work & (t % 2 == 0))(functools.partial(step, 1))
    pl.when((t == tiles + 1) & (j == 0))(finish_previous)


IN_TN = 512
QKV_SPARE = QKV_WIDTH // IN_TN
REST_SPARE = REST_WIDTH // IN_TN


def _in_projection(x, g, shift, scale, w, layer, tables, rope):
    b, n, d = x.shape
    tm = min(1024, n)
    tn = w.shape[-1]
    nj = w.shape[1]
    nt = n // tm
    tiles = b * nt
    half = tn // 2
    assert tn == IN_TN and nj * tn == IN_WIDTH and Q_END % tn == 0 and K_END % tn == half and V_END % tn == 0
    rc = -(-tm // nj)
    rc = -(-rc // (2 * BF16_ROWS)) * 2 * BF16_ROWS
    nq, nqkv = Q_END // tn, QKV_WIDTH // tn

    def norm_tile(t):
        return jnp.minimum(t, tiles - 1)

    def lagged(t, j):
        step = jnp.clip((t - 1) * nj + j - 1, 0, tiles * nj - 1)
        return step // nj, step % nj

    def kind_a(j):
        return jnp.where(j < nq, TABLE_Q if rope else TABLE_SCALE, jnp.where(j < nqkv, TABLE_K if rope else TABLE_ID, TABLE_ID))

    def kind_b(j):
        return jnp.where(j < nq, TABLE_Q if rope else TABLE_SCALE, TABLE_ID)

    vec = pl.BlockSpec((1, d), lambda t, j: (0, 0))
    bvec = lambda sel: pl.BlockSpec((None, 1, d), lambda t, j: (sel(t) // nt, 0, 0))
    def table(kind):
        def index(t, j):
            tp, jp = lagged(t, j)
            k = kind(jp)
            return k, jnp.where(k == TABLE_ID, 0, tp % nt), 0
        return pl.BlockSpec((None, tm, LANES), index)

    def out_spec(col_tile):
        return pl.BlockSpec((None, tm, tn), lambda t, j: (lagged(t, j)[0] // nt, lagged(t, j)[0] % nt,
                                                          col_tile(lagged(t, j)[1])))

    cos, sin = tables
    return pl.pallas_call(
        functools.partial(_inproj_kernel, tm=tm, tn=tn, rc=rc, tiles=tiles),
        out_shape=[jax.ShapeDtypeStruct((b, n, QKV_WIDTH + tn), BF16), jax.ShapeDtypeStruct((b, n, REST_WIDTH + tn), F32)],
        grid=(tiles + 2, nj),
        in_specs=[pl.BlockSpec((None, tm, d), lambda t, j: (norm_tile(t) // nt, norm_tile(t) % nt, 0)),
                  vec, bvec(norm_tile), bvec(norm_tile),
                  pl.BlockSpec((None, *w.shape[1:]), lambda t, j: (layer, 0, 0, 0), pipeline_mode=pl.Buffered(1)),
                  table(kind_a), table(kind_a), table(kind_b), table(kind_b)],
        out_specs=[out_spec(lambda jp: jnp.minimum(jp, QKV_SPARE)),
                   out_spec(lambda jp: jnp.where(jp >= nqkv, jp - nqkv, REST_SPARE))],
        scratch_shapes=[pltpu.VMEM((tm, d), BF16)] * 2 + [pltpu.VMEM((tm, tn), F32)],
        compiler_params=_params(("arbitrary", "arbitrary")),
        name="in_projection_rope" if rope else "in_projection",
    )(x, g.reshape(1, d), shift, scale, w, cos, sin, cos, sin)


def _prep_w_in_kernel(w_ref, o_ref, *, tn):
    j = pl.program_id(1)
    quarter = HEAD_DIM // 4
    nch = tn // LANES

    def write(n_swapped):
        for ch in range(nch):
            a = w_ref[:, ch * LANES:(ch + 1) * LANES]
            if ch < n_swapped:
                q = lax.broadcasted_iota(jnp.int32, a.shape, 1) // quarter
                a = jnp.where(q == 1, pltpu.roll(a, LANES - quarter, 1), jnp.where(q == 2, pltpu.roll(a, quarter, 1), a))
            o_ref[:, ch * LANES:(ch + 1) * LANES] = a.astype(BF16)

    nfull, nrem = divmod(K_END // LANES, nch)
    pl.when(j < nfull)(functools.partial(write, nch))
    pl.when(j == nfull)(functools.partial(write, nrem))
    pl.when(j > nfull)(functools.partial(write, 0))


def _prepare_w_in(w_in, tn):
    depth, d, width = w_in.shape
    return pl.pallas_call(
        functools.partial(_prep_w_in_kernel, tn=tn),
        out_shape=jax.ShapeDtypeStruct((depth, width // tn, d, tn), BF16),
        grid=(depth, width // tn),
        in_specs=[pl.BlockSpec((None, d, tn), lambda l, j: (l, 0, j))],
        out_specs=pl.BlockSpec((None, None, d, tn), lambda l, j: (l, j, 0, 0)),
        compiler_params=_params(("parallel", "parallel")),
        name="prepare_w_in",
    )(w_in)


def _rope_tables(n):
    quarter = HEAD_DIM // 4
    inv_freq = ROPE_BASE ** (-jnp.arange(quarter, dtype=F32) / quarter)
    t = jnp.arange(n, dtype=jnp.int32)
    row = (t // GRID_W).astype(F32)[:, None] * inv_freq[None, :]
    col = (t % GRID_W).astype(F32)[:, None] * inv_freq[None, :]
    cos = jnp.concatenate([jnp.cos(row), jnp.cos(col), jnp.cos(row), jnp.cos(col)], axis=-1)
    sin = jnp.concatenate([-jnp.sin(row), -jnp.sin(col), jnp.sin(row), jnp.sin(col)], axis=-1)
    one, zero = jnp.ones_like(cos), jnp.zeros_like(sin)
    return (jnp.stack([cos * SM_SCALE, cos, one, one * SM_SCALE]), jnp.stack([sin * SM_SCALE, sin, zero, zero]))


def _softmax_pv(parts, sink_col):
    m = sink_col
    for s, _ in parts:
        m = jnp.maximum(m, jnp.max(s, axis=-1, keepdims=True))
    den = jnp.exp(sink_col - m)
    out = None
    for s, v in parts:
        p = jnp.exp(s - m)
        den = den + jnp.sum(p, axis=-1, keepdims=True)
        pv = _dot(p.astype(BF16), v)
        out = pv if out is None else out + pv
    return out / den


def _ctx_attn_kernel(sink_ref, q_ref, kvc_ref, o_ref, *, tq):
    for g in range(N_KV_HEADS):
        heads = [g * KV_GROUP + h for h in range(KV_GROUP)]
        qg = jnp.concatenate([q_ref[:, h * HEAD_DIM:(h + 1) * HEAD_DIM] for h in heads], axis=0)
        sink_col = jnp.concatenate([jnp.full((tq, 1), sink_ref[h], F32) for h in heads], axis=0)
        kc = kvc_ref[:, g * HEAD_DIM:(g + 1) * HEAD_DIM]
        vc = kvc_ref[:, KV_WIDTH + g * HEAD_DIM:KV_WIDTH + (g + 1) * HEAD_DIM]
        s = lax.dot_general(qg, kc, (((1,), (1,)), ((), ())), preferred_element_type=F32)
        o = _softmax_pv([(s, vc)], sink_col)
        for hi, h in enumerate(heads):
            o_ref[:, h * HEAD_DIM:(h + 1) * HEAD_DIM] = o[hi * tq:(hi + 1) * tq]


ATTN_ROWS = KV_GROUP * BLOCK
SOFTMAX_CHUNK = 32


def _win_attn_kernel(sink_ref, q_ref, km_ref, kp_ref, kn_ref, vm_ref, vp_ref, vn_ref, kvc_ref, band_ref,
                     o_ref, ktw, vw, s_scr, p_scr, m_scr, *, tq, nb, nctx):
    i = pl.program_id(1)
    nsub = tq // BLOCK
    nloc = 3 * BLOCK

    def transposed(x):
        return x.astype(F32).T.astype(BF16)

    def block_rows(main_ref, prev_ref, next_ref, w, lanes):
        if w == 0:
            return prev_ref[:, lanes]
        if w == nsub + 1:
            return next_ref[:, lanes]
        return main_ref[(w - 1) * BLOCK:w * BLOCK, lanes]

    ones = jnp.ones((nloc + nctx, HEAD_DIM), BF16)
    for g in range(N_KV_HEADS):
        lanes = slice(g * HEAD_DIM, (g + 1) * HEAD_DIM)
        vlanes = slice(KV_WIDTH + g * HEAD_DIM, KV_WIDTH + (g + 1) * HEAD_DIM)
        kts = [transposed(block_rows(km_ref, kp_ref, kn_ref, w, lanes)) for w in range(nsub + 2)]
        kct = [transposed(kvc_ref[cb * BLOCK:(cb + 1) * BLOCK, lanes]) for cb in range(nctx // BLOCK)]
        for jb in range(nsub):
            for w in range(3):
                ktw[jb, g, :, w * BLOCK:(w + 1) * BLOCK] = kts[jb + w]
                vw[jb, g, w * BLOCK:(w + 1) * BLOCK, :HEAD_DIM] = block_rows(vm_ref, vp_ref, vn_ref, jb + w, lanes)
            for cb in range(nctx // BLOCK):
                ktw[jb, g, :, nloc + cb * BLOCK:nloc + (cb + 1) * BLOCK] = kct[cb]
            vw[jb, g, nloc:, :HEAD_DIM] = kvc_ref[:, vlanes]
            vw[jb, g, :, HEAD_DIM:] = ones

    col = lax.broadcasted_iota(jnp.int32, (1, nloc), 1)

    def rows_of(jb):
        start = jb * BLOCK
        return pl.ds(start if isinstance(start, int) else pl.multiple_of(start, BLOCK), BLOCK)

    def stage_a(jb, g):
        qg = jnp.concatenate([q_ref[rows_of(jb), (g * KV_GROUP + h) * HEAD_DIM:(g * KV_GROUP + h + 1) * HEAD_DIM]
                              for h in range(KV_GROUP)], axis=0)
        s_scr[g] = _dot(qg, ktw[jb, g])

    def stage_b(jb, g):
        blk = i * nsub + jb
        pen_prev = jnp.where(blk == 0, NEG_INF, 0.0).astype(F32)
        pen_next = jnp.where(blk == nb - 1, NEG_INF, 0.0).astype(F32)
        rowbias = jnp.where(col < BLOCK, pen_prev, jnp.where(col >= 2 * BLOCK, pen_next, 0.0))
        for c in range(ATTN_ROWS // SOFTMAX_CHUNK):
            rows = slice(c * SOFTMAX_CHUNK, (c + 1) * SOFTMAX_CHUNK)
            sink = sink_ref[g * KV_GROUP + (c * SOFTMAX_CHUNK) // BLOCK]
            s_loc = s_scr[g, rows, :nloc] + band_ref[rows, :] + rowbias
            s_ctx = s_scr[g, rows, nloc:]
            m = jnp.maximum(jnp.max(s_loc, axis=-1, keepdims=True), jnp.max(s_ctx, axis=-1, keepdims=True))
            m = jnp.maximum(m, sink)
            p_scr[g, rows, :nloc] = jnp.exp(s_loc - m).astype(BF16)
            p_scr[g, rows, nloc:] = jnp.exp(s_ctx - m).astype(BF16)
            m_scr[g, rows, :] = m

    def stage_c(jb, g):
        o = _dot(p_scr[g], vw[jb, g])
        for hi in range(KV_GROUP):
            h = g * KV_GROUP + hi
            rows = slice(hi * BLOCK, (hi + 1) * BLOCK)
            den = o[rows, HEAD_DIM:HEAD_DIM + 1] + jnp.exp(sink_ref[h] - m_scr[g, rows, :])
            o_ref[rows_of(jb), h * HEAD_DIM:(h + 1) * HEAD_DIM] = o[rows, :HEAD_DIM] / den

    stage_a(0, 0)
    stage_a(0, 1)
    stage_b(0, 0)

    def body(j, carry):
        stage_a(j, 0)
        stage_c(j - 1, 0)
        stage_b(j - 1, 1)
        stage_a(j, 1)
        stage_c(j - 1, 1)
        stage_b(j, 0)
        return carry

    lax.fori_loop(1, nsub, body, 0)
    stage_c(nsub - 1, 0)
    stage_b(nsub - 1, 1)
    stage_c(nsub - 1, 1)


def _band_bias():
    qi = np.arange(ATTN_ROWS)[:, None] % BLOCK
    sj = np.arange(3 * BLOCK)[None, :]
    return jnp.asarray(np.where(np.abs(sj - BLOCK - qi) <= WINDOW, 0.0, NEG_INF), dtype=F32)


def _attention(qkv, kv_ctx, sink, local):
    b, n, _ = qkv.shape
    c = kv_ctx.shape[1]
    tq = min(1024 if local else 512, n)
    nsub = tq // BLOCK
    nb = n // BLOCK
    kcol = Q_END // KV_WIDTH
    vcol = K_END // KV_WIDTH
    smem = pl.BlockSpec(memory_space=pltpu.SMEM)
    q_spec = pl.BlockSpec((None, tq, ATTN_WIDTH), lambda bi, i: (bi, i, 0))
    kvc_spec = pl.BlockSpec((None, c, 2 * KV_WIDTH), lambda bi, i: (bi, 0, 0))
    if local:
        def main(colblk):
            return pl.BlockSpec((None, tq, KV_WIDTH), lambda bi, i: (bi, i, colblk))

        def prev(colblk):
            return pl.BlockSpec((None, BLOCK, KV_WIDTH), lambda bi, i: (bi, jnp.maximum(i * nsub - 1, 0), colblk))

        def nxt(colblk):
            return pl.BlockSpec((None, BLOCK, KV_WIDTH), lambda bi, i: (bi, jnp.minimum((i + 1) * nsub, nb - 1), colblk))

        keys = 3 * BLOCK + c
        kern = functools.partial(_win_attn_kernel, tq=tq, nb=nb, nctx=c)
        in_specs = [smem, q_spec, main(kcol), prev(kcol), nxt(kcol), main(vcol), prev(vcol), nxt(vcol), kvc_spec,
                    pl.BlockSpec((ATTN_ROWS, 3 * BLOCK), lambda bi, i: (0, 0))]
        args = [sink, qkv, qkv, qkv, qkv, qkv, qkv, qkv, kv_ctx, _band_bias()]
        scratch = [pltpu.VMEM((nsub, N_KV_HEADS, HEAD_DIM, keys), BF16),
                   pltpu.VMEM((nsub, N_KV_HEADS, keys, 2 * HEAD_DIM), BF16),
                   pltpu.VMEM((2, ATTN_ROWS, keys), F32),
                   pltpu.VMEM((2, ATTN_ROWS, keys), BF16),
                   pltpu.VMEM((2, ATTN_ROWS, 1), F32)]
    else:
        kern = functools.partial(_ctx_attn_kernel, tq=tq)
        in_specs = [smem, q_spec, kvc_spec]
        args = [sink, qkv, kv_ctx]
        scratch = []
    return pl.pallas_call(
        kern,
        out_shape=jax.ShapeDtypeStruct((b, n, ATTN_WIDTH), F32),
        grid=(b, n // tq),
        in_specs=in_specs,
        out_specs=pl.BlockSpec((None, tq, ATTN_WIDTH), lambda bi, i: (bi, i, 0)),
        scratch_shapes=scratch,
        compiler_params=_params(("parallel", "parallel")),
        name="window_attention" if local else "context_attention",
    )(*args)


GROUPS = FFT_N2 // SUBLANES


def _gm_shape(lead, t1, width):
    return (*lead, width // LANES, GROUPS, t1, SUBLANES, LANES)


def _store_group_major(o_ref, val, tl):
    for t1 in range(tl // FFT_N2):
        for jg in range(GROUPS):
            r0 = (t1 * GROUPS + jg) * SUBLANES
            for cc in range(val.shape[1] // LANES):
                o_ref[cc, jg, t1] = val[r0:r0 + SUBLANES, cc * LANES:(cc + 1) * LANES]


def _hyena_prep_kernel(u_ref, p_ref, n_ref, w_ref, b_ref, vx_ref, x0_ref, *, tl, nt, group_major):
    i = pl.program_id(1)
    u = u_ref[...]
    prev_row = jnp.where(i > 0, p_ref[SUBLANES - 1:SUBLANES, :], 0.0)
    next_row = jnp.where(i < nt - 1, n_ref[0:1, :], 0.0)
    row = lax.broadcasted_iota(jnp.int32, u.shape, 0)
    um = jnp.where(row == 0, prev_row, pltpu.roll(u, 1, 0))
    up = jnp.where(row == tl - 1, next_row, pltpu.roll(u, tl - 1, 0))
    z = um * w_ref[0:1, :] + u * w_ref[1:2, :] + up * w_ref[2:3, :] + b_ref[...]
    x0 = z[:, :HYENA_WIDTH]
    vx = z[:, 2 * HYENA_WIDTH:] * z[:, HYENA_WIDTH:2 * HYENA_WIDTH]
    if group_major:
        _store_group_major(vx_ref, vx, tl)
        _store_group_major(x0_ref, x0, tl)
    else:
        vx_ref[...] = vx
        x0_ref[...] = x0


def _hyena_prep(p, conv_w, conv_b, group_major):
    b, n, _ = p.shape
    tl = min(512, n)
    nt = n // tl
    hw = 3 * HYENA_WIDTH
    colblk = 0
    nrow8 = n // SUBLANES
    per = tl // SUBLANES
    if group_major:
        shape = _gm_shape((b,), n // FFT_N2, HYENA_WIDTH)
        out_spec = pl.BlockSpec((None, *_gm_shape((), tl // FFT_N2, HYENA_WIDTH)), lambda bi, i: (bi, 0, 0, i, 0, 0))
    else:
        shape = (b, n, HYENA_WIDTH)
        out_spec = pl.BlockSpec((None, tl, HYENA_WIDTH), lambda bi, i: (bi, i, 0))
    return pl.pallas_call(
        functools.partial(_hyena_prep_kernel, tl=tl, nt=nt, group_major=group_major),
        out_shape=[jax.ShapeDtypeStruct(shape, F32)] * 2,
        grid=(b, nt),
        in_specs=[
            pl.BlockSpec((None, tl, hw), lambda bi, i: (bi, i, colblk)),
            pl.BlockSpec((None, SUBLANES, hw), lambda bi, i: (bi, jnp.maximum(i * per - 1, 0), colblk)),
            pl.BlockSpec((None, SUBLANES, hw), lambda bi, i: (bi, jnp.minimum((i + 1) * per, nrow8 - 1), colblk)),
            pl.BlockSpec((3, hw), lambda bi, i: (0, 0)),
            pl.BlockSpec((1, hw), lambda bi, i: (0, 0)),
        ],
        out_specs=[out_spec, out_spec],
        compiler_params=_params(("parallel", "parallel")),
        name="hyena_prep",
    )(p, p, p, conv_w, conv_b.reshape(1, hw))


def _filter_kernel(ft_ref, w1_ref, b1_ref, f1_ref, w2_ref, b2_ref, f2_ref, w3_ref, dl_ref, h_ref, s_ref,
                   *, tl, n, group_major):
    i = pl.program_id(0)

    def dense(a, w_ref):
        ah, al = _split(a)
        wh, wl = _split(w_ref[...])
        return _dot3(ah, al, wh, wl)

    h = jnp.sin(f1_ref[...] * (dense(ft_ref[...], w1_ref) + b1_ref[...]))
    h = jnp.sin(f2_ref[...] * (dense(h, w2_ref) + b2_ref[...]))
    h = dense(h, w3_ref)
    t = (i * tl + lax.broadcasted_iota(jnp.int32, (tl, HYENA_WIDTH), 0)).astype(F32) / float(n - 1)
    decay = jnp.exp(-t * dl_ref[...])
    h = h * jnp.concatenate([decay, decay], axis=1)
    if group_major:
        _store_group_major(h_ref, h, tl)
    else:
        h_ref[...] = h

    @pl.when(i == 0)
    def _():
        s_ref[...] = jnp.zeros_like(s_ref)

    s_ref[...] += jnp.sum(jnp.abs(h).reshape(tl // SUBLANES, SUBLANES, 2 * HYENA_WIDTH), axis=0)


def _filter_features(n):
    t = jnp.linspace(0.0, 1.0, n, dtype=F32)[:, None]
    bands = (HYENA_EMB_DIM - 1) // 2
    omega = 2.0 * math.pi * jnp.arange(n, dtype=F32)[:, None] / n
    f = jnp.linspace(1e-4, bands - 1, bands, dtype=F32)[None, :]
    feats = jnp.concatenate([t, jnp.cos(f * omega), -jnp.sin(f * omega)], axis=-1)
    return jnp.pad(feats, ((0, 0), (0, LANES - HYENA_EMB_DIM)))


def _filter_deltas():
    max_decay = math.log(HYENA_DECAY_TARGET) / HYENA_FAST_DECAY_PCT
    min_decay = math.log(HYENA_DECAY_TARGET) / HYENA_SLOW_DECAY_PCT
    return jnp.abs(jnp.linspace(min_decay, max_decay, HYENA_WIDTH, dtype=F32)).reshape(1, HYENA_WIDTH)


def _hyena_filter(n, feats, deltas, w1, b1, f1, w2, b2, f2, w3, group_major):
    tl = min(512, n)
    hid = LANES
    pad_h = hid - HYENA_FILTER_HIDDEN
    w1p = jnp.pad(w1, ((0, LANES - HYENA_EMB_DIM), (0, pad_h)))
    w2p = jnp.pad(w2, ((0, pad_h), (0, pad_h)))
    w3p = jnp.pad(w3, ((0, pad_h), (0, 0)))
    vec = lambda v: jnp.pad(v, (0, pad_h)).reshape(1, hid)
    hw2 = 2 * HYENA_WIDTH
    full = lambda shape: pl.BlockSpec(shape, lambda i: (0,) * len(shape))
    if group_major:
        shape = _gm_shape((), n // FFT_N2, hw2)
        out_spec = pl.BlockSpec(_gm_shape((), tl // FFT_N2, hw2), lambda i: (0, 0, i, 0, 0))
    else:
        shape = (n, hw2)
        out_spec = pl.BlockSpec((tl, hw2), lambda i: (i, 0))
    return pl.pallas_call(
        functools.partial(_filter_kernel, tl=tl, n=n, group_major=group_major),
        out_shape=[jax.ShapeDtypeStruct(shape, F32), jax.ShapeDtypeStruct((SUBLANES, hw2), F32)],
        grid=(n // tl,),
        in_specs=[pl.BlockSpec((tl, LANES), lambda i: (i, 0)), full((LANES, hid)), full((1, hid)), full((1, hid)),
                  full((hid, hid)), full((1, hid)), full((1, hid)), full((hid, hw2)), full((1, HYENA_WIDTH))],
        out_specs=[out_spec, full((SUBLANES, hw2))],
        compiler_params=_params(("arbitrary",)),
        name="hyena_filter",
    )(feats, w1p, vec(b1), vec(f1), w2p, vec(b2), vec(f2), w3p, deltas)


def _stack_complex(m):
    return np.block([[m.real, -m.imag], [m.imag, m.real]])


def _hilo(m):
    m = jnp.asarray(m, dtype=F32)
    return _split(m)


@functools.lru_cache(maxsize=None)
def _fft_constants(n):
    m = 2 * n
    n2 = FFT_N2
    n1 = m // n2
    n1h = n1 // 2
    k1 = np.arange(n1)
    t1 = np.arange(n1h)
    f1 = np.exp(-2j * np.pi * np.outer(k1, t1) / n1)
    f3 = np.exp(2j * np.pi * np.outer(t1, k1) / n1) / m
    k2 = np.arange(n2)
    t2 = np.arange(n2)
    w2 = np.exp(-2j * np.pi * np.outer(k2, t2) / n2)
    tw = np.exp(-2j * np.pi * np.outer(k1, t2) / m)
    return dict(
        n1=n1, n1h=n1h,
        f1c=_stack_complex(f1), f1r=np.concatenate([f1.real, f1.imag], axis=0),
        f3c=_stack_complex(f3),
        w2r=w2.real.astype(np.float32), w2i=w2.imag.astype(np.float32),
        twr=tw.real.astype(np.float32), twi=tw.imag.astype(np.float32),
    )


def _stage2_tables(n):
    c = _fft_constants(n)
    w2r, w2i = jnp.asarray(c["w2r"])[None], jnp.asarray(c["w2i"])[None]
    twr, twi = jnp.asarray(c["twr"])[:, None, :], jnp.asarray(c["twi"])[:, None, :]
    return (*_split(w2r * twr - w2i * twi), *_split(w2r * twi + w2i * twr))


def _stacked_tables(grh_ref, grl_ref, gih_ref, gil_ref, q):
    def stack(gr, gi):
        return jnp.concatenate([jnp.concatenate([gr, -gi], axis=1), jnp.concatenate([gi, gr], axis=1)], axis=0)
    return stack(grh_ref[q], gih_ref[q]), stack(grl_ref[q], gil_ref[q])


def _s1_kernel(x_ref, fh_ref, fl_ref, o_ref, *, nparts, ncw, n1, n1h):
    fh = fh_ref[...]
    fl = fl_ref[...]
    for r in range(SUBLANES):
        rows = pl.ds(r, n1h, stride=SUBLANES)
        xs = jnp.concatenate(
            [jnp.concatenate([x_ref[p, cc, rows, :] for p in range(nparts)], axis=0) for cc in range(ncw)], axis=1)
        xh, xl = _split(xs)
        res = _dot3(fh, fl, xh, xl)
        for ri in range(2):
            for cc in range(ncw):
                o_ref[ri, cc, pl.ds(r, n1, stride=SUBLANES), :] = res[ri * n1:(ri + 1) * n1, cc * LANES:(cc + 1) * LANES]


def _fft_stage1(x, fmat, n1, n1h, ncw):
    nparts, ncc, groups = x.shape[:3]
    fh, fl = _hilo(fmat)
    return pl.pallas_call(
        functools.partial(_s1_kernel, nparts=nparts, ncw=ncw, n1=n1, n1h=n1h),
        out_shape=jax.ShapeDtypeStruct((2, ncc, groups, n1 * SUBLANES, LANES), F32),
        grid=(groups, ncc // ncw),
        in_specs=[
            pl.BlockSpec((nparts, ncw, None, n1h * SUBLANES, LANES), lambda j, ci: (0, ci, j, 0, 0)),
            pl.BlockSpec(fh.shape, lambda j, ci: (0, 0)),
            pl.BlockSpec(fl.shape, lambda j, ci: (0, 0)),
        ],
        out_specs=pl.BlockSpec((2, ncw, None, n1 * SUBLANES, LANES), lambda j, ci: (0, ci, j, 0, 0)),
        compiler_params=_params(("parallel", "parallel")),
        name="fft_stage1",
    )(x, fh, fl)


def _load_k1(a_ref, q):
    ncc = a_ref.shape[1]
    return jnp.concatenate([a_ref[:, cc, :, q].reshape(2 * FFT_N2, LANES) for cc in range(ncc)], axis=1)


def _filter_spectrum_kernel(a_ref, grh_ref, grl_ref, gih_ref, gil_ref, s_ref, o_ref, *, kg):
    s = jnp.sum(s_ref[...], axis=0, keepdims=True)
    inv = 1.0 / (s[:, :HYENA_WIDTH] + s[:, HYENA_WIDTH:])
    half = FFT_N2
    for q in range(kg):
        ah, al = _split(_load_k1(a_ref, q))
        gh, gl = _stacked_tables(grh_ref, grl_ref, gih_ref, gil_ref, q)
        h = _dot3(gh, gl, ah, al)
        hf = h[:, :HYENA_WIDTH]
        hb = h[:, HYENA_WIDTH:]
        o_ref[q, :half] = (hf[:half] + hb[:half]) * inv
        o_ref[q, half:] = (hf[half:] - hb[half:]) * inv


FFT_K1_GROUP = 8


def _k1_spec(kg, width):
    return pl.BlockSpec((2, width // LANES, GROUPS, kg, SUBLANES, LANES), lambda i: (0, 0, 0, i, 0, 0))


def _filter_spectrum(a, tables, sums, n1):
    kg = FFT_K1_GROUP
    a6 = a.reshape(2, 2 * HYENA_WIDTH // LANES, GROUPS, n1, SUBLANES, LANES)
    tspec = pl.BlockSpec((kg, FFT_N2, FFT_N2), lambda i: (i, 0, 0))
    return pl.pallas_call(
        functools.partial(_filter_spectrum_kernel, kg=kg),
        out_shape=jax.ShapeDtypeStruct((n1, 2 * FFT_N2, HYENA_WIDTH), F32),
        grid=(n1 // kg,),
        in_specs=[_k1_spec(kg, 2 * HYENA_WIDTH), tspec, tspec, tspec, tspec,
                  pl.BlockSpec((SUBLANES, 2 * HYENA_WIDTH), lambda i: (0, 0))],
        out_specs=pl.BlockSpec((kg, 2 * FFT_N2, HYENA_WIDTH), lambda i: (i, 0, 0)),
        compiler_params=_params(("parallel",)),
        name="filter_spectrum",
    )(a6, *tables, sums)


def _dot_t(a, b):
    return lax.dot_general(a, b, (((0,), (0,)), ((), ())), preferred_element_type=F32)


def _s2_kernel(a_ref, kf_ref, grh_ref, grl_ref, gih_ref, gil_ref, o_ref, *, kg):
    half = FFT_N2
    for q in range(kg):
        ah, al = _split(_load_k1(a_ref, q))
        gh, gl = _stacked_tables(grh_ref, grl_ref, gih_ref, gil_ref, q)
        x = _dot3(gh, gl, ah, al)
        xr, xi = x[:half], x[half:]
        kr, ki = kf_ref[q, :half], kf_ref[q, half:]
        y = jnp.concatenate([xr * kr - xi * ki, xr * ki + xi * kr], axis=0)
        yh, yl = _split(y)
        bt = _dot_t(gh, yh) + _dot_t(gh, yl) + _dot_t(gl, yh)
        for cc in range(HYENA_WIDTH // LANES):
            o_ref[:, cc, :, q] = bt[:, cc * LANES:(cc + 1) * LANES].reshape(2, GROUPS, SUBLANES, LANES)


def _fft_stage2(a, kf, tables, n1):
    kg = FFT_K1_GROUP
    a6 = a.reshape(2, HYENA_WIDTH // LANES, GROUPS, n1, SUBLANES, LANES)
    tspec = pl.BlockSpec((kg, FFT_N2, FFT_N2), lambda i: (i, 0, 0))
    dspec = _k1_spec(kg, HYENA_WIDTH)
    out = pl.pallas_call(
        functools.partial(_s2_kernel, kg=kg),
        out_shape=jax.ShapeDtypeStruct(a6.shape, F32),
        grid=(n1 // kg,),
        in_specs=[dspec, pl.BlockSpec((kg, 2 * FFT_N2, HYENA_WIDTH), lambda i: (i, 0, 0)), tspec, tspec, tspec, tspec],
        out_specs=dspec,
        compiler_params=_params(("parallel",)),
        name="fft_stage2",
    )(a6, kf, *tables)
    return out.reshape(a.shape)


def _s3_kernel(b_ref, vx_ref, x0_ref, bias_ref, fh_ref, fl_ref, o_ref, *, ncw, n1, n1h):
    fh = fh_ref[...]
    fl = fl_ref[...]
    for r in range(SUBLANES):
        krows = pl.ds(r, n1, stride=SUBLANES)
        z = jnp.concatenate(
            [jnp.concatenate([b_ref[ri, cc, krows, :] for ri in range(2)], axis=0) for cc in range(ncw)], axis=1)
        zh, zl = _split(z)
        y = _dot3(fh, fl, zh, zl)
        trows = pl.ds(r, n1h, stride=SUBLANES)
        for p in range(2):
            for cc in range(ncw):
                yy = y[p * n1h:(p + 1) * n1h, cc * LANES:(cc + 1) * LANES]
                o_ref[p, cc, trows, :] = (yy + vx_ref[p, cc, trows, :] * bias_ref[cc]) * x0_ref[p, cc, trows, :]


def _fft_stage3(bt, vx, x0, bias, fmat, n1, n1h, ncw):
    ncc, groups = bt.shape[1:3]
    fh, fl = _hilo(fmat)
    tspec = pl.BlockSpec((2, ncw, None, n1h * SUBLANES, LANES), lambda j, ci: (0, ci, j, 0, 0))
    return pl.pallas_call(
        functools.partial(_s3_kernel, ncw=ncw, n1=n1, n1h=n1h),
        out_shape=jax.ShapeDtypeStruct(vx.shape, F32),
        grid=(groups, ncc // ncw),
        in_specs=[
            pl.BlockSpec((2, ncw, None, n1 * SUBLANES, LANES), lambda j, ci: (0, ci, j, 0, 0)),
            tspec, tspec,
            pl.BlockSpec((ncw, 1, LANES), lambda j, ci: (ci, 0, 0)),
            pl.BlockSpec(fh.shape, lambda j, ci: (0, 0)),
            pl.BlockSpec(fl.shape, lambda j, ci: (0, 0)),
        ],
        out_specs=tspec,
        compiler_params=_params(("parallel", "parallel")),
        name="fft_stage3",
    )(bt, vx, x0, bias, fh, fl)


def _hyena_latent(p, hy, feats, deltas, tables, n):
    conv_w, conv_b, w1, b1, f1, w2, b2, f2, w3, bias = hy
    c = _fft_constants(n)
    n1, n1h = c["n1"], c["n1h"]
    vx, x0 = _hyena_prep(p, conv_w, conv_b, group_major=True)
    gm_shape = vx.shape
    rows = lambda a: a.reshape(*a.shape[:-3], n1h * SUBLANES, LANES)
    vx, x0 = rows(vx), rows(x0)
    taps, sums = _hyena_filter(n, feats, deltas, w1, b1, f1, w2, b2, f2, w3, group_major=True)
    a_f = _fft_stage1(rows(taps)[None], c["f1r"], n1, n1h, ncw=2)
    kf = _filter_spectrum(a_f, tables, sums, n1)
    a = _fft_stage1(vx, c["f1c"], n1, n1h, ncw=2)
    bt = _fft_stage2(a, kf, tables, n1)
    out = _fft_stage3(bt, vx, x0, bias.reshape(HYENA_WIDTH // LANES, 1, LANES), c["f3c"], n1, n1h, ncw=2)
    return out.reshape(gm_shape)


@functools.lru_cache(maxsize=None)
def _dense_dft_constants(n):
    m = 2 * n
    k = np.arange(m)
    t = np.arange(n)
    f = np.exp(-2j * np.pi * np.outer(k, t) / m)
    finv = np.exp(2j * np.pi * np.outer(t, k) / m) / m
    return _stack_complex(f), np.concatenate([f.real, f.imag], axis=0), _stack_complex(finv)


def _dense_conv_kernel(vx_ref, x0_ref, hf_ref, hb_ref, sf_ref, sb_ref, bias_ref,
                       fch, fcl, frh, frl, fih, fil, o_ref, *, n):
    m = 2 * n
    z = jnp.concatenate([vx_ref[0], vx_ref[1]], axis=0)
    zh, zl = _split(z)
    zf = _dot3(fch[...], fcl[...], zh, zl)
    hfh, hfl = _split(hf_ref[...])
    hbh, hbl = _split(hb_ref[...])
    hf = _dot3(frh[...], frl[...], hfh, hfl)
    hb = _dot3(frh[...], frl[...], hbh, hbl)
    inv = 1.0 / (jnp.sum(sf_ref[...], axis=0, keepdims=True) + jnp.sum(sb_ref[...], axis=0, keepdims=True))
    kr = (hf[:m] + hb[:m]) * inv
    ki = (hf[m:] - hb[m:]) * inv
    zr, zi = zf[:m], zf[m:]
    y = jnp.concatenate([zr * kr - zi * ki, zr * ki + zi * kr], axis=0)
    yh, yl = _split(y)
    out = _dot3(fih[...], fil[...], yh, yl)
    bias = bias_ref[...]
    for p in range(2):
        o_ref[p] = (out[p * n:(p + 1) * n] + vx_ref[p] * bias) * x0_ref[p]


def _hyena_context(p, hy, feats, deltas, n):
    conv_w, conv_b, w1, b1, f1, w2, b2, f2, w3, bias = hy
    vx, x0 = _hyena_prep(p, conv_w, conv_b, group_major=False)
    taps, sums = _hyena_filter(n, feats, deltas, w1, b1, f1, w2, b2, f2, w3, group_major=False)
    fc, fr, fi = _dense_dft_constants(n)
    mats = [*_hilo(fc), *_hilo(fr), *_hilo(fi)]
    cw = 256
    nct = HYENA_WIDTH // cw
    dspec = pl.BlockSpec((2, n, cw), lambda ci: (0, 0, ci))
    return pl.pallas_call(
        functools.partial(_dense_conv_kernel, n=n),
        out_shape=jax.ShapeDtypeStruct(vx.shape, F32),
        grid=(nct,),
        in_specs=[dspec, dspec,
                  pl.BlockSpec((n, cw), lambda ci: (0, ci)), pl.BlockSpec((n, cw), lambda ci: (0, nct + ci)),
                  pl.BlockSpec((SUBLANES, cw), lambda ci: (0, ci)), pl.BlockSpec((SUBLANES, cw), lambda ci: (0, nct + ci)),
                  pl.BlockSpec((1, cw), lambda ci: (0, ci))]
                 + [pl.BlockSpec(mt.shape, lambda ci: (0, 0)) for mt in mats],
        out_specs=dspec,
        compiler_params=_params(("parallel",)),
        name="context_long_conv",
    )(vx, x0, taps, taps, sums, sums, bias.reshape(1, HYENA_WIDTH), *mats)


def _pool_kernel(x_ref, p_ref, n_ref, w_ref, sc_ref, o_ref, *, tl, nt, n):
    i = pl.program_id(1)
    x = x_ref[...]
    pv = jnp.where(i > 0, p_ref[...], 0.0)
    nx = jnp.where(i < nt - 1, n_ref[...], 0.0)
    ext = jnp.concatenate([pv, x, nx], axis=0)
    rows = tl + 2 * POOL_HALO
    t = i * tl + lax.broadcasted_iota(jnp.int32, (tl, POOL_GROUP), 0)
    for g, w in enumerate(POOL_WINDOWS):
        lanes = slice(g * POOL_GROUP, (g + 1) * POOL_GROUP)
        a = ext[:, lanes]
        c = a + pltpu.roll(a, 1, 0)
        h = 1
        while 2 * h < w:
            c = pltpu.roll(c, h, 0) + pltpu.roll(c, rows - h, 0)
            h *= 2
        total = c[POOL_HALO:POOL_HALO + tl]
        count = (jnp.minimum(t + h, n) - jnp.maximum(t - h, 0)).astype(F32)
        y = (total / count - x[:, lanes]).astype(BF16)
        o_ref[:, lanes] = _dot(y, w_ref[g].astype(BF16)) * sc_ref[:, lanes]


def _pool_mixer(p, w_pool, scale):
    b, n, _ = p.shape
    tl = min(512, n)
    nt = n // tl
    colblk = 3 * HYENA_WIDTH // POOL_WIDTH
    assert colblk * POOL_WIDTH == 3 * HYENA_WIDTH
    per = tl // POOL_HALO
    nrow = n // POOL_HALO
    return pl.pallas_call(
        functools.partial(_pool_kernel, tl=tl, nt=nt, n=n),
        out_shape=jax.ShapeDtypeStruct((b, n, POOL_WIDTH), F32),
        grid=(b, nt),
        in_specs=[
            pl.BlockSpec((None, tl, POOL_WIDTH), lambda bi, i: (bi, i, colblk)),
            pl.BlockSpec((None, POOL_HALO, POOL_WIDTH), lambda bi, i: (bi, jnp.maximum(i * per - 1, 0), colblk)),
            pl.BlockSpec((None, POOL_HALO, POOL_WIDTH), lambda bi, i: (bi, jnp.minimum((i + 1) * per, nrow - 1), colblk)),
            pl.BlockSpec(w_pool.shape, lambda bi, i: (0, 0, 0)),
            pl.BlockSpec((1, POOL_WIDTH), lambda bi, i: (0, 0)),
        ],
        out_specs=pl.BlockSpec((None, tl, POOL_WIDTH), lambda bi, i: (bi, i, 0)),
        compiler_params=_params(("parallel", "parallel")),
        name="pool_mixer",
    )(p, p, p, w_pool, scale.reshape(1, POOL_WIDTH))


def _load_group_major(ref, tl):
    ncc = ref.shape[0]
    return jnp.concatenate(
        [jnp.concatenate([ref[cc, jg, t1] for cc in range(ncc)], axis=1)
         for t1 in range(tl // FFT_N2) for jg in range(GROUPS)], axis=0)


def _outproj_kernel(at_ref, hy_ref, po_ref, x_ref, gb_ref, w_ref, gp_ref, gt_ref, o_ref, ox_ref, m_ref,
                    *, tm, tiles, group_major):
    t = pl.program_id(0)

    @pl.when(t == 0)
    def _():
        ox_ref[...] = jnp.zeros_like(ox_ref)

    def finish_previous():
        o_ref[...] = x_ref[...] + gt_ref[...] * _rms(ox_ref[...], gp_ref[...])

    @pl.when(t < tiles)
    def _():
        finish_previous()
        hy = _load_group_major(hy_ref, tm) if group_major else hy_ref[...]
        a0, a1 = ATTN_WIDTH, ATTN_WIDTH + HYENA_WIDTH
        m_ref[:, :a0] = _rms(at_ref[...], gb_ref[:, :a0]).astype(BF16)
        m_ref[:, a0:a1] = _rms(hy, gb_ref[:, a0:a1]).astype(BF16)
        m_ref[:, a1:] = _rms(po_ref[...], gb_ref[:, a1:]).astype(BF16)
        ox_ref[...] = _dot(m_ref[...], w_ref[...])

    pl.when(t == tiles)(finish_previous)


def _out_projection(attn, hy, po, x, g_branch, w_out, layer, g_post, gate, group_major):
    b, n, d = x.shape
    tm = min(512, n)
    nt = n // tm
    tiles = b * nt

    def cur(t):
        return jnp.minimum(t, tiles - 1)

    def prev(t):
        return jnp.maximum(t - 1, 0)

    row = lambda width, sel: pl.BlockSpec((None, tm, width), lambda t: (sel(t) // nt, sel(t) % nt, 0))
    vec = lambda width: pl.BlockSpec((1, width), lambda t: (0, 0))
    if group_major:
        hy_spec = pl.BlockSpec((None, *_gm_shape((), tm // FFT_N2, HYENA_WIDTH)),
                               lambda t: (cur(t) // nt, 0, 0, cur(t) % nt, 0, 0))
    else:
        hy_spec = row(HYENA_WIDTH, cur)
    return pl.pallas_call(
        functools.partial(_outproj_kernel, tm=tm, tiles=tiles, group_major=group_major),
        out_shape=jax.ShapeDtypeStruct(x.shape, F32),
        grid=(tiles + 1,),
        in_specs=[row(ATTN_WIDTH, cur), hy_spec, row(POOL_WIDTH, cur), row(d, prev), vec(d),
                  pl.BlockSpec((None, *w_out.shape[1:]), lambda t: (layer, 0, 0), pipeline_mode=pl.Buffered(1)), vec(d),
                  pl.BlockSpec((None, 1, d), lambda t: (prev(t) // nt, 0, 0))],
        out_specs=row(d, prev),
        scratch_shapes=[pltpu.VMEM((tm, d), F32), pltpu.VMEM((tm, d), BF16)],
        compiler_params=_params(("arbitrary",)),
        name="out_projection",
    )(attn, hy, po, x, g_branch.reshape(1, d), w_out, g_post.reshape(1, d), gate)


def _mlp_kernel(xe_ref, xp_ref, g_ref, shn_ref, scn_ref, wu_ref, wd_ref, gp_ref, gtp_ref, o_ref,
                h0_ref, h1_ref, acc0_ref, acc1_ref, *, rc, tiles):
    t = pl.program_id(0)
    k = pl.program_id(1)
    rows = pl.ds(pl.multiple_of(k * rc, rc), rc)

    def normalise(h_ref):
        a = g_ref[...] * (1.0 + scn_ref[...])
        h_ref[rows, :] = (_rms(xp_ref[...], a) + shn_ref[...]).astype(BF16)

    def finish(acc_ref):
        o_ref[...] = xe_ref[...] + _rms(acc_ref[rows, :], gp_ref[...] * gtp_ref[...])

    @pl.when(t == 0)
    def _():
        normalise(h0_ref)
        acc0_ref[rows, :] = jnp.zeros((rc, acc0_ref.shape[1]), F32)
        acc1_ref[rows, :] = jnp.zeros((rc, acc1_ref.shape[1]), F32)

    def step(slot):
        h_cur, h_oth = (h0_ref, h1_ref) if slot == 0 else (h1_ref, h0_ref)
        acc_cur, acc_oth = (acc0_ref, acc1_ref) if slot == 0 else (acc1_ref, acc0_ref)
        finish(acc_oth)
        acc_oth[rows, :] = jnp.zeros((rc, acc_oth.shape[1]), F32)
        normalise(h_oth)
        u = jnp.maximum(_dot(h_cur[...], wu_ref[...]), 0.0)
        acc_cur[...] += _dot((u * u).astype(BF16), wd_ref[...])

    work = (t >= 1) & (t <= tiles)
    pl.when(work & (t % 2 == 1))(functools.partial(step, 0))
    pl.when(work & (t % 2 == 0))(functools.partial(step, 1))
    pl.when(t == tiles + 1)(functools.partial(finish, acc0_ref if (tiles - 1) % 2 == 0 else acc1_ref))


MLP_TH = 1024


def _mlp(x, g_pre, shift, scale, w_up, w_down, layer, g_post, gate):
    b, n, d = x.shape
    th = MLP_TH
    nk = w_up.shape[2] // th
    tm = min(1024, n)
    nt = n // tm
    tiles = b * nt
    rc = tm // nk
    assert rc * nk == tm and rc % BF16_ROWS == 0

    def norm_tile(t):
        return jnp.minimum(t, tiles - 1)

    def done_tile(t):
        return jnp.clip(t - 2, 0, tiles - 1)

    vec = pl.BlockSpec((1, d), lambda t, k: (0, 0))
    bvec = lambda sel: pl.BlockSpec((None, 1, d), lambda t, k: (sel(t) // nt, 0, 0))
    chunk = lambda sel, first: pl.BlockSpec(
        (None, rc, d), lambda t, k: (sel(t) // nt, (sel(t) % nt) * nk + jnp.where(t < first, 0, k), 0))
    busy = lambda t: (t >= 1) & (t <= tiles)
    return pl.pallas_call(
        functools.partial(_mlp_kernel, rc=rc, tiles=tiles),
        out_shape=jax.ShapeDtypeStruct(x.shape, F32),
        grid=(tiles + 2, nk),
        in_specs=[chunk(done_tile, 2), chunk(norm_tile, 0), vec, bvec(norm_tile), bvec(norm_tile),
                  pl.BlockSpec((None, d, th), lambda t, k: (layer, 0, jnp.where(busy(t), k, 0))),
                  pl.BlockSpec((None, th, d), lambda t, k: (layer, jnp.where(busy(t), k, 0), 0)),
                  vec, bvec(done_tile)],
        out_specs=chunk(done_tile, 2),
        scratch_shapes=[pltpu.VMEM((tm, d), BF16)] * 2 + [pltpu.VMEM((tm, d), F32)] * 2,
        compiler_params=_params(("arbitrary", "arbitrary")),
        name="mlp",
    )(x, x, g_pre.reshape(1, d), shift, scale, w_up, w_down, g_post.reshape(1, d), gate)


def kernel(x, c, ctx, c_ctx, w_mod, b_mod, g_pre_mix, g_post_mix, g_pre_mlp, g_post_mlp, w_in, w_out, g_branch,
           attn_sink, hy_conv_w, hy_conv_b, hy_w1, hy_b1, hy_freq1, hy_w2, hy_b2, hy_freq2, hy_w3, hy_bias,
           pool_w, pool_scale, w_up, w_down):
    b, n, d = x.shape
    n_ctx = ctx.shape[1]
    depth = w_mod.shape[0]
    assert b == 2 and d == D_MODEL and n % 512 == 0 and n_ctx % BLOCK == 0

    cond = jnp.concatenate([c, c_ctx[None], jnp.zeros((SUBLANES - b - 1, d), F32)], axis=0)
    mods = _modulation(cond, w_mod, b_mod)

    w_in_b = _prepare_w_in(w_in, IN_TN)
    w_up_b = w_up.astype(BF16)
    w_out_b, w_down_b = w_out.astype(BF16), w_down.astype(BF16)
    tables_x, tables_c = _rope_tables(n), _rope_tables(n_ctx)
    feats_x, feats_c = _filter_features(n), _filter_features(n_ctx)
    deltas = _filter_deltas()
    tables = _stage2_tables(n)

    for i in range(depth):
        last = i == depth - 1
        hy = (hy_conv_w[i], hy_conv_b[i], hy_w1[i], hy_b1[i], hy_freq1[i], hy_w2[i], hy_b2[i], hy_freq2[i],
              hy_w3[i], hy_bias[i])
        mx = [m[:, None, :] for m in jnp.split(mods[i, :b], N_MOD, axis=-1)]
        mc = [jnp.broadcast_to(m[None, None, :], (b, 1, d)) for m in jnp.split(mods[i, b], N_MOD, axis=-1)]

        qkv_x, px = _in_projection(x, g_pre_mix[i], mx[0], mx[1], w_in_b, i, tables_x, rope=True)
        qkv_c, pc = _in_projection(ctx, g_pre_mix[i], mc[0], mc[1], w_in_b, i, tables_c, rope=False)
        kv_ctx = qkv_c[..., Q_END:V_END]

        attn_x = _attention(qkv_x, kv_ctx, attn_sink[i], local=True)
        hy_x = _hyena_latent(px, hy, feats_x, deltas, tables, n)
        po_x = _pool_mixer(px, pool_w[i], pool_scale[i])
        x = _out_projection(attn_x, hy_x, po_x, x, g_branch[i], w_out_b, i, g_post_mix[i], mx[2], group_major=True)
        x = _mlp(x, g_pre_mlp[i], mx[3], mx[4], w_up_b, w_down_b, i, g_post_mlp[i], mx[5])

        if not last:
            attn_c = _attention(qkv_c, kv_ctx, attn_sink[i], local=False)
            hy_c = _hyena_context(pc, hy, feats_c, deltas, n_ctx)
            po_c = _pool_mixer(pc, pool_w[i], pool_scale[i])
            ctx = _out_projection(attn_c, hy_c, po_c, ctx, g_branch[i], w_out_b, i, g_post_mix[i], mc[2],
                                  group_major=False)
            ctx = _mlp(ctx, g_pre_mlp[i], mc[3], mc[4], w_up_b, w_down_b, i, g_post_mlp[i], mc[5])
    return x
```

```python
import functools
import math

import numpy as np
import jax
import jax.numpy as jnp
from jax import lax
from jax.experimental import pallas as pl
from jax.experimental.pallas import tpu as pltpu

F32 = jnp.float32
BF16 = jnp.bfloat16

D_MODEL = 2048
DEPTH = 4
GRID_W = 64
ATTN_WIDTH = D_MODEL // 2
HYENA_WIDTH = D_MODEL // 4
POOL_WIDTH = D_MODEL - ATTN_WIDTH - HYENA_WIDTH
HEAD_DIM = 128
N_HEADS = ATTN_WIDTH // HEAD_DIM
N_KV_HEADS = 2
KV_GROUP = N_HEADS // N_KV_HEADS
KV_WIDTH = N_KV_HEADS * HEAD_DIM
WINDOW = 128
BLOCK = 128
ROPE_BASE = 10000.0
HYENA_EMB_DIM = 33
HYENA_FILTER_HIDDEN = 64
HYENA_FAST_DECAY_PCT = 0.3
HYENA_SLOW_DECAY_PCT = 1.5
HYENA_DECAY_TARGET = 1e-2
POOL_WINDOWS = (2, 4, 8, 16)
POOL_GROUP = POOL_WIDTH // len(POOL_WINDOWS)
MLP_HIDDEN = 4 * D_MODEL
N_MOD = 6
EPS = 1e-6
NEG_INF = -1e30

Q_END = ATTN_WIDTH
K_END = Q_END + KV_WIDTH
V_END = K_END + KV_WIDTH
HY_END = V_END + 3 * HYENA_WIDTH
IN_WIDTH = HY_END + POOL_WIDTH

LANES = 128
SUBLANES = 8
FFT_N2 = 128
POOL_HALO = 16
VMEM_LIMIT = 56 * 1024 * 1024


def _params(sem, vmem=VMEM_LIMIT):
    return pltpu.CompilerParams(dimension_semantics=sem, vmem_limit_bytes=vmem)


def _split(x):
    hi = x.astype(BF16)
    lo = (x - hi.astype(F32)).astype(BF16)
    return hi, lo


def _dot(a, b):
    return jnp.dot(a, b, preferred_element_type=F32)


def _dot3(ah, al, bh, bl):
    return _dot(ah, bh) + _dot(ah, bl) + _dot(al, bh)


def _rms(x, g):
    return x * lax.rsqrt(jnp.mean(x * x, axis=-1, keepdims=True) + EPS) * g


BF16_ROWS = 2 * SUBLANES


def _norm_scale_rows(x_ref, h_ref, a_ref, s_ref, rows, cols=None):
    cols = slice(None) if cols is None else cols
    nchunks = rows // BF16_ROWS

    def chunk(c):
        return pl.ds(pl.multiple_of(c * BF16_ROWS, BF16_ROWS), BF16_ROWS)

    def inv_rms(c):
        x = x_ref[chunk(c), :]
        return lax.rsqrt(jnp.mean(x * x, axis=-1, keepdims=True) + EPS)

    def body(c, inv):
        inv_next = inv_rms(jnp.minimum(c + 1, nchunks - 1))
        y = x_ref[chunk(c), :] * inv * a_ref[...]
        if s_ref is not None:
            y = y + s_ref[...]
        h_ref[chunk(c), cols] = y.astype(BF16)
        return inv_next

    lax.fori_loop(0, nchunks, body, inv_rms(0), unroll=8)


def _residual_norm_rows(x_ref, y_ref, o_ref, pg_ref, rows):
    nchunks = rows // SUBLANES

    def chunk(c):
        return pl.ds(pl.multiple_of(c * SUBLANES, SUBLANES), SUBLANES)

    def inv_rms(c):
        y = y_ref[chunk(c), :]
        return lax.rsqrt(jnp.mean(y * y, axis=-1, keepdims=True) + EPS)

    def body(c, inv):
        inv_next = inv_rms(jnp.minimum(c + 1, nchunks - 1))
        o_ref[chunk(c), :] = x_ref[chunk(c), :] + y_ref[chunk(c), :] * inv * pg_ref[...]
        return inv_next

    lax.fori_loop(0, nchunks, body, inv_rms(0), unroll=16)


def _mod_kernel(c_ref, w_ref, b_ref, o_ref):
    c = c_ref[...]
    s = c / (1.0 + jnp.exp(-c))
    sh, sl = _split(s)
    wh, wl = _split(w_ref[...])
    o_ref[...] = _dot3(sh, sl, wh, wl) + b_ref[...]


def _modulation(cond, w_mod, b_mod):
    depth, d, width = w_mod.shape
    tn = 1024
    return pl.pallas_call(
        _mod_kernel,
        out_shape=jax.ShapeDtypeStruct((depth, SUBLANES, width), F32),
        grid=(depth, width // tn),
        in_specs=[
            pl.BlockSpec((SUBLANES, d), lambda l, j: (0, 0)),
            pl.BlockSpec((None, d, tn), lambda l, j: (l, 0, j)),
            pl.BlockSpec((None, 1, tn), lambda l, j: (l, 0, j)),
        ],
        out_specs=pl.BlockSpec((None, SUBLANES, tn), lambda l, j: (l, 0, j)),
        compiler_params=_params(("parallel", "parallel")),
        name="modulation",
    )(cond, w_mod, b_mod.reshape(depth, 1, width))


QKV_WIDTH = V_END
REST_WIDTH = IN_WIDTH - V_END
SM_SCALE = HEAD_DIM ** -0.5


TABLE_Q, TABLE_K, TABLE_ID, TABLE_SCALE = range(4)


def _inproj_kernel(xn_ref, g_ref, shn_ref, scn_ref, w_ref, ca_ref, sa_ref, cb_ref, sb_ref,
                   qkv_ref, rest_ref, h0_ref, h1_ref, acc_ref, *, tm, tn, rc, tiles):
    t = pl.program_id(0)
    j = pl.program_id(1)

    def normalise(h_ref):
        rows = pl.ds(pl.multiple_of(jnp.minimum(j, tm // rc - 1) * rc, rc), rc)
        a = g_ref[...] * (1.0 + scn_ref[...])
        h_ref[rows, :] = (_rms(xn_ref[...], a) + shn_ref[...]).astype(BF16)

    def finish_previous():
        acc = acc_ref[...]
        rest_ref[...] = acc
        nch = tn // LANES
        for ch in range(nch):
            a = acc[:, ch * LANES:(ch + 1) * LANES]
            cos, sin = (ca_ref, sa_ref) if ch < nch // 2 else (cb_ref, sb_ref)
            qkv_ref[:, ch * LANES:(ch + 1) * LANES] = (a * cos[...] + pltpu.roll(a, HEAD_DIM // 2, 1) * sin[...]).astype(BF16)

    @pl.when(t == 0)
    def _():
        normalise(h0_ref)

    @pl.when((t == 0) & (j == 0))
    def _():
        acc_ref[...] = jnp.zeros_like(acc_ref)

    def step(slot):
        h_cur, h_oth = (h0_ref, h1_ref) if slot == 0 else (h1_ref, h0_ref)
        finish_previous()
        normalise(h_oth)
        acc_ref[...] = _dot(h_cur[...], w_ref[j])

    work = (t >= 1) & (t <= tiles)
    pl.when(work & (t % 2 == 1))(functools.partial(step, 0))
    pl.when(work & (t % 2 == 0))(functools.partial(step, 1))
    pl.when((t == tiles + 1) & (j == 0))(finish_previous)


IN_TN = 512
QKV_SPARE = QKV_WIDTH // IN_TN
REST_SPARE = REST_WIDTH // IN_TN


def _in_projection(x, g, shift, scale, w, layer, tables, rope):
    b, n, d = x.shape
    tm = min(1024, n)
    tn = w.shape[-1]
    nj = w.shape[1]
    nt = n // tm
    tiles = b * nt
    half = tn // 2
    assert tn == IN_TN and nj * tn == IN_WIDTH and Q_END % tn == 0 and K_END % tn == half and V_END % tn == 0
    nslices = 4
    rc = tm // nslices
    assert nslices <= nj and rc % BF16_ROWS == 0
    nq, nqkv = Q_END // tn, QKV_WIDTH // tn

    def norm_tile(t):
        return jnp.minimum(t, tiles - 1)

    def lagged(t, j):
        step = jnp.clip((t - 1) * nj + j - 1, 0, tiles * nj - 1)
        return step // nj, step % nj

    def kind_a(j):
        return jnp.where(j < nq, TABLE_Q if rope else TABLE_SCALE, jnp.where(j < nqkv, TABLE_K if rope else TABLE_ID, TABLE_ID))

    def kind_b(j):
        return jnp.where(j < nq, TABLE_Q if rope else TABLE_SCALE, TABLE_ID)

    vec = pl.BlockSpec((1, d), lambda t, j: (0, 0))
    bvec = lambda sel: pl.BlockSpec((None, 1, d), lambda t, j: (sel(t) // nt, 0, 0))
    def table(kind):
        def index(t, j):
            tp, jp = lagged(t, j)
            k = kind(jp)
            return k, jnp.where(k == TABLE_ID, 0, tp % nt), 0
        return pl.BlockSpec((None, tm, LANES), index)

    def out_spec(col_tile):
        return pl.BlockSpec((None, tm, tn), lambda t, j: (lagged(t, j)[0] // nt, lagged(t, j)[0] % nt,
                                                          col_tile(lagged(t, j)[1])))

    cos, sin = tables
    return pl.pallas_call(
        functools.partial(_inproj_kernel, tm=tm, tn=tn, rc=rc, tiles=tiles),
        out_shape=[jax.ShapeDtypeStruct((b, n, QKV_WIDTH + tn), BF16), jax.ShapeDtypeStruct((b, n, REST_WIDTH + tn), F32)],
        grid=(tiles + 2, nj),
        in_specs=[pl.BlockSpec((None, rc, d), lambda t, j: (norm_tile(t) // nt, (norm_tile(t) % nt) * nslices
                                                            + jnp.minimum(j, nslices - 1), 0)),
                  vec, bvec(norm_tile), bvec(norm_tile),
                  pl.BlockSpec((None, *w.shape[1:]), lambda t, j: (layer, 0, 0, 0), pipeline_mode=pl.Buffered(1)),
                  table(kind_a), table(kind_a), table(kind_b), table(kind_b)],
        out_specs=[out_spec(lambda jp: jnp.minimum(jp, QKV_SPARE)),
                   out_spec(lambda jp: jnp.where(jp >= nqkv, jp - nqkv, REST_SPARE))],
        scratch_shapes=[pltpu.VMEM((tm, d), BF16)] * 2 + [pltpu.VMEM((tm, tn), F32)],
        compiler_params=_params(("arbitrary", "arbitrary")),
        name="in_projection_rope" if rope else "in_projection",
    )(x, g.reshape(1, d), shift, scale, w, cos, sin, cos, sin)


def _prep_w_in_kernel(w_ref, o_ref, *, tn):
    j = pl.program_id(1)
    quarter = HEAD_DIM // 4
    nch = tn // LANES

    def write(n_swapped):
        for ch in range(nch):
            a = w_ref[:, ch * LANES:(ch + 1) * LANES]
            if ch < n_swapped:
                q = lax.broadcasted_iota(jnp.int32, a.shape, 1) // quarter
                a = jnp.where(q == 1, pltpu.roll(a, LANES - quarter, 1), jnp.where(q == 2, pltpu.roll(a, quarter, 1), a))
            o_ref[:, ch * LANES:(ch + 1) * LANES] = a.astype(BF16)

    nfull, nrem = divmod(K_END // LANES, nch)
    pl.when(j < nfull)(functools.partial(write, nch))
    pl.when(j == nfull)(functools.partial(write, nrem))
    pl.when(j > nfull)(functools.partial(write, 0))


def _prepare_w_in(w_in, tn):
    depth, d, width = w_in.shape
    return pl.pallas_call(
        functools.partial(_prep_w_in_kernel, tn=tn),
        out_shape=jax.ShapeDtypeStruct((depth, width // tn, d, tn), BF16),
        grid=(depth, width // tn),
        in_specs=[pl.BlockSpec((None, d, tn), lambda l, j: (l, 0, j))],
        out_specs=pl.BlockSpec((None, None, d, tn), lambda l, j: (l, j, 0, 0)),
        compiler_params=_params(("parallel", "parallel")),
        name="prepare_w_in",
    )(w_in)


def _rope_tables(n):
    quarter = HEAD_DIM // 4
    inv_freq = ROPE_BASE ** (-jnp.arange(quarter, dtype=F32) / quarter)
    t = jnp.arange(n, dtype=jnp.int32)
    row = (t // GRID_W).astype(F32)[:, None] * inv_freq[None, :]
    col = (t % GRID_W).astype(F32)[:, None] * inv_freq[None, :]
    cos = jnp.concatenate([jnp.cos(row), jnp.cos(col), jnp.cos(row), jnp.cos(col)], axis=-1)
    sin = jnp.concatenate([-jnp.sin(row), -jnp.sin(col), jnp.sin(row), jnp.sin(col)], axis=-1)
    one, zero = jnp.ones_like(cos), jnp.zeros_like(sin)
    return (jnp.stack([cos * SM_SCALE, cos, one, one * SM_SCALE]), jnp.stack([sin * SM_SCALE, sin, zero, zero]))


def _softmax_pv(parts, sink_col):
    m = sink_col
    for s, _ in parts:
        m = jnp.maximum(m, jnp.max(s, axis=-1, keepdims=True))
    den = jnp.exp(sink_col - m)
    out = None
    for s, v in parts:
        p = jnp.exp(s - m)
        den = den + jnp.sum(p, axis=-1, keepdims=True)
        pv = _dot(p.astype(BF16), v)
        out = pv if out is None else out + pv
    return out / den


def _ctx_attn_kernel(sink_ref, q_ref, kvc_ref, o_ref, *, tq):
    for g in range(N_KV_HEADS):
        heads = [g * KV_GROUP + h for h in range(KV_GROUP)]
        qg = jnp.concatenate([q_ref[:, h * HEAD_DIM:(h + 1) * HEAD_DIM] for h in heads], axis=0)
        sink_col = jnp.concatenate([jnp.full((tq, 1), sink_ref[h], F32) for h in heads], axis=0)
        kc = kvc_ref[:, g * HEAD_DIM:(g + 1) * HEAD_DIM]
        vc = kvc_ref[:, KV_WIDTH + g * HEAD_DIM:KV_WIDTH + (g + 1) * HEAD_DIM]
        s = lax.dot_general(qg, kc, (((1,), (1,)), ((), ())), preferred_element_type=F32)
        o = _softmax_pv([(s, vc)], sink_col)
        for hi, h in enumerate(heads):
            o_ref[:, h * HEAD_DIM:(h + 1) * HEAD_DIM] = o[hi * tq:(hi + 1) * tq]


ATTN_ROWS = KV_GROUP * BLOCK
SOFTMAX_CHUNK = 32


def _win_attn_kernel(sink_ref, q_ref, km_ref, kp_ref, kn_ref, vm_ref, vp_ref, vn_ref, kvc_ref, band_ref,
                     o_ref, ktw, vw, s_scr, p_scr, m_scr, *, tq, nb, nctx):
    i = pl.program_id(1)
    nsub = tq // BLOCK
    nloc = 3 * BLOCK

    def transposed(x):
        return x.astype(F32).T.astype(BF16)

    def block_rows(main_ref, prev_ref, next_ref, w, lanes):
        if w == 0:
            return prev_ref[:, lanes]
        if w == nsub + 1:
            return next_ref[:, lanes]
        return main_ref[(w - 1) * BLOCK:w * BLOCK, lanes]

    ones = jnp.ones((nloc + nctx, HEAD_DIM), BF16)
    for g in range(N_KV_HEADS):
        lanes = slice(g * HEAD_DIM, (g + 1) * HEAD_DIM)
        vlanes = slice(KV_WIDTH + g * HEAD_DIM, KV_WIDTH + (g + 1) * HEAD_DIM)
        kts = [transposed(block_rows(km_ref, kp_ref, kn_ref, w, lanes)) for w in range(nsub + 2)]
        kct = [transposed(kvc_ref[cb * BLOCK:(cb + 1) * BLOCK, lanes]) for cb in range(nctx // BLOCK)]
        for jb in range(nsub):
            for w in range(3):
                ktw[jb, g, :, w * BLOCK:(w + 1) * BLOCK] = kts[jb + w]
                vw[jb, g, w * BLOCK:(w + 1) * BLOCK, :HEAD_DIM] = block_rows(vm_ref, vp_ref, vn_ref, jb + w, lanes)
            for cb in range(nctx // BLOCK):
                ktw[jb, g, :, nloc + cb * BLOCK:nloc + (cb + 1) * BLOCK] = kct[cb]
            vw[jb, g, nloc:, :HEAD_DIM] = kvc_ref[:, vlanes]
            vw[jb, g, :, HEAD_DIM:] = ones

    col = lax.broadcasted_iota(jnp.int32, (1, nloc), 1)

    def rows_of(jb):
        start = jb * BLOCK
        return pl.ds(start if isinstance(start, int) else pl.multiple_of(start, BLOCK), BLOCK)

    def stage_a(jb, g):
        qg = jnp.concatenate([q_ref[rows_of(jb), (g * KV_GROUP + h) * HEAD_DIM:(g * KV_GROUP + h + 1) * HEAD_DIM]
                              for h in range(KV_GROUP)], axis=0)
        s_scr[g] = _dot(qg, ktw[jb, g])

    def stage_b(jb, g):
        blk = i * nsub + jb
        pen_prev = jnp.where(blk == 0, NEG_INF, 0.0).astype(F32)
        pen_next = jnp.where(blk == nb - 1, NEG_INF, 0.0).astype(F32)
        rowbias = jnp.where(col < BLOCK, pen_prev, jnp.where(col >= 2 * BLOCK, pen_next, 0.0))
        for c in range(ATTN_ROWS // SOFTMAX_CHUNK):
            rows = slice(c * SOFTMAX_CHUNK, (c + 1) * SOFTMAX_CHUNK)
            sink = sink_ref[g * KV_GROUP + (c * SOFTMAX_CHUNK) // BLOCK]
            s_loc = s_scr[g, rows, :nloc] + band_ref[rows, :] + rowbias
            s_ctx = s_scr[g, rows, nloc:]
            m = jnp.maximum(jnp.max(s_loc, axis=-1, keepdims=True), jnp.max(s_ctx, axis=-1, keepdims=True))
            m = jnp.maximum(m, sink)
            p_scr[g, rows, :nloc] = jnp.exp(s_loc - m).astype(BF16)
            p_scr[g, rows, nloc:] = jnp.exp(s_ctx - m).astype(BF16)
            m_scr[g, rows, :] = m

    def stage_c(jb, g):
        o = _dot(p_scr[g], vw[jb, g])
        for hi in range(KV_GROUP):
            h = g * KV_GROUP + hi
            rows = slice(hi * BLOCK, (hi + 1) * BLOCK)
            den = o[rows, HEAD_DIM:HEAD_DIM + 1] + jnp.exp(sink_ref[h] - m_scr[g, rows, :])
            o_ref[rows_of(jb), h * HEAD_DIM:(h + 1) * HEAD_DIM] = o[rows, :HEAD_DIM] / den

    stage_a(0, 0)
    stage_a(0, 1)
    stage_b(0, 0)

    def body(j, carry):
        stage_a(j, 0)
        stage_c(j - 1, 0)
        stage_b(j - 1, 1)
        stage_a(j, 1)
        stage_c(j - 1, 1)
        stage_b(j, 0)
        return carry

    lax.fori_loop(1, nsub, body, 0)
    stage_c(nsub - 1, 0)
    stage_b(nsub - 1, 1)
    stage_c(nsub - 1, 1)


def _band_bias():
    qi = np.arange(ATTN_ROWS)[:, None] % BLOCK
    sj = np.arange(3 * BLOCK)[None, :]
    return jnp.asarray(np.where(np.abs(sj - BLOCK - qi) <= WINDOW, 0.0, NEG_INF), dtype=F32)


def _attention(qkv, kv_ctx, sink, local):
    b, n, _ = qkv.shape
    c = kv_ctx.shape[1]
    tq = min(1024 if local else 512, n)
    nsub = tq // BLOCK
    nb = n // BLOCK
    kcol = Q_END // KV_WIDTH
    vcol = K_END // KV_WIDTH
    smem = pl.BlockSpec(memory_space=pltpu.SMEM)
    q_spec = pl.BlockSpec((None, tq, ATTN_WIDTH), lambda bi, i: (bi, i, 0))
    kvc_spec = pl.BlockSpec((None, c, 2 * KV_WIDTH), lambda bi, i: (bi, 0, 0))
    if local:
        def main(colblk):
            return pl.BlockSpec((None, tq, KV_WIDTH), lambda bi, i: (bi, i, colblk))

        def prev(colblk):
            return pl.BlockSpec((None, BLOCK, KV_WIDTH), lambda bi, i: (bi, jnp.maximum(i * nsub - 1, 0), colblk))

        def nxt(colblk):
            return pl.BlockSpec((None, BLOCK, KV_WIDTH), lambda bi, i: (bi, jnp.minimum((i + 1) * nsub, nb - 1), colblk))

        keys = 3 * BLOCK + c
        kern = functools.partial(_win_attn_kernel, tq=tq, nb=nb, nctx=c)
        in_specs = [smem, q_spec, main(kcol), prev(kcol), nxt(kcol), main(vcol), prev(vcol), nxt(vcol), kvc_spec,
                    pl.BlockSpec((ATTN_ROWS, 3 * BLOCK), lambda bi, i: (0, 0))]
        args = [sink, qkv, qkv, qkv, qkv, qkv, qkv, qkv, kv_ctx, _band_bias()]
        scratch = [pltpu.VMEM((nsub, N_KV_HEADS, HEAD_DIM, keys), BF16),
                   pltpu.VMEM((nsub, N_KV_HEADS, keys, 2 * HEAD_DIM), BF16),
                   pltpu.VMEM((2, ATTN_ROWS, keys), F32),
                   pltpu.VMEM((2, ATTN_ROWS, keys), BF16),
                   pltpu.VMEM((2, ATTN_ROWS, 1), F32)]
    else:
        kern = functools.partial(_ctx_attn_kernel, tq=tq)
        in_specs = [smem, q_spec, kvc_spec]
        args = [sink, qkv, kv_ctx]
        scratch = []
    return pl.pallas_call(
        kern,
        out_shape=jax.ShapeDtypeStruct((b, n, ATTN_WIDTH), F32),
        grid=(b, n // tq),
        in_specs=in_specs,
        out_specs=pl.BlockSpec((None, tq, ATTN_WIDTH), lambda bi, i: (bi, i, 0)),
        scratch_shapes=scratch,
        compiler_params=_params(("parallel", "parallel")),
        name="window_attention" if local else "context_attention",
    )(*args)


GROUPS = FFT_N2 // SUBLANES


def _gm_shape(lead, t1, width):
    return (*lead, width // LANES, GROUPS, t1, SUBLANES, LANES)


def _store_group_major(o_ref, val, tl):
    for t1 in range(tl // FFT_N2):
        for jg in range(GROUPS):
            r0 = (t1 * GROUPS + jg) * SUBLANES
            for cc in range(val.shape[1] // LANES):
                o_ref[cc, jg, t1] = val[r0:r0 + SUBLANES, cc * LANES:(cc + 1) * LANES]


def _hyena_prep_kernel(u_ref, p_ref, n_ref, w_ref, b_ref, vx_ref, x0_ref, *, tl, nt, group_major):
    i = pl.program_id(1)
    u = u_ref[...]
    prev_row = jnp.where(i > 0, p_ref[SUBLANES - 1:SUBLANES, :], 0.0)
    next_row = jnp.where(i < nt - 1, n_ref[0:1, :], 0.0)
    row = lax.broadcasted_iota(jnp.int32, u.shape, 0)
    um = jnp.where(row == 0, prev_row, pltpu.roll(u, 1, 0))
    up = jnp.where(row == tl - 1, next_row, pltpu.roll(u, tl - 1, 0))
    z = um * w_ref[0:1, :] + u * w_ref[1:2, :] + up * w_ref[2:3, :] + b_ref[...]
    x0 = z[:, :HYENA_WIDTH]
    vx = z[:, 2 * HYENA_WIDTH:] * z[:, HYENA_WIDTH:2 * HYENA_WIDTH]
    if group_major:
        _store_group_major(vx_ref, vx, tl)
        _store_group_major(x0_ref, x0, tl)
    else:
        vx_ref[...] = vx
        x0_ref[...] = x0


def _hyena_prep(p, conv_w, conv_b, group_major):
    b, n, _ = p.shape
    tl = min(512, n)
    nt = n // tl
    hw = 3 * HYENA_WIDTH
    colblk = 0
    nrow8 = n // SUBLANES
    per = tl // SUBLANES
    if group_major:
        shape = _gm_shape((b,), n // FFT_N2, HYENA_WIDTH)
        out_spec = pl.BlockSpec((None, *_gm_shape((), tl // FFT_N2, HYENA_WIDTH)), lambda bi, i: (bi, 0, 0, i, 0, 0))
    else:
        shape = (b, n, HYENA_WIDTH)
        out_spec = pl.BlockSpec((None, tl, HYENA_WIDTH), lambda bi, i: (bi, i, 0))
    return pl.pallas_call(
        functools.partial(_hyena_prep_kernel, tl=tl, nt=nt, group_major=group_major),
        out_shape=[jax.ShapeDtypeStruct(shape, F32)] * 2,
        grid=(b, nt),
        in_specs=[
            pl.BlockSpec((None, tl, hw), lambda bi, i: (bi, i, colblk)),
            pl.BlockSpec((None, SUBLANES, hw), lambda bi, i: (bi, jnp.maximum(i * per - 1, 0), colblk)),
            pl.BlockSpec((None, SUBLANES, hw), lambda bi, i: (bi, jnp.minimum((i + 1) * per, nrow8 - 1), colblk)),
            pl.BlockSpec((3, hw), lambda bi, i: (0, 0)),
            pl.BlockSpec((1, hw), lambda bi, i: (0, 0)),
        ],
        out_specs=[out_spec, out_spec],
        compiler_params=_params(("parallel", "parallel")),
        name="hyena_prep",
    )(p, p, p, conv_w, conv_b.reshape(1, hw))


def _filter_kernel(ft_ref, w1_ref, b1_ref, f1_ref, w2_ref, b2_ref, f2_ref, w3_ref, dl_ref, h_ref, s_ref,
                   *, tl, n, group_major):
    i = pl.program_id(0)

    def dense(a, w_ref):
        ah, al = _split(a)
        wh, wl = _split(w_ref[...])
        return _dot3(ah, al, wh, wl)

    h = jnp.sin(f1_ref[...] * (dense(ft_ref[...], w1_ref) + b1_ref[...]))
    h = jnp.sin(f2_ref[...] * (dense(h, w2_ref) + b2_ref[...]))
    h = dense(h, w3_ref)
    hw2 = 2 * HYENA_WIDTH
    h = jnp.concatenate([h[:, :hw2], h[:, hw2:]], axis=0)
    t = (i * tl + lax.broadcasted_iota(jnp.int32, (tl, HYENA_WIDTH), 0)).astype(F32) / float(n - 1)
    decay = jnp.exp(-t * dl_ref[...])
    h = h * jnp.concatenate([decay, decay], axis=1)
    if group_major:
        _store_group_major(h_ref, h, tl)
    else:
        h_ref[...] = h

    @pl.when(i == 0)
    def _():
        s_ref[...] = jnp.zeros_like(s_ref)

    s_ref[...] += jnp.sum(jnp.abs(h).reshape(tl // SUBLANES, SUBLANES, 2 * HYENA_WIDTH), axis=0)


def _filter_features(n):
    t = jnp.linspace(0.0, 1.0, n, dtype=F32)[:, None]
    bands = (HYENA_EMB_DIM - 1) // 2
    omega = 2.0 * math.pi * jnp.arange(n, dtype=F32)[:, None] / n
    f = jnp.linspace(1e-4, bands - 1, bands, dtype=F32)[None, :]
    feats = jnp.concatenate([t, jnp.cos(f * omega), -jnp.sin(f * omega)], axis=-1)
    return jnp.pad(feats, ((0, 0), (0, HYENA_FILTER_HIDDEN - HYENA_EMB_DIM)))


def _filter_deltas():
    max_decay = math.log(HYENA_DECAY_TARGET) / HYENA_FAST_DECAY_PCT
    min_decay = math.log(HYENA_DECAY_TARGET) / HYENA_SLOW_DECAY_PCT
    return jnp.abs(jnp.linspace(min_decay, max_decay, HYENA_WIDTH, dtype=F32)).reshape(1, HYENA_WIDTH)


def _hyena_filter(n, feats, deltas, w1, b1, f1, w2, b2, f2, w3, group_major):
    tl = min(512, n)
    hh = HYENA_FILTER_HIDDEN
    hid = 2 * hh
    assert hid == LANES and HYENA_EMB_DIM <= hh
    hw2 = 2 * HYENA_WIDTH
    blockdiag = lambda w: jnp.concatenate(
        [jnp.concatenate([w, jnp.zeros_like(w)], axis=1), jnp.concatenate([jnp.zeros_like(w), w], axis=1)], axis=0)
    w1p = blockdiag(jnp.pad(w1, ((0, hh - HYENA_EMB_DIM), (0, 0))))
    w2p = blockdiag(w2)
    w3p = blockdiag(w3)
    vec = lambda v: jnp.tile(v, 2).reshape(1, hid)
    feats = feats.reshape(n // tl, 2, tl // 2, hh).transpose(0, 2, 1, 3).reshape(n // 2, hid)
    full = lambda shape: pl.BlockSpec(shape, lambda i: (0,) * len(shape))
    if group_major:
        shape = _gm_shape((), n // FFT_N2, hw2)
        out_spec = pl.BlockSpec(_gm_shape((), tl // FFT_N2, hw2), lambda i: (0, 0, i, 0, 0))
    else:
        shape = (n, hw2)
        out_spec = pl.BlockSpec((tl, hw2), lambda i: (i, 0))
    return pl.pallas_call(
        functools.partial(_filter_kernel, tl=tl, n=n, group_major=group_major),
        out_shape=[jax.ShapeDtypeStruct(shape, F32), jax.ShapeDtypeStruct((SUBLANES, hw2), F32)],
        grid=(n // tl,),
        in_specs=[pl.BlockSpec((tl // 2, hid), lambda i: (i, 0)), full((hid, hid)), full((1, hid)), full((1, hid)),
                  full((hid, hid)), full((1, hid)), full((1, hid)), full((hid, 2 * hw2)), full((1, HYENA_WIDTH))],
        out_specs=[out_spec, full((SUBLANES, hw2))],
        compiler_params=_params(("arbitrary",)),
        name="hyena_filter",
    )(feats, w1p, vec(b1), vec(f1), w2p, vec(b2), vec(f2), w3p, deltas)


def _stack_complex(m):
    return np.block([[m.real, -m.imag], [m.imag, m.real]])


def _hilo(m):
    m = jnp.asarray(m, dtype=F32)
    return _split(m)


@functools.lru_cache(maxsize=None)
def _fft_constants(n):
    m = 2 * n
    n2 = FFT_N2
    n1 = m // n2
    n1h = n1 // 2
    k1 = np.arange(n1)
    t1 = np.arange(n1h)
    f1 = np.exp(-2j * np.pi * np.outer(k1, t1) / n1)
    f3 = np.exp(2j * np.pi * np.outer(t1, k1) / n1) / m
    k2 = np.arange(n2)
    t2 = np.arange(n2)
    w2 = np.exp(-2j * np.pi * np.outer(k2, t2) / n2)
    tw = np.exp(-2j * np.pi * np.outer(k1, t2) / m)
    return dict(
        n1=n1, n1h=n1h,
        f1c=_stack_complex(f1), f1r=np.concatenate([f1.real, f1.imag], axis=0),
        f3c=_stack_complex(f3),
        w2r=w2.real.astype(np.float32), w2i=w2.imag.astype(np.float32),
        twr=tw.real.astype(np.float32), twi=tw.imag.astype(np.float32),
    )


def _stage2_tables(n):
    c = _fft_constants(n)
    w2r, w2i = jnp.asarray(c["w2r"])[None], jnp.asarray(c["w2i"])[None]
    twr, twi = jnp.asarray(c["twr"])[:, None, :], jnp.asarray(c["twi"])[:, None, :]
    return (*_split(w2r * twr - w2i * twi), *_split(w2r * twi + w2i * twr))


def _stacked_tables(grh_ref, grl_ref, gih_ref, gil_ref, q):
    def stack(gr, gi):
        return jnp.concatenate([jnp.concatenate([gr, -gi], axis=1), jnp.concatenate([gi, gr], axis=1)], axis=0)
    return stack(grh_ref[q], gih_ref[q]), stack(grl_ref[q], gil_ref[q])


def _s1_kernel(x_ref, fh_ref, fl_ref, o_ref, *, nparts, ncw, n1, n1h):
    fh = fh_ref[...]
    fl = fl_ref[...]
    for r in range(SUBLANES):
        rows = pl.ds(r, n1h, stride=SUBLANES)
        xs = jnp.concatenate(
            [jnp.concatenate([x_ref[p, cc, rows, :] for p in range(nparts)], axis=0) for cc in range(ncw)], axis=1)
        xh, xl = _split(xs)
        res = _dot3(fh, fl, xh, xl)
        for ri in range(2):
            for cc in range(ncw):
                o_ref[ri, cc, pl.ds(r, n1, stride=SUBLANES), :] = res[ri * n1:(ri + 1) * n1, cc * LANES:(cc + 1) * LANES]


def _fft_stage1(x, fmat, n1, n1h, ncw):
    nparts, ncc, groups = x.shape[:3]
    fh, fl = _hilo(fmat)
    return pl.pallas_call(
        functools.partial(_s1_kernel, nparts=nparts, ncw=ncw, n1=n1, n1h=n1h),
        out_shape=jax.ShapeDtypeStruct((2, ncc, groups, n1 * SUBLANES, LANES), F32),
        grid=(groups, ncc // ncw),
        in_specs=[
            pl.BlockSpec((nparts, ncw, None, n1h * SUBLANES, LANES), lambda j, ci: (0, ci, j, 0, 0)),
            pl.BlockSpec(fh.shape, lambda j, ci: (0, 0)),
            pl.BlockSpec(fl.shape, lambda j, ci: (0, 0)),
        ],
        out_specs=pl.BlockSpec((2, ncw, None, n1 * SUBLANES, LANES), lambda j, ci: (0, ci, j, 0, 0)),
        compiler_params=_params(("parallel", "parallel")),
        name="fft_stage1",
    )(x, fh, fl)


def _load_k1(a_ref, q):
    ncc = a_ref.shape[1]
    return jnp.concatenate([a_ref[:, cc, :, q].reshape(2 * FFT_N2, LANES) for cc in range(ncc)], axis=1)


def _filter_spectrum_kernel(a_ref, grh_ref, grl_ref, gih_ref, gil_ref, s_ref, o_ref, *, kg):
    s = jnp.sum(s_ref[...], axis=0, keepdims=True)
    inv = 1.0 / (s[:, :HYENA_WIDTH] + s[:, HYENA_WIDTH:])
    half = FFT_N2
    for q in range(kg):
        ah, al = _split(_load_k1(a_ref, q))
        gh, gl = _stacked_tables(grh_ref, grl_ref, gih_ref, gil_ref, q)
        h = _dot3(gh, gl, ah, al)
        hf = h[:, :HYENA_WIDTH]
        hb = h[:, HYENA_WIDTH:]
        o_ref[q, :half] = (hf[:half] + hb[:half]) * inv
        o_ref[q, half:] = (hf[half:] - hb[half:]) * inv


FFT_K1_GROUP = 8


def _k1_spec(kg, width):
    return pl.BlockSpec((2, width // LANES, GROUPS, kg, SUBLANES, LANES), lambda i: (0, 0, 0, i, 0, 0))


def _filter_spectrum(a, tables, sums, n1):
    kg = FFT_K1_GROUP
    a6 = a.reshape(2, 2 * HYENA_WIDTH // LANES, GROUPS, n1, SUBLANES, LANES)
    tspec = pl.BlockSpec((kg, FFT_N2, FFT_N2), lambda i: (i, 0, 0))
    return pl.pallas_call(
        functools.partial(_filter_spectrum_kernel, kg=kg),
        out_shape=jax.ShapeDtypeStruct((n1, 2 * FFT_N2, HYENA_WIDTH), F32),
        grid=(n1 // kg,),
        in_specs=[_k1_spec(kg, 2 * HYENA_WIDTH), tspec, tspec, tspec, tspec,
                  pl.BlockSpec((SUBLANES, 2 * HYENA_WIDTH), lambda i: (0, 0))],
        out_specs=pl.BlockSpec((kg, 2 * FFT_N2, HYENA_WIDTH), lambda i: (i, 0, 0)),
        compiler_params=_params(("parallel",)),
        name="filter_spectrum",
    )(a6, *tables, sums)


def _dot_t(a, b):
    return lax.dot_general(a, b, (((0,), (0,)), ((), ())), preferred_element_type=F32)


def _s2_kernel(a_ref, kf_ref, grh_ref, grl_ref, gih_ref, gil_ref, o_ref, *, kg):
    half = FFT_N2
    for q in range(kg):
        ah, al = _split(_load_k1(a_ref, q))
        gh, gl = _stacked_tables(grh_ref, grl_ref, gih_ref, gil_ref, q)
        x = _dot3(gh, gl, ah, al)
        xr, xi = x[:half], x[half:]
        kr, ki = kf_ref[q, :half], kf_ref[q, half:]
        y = jnp.concatenate([xr * kr - xi * ki, xr * ki + xi * kr], axis=0)
        yh, yl = _split(y)
        bt = _dot_t(gh, yh) + _dot_t(gh, yl) + _dot_t(gl, yh)
        for cc in range(HYENA_WIDTH // LANES):
            o_ref[:, cc, :, q] = bt[:, cc * LANES:(cc + 1) * LANES].reshape(2, GROUPS, SUBLANES, LANES)


def _fft_stage2(a, kf, tables, n1):
    kg = FFT_K1_GROUP
    a6 = a.reshape(2, HYENA_WIDTH // LANES, GROUPS, n1, SUBLANES, LANES)
    tspec = pl.BlockSpec((kg, FFT_N2, FFT_N2), lambda i: (i, 0, 0))
    dspec = _k1_spec(kg, HYENA_WIDTH)
    out = pl.pallas_call(
        functools.partial(_s2_kernel, kg=kg),
        out_shape=jax.ShapeDtypeStruct(a6.shape, F32),
        grid=(n1 // kg,),
        in_specs=[dspec, pl.BlockSpec((kg, 2 * FFT_N2, HYENA_WIDTH), lambda i: (i, 0, 0)), tspec, tspec, tspec, tspec],
        out_specs=dspec,
        compiler_params=_params(("parallel",)),
        name="fft_stage2",
    )(a6, kf, *tables)
    return out.reshape(a.shape)


def _s3_kernel(b_ref, vx_ref, x0_ref, bias_ref, fh_ref, fl_ref, o_ref, *, ncw, n1, n1h):
    fh = fh_ref[...]
    fl = fl_ref[...]
    for r in range(SUBLANES):
        krows = pl.ds(r, n1, stride=SUBLANES)
        z = jnp.concatenate(
            [jnp.concatenate([b_ref[ri, cc, krows, :] for ri in range(2)], axis=0) for cc in range(ncw)], axis=1)
        zh, zl = _split(z)
        y = _dot3(fh, fl, zh, zl)
        trows = pl.ds(r, n1h, stride=SUBLANES)
        for p in range(2):
            for cc in range(ncw):
                yy = y[p * n1h:(p + 1) * n1h, cc * LANES:(cc + 1) * LANES]
                o_ref[p, cc, trows, :] = (yy + vx_ref[p, cc, trows, :] * bias_ref[cc]) * x0_ref[p, cc, trows, :]


def _fft_stage3(bt, vx, x0, bias, fmat, n1, n1h, ncw):
    ncc, groups = bt.shape[1:3]
    fh, fl = _hilo(fmat)
    tspec = pl.BlockSpec((2, ncw, None, n1h * SUBLANES, LANES), lambda j, ci: (0, ci, j, 0, 0))
    return pl.pallas_call(
        functools.partial(_s3_kernel, ncw=ncw, n1=n1, n1h=n1h),
        out_shape=jax.ShapeDtypeStruct(vx.shape, F32),
        grid=(groups, ncc // ncw),
        in_specs=[
            pl.BlockSpec((2, ncw, None, n1 * SUBLANES, LANES), lambda j, ci: (0, ci, j, 0, 0)),
            tspec, tspec,
            pl.BlockSpec((ncw, 1, LANES), lambda j, ci: (ci, 0, 0)),
            pl.BlockSpec(fh.shape, lambda j, ci: (0, 0)),
            pl.BlockSpec(fl.shape, lambda j, ci: (0, 0)),
        ],
        out_specs=tspec,
        compiler_params=_params(("parallel", "parallel")),
        name="fft_stage3",
    )(bt, vx, x0, bias, fh, fl)


def _hyena_latent(p, hy, feats, deltas, tables, n):
    conv_w, conv_b, w1, b1, f1, w2, b2, f2, w3, bias = hy
    c = _fft_constants(n)
    n1, n1h = c["n1"], c["n1h"]
    vx, x0 = _hyena_prep(p, conv_w, conv_b, group_major=True)
    gm_shape = vx.shape
    rows = lambda a: a.reshape(*a.shape[:-3], n1h * SUBLANES, LANES)
    vx, x0 = rows(vx), rows(x0)
    taps, sums = _hyena_filter(n, feats, deltas, w1, b1, f1, w2, b2, f2, w3, group_major=True)
    a_f = _fft_stage1(rows(taps)[None], c["f1r"], n1, n1h, ncw=2)
    kf = _filter_spectrum(a_f, tables, sums, n1)
    a = _fft_stage1(vx, c["f1c"], n1, n1h, ncw=2)
    bt = _fft_stage2(a, kf, tables, n1)
    out = _fft_stage3(bt, vx, x0, bias.reshape(HYENA_WIDTH // LANES, 1, LANES), c["f3c"], n1, n1h, ncw=2)
    return out.reshape(gm_shape)


@functools.lru_cache(maxsize=None)
def _dense_dft_constants(n):
    m = 2 * n
    k = np.arange(m)
    t = np.arange(n)
    f = np.exp(-2j * np.pi * np.outer(k, t) / m)
    finv = np.exp(2j * np.pi * np.outer(t, k) / m) / m
    return _stack_complex(f), np.concatenate([f.real, f.imag], axis=0), _stack_complex(finv)


def _dense_conv_kernel(vx_ref, x0_ref, hf_ref, hb_ref, sf_ref, sb_ref, bias_ref,
                       fch, fcl, frh, frl, fih, fil, o_ref, *, n):
    m = 2 * n
    z = jnp.concatenate([vx_ref[0], vx_ref[1]], axis=0)
    zh, zl = _split(z)
    zf = _dot3(fch[...], fcl[...], zh, zl)
    hfh, hfl = _split(hf_ref[...])
    hbh, hbl = _split(hb_ref[...])
    hf = _dot3(frh[...], frl[...], hfh, hfl)
    hb = _dot3(frh[...], frl[...], hbh, hbl)
    inv = 1.0 / (jnp.sum(sf_ref[...], axis=0, keepdims=True) + jnp.sum(sb_ref[...], axis=0, keepdims=True))
    kr = (hf[:m] + hb[:m]) * inv
    ki = (hf[m:] - hb[m:]) * inv
    zr, zi = zf[:m], zf[m:]
    y = jnp.concatenate([zr * kr - zi * ki, zr * ki + zi * kr], axis=0)
    yh, yl = _split(y)
    out = _dot3(fih[...], fil[...], yh, yl)
    bias = bias_ref[...]
    for p in range(2):
        o_ref[p] = (out[p * n:(p + 1) * n] + vx_ref[p] * bias) * x0_ref[p]


def _hyena_context(p, hy, feats, deltas, n):
    conv_w, conv_b, w1, b1, f1, w2, b2, f2, w3, bias = hy
    vx, x0 = _hyena_prep(p, conv_w, conv_b, group_major=False)
    taps, sums = _hyena_filter(n, feats, deltas, w1, b1, f1, w2, b2, f2, w3, group_major=False)
    fc, fr, fi = _dense_dft_constants(n)
    mats = [*_hilo(fc), *_hilo(fr), *_hilo(fi)]
    cw = 256
    nct = HYENA_WIDTH // cw
    dspec = pl.BlockSpec((2, n, cw), lambda ci: (0, 0, ci))
    return pl.pallas_call(
        functools.partial(_dense_conv_kernel, n=n),
        out_shape=jax.ShapeDtypeStruct(vx.shape, F32),
        grid=(nct,),
        in_specs=[dspec, dspec,
                  pl.BlockSpec((n, cw), lambda ci: (0, ci)), pl.BlockSpec((n, cw), lambda ci: (0, nct + ci)),
                  pl.BlockSpec((SUBLANES, cw), lambda ci: (0, ci)), pl.BlockSpec((SUBLANES, cw), lambda ci: (0, nct + ci)),
                  pl.BlockSpec((1, cw), lambda ci: (0, ci))]
                 + [pl.BlockSpec(mt.shape, lambda ci: (0, 0)) for mt in mats],
        out_specs=dspec,
        compiler_params=_params(("parallel",)),
        name="context_long_conv",
    )(vx, x0, taps, taps, sums, sums, bias.reshape(1, HYENA_WIDTH), *mats)


def _pool_kernel(x_ref, p_ref, n_ref, w_ref, sc_ref, o_ref, *, tl, nt, n):
    i = pl.program_id(1)
    x = x_ref[...]
    pv = jnp.where(i > 0, p_ref[...], 0.0)
    nx = jnp.where(i < nt - 1, n_ref[...], 0.0)
    ext = jnp.concatenate([pv, x, nx], axis=0)
    rows = tl + 2 * POOL_HALO
    t = i * tl + lax.broadcasted_iota(jnp.int32, (tl, POOL_GROUP), 0)
    for g, w in enumerate(POOL_WINDOWS):
        lanes = slice(g * POOL_GROUP, (g + 1) * POOL_GROUP)
        a = ext[:, lanes]
        c = a + pltpu.roll(a, 1, 0)
        h = 1
        while 2 * h < w:
            c = pltpu.roll(c, h, 0) + pltpu.roll(c, rows - h, 0)
            h *= 2
        total = c[POOL_HALO:POOL_HALO + tl]
        count = (jnp.minimum(t + h, n) - jnp.maximum(t - h, 0)).astype(F32)
        y = (total / count - x[:, lanes]).astype(BF16)
        o_ref[:, lanes] = _dot(y, w_ref[g].astype(BF16)) * sc_ref[:, lanes]


def _pool_mixer(p, w_pool, scale):
    b, n, _ = p.shape
    tl = min(512, n)
    nt = n // tl
    colblk = 3 * HYENA_WIDTH // POOL_WIDTH
    assert colblk * POOL_WIDTH == 3 * HYENA_WIDTH
    per = tl // POOL_HALO
    nrow = n // POOL_HALO
    return pl.pallas_call(
        functools.partial(_pool_kernel, tl=tl, nt=nt, n=n),
        out_shape=jax.ShapeDtypeStruct((b, n, POOL_WIDTH), F32),
        grid=(b, nt),
        in_specs=[
            pl.BlockSpec((None, tl, POOL_WIDTH), lambda bi, i: (bi, i, colblk)),
            pl.BlockSpec((None, POOL_HALO, POOL_WIDTH), lambda bi, i: (bi, jnp.maximum(i * per - 1, 0), colblk)),
            pl.BlockSpec((None, POOL_HALO, POOL_WIDTH), lambda bi, i: (bi, jnp.minimum((i + 1) * per, nrow - 1), colblk)),
            pl.BlockSpec(w_pool.shape, lambda bi, i: (0, 0, 0)),
            pl.BlockSpec((1, POOL_WIDTH), lambda bi, i: (0, 0)),
        ],
        out_specs=pl.BlockSpec((None, tl, POOL_WIDTH), lambda bi, i: (bi, i, 0)),
        compiler_params=_params(("parallel", "parallel")),
        name="pool_mixer",
    )(p, p, p, w_pool, scale.reshape(1, POOL_WIDTH))


def _load_group_major(ref, tl):
    ncc = ref.shape[0]
    return jnp.concatenate(
        [jnp.concatenate([ref[cc, jg, t1] for cc in range(ncc)], axis=1)
         for t1 in range(tl // FFT_N2) for jg in range(GROUPS)], axis=0)


def _outproj_kernel(at_ref, hy_ref, po_ref, x_ref, gb_ref, w_ref, gp_ref, gt_ref, o_ref, ox_ref, m_ref,
                    *, tm, tiles, group_major):
    t = pl.program_id(0)

    @pl.when(t == 0)
    def _():
        ox_ref[...] = jnp.zeros_like(ox_ref)

    def finish_previous():
        o_ref[...] = x_ref[...] + gt_ref[...] * _rms(ox_ref[...], gp_ref[...])

    @pl.when(t < tiles)
    def _():
        finish_previous()
        hy = _load_group_major(hy_ref, tm) if group_major else hy_ref[...]
        a0, a1 = ATTN_WIDTH, ATTN_WIDTH + HYENA_WIDTH
        m_ref[:, :a0] = _rms(at_ref[...], gb_ref[:, :a0]).astype(BF16)
        m_ref[:, a0:a1] = _rms(hy, gb_ref[:, a0:a1]).astype(BF16)
        m_ref[:, a1:] = _rms(po_ref[...], gb_ref[:, a1:]).astype(BF16)
        ox_ref[...] = _dot(m_ref[...], w_ref[...])

    pl.when(t == tiles)(finish_previous)


def _out_projection(attn, hy, po, x, g_branch, w_out, layer, g_post, gate, group_major):
    b, n, d = x.shape
    tm = min(512, n)
    nt = n // tm
    tiles = b * nt

    def cur(t):
        return jnp.minimum(t, tiles - 1)

    def prev(t):
        return jnp.maximum(t - 1, 0)

    row = lambda width, sel: pl.BlockSpec((None, tm, width), lambda t: (sel(t) // nt, sel(t) % nt, 0))
    vec = lambda width: pl.BlockSpec((1, width), lambda t: (0, 0))
    if group_major:
        hy_spec = pl.BlockSpec((None, *_gm_shape((), tm // FFT_N2, HYENA_WIDTH)),
                               lambda t: (cur(t) // nt, 0, 0, cur(t) % nt, 0, 0))
    else:
        hy_spec = row(HYENA_WIDTH, cur)
    return pl.pallas_call(
        functools.partial(_outproj_kernel, tm=tm, tiles=tiles, group_major=group_major),
        out_shape=jax.ShapeDtypeStruct(x.shape, F32),
        grid=(tiles + 1,),
        in_specs=[row(ATTN_WIDTH, cur), hy_spec, row(POOL_WIDTH, cur), row(d, prev), vec(d),
                  pl.BlockSpec((None, *w_out.shape[1:]), lambda t: (layer, 0, 0), pipeline_mode=pl.Buffered(1)), vec(d),
                  pl.BlockSpec((None, 1, d), lambda t: (prev(t) // nt, 0, 0))],
        out_specs=row(d, prev),
        scratch_shapes=[pltpu.VMEM((tm, d), F32), pltpu.VMEM((tm, d), BF16)],
        compiler_params=_params(("arbitrary",)),
        name="out_projection",
    )(attn, hy, po, x, g_branch.reshape(1, d), w_out, g_post.reshape(1, d), gate)


def _mlp_kernel(xe_ref, xp_ref, g_ref, shn_ref, scn_ref, wu_ref, wd_ref, gp_ref, gtp_ref, o_ref,
                h0_ref, h1_ref, acc0_ref, acc1_ref, *, rc, tiles):
    t = pl.program_id(0)
    k = pl.program_id(1)
    rows = pl.ds(pl.multiple_of(k * rc, rc), rc)

    def normalise(h_ref):
        a = g_ref[...] * (1.0 + scn_ref[...])
        h_ref[rows, :] = (_rms(xp_ref[...], a) + shn_ref[...]).astype(BF16)

    def finish(acc_ref):
        o_ref[...] = xe_ref[...] + _rms(acc_ref[rows, :], gp_ref[...] * gtp_ref[...])

    @pl.when(t == 0)
    def _():
        normalise(h0_ref)
        acc0_ref[rows, :] = jnp.zeros((rc, acc0_ref.shape[1]), F32)
        acc1_ref[rows, :] = jnp.zeros((rc, acc1_ref.shape[1]), F32)

    def step(slot):
        h_cur, h_oth = (h0_ref, h1_ref) if slot == 0 else (h1_ref, h0_ref)
        acc_cur, acc_oth = (acc0_ref, acc1_ref) if slot == 0 else (acc1_ref, acc0_ref)
        finish(acc_oth)
        acc_oth[rows, :] = jnp.zeros((rc, acc_oth.shape[1]), F32)
        normalise(h_oth)
        u = jnp.maximum(_dot(h_cur[...], wu_ref[...]), 0.0)
        acc_cur[...] += _dot((u * u).astype(BF16), wd_ref[...])

    work = (t >= 1) & (t <= tiles)
    pl.when(work & (t % 2 == 1))(functools.partial(step, 0))
    pl.when(work & (t % 2 == 0))(functools.partial(step, 1))
    pl.when(t == tiles + 1)(functools.partial(finish, acc0_ref if (tiles - 1) % 2 == 0 else acc1_ref))


MLP_TH = 1024


def _mlp(x, g_pre, shift, scale, w_up, w_down, layer, g_post, gate):
    b, n, d = x.shape
    th = MLP_TH
    nk = w_up.shape[2] // th
    tm = min(1024, n)
    nt = n // tm
    tiles = b * nt
    rc = tm // nk
    assert rc * nk == tm and rc % BF16_ROWS == 0

    def norm_tile(t):
        return jnp.minimum(t, tiles - 1)

    def done_tile(t):
        return jnp.clip(t - 2, 0, tiles - 1)

    vec = pl.BlockSpec((1, d), lambda t, k: (0, 0))
    bvec = lambda sel: pl.BlockSpec((None, 1, d), lambda t, k: (sel(t) // nt, 0, 0))
    chunk = lambda sel, first: pl.BlockSpec(
        (None, rc, d), lambda t, k: (sel(t) // nt, (sel(t) % nt) * nk + jnp.where(t < first, 0, k), 0))
    busy = lambda t: (t >= 1) & (t <= tiles)
    return pl.pallas_call(
        functools.partial(_mlp_kernel, rc=rc, tiles=tiles),
        out_shape=jax.ShapeDtypeStruct(x.shape, F32),
        grid=(tiles + 2, nk),
        in_specs=[chunk(done_tile, 2), chunk(norm_tile, 0), vec, bvec(norm_tile), bvec(norm_tile),
                  pl.BlockSpec((None, d, th), lambda t, k: (layer, 0, jnp.where(busy(t), k, 0))),
                  pl.BlockSpec((None, th, d), lambda t, k: (layer, jnp.where(busy(t), k, 0), 0)),
                  vec, bvec(done_tile)],
        out_specs=chunk(done_tile, 2),
        scratch_shapes=[pltpu.VMEM((tm, d), BF16)] * 2 + [pltpu.VMEM((tm, d), F32)] * 2,
        compiler_params=_params(("arbitrary", "arbitrary")),
        name="mlp",
    )(x, x, g_pre.reshape(1, d), shift, scale, w_up, w_down, g_post.reshape(1, d), gate)


def kernel(x, c, ctx, c_ctx, w_mod, b_mod, g_pre_mix, g_post_mix, g_pre_mlp, g_post_mlp, w_in, w_out, g_branch,
           attn_sink, hy_conv_w, hy_conv_b, hy_w1, hy_b1, hy_freq1, hy_w2, hy_b2, hy_freq2, hy_w3, hy_bias,
           pool_w, pool_scale, w_up, w_down):
    b, n, d = x.shape
    n_ctx = ctx.shape[1]
    depth = w_mod.shape[0]
    assert b == 2 and d == D_MODEL and n % 512 == 0 and n_ctx % BLOCK == 0

    cond = jnp.concatenate([c, c_ctx[None], jnp.zeros((SUBLANES - b - 1, d), F32)], axis=0)
    mods = _modulation(cond, w_mod, b_mod)

    w_in_b = _prepare_w_in(w_in, IN_TN)
    w_up_b = w_up.astype(BF16)
    w_out_b, w_down_b = w_out.astype(BF16), w_down.astype(BF16)
    tables_x, tables_c = _rope_tables(n), _rope_tables(n_ctx)
    feats_x, feats_c = _filter_features(n), _filter_features(n_ctx)
    deltas = _filter_deltas()
    tables = _stage2_tables(n)

    for i in range(depth):
        last = i == depth - 1
        hy = (hy_conv_w[i], hy_conv_b[i], hy_w1[i], hy_b1[i], hy_freq1[i], hy_w2[i], hy_b2[i], hy_freq2[i],
              hy_w3[i], hy_bias[i])
        mx = [m[:, None, :] for m in jnp.split(mods[i, :b], N_MOD, axis=-1)]
        mc = [jnp.broadcast_to(m[None, None, :], (b, 1, d)) for m in jnp.split(mods[i, b], N_MOD, axis=-1)]

        qkv_x, px = _in_projection(x, g_pre_mix[i], mx[0], mx[1], w_in_b, i, tables_x, rope=True)
        qkv_c, pc = _in_projection(ctx, g_pre_mix[i], mc[0], mc[1], w_in_b, i, tables_c, rope=False)
        kv_ctx = qkv_c[..., Q_END:V_END]

        attn_x = _attention(qkv_x, kv_ctx, attn_sink[i], local=True)
        hy_x = _hyena_latent(px, hy, feats_x, deltas, tables, n)
        po_x = _pool_mixer(px, pool_w[i], pool_scale[i])
        x = _out_projection(attn_x, hy_x, po_x, x, g_branch[i], w_out_b, i, g_post_mix[i], mx[2], group_major=True)
        x = _mlp(x, g_pre_mlp[i], mx[3], mx[4], w_up_b, w_down_b, i, g_post_mlp[i], mx[5])

        if not last:
            attn_c = _attention(qkv_c, kv_ctx, attn_sink[i], local=False)
            hy_c = _hyena_context(pc, hy, feats_c, deltas, n_ctx)
            po_c = _pool_mixer(pc, pool_w[i], pool_scale[i])
            ctx = _out_projection(attn_c, hy_c, po_c, ctx, g_branch[i], w_out_b, i, g_post_mix[i], mc[2],
                                  group_major=False)
            ctx = _mlp(ctx, g_pre_mlp[i], mc[3], mc[4], w_up_b, w_down_b, i, g_post_mlp[i], mc[5])
    return x
```

```python
import functools
import math

import numpy as np
import jax
import jax.numpy as jnp
from jax import lax
from jax.experimental import pallas as pl
from jax.experimental.pallas import tpu as pltpu

F32 = jnp.float32
BF16 = jnp.bfloat16

D_MODEL = 2048
DEPTH = 4
GRID_W = 64
ATTN_WIDTH = D_MODEL // 2
HYENA_WIDTH = D_MODEL // 4
POOL_WIDTH = D_MODEL - ATTN_WIDTH - HYENA_WIDTH
HEAD_DIM = 128
N_HEADS = ATTN_WIDTH // HEAD_DIM
N_KV_HEADS = 2
KV_GROUP = N_HEADS // N_KV_HEADS
KV_WIDTH = N_KV_HEADS * HEAD_DIM
WINDOW = 128
BLOCK = 128
ROPE_BASE = 10000.0
HYENA_EMB_DIM = 33
HYENA_FILTER_HIDDEN = 64
HYENA_FAST_DECAY_PCT = 0.3
HYENA_SLOW_DECAY_PCT = 1.5
HYENA_DECAY_TARGET = 1e-2
POOL_WINDOWS = (2, 4, 8, 16)
POOL_GROUP = POOL_WIDTH // len(POOL_WINDOWS)
MLP_HIDDEN = 4 * D_MODEL
N_MOD = 6
EPS = 1e-6
NEG_INF = -1e30

Q_END = ATTN_WIDTH
K_END = Q_END + KV_WIDTH
V_END = K_END + KV_WIDTH
HY_END = V_END + 3 * HYENA_WIDTH
IN_WIDTH = HY_END + POOL_WIDTH

LANES = 128
SUBLANES = 8
FFT_N2 = 128
POOL_HALO = 16
VMEM_LIMIT = 56 * 1024 * 1024


def _params(sem, vmem=VMEM_LIMIT):
    return pltpu.CompilerParams(dimension_semantics=sem, vmem_limit_bytes=vmem)


def _split(x):
    hi = x.astype(BF16)
    lo = (x - hi.astype(F32)).astype(BF16)
    return hi, lo


def _dot(a, b):
    return jnp.dot(a, b, preferred_element_type=F32)


def _dot3(ah, al, bh, bl):
    return _dot(ah, bh) + _dot(ah, bl) + _dot(al, bh)


def _rms(x, g):
    return x * lax.rsqrt(jnp.mean(x * x, axis=-1, keepdims=True) + EPS) * g


BF16_ROWS = 2 * SUBLANES


def _norm_scale_rows(x_ref, h_ref, a_ref, s_ref, rows, cols=None):
    cols = slice(None) if cols is None else cols
    nchunks = rows // BF16_ROWS

    def chunk(c):
        return pl.ds(pl.multiple_of(c * BF16_ROWS, BF16_ROWS), BF16_ROWS)

    def inv_rms(c):
        x = x_ref[chunk(c), :]
        return lax.rsqrt(jnp.mean(x * x, axis=-1, keepdims=True) + EPS)

    def body(c, inv):
        inv_next = inv_rms(jnp.minimum(c + 1, nchunks - 1))
        y = x_ref[chunk(c), :] * inv * a_ref[...]
        if s_ref is not None:
            y = y + s_ref[...]
        h_ref[chunk(c), cols] = y.astype(BF16)
        return inv_next

    lax.fori_loop(0, nchunks, body, inv_rms(0), unroll=8)


def _residual_norm_rows(x_ref, y_ref, o_ref, pg_ref, rows):
    nchunks = rows // SUBLANES

    def chunk(c):
        return pl.ds(pl.multiple_of(c * SUBLANES, SUBLANES), SUBLANES)

    def inv_rms(c):
        y = y_ref[chunk(c), :]
        return lax.rsqrt(jnp.mean(y * y, axis=-1, keepdims=True) + EPS)

    def body(c, inv):
        inv_next = inv_rms(jnp.minimum(c + 1, nchunks - 1))
        o_ref[chunk(c), :] = x_ref[chunk(c), :] + y_ref[chunk(c), :] * inv * pg_ref[...]
        return inv_next

    lax.fori_loop(0, nchunks, body, inv_rms(0), unroll=16)


def _mod_kernel(c_ref, w_ref, b_ref, o_ref):
    c = c_ref[...]
    s = c / (1.0 + jnp.exp(-c))
    sh, sl = _split(s)
    wh, wl = _split(w_ref[...])
    o_ref[...] = _dot3(sh, sl, wh, wl) + b_ref[...]


def _modulation(cond, w_mod, b_mod):
    depth, d, width = w_mod.shape
    tn = 1024
    return pl.pallas_call(
        _mod_kernel,
        out_shape=jax.ShapeDtypeStruct((depth, SUBLANES, width), F32),
        grid=(depth, width // tn),
        in_specs=[
            pl.BlockSpec((SUBLANES, d), lambda l, j: (0, 0)),
            pl.BlockSpec((None, d, tn), lambda l, j: (l, 0, j)),
            pl.BlockSpec((None, 1, tn), lambda l, j: (l, 0, j)),
        ],
        out_specs=pl.BlockSpec((None, SUBLANES, tn), lambda l, j: (l, 0, j)),
        compiler_params=_params(("parallel", "parallel")),
        name="modulation",
    )(cond, w_mod, b_mod.reshape(depth, 1, width))


QKV_WIDTH = V_END
REST_WIDTH = IN_WIDTH - V_END
SM_SCALE = HEAD_DIM ** -0.5


TABLE_Q, TABLE_K, TABLE_ID, TABLE_SCALE = range(4)


def _inproj_kernel(xn_ref, g_ref, shn_ref, scn_ref, w_ref, ca_ref, sa_ref, cb_ref, sb_ref,
                   qkv_ref, rest_ref, h0_ref, h1_ref, acc_ref, *, tm, tn, rc, tiles):
    t = pl.program_id(0)
    j = pl.program_id(1)

    def normalise(h_ref):
        rows = pl.ds(pl.multiple_of(jnp.minimum(j, tm // rc - 1) * rc, rc), rc)
        a = g_ref[...] * (1.0 + scn_ref[...])
        h_ref[rows, :] = (_rms(xn_ref[...], a) + shn_ref[...]).astype(BF16)

    def finish_previous():
        acc = acc_ref[...]
        rest_ref[...] = acc
        nch = tn // LANES
        for ch in range(nch):
            a = acc[:, ch * LANES:(ch + 1) * LANES]
            cos, sin = (ca_ref, sa_ref) if ch < nch // 2 else (cb_ref, sb_ref)
            qkv_ref[:, ch * LANES:(ch + 1) * LANES] = (a * cos[...] + pltpu.roll(a, HEAD_DIM // 2, 1) * sin[...]).astype(BF16)

    @pl.when(t == 0)
    def _():
        normalise(h0_ref)

    @pl.when((t == 0) & (j == 0))
    def _():
        acc_ref[...] = jnp.zeros_like(acc_ref)

    def step(slot):
        h_cur, h_oth = (h0_ref, h1_ref) if slot == 0 else (h1_ref, h0_ref)
        finish_previous()
        normalise(h_oth)
        acc_ref[...] = _dot(h_cur[...], w_ref[j])

    work = (t >= 1) & (t <= tiles)
    pl.when(work & (t % 2 == 1))(functools.partial(step, 0))
    pl.when(work & (t % 2 == 0))(functools.partial(step, 1))
    pl.when((t == tiles + 1) & (j == 0))(finish_previous)


IN_TN = 512
QKV_SPARE = QKV_WIDTH // IN_TN
REST_SPARE = REST_WIDTH // IN_TN


def _in_projection(x, g, shift, scale, w, layer, tables, rope):
    b, n, d = x.shape
    tm = min(1024, n)
    tn = w.shape[-1]
    nj = w.shape[1]
    nt = n // tm
    tiles = b * nt
    half = tn // 2
    assert tn == IN_TN and nj * tn == IN_WIDTH and Q_END % tn == 0 and K_END % tn == half and V_END % tn == 0
    nslices = 4
    rc = tm // nslices
    assert nslices <= nj and rc % BF16_ROWS == 0
    nq, nqkv = Q_END // tn, QKV_WIDTH // tn

    def norm_tile(t):
        return jnp.minimum(t, tiles - 1)

    def lagged(t, j):
        step = jnp.clip((t - 1) * nj + j - 1, 0, tiles * nj - 1)
        return step // nj, step % nj

    def kind_a(j):
        return jnp.where(j < nq, TABLE_Q if rope else TABLE_SCALE, jnp.where(j < nqkv, TABLE_K if rope else TABLE_ID, TABLE_ID))

    def kind_b(j):
        return jnp.where(j < nq, TABLE_Q if rope else TABLE_SCALE, TABLE_ID)

    vec = pl.BlockSpec((1, d), lambda t, j: (0, 0))
    bvec = lambda sel: pl.BlockSpec((None, 1, d), lambda t, j: (sel(t) // nt, 0, 0))
    def table(kind):
        def index(t, j):
            tp, jp = lagged(t, j)
            k = kind(jp)
            return k, jnp.where(k == TABLE_ID, 0, tp % nt), 0
        return pl.BlockSpec((None, tm, LANES), index)

    def out_spec(col_tile):
        return pl.BlockSpec((None, tm, tn), lambda t, j: (lagged(t, j)[0] // nt, lagged(t, j)[0] % nt,
                                                          col_tile(lagged(t, j)[1])))

    cos, sin = tables
    return pl.pallas_call(
        functools.partial(_inproj_kernel, tm=tm, tn=tn, rc=rc, tiles=tiles),
        out_shape=[jax.ShapeDtypeStruct((b, n, QKV_WIDTH + tn), BF16), jax.ShapeDtypeStruct((b, n, REST_WIDTH + tn), F32)],
        grid=(tiles + 2, nj),
        in_specs=[pl.BlockSpec((None, rc, d), lambda t, j: (norm_tile(t) // nt, (norm_tile(t) % nt) * nslices
                                                            + jnp.minimum(j, nslices - 1), 0)),
                  vec, bvec(norm_tile), bvec(norm_tile),
                  pl.BlockSpec((None, *w.shape[1:]), lambda t, j: (layer, 0, 0, 0), pipeline_mode=pl.Buffered(1)),
                  table(kind_a), table(kind_a), table(kind_b), table(kind_b)],
        out_specs=[out_spec(lambda jp: jnp.minimum(jp, QKV_SPARE)),
                   out_spec(lambda jp: jnp.where(jp >= nqkv, jp - nqkv, REST_SPARE))],
        scratch_shapes=[pltpu.VMEM((tm, d), BF16)] * 2 + [pltpu.VMEM((tm, tn), F32)],
        compiler_params=_params(("arbitrary", "arbitrary")),
        name="in_projection_rope" if rope else "in_projection",
    )(x, g.reshape(1, d), shift, scale, w, cos, sin, cos, sin)


def _prep_w_in_kernel(w_ref, o_ref, *, tn):
    j = pl.program_id(1)
    quarter = HEAD_DIM // 4
    nch = tn // LANES

    def write(n_swapped):
        for ch in range(nch):
            a = w_ref[:, ch * LANES:(ch + 1) * LANES]
            if ch < n_swapped:
                q = lax.broadcasted_iota(jnp.int32, a.shape, 1) // quarter
                a = jnp.where(q == 1, pltpu.roll(a, LANES - quarter, 1), jnp.where(q == 2, pltpu.roll(a, quarter, 1), a))
            o_ref[:, ch * LANES:(ch + 1) * LANES] = a.astype(BF16)

    nfull, nrem = divmod(K_END // LANES, nch)
    pl.when(j < nfull)(functools.partial(write, nch))
    pl.when(j == nfull)(functools.partial(write, nrem))
    pl.when(j > nfull)(functools.partial(write, 0))


def _prepare_w_in(w_in, tn):
    depth, d, width = w_in.shape
    return pl.pallas_call(
        functools.partial(_prep_w_in_kernel, tn=tn),
        out_shape=jax.ShapeDtypeStruct((depth, width // tn, d, tn), BF16),
        grid=(depth, width // tn),
        in_specs=[pl.BlockSpec((None, d, tn), lambda l, j: (l, 0, j))],
        out_specs=pl.BlockSpec((None, None, d, tn), lambda l, j: (l, j, 0, 0)),
        compiler_params=_params(("parallel", "parallel")),
        name="prepare_w_in",
    )(w_in)


def _rope_tables(n):
    quarter = HEAD_DIM // 4
    inv_freq = ROPE_BASE ** (-jnp.arange(quarter, dtype=F32) / quarter)
    t = jnp.arange(n, dtype=jnp.int32)
    row = (t // GRID_W).astype(F32)[:, None] * inv_freq[None, :]
    col = (t % GRID_W).astype(F32)[:, None] * inv_freq[None, :]
    cos = jnp.concatenate([jnp.cos(row), jnp.cos(col), jnp.cos(row), jnp.cos(col)], axis=-1)
    sin = jnp.concatenate([-jnp.sin(row), -jnp.sin(col), jnp.sin(row), jnp.sin(col)], axis=-1)
    one, zero = jnp.ones_like(cos), jnp.zeros_like(sin)
    return (jnp.stack([cos * SM_SCALE, cos, one, one * SM_SCALE]), jnp.stack([sin * SM_SCALE, sin, zero, zero]))


def _softmax_pv(parts, sink_col):
    m = sink_col
    for s, _ in parts:
        m = jnp.maximum(m, jnp.max(s, axis=-1, keepdims=True))
    den = jnp.exp(sink_col - m)
    out = None
    for s, v in parts:
        p = jnp.exp(s - m)
        den = den + jnp.sum(p, axis=-1, keepdims=True)
        pv = _dot(p.astype(BF16), v)
        out = pv if out is None else out + pv
    return out / den


def _ctx_attn_kernel(sink_ref, q_ref, kvc_ref, o_ref, *, tq):
    for g in range(N_KV_HEADS):
        heads = [g * KV_GROUP + h for h in range(KV_GROUP)]
        qg = jnp.concatenate([q_ref[:, h * HEAD_DIM:(h + 1) * HEAD_DIM] for h in heads], axis=0)
        sink_col = jnp.concatenate([jnp.full((tq, 1), sink_ref[h], F32) for h in heads], axis=0)
        kc = kvc_ref[:, g * HEAD_DIM:(g + 1) * HEAD_DIM]
        vc = kvc_ref[:, KV_WIDTH + g * HEAD_DIM:KV_WIDTH + (g + 1) * HEAD_DIM]
        s = lax.dot_general(qg, kc, (((1,), (1,)), ((), ())), preferred_element_type=F32)
        o = _softmax_pv([(s, vc)], sink_col)
        for hi, h in enumerate(heads):
            o_ref[:, h * HEAD_DIM:(h + 1) * HEAD_DIM] = o[hi * tq:(hi + 1) * tq].astype(BF16)


ATTN_ROWS = KV_GROUP * BLOCK
SOFTMAX_CHUNK = 32


def _win_attn_kernel(sink_ref, q_ref, km_ref, kp_ref, kn_ref, vm_ref, vp_ref, vn_ref, kvc_ref, band_ref,
                     o_ref, ktw, vw, s_scr, p_scr, m_scr, *, tq, nb, nctx):
    i = pl.program_id(1)
    nsub = tq // BLOCK
    nloc = 3 * BLOCK

    def transposed(x):
        return x.astype(F32).T.astype(BF16)

    def block_rows(main_ref, prev_ref, next_ref, w, lanes):
        if w == 0:
            return prev_ref[:, lanes]
        if w == nsub + 1:
            return next_ref[:, lanes]
        return main_ref[(w - 1) * BLOCK:w * BLOCK, lanes]

    ones = jnp.ones((nloc + nctx, HEAD_DIM), BF16)
    for g in range(N_KV_HEADS):
        lanes = slice(g * HEAD_DIM, (g + 1) * HEAD_DIM)
        vlanes = slice(KV_WIDTH + g * HEAD_DIM, KV_WIDTH + (g + 1) * HEAD_DIM)
        kts = [transposed(block_rows(km_ref, kp_ref, kn_ref, w, lanes)) for w in range(nsub + 2)]
        kct = [transposed(kvc_ref[cb * BLOCK:(cb + 1) * BLOCK, lanes]) for cb in range(nctx // BLOCK)]
        for jb in range(nsub):
            for w in range(3):
                ktw[jb, g, :, w * BLOCK:(w + 1) * BLOCK] = kts[jb + w]
                vw[jb, g, w * BLOCK:(w + 1) * BLOCK, :HEAD_DIM] = block_rows(vm_ref, vp_ref, vn_ref, jb + w, lanes)
            for cb in range(nctx // BLOCK):
                ktw[jb, g, :, nloc + cb * BLOCK:nloc + (cb + 1) * BLOCK] = kct[cb]
            vw[jb, g, nloc:, :HEAD_DIM] = kvc_ref[:, vlanes]
            vw[jb, g, :, HEAD_DIM:] = ones

    col = lax.broadcasted_iota(jnp.int32, (1, nloc), 1)

    def rows_of(jb):
        start = jb * BLOCK
        return pl.ds(start if isinstance(start, int) else pl.multiple_of(start, BLOCK), BLOCK)

    def stage_a(jb, g):
        qg = jnp.concatenate([q_ref[rows_of(jb), (g * KV_GROUP + h) * HEAD_DIM:(g * KV_GROUP + h + 1) * HEAD_DIM]
                              for h in range(KV_GROUP)], axis=0)
        s_scr[g] = _dot(qg, ktw[jb, g])

    def stage_b(jb, g):
        blk = i * nsub + jb
        pen_prev = jnp.where(blk == 0, NEG_INF, 0.0).astype(F32)
        pen_next = jnp.where(blk == nb - 1, NEG_INF, 0.0).astype(F32)
        rowbias = jnp.where(col < BLOCK, pen_prev, jnp.where(col >= 2 * BLOCK, pen_next, 0.0))
        for c in range(ATTN_ROWS // SOFTMAX_CHUNK):
            rows = slice(c * SOFTMAX_CHUNK, (c + 1) * SOFTMAX_CHUNK)
            sink = sink_ref[g * KV_GROUP + (c * SOFTMAX_CHUNK) // BLOCK]
            s_loc = s_scr[g, rows, :nloc] + band_ref[rows, :] + rowbias
            s_ctx = s_scr[g, rows, nloc:]
            m = jnp.maximum(jnp.max(s_loc, axis=-1, keepdims=True), jnp.max(s_ctx, axis=-1, keepdims=True))
            m = jnp.maximum(m, sink)
            p_scr[g, rows, :nloc] = jnp.exp(s_loc - m).astype(BF16)
            p_scr[g, rows, nloc:] = jnp.exp(s_ctx - m).astype(BF16)
            m_scr[g, rows, :] = m

    def stage_c(jb, g):
        o = _dot(p_scr[g], vw[jb, g])
        for hi in range(KV_GROUP):
            h = g * KV_GROUP + hi
            rows = slice(hi * BLOCK, (hi + 1) * BLOCK)
            den = o[rows, HEAD_DIM:HEAD_DIM + 1] + jnp.exp(sink_ref[h] - m_scr[g, rows, :])
            o_ref[rows_of(jb), h * HEAD_DIM:(h + 1) * HEAD_DIM] = (o[rows, :HEAD_DIM] / den).astype(BF16)

    stage_a(0, 0)
    stage_a(0, 1)
    stage_b(0, 0)

    def body(j, carry):
        stage_a(j, 0)
        stage_c(j - 1, 0)
        stage_b(j - 1, 1)
        stage_a(j, 1)
        stage_c(j - 1, 1)
        stage_b(j, 0)
        return carry

    lax.fori_loop(1, nsub, body, 0)
    stage_c(nsub - 1, 0)
    stage_b(nsub - 1, 1)
    stage_c(nsub - 1, 1)


def _band_bias():
    qi = np.arange(ATTN_ROWS)[:, None] % BLOCK
    sj = np.arange(3 * BLOCK)[None, :]
    return jnp.asarray(np.where(np.abs(sj - BLOCK - qi) <= WINDOW, 0.0, NEG_INF), dtype=F32)


def _attention(qkv, kv_ctx, sink, local):
    b, n, _ = qkv.shape
    c = kv_ctx.shape[1]
    tq = min(1024 if local else 512, n)
    nsub = tq // BLOCK
    nb = n // BLOCK
    kcol = Q_END // KV_WIDTH
    vcol = K_END // KV_WIDTH
    smem = pl.BlockSpec(memory_space=pltpu.SMEM)
    q_spec = pl.BlockSpec((None, tq, ATTN_WIDTH), lambda bi, i: (bi, i, 0))
    kvc_spec = pl.BlockSpec((None, c, 2 * KV_WIDTH), lambda bi, i: (bi, 0, 0))
    if local:
        def main(colblk):
            return pl.BlockSpec((None, tq, KV_WIDTH), lambda bi, i: (bi, i, colblk))

        def prev(colblk):
            return pl.BlockSpec((None, BLOCK, KV_WIDTH), lambda bi, i: (bi, jnp.maximum(i * nsub - 1, 0), colblk))

        def nxt(colblk):
            return pl.BlockSpec((None, BLOCK, KV_WIDTH), lambda bi, i: (bi, jnp.minimum((i + 1) * nsub, nb - 1), colblk))

        keys = 3 * BLOCK + c
        kern = functools.partial(_win_attn_kernel, tq=tq, nb=nb, nctx=c)
        in_specs = [smem, q_spec, main(kcol), prev(kcol), nxt(kcol), main(vcol), prev(vcol), nxt(vcol), kvc_spec,
                    pl.BlockSpec((ATTN_ROWS, 3 * BLOCK), lambda bi, i: (0, 0))]
        args = [sink, qkv, qkv, qkv, qkv, qkv, qkv, qkv, kv_ctx, _band_bias()]
        scratch = [pltpu.VMEM((nsub, N_KV_HEADS, HEAD_DIM, keys), BF16),
                   pltpu.VMEM((nsub, N_KV_HEADS, keys, 2 * HEAD_DIM), BF16),
                   pltpu.VMEM((2, ATTN_ROWS, keys), F32),
                   pltpu.VMEM((2, ATTN_ROWS, keys), BF16),
                   pltpu.VMEM((2, ATTN_ROWS, 1), F32)]
    else:
        kern = functools.partial(_ctx_attn_kernel, tq=tq)
        in_specs = [smem, q_spec, kvc_spec]
        args = [sink, qkv, kv_ctx]
        scratch = []
    return pl.pallas_call(
        kern,
        out_shape=jax.ShapeDtypeStruct((b, n, ATTN_WIDTH), BF16),
        grid=(b, n // tq),
        in_specs=in_specs,
        out_specs=pl.BlockSpec((None, tq, ATTN_WIDTH), lambda bi, i: (bi, i, 0)),
        scratch_shapes=scratch,
        compiler_params=_params(("parallel", "parallel")),
        name="window_attention" if local else "context_attention",
    )(*args)


GROUPS = FFT_N2 // SUBLANES


def _gm_shape(lead, t1, width):
    return (*lead, width // LANES, GROUPS, t1, SUBLANES, LANES)


def _store_group_major(o_ref, val, tl):
    for t1 in range(tl // FFT_N2):
        for jg in range(GROUPS):
            r0 = (t1 * GROUPS + jg) * SUBLANES
            for cc in range(val.shape[1] // LANES):
                o_ref[cc, jg, t1] = val[r0:r0 + SUBLANES, cc * LANES:(cc + 1) * LANES]


def _hyena_prep_kernel(u_ref, p_ref, n_ref, w_ref, b_ref, vx_ref, x0_ref, *, tl, nt, group_major):
    i = pl.program_id(1)
    u = u_ref[...]
    prev_row = jnp.where(i > 0, p_ref[SUBLANES - 1:SUBLANES, :], 0.0)
    next_row = jnp.where(i < nt - 1, n_ref[0:1, :], 0.0)
    row = lax.broadcasted_iota(jnp.int32, u.shape, 0)
    um = jnp.where(row == 0, prev_row, pltpu.roll(u, 1, 0))
    up = jnp.where(row == tl - 1, next_row, pltpu.roll(u, tl - 1, 0))
    z = um * w_ref[0:1, :] + u * w_ref[1:2, :] + up * w_ref[2:3, :] + b_ref[...]
    x0 = z[:, :HYENA_WIDTH]
    vx = z[:, 2 * HYENA_WIDTH:] * z[:, HYENA_WIDTH:2 * HYENA_WIDTH]
    if group_major:
        _store_group_major(vx_ref, vx, tl)
        _store_group_major(x0_ref, x0, tl)
    else:
        vx_ref[...] = vx
        x0_ref[...] = x0


def _hyena_prep(p, conv_w, conv_b, group_major):
    b, n, _ = p.shape
    tl = min(512, n)
    nt = n // tl
    hw = 3 * HYENA_WIDTH
    colblk = 0
    nrow8 = n // SUBLANES
    per = tl // SUBLANES
    if group_major:
        shape = _gm_shape((b,), n // FFT_N2, HYENA_WIDTH)
        out_spec = pl.BlockSpec((None, *_gm_shape((), tl // FFT_N2, HYENA_WIDTH)), lambda bi, i: (bi, 0, 0, i, 0, 0))
    else:
        shape = (b, n, HYENA_WIDTH)
        out_spec = pl.BlockSpec((None, tl, HYENA_WIDTH), lambda bi, i: (bi, i, 0))
    return pl.pallas_call(
        functools.partial(_hyena_prep_kernel, tl=tl, nt=nt, group_major=group_major),
        out_shape=[jax.ShapeDtypeStruct(shape, F32)] * 2,
        grid=(b, nt),
        in_specs=[
            pl.BlockSpec((None, tl, hw), lambda bi, i: (bi, i, colblk)),
            pl.BlockSpec((None, SUBLANES, hw), lambda bi, i: (bi, jnp.maximum(i * per - 1, 0), colblk)),
            pl.BlockSpec((None, SUBLANES, hw), lambda bi, i: (bi, jnp.minimum((i + 1) * per, nrow8 - 1), colblk)),
            pl.BlockSpec((3, hw), lambda bi, i: (0, 0)),
            pl.BlockSpec((1, hw), lambda bi, i: (0, 0)),
        ],
        out_specs=[out_spec, out_spec],
        compiler_params=_params(("parallel", "parallel")),
        name="hyena_prep",
    )(p, p, p, conv_w, conv_b.reshape(1, hw))


def _filter_kernel(ft_ref, w1_ref, b1_ref, f1_ref, w2_ref, b2_ref, f2_ref, w3_ref, dl_ref, h_ref, s_ref,
                   *, tl, n, group_major):
    i = pl.program_id(0)

    def dense(a, w_ref):
        ah, al = _split(a)
        wh, wl = _split(w_ref[...])
        return _dot3(ah, al, wh, wl)

    h = jnp.sin(f1_ref[...] * (dense(ft_ref[...], w1_ref) + b1_ref[...]))
    h = jnp.sin(f2_ref[...] * (dense(h, w2_ref) + b2_ref[...]))
    h = dense(h, w3_ref)
    hw2 = 2 * HYENA_WIDTH
    h = jnp.concatenate([h[:, :hw2], h[:, hw2:]], axis=0)
    t = (i * tl + lax.broadcasted_iota(jnp.int32, (tl, HYENA_WIDTH), 0)).astype(F32) / float(n - 1)
    decay = jnp.exp(-t * dl_ref[...])
    h = h * jnp.concatenate([decay, decay], axis=1)
    if group_major:
        _store_group_major(h_ref, h, tl)
    else:
        h_ref[...] = h

    @pl.when(i == 0)
    def _():
        s_ref[...] = jnp.zeros_like(s_ref)

    s_ref[...] += jnp.sum(jnp.abs(h).reshape(tl // SUBLANES, SUBLANES, 2 * HYENA_WIDTH), axis=0)


def _filter_features(n):
    t = jnp.linspace(0.0, 1.0, n, dtype=F32)[:, None]
    bands = (HYENA_EMB_DIM - 1) // 2
    omega = 2.0 * math.pi * jnp.arange(n, dtype=F32)[:, None] / n
    f = jnp.linspace(1e-4, bands - 1, bands, dtype=F32)[None, :]
    feats = jnp.concatenate([t, jnp.cos(f * omega), -jnp.sin(f * omega)], axis=-1)
    return jnp.pad(feats, ((0, 0), (0, HYENA_FILTER_HIDDEN - HYENA_EMB_DIM)))


def _filter_deltas():
    max_decay = math.log(HYENA_DECAY_TARGET) / HYENA_FAST_DECAY_PCT
    min_decay = math.log(HYENA_DECAY_TARGET) / HYENA_SLOW_DECAY_PCT
    return jnp.abs(jnp.linspace(min_decay, max_decay, HYENA_WIDTH, dtype=F32)).reshape(1, HYENA_WIDTH)


def _hyena_filter(n, feats, deltas, w1, b1, f1, w2, b2, f2, w3, group_major):
    tl = min(512, n)
    hh = HYENA_FILTER_HIDDEN
    hid = 2 * hh
    assert hid == LANES and HYENA_EMB_DIM <= hh
    hw2 = 2 * HYENA_WIDTH
    blockdiag = lambda w: jnp.concatenate(
        [jnp.concatenate([w, jnp.zeros_like(w)], axis=1), jnp.concatenate([jnp.zeros_like(w), w], axis=1)], axis=0)
    w1p = blockdiag(jnp.pad(w1, ((0, hh - HYENA_EMB_DIM), (0, 0))))
    w2p = blockdiag(w2)
    w3p = blockdiag(w3)
    vec = lambda v: jnp.tile(v, 2).reshape(1, hid)
    feats = feats.reshape(n // tl, 2, tl // 2, hh).transpose(0, 2, 1, 3).reshape(n // 2, hid)
    full = lambda shape: pl.BlockSpec(shape, lambda i: (0,) * len(shape))
    if group_major:
        shape = _gm_shape((), n // FFT_N2, hw2)
        out_spec = pl.BlockSpec(_gm_shape((), tl // FFT_N2, hw2), lambda i: (0, 0, i, 0, 0))
    else:
        shape = (n, hw2)
        out_spec = pl.BlockSpec((tl, hw2), lambda i: (i, 0))
    return pl.pallas_call(
        functools.partial(_filter_kernel, tl=tl, n=n, group_major=group_major),
        out_shape=[jax.ShapeDtypeStruct(shape, F32), jax.ShapeDtypeStruct((SUBLANES, hw2), F32)],
        grid=(n // tl,),
        in_specs=[pl.BlockSpec((tl // 2, hid), lambda i: (i, 0)), full((hid, hid)), full((1, hid)), full((1, hid)),
                  full((hid, hid)), full((1, hid)), full((1, hid)), full((hid, 2 * hw2)), full((1, HYENA_WIDTH))],
        out_specs=[out_spec, full((SUBLANES, hw2))],
        compiler_params=_params(("arbitrary",)),
        name="hyena_filter",
    )(feats, w1p, vec(b1), vec(f1), w2p, vec(b2), vec(f2), w3p, deltas)


def _stack_complex(m):
    return np.block([[m.real, -m.imag], [m.imag, m.real]])


def _hilo(m):
    m = jnp.asarray(m, dtype=F32)
    return _split(m)


@functools.lru_cache(maxsize=None)
def _fft_constants(n):
    m = 2 * n
    n2 = FFT_N2
    n1 = m // n2
    n1h = n1 // 2
    k1 = np.arange(n1)
    t1 = np.arange(n1h)
    f1 = np.exp(-2j * np.pi * np.outer(k1, t1) / n1)
    f3 = np.exp(2j * np.pi * np.outer(t1, k1) / n1) / m
    k2 = np.arange(n2)
    t2 = np.arange(n2)
    w2 = np.exp(-2j * np.pi * np.outer(k2, t2) / n2)
    tw = np.exp(-2j * np.pi * np.outer(k1, t2) / m)
    return dict(
        n1=n1, n1h=n1h,
        f1c=_stack_complex(f1), f1r=np.concatenate([f1.real, f1.imag], axis=0),
        f3c=_stack_complex(f3),
        w2r=w2.real.astype(np.float32), w2i=w2.imag.astype(np.float32),
        twr=tw.real.astype(np.float32), twi=tw.imag.astype(np.float32),
    )


def _stage2_tables(n):
    c = _fft_constants(n)
    w2r, w2i = jnp.asarray(c["w2r"])[None], jnp.asarray(c["w2i"])[None]
    twr, twi = jnp.asarray(c["twr"])[:, None, :], jnp.asarray(c["twi"])[:, None, :]
    return (*_split(w2r * twr - w2i * twi), *_split(w2r * twi + w2i * twr))


def _stacked_tables(grh_ref, grl_ref, gih_ref, gil_ref, q):
    def stack(gr, gi):
        return jnp.concatenate([jnp.concatenate([gr, -gi], axis=1), jnp.concatenate([gi, gr], axis=1)], axis=0)
    return stack(grh_ref[q], gih_ref[q]), stack(grl_ref[q], gil_ref[q])


def _s1_kernel(x_ref, fh_ref, fl_ref, o_ref, *, nparts, ncw, n1, n1h):
    fh = fh_ref[...]
    fl = fl_ref[...]
    for r in range(SUBLANES):
        rows = pl.ds(r, n1h, stride=SUBLANES)
        xs = jnp.concatenate(
            [jnp.concatenate([x_ref[p, cc, rows, :] for p in range(nparts)], axis=0) for cc in range(ncw)], axis=1)
        xh, xl = _split(xs)
        res = _dot3(fh, fl, xh, xl)
        for ri in range(2):
            for cc in range(ncw):
                o_ref[ri, cc, pl.ds(r, n1, stride=SUBLANES), :] = res[ri * n1:(ri + 1) * n1, cc * LANES:(cc + 1) * LANES]


def _fft_stage1(x, fmat, n1, n1h, ncw):
    nparts, ncc, groups = x.shape[:3]
    fh, fl = _hilo(fmat)
    return pl.pallas_call(
        functools.partial(_s1_kernel, nparts=nparts, ncw=ncw, n1=n1, n1h=n1h),
        out_shape=jax.ShapeDtypeStruct((2, ncc, groups, n1 * SUBLANES, LANES), F32),
        grid=(groups, ncc // ncw),
        in_specs=[
            pl.BlockSpec((nparts, ncw, None, n1h * SUBLANES, LANES), lambda j, ci: (0, ci, j, 0, 0)),
            pl.BlockSpec(fh.shape, lambda j, ci: (0, 0)),
            pl.BlockSpec(fl.shape, lambda j, ci: (0, 0)),
        ],
        out_specs=pl.BlockSpec((2, ncw, None, n1 * SUBLANES, LANES), lambda j, ci: (0, ci, j, 0, 0)),
        compiler_params=_params(("parallel", "parallel")),
        name="fft_stage1",
    )(x, fh, fl)


def _load_k1(a_ref, q):
    ncc = a_ref.shape[1]
    return jnp.concatenate([a_ref[:, cc, :, q].reshape(2 * FFT_N2, LANES) for cc in range(ncc)], axis=1)


def _filter_spectrum_kernel(a_ref, grh_ref, grl_ref, gih_ref, gil_ref, s_ref, o_ref, *, kg):
    s = jnp.sum(s_ref[...], axis=0, keepdims=True)
    inv = 1.0 / (s[:, :HYENA_WIDTH] + s[:, HYENA_WIDTH:])
    half = FFT_N2
    for q in range(kg):
        ah, al = _split(_load_k1(a_ref, q))
        gh, gl = _stacked_tables(grh_ref, grl_ref, gih_ref, gil_ref, q)
        h = _dot3(gh, gl, ah, al)
        hf = h[:, :HYENA_WIDTH]
        hb = h[:, HYENA_WIDTH:]
        o_ref[q, :half] = (hf[:half] + hb[:half]) * inv
        o_ref[q, half:] = (hf[half:] - hb[half:]) * inv


FFT_K1_GROUP = 8


def _k1_spec(kg, width):
    return pl.BlockSpec((2, width // LANES, GROUPS, kg, SUBLANES, LANES), lambda i: (0, 0, 0, i, 0, 0))


def _filter_spectrum(a, tables, sums, n1):
    kg = FFT_K1_GROUP
    a6 = a.reshape(2, 2 * HYENA_WIDTH // LANES, GROUPS, n1, SUBLANES, LANES)
    tspec = pl.BlockSpec((kg, FFT_N2, FFT_N2), lambda i: (i, 0, 0))
    return pl.pallas_call(
        functools.partial(_filter_spectrum_kernel, kg=kg),
        out_shape=jax.ShapeDtypeStruct((n1, 2 * FFT_N2, HYENA_WIDTH), F32),
        grid=(n1 // kg,),
        in_specs=[_k1_spec(kg, 2 * HYENA_WIDTH), tspec, tspec, tspec, tspec,
                  pl.BlockSpec((SUBLANES, 2 * HYENA_WIDTH), lambda i: (0, 0))],
        out_specs=pl.BlockSpec((kg, 2 * FFT_N2, HYENA_WIDTH), lambda i: (i, 0, 0)),
        compiler_params=_params(("parallel",)),
        name="filter_spectrum",
    )(a6, *tables, sums)


def _dot_t(a, b):
    return lax.dot_general(a, b, (((0,), (0,)), ((), ())), preferred_element_type=F32)


def _s2_kernel(a_ref, kf_ref, grh_ref, grl_ref, gih_ref, gil_ref, o_ref, *, kg):
    half = FFT_N2
    for q in range(kg):
        ah, al = _split(_load_k1(a_ref, q))
        gh, gl = _stacked_tables(grh_ref, grl_ref, gih_ref, gil_ref, q)
        x = _dot3(gh, gl, ah, al)
        xr, xi = x[:half], x[half:]
        kr, ki = kf_ref[q, :half], kf_ref[q, half:]
        y = jnp.concatenate([xr * kr - xi * ki, xr * ki + xi * kr], axis=0)
        yh, yl = _split(y)
        bt = _dot_t(gh, yh) + _dot_t(gh, yl) + _dot_t(gl, yh)
        for cc in range(HYENA_WIDTH // LANES):
            o_ref[:, cc, :, q] = bt[:, cc * LANES:(cc + 1) * LANES].reshape(2, GROUPS, SUBLANES, LANES)


def _fft_stage2(a, kf, tables, n1):
    kg = FFT_K1_GROUP
    a6 = a.reshape(2, HYENA_WIDTH // LANES, GROUPS, n1, SUBLANES, LANES)
    tspec = pl.BlockSpec((kg, FFT_N2, FFT_N2), lambda i: (i, 0, 0))
    dspec = _k1_spec(kg, HYENA_WIDTH)
    out = pl.pallas_call(
        functools.partial(_s2_kernel, kg=kg),
        out_shape=jax.ShapeDtypeStruct(a6.shape, F32),
        grid=(n1 // kg,),
        in_specs=[dspec, pl.BlockSpec((kg, 2 * FFT_N2, HYENA_WIDTH), lambda i: (i, 0, 0)), tspec, tspec, tspec, tspec],
        out_specs=dspec,
        compiler_params=_params(("parallel",)),
        name="fft_stage2",
    )(a6, kf, *tables)
    return out.reshape(a.shape)


def _s3_kernel(b_ref, vx_ref, x0_ref, bias_ref, fh_ref, fl_ref, o_ref, *, ncw, n1, n1h):
    fh = fh_ref[...]
    fl = fl_ref[...]
    for r in range(SUBLANES):
        krows = pl.ds(r, n1, stride=SUBLANES)
        z = jnp.concatenate(
            [jnp.concatenate([b_ref[ri, cc, krows, :] for ri in range(2)], axis=0) for cc in range(ncw)], axis=1)
        zh, zl = _split(z)
        y = _dot3(fh, fl, zh, zl)
        trows = pl.ds(r, n1h, stride=SUBLANES)
        for p in range(2):
            for cc in range(ncw):
                yy = y[p * n1h:(p + 1) * n1h, cc * LANES:(cc + 1) * LANES]
                o_ref[p, cc, trows, :] = (yy + vx_ref[p, cc, trows, :] * bias_ref[cc]) * x0_ref[p, cc, trows, :]


def _fft_stage3(bt, vx, x0, bias, fmat, n1, n1h, ncw):
    ncc, groups = bt.shape[1:3]
    fh, fl = _hilo(fmat)
    tspec = pl.BlockSpec((2, ncw, None, n1h * SUBLANES, LANES), lambda j, ci: (0, ci, j, 0, 0))
    return pl.pallas_call(
        functools.partial(_s3_kernel, ncw=ncw, n1=n1, n1h=n1h),
        out_shape=jax.ShapeDtypeStruct(vx.shape, F32),
        grid=(groups, ncc // ncw),
        in_specs=[
            pl.BlockSpec((2, ncw, None, n1 * SUBLANES, LANES), lambda j, ci: (0, ci, j, 0, 0)),
            tspec, tspec,
            pl.BlockSpec((ncw, 1, LANES), lambda j, ci: (ci, 0, 0)),
            pl.BlockSpec(fh.shape, lambda j, ci: (0, 0)),
            pl.BlockSpec(fl.shape, lambda j, ci: (0, 0)),
        ],
        out_specs=tspec,
        compiler_params=_params(("parallel", "parallel")),
        name="fft_stage3",
    )(bt, vx, x0, bias, fh, fl)


def _hyena_latent(p, hy, feats, deltas, tables, n):
    conv_w, conv_b, w1, b1, f1, w2, b2, f2, w3, bias = hy
    c = _fft_constants(n)
    n1, n1h = c["n1"], c["n1h"]
    vx, x0 = _hyena_prep(p, conv_w, conv_b, group_major=True)
    gm_shape = vx.shape
    rows = lambda a: a.reshape(*a.shape[:-3], n1h * SUBLANES, LANES)
    vx, x0 = rows(vx), rows(x0)
    taps, sums = _hyena_filter(n, feats, deltas, w1, b1, f1, w2, b2, f2, w3, group_major=True)
    a_f = _fft_stage1(rows(taps)[None], c["f1r"], n1, n1h, ncw=2)
    kf = _filter_spectrum(a_f, tables, sums, n1)
    a = _fft_stage1(vx, c["f1c"], n1, n1h, ncw=2)
    bt = _fft_stage2(a, kf, tables, n1)
    out = _fft_stage3(bt, vx, x0, bias.reshape(HYENA_WIDTH // LANES, 1, LANES), c["f3c"], n1, n1h, ncw=2)
    return out.reshape(gm_shape)


@functools.lru_cache(maxsize=None)
def _dense_dft_constants(n):
    m = 2 * n
    k = np.arange(m)
    t = np.arange(n)
    f = np.exp(-2j * np.pi * np.outer(k, t) / m)
    finv = np.exp(2j * np.pi * np.outer(t, k) / m) / m
    return _stack_complex(f), np.concatenate([f.real, f.imag], axis=0), _stack_complex(finv)


def _dense_conv_kernel(vx_ref, x0_ref, hf_ref, hb_ref, sf_ref, sb_ref, bias_ref,
                       fch, fcl, frh, frl, fih, fil, o_ref, *, n):
    m = 2 * n
    z = jnp.concatenate([vx_ref[0], vx_ref[1]], axis=0)
    zh, zl = _split(z)
    zf = _dot3(fch[...], fcl[...], zh, zl)
    hfh, hfl = _split(hf_ref[...])
    hbh, hbl = _split(hb_ref[...])
    hf = _dot3(frh[...], frl[...], hfh, hfl)
    hb = _dot3(frh[...], frl[...], hbh, hbl)
    inv = 1.0 / (jnp.sum(sf_ref[...], axis=0, keepdims=True) + jnp.sum(sb_ref[...], axis=0, keepdims=True))
    kr = (hf[:m] + hb[:m]) * inv
    ki = (hf[m:] - hb[m:]) * inv
    zr, zi = zf[:m], zf[m:]
    y = jnp.concatenate([zr * kr - zi * ki, zr * ki + zi * kr], axis=0)
    yh, yl = _split(y)
    out = _dot3(fih[...], fil[...], yh, yl)
    bias = bias_ref[...]
    for p in range(2):
        o_ref[p] = (out[p * n:(p + 1) * n] + vx_ref[p] * bias) * x0_ref[p]


def _hyena_context(p, hy, feats, deltas, n):
    conv_w, conv_b, w1, b1, f1, w2, b2, f2, w3, bias = hy
    vx, x0 = _hyena_prep(p, conv_w, conv_b, group_major=False)
    taps, sums = _hyena_filter(n, feats, deltas, w1, b1, f1, w2, b2, f2, w3, group_major=False)
    fc, fr, fi = _dense_dft_constants(n)
    mats = [*_hilo(fc), *_hilo(fr), *_hilo(fi)]
    cw = 256
    nct = HYENA_WIDTH // cw
    dspec = pl.BlockSpec((2, n, cw), lambda ci: (0, 0, ci))
    return pl.pallas_call(
        functools.partial(_dense_conv_kernel, n=n),
        out_shape=jax.ShapeDtypeStruct(vx.shape, F32),
        grid=(nct,),
        in_specs=[dspec, dspec,
                  pl.BlockSpec((n, cw), lambda ci: (0, ci)), pl.BlockSpec((n, cw), lambda ci: (0, nct + ci)),
                  pl.BlockSpec((SUBLANES, cw), lambda ci: (0, ci)), pl.BlockSpec((SUBLANES, cw), lambda ci: (0, nct + ci)),
                  pl.BlockSpec((1, cw), lambda ci: (0, ci))]
                 + [pl.BlockSpec(mt.shape, lambda ci: (0, 0)) for mt in mats],
        out_specs=dspec,
        compiler_params=_params(("parallel",)),
        name="context_long_conv",
    )(vx, x0, taps, taps, sums, sums, bias.reshape(1, HYENA_WIDTH), *mats)


def _pool_kernel(x_ref, p_ref, n_ref, w_ref, sc_ref, o_ref, *, tl, nt, n):
    i = pl.program_id(1)
    x = x_ref[...]
    pv = jnp.where(i > 0, p_ref[...], 0.0)
    nx = jnp.where(i < nt - 1, n_ref[...], 0.0)
    ext = jnp.concatenate([pv, x, nx], axis=0)
    rows = tl + 2 * POOL_HALO
    t = i * tl + lax.broadcasted_iota(jnp.int32, (tl, POOL_GROUP), 0)
    for g, w in enumerate(POOL_WINDOWS):
        lanes = slice(g * POOL_GROUP, (g + 1) * POOL_GROUP)
        a = ext[:, lanes]
        c = a + pltpu.roll(a, 1, 0)
        h = 1
        while 2 * h < w:
            c = pltpu.roll(c, h, 0) + pltpu.roll(c, rows - h, 0)
            h *= 2
        total = c[POOL_HALO:POOL_HALO + tl]
        count = (jnp.minimum(t + h, n) - jnp.maximum(t - h, 0)).astype(F32)
        y = (total / count - x[:, lanes]).astype(BF16)
        o_ref[:, lanes] = (_dot(y, w_ref[g].astype(BF16)) * sc_ref[:, lanes]).astype(BF16)


def _pool_mixer(p, w_pool, scale):
    b, n, _ = p.shape
    tl = min(512, n)
    nt = n // tl
    colblk = 3 * HYENA_WIDTH // POOL_WIDTH
    assert colblk * POOL_WIDTH == 3 * HYENA_WIDTH
    per = tl // POOL_HALO
    nrow = n // POOL_HALO
    return pl.pallas_call(
        functools.partial(_pool_kernel, tl=tl, nt=nt, n=n),
        out_shape=jax.ShapeDtypeStruct((b, n, POOL_WIDTH), BF16),
        grid=(b, nt),
        in_specs=[
            pl.BlockSpec((None, tl, POOL_WIDTH), lambda bi, i: (bi, i, colblk)),
            pl.BlockSpec((None, POOL_HALO, POOL_WIDTH), lambda bi, i: (bi, jnp.maximum(i * per - 1, 0), colblk)),
            pl.BlockSpec((None, POOL_HALO, POOL_WIDTH), lambda bi, i: (bi, jnp.minimum((i + 1) * per, nrow - 1), colblk)),
            pl.BlockSpec(w_pool.shape, lambda bi, i: (0, 0, 0)),
            pl.BlockSpec((1, POOL_WIDTH), lambda bi, i: (0, 0)),
        ],
        out_specs=pl.BlockSpec((None, tl, POOL_WIDTH), lambda bi, i: (bi, i, 0)),
        compiler_params=_params(("parallel", "parallel")),
        name="pool_mixer",
    )(p, p, p, w_pool, scale.reshape(1, POOL_WIDTH))


def _load_group_major(ref, tl):
    ncc = ref.shape[0]
    return jnp.concatenate(
        [jnp.concatenate([ref[cc, jg, t1] for cc in range(ncc)], axis=1)
         for t1 in range(tl // FFT_N2) for jg in range(GROUPS)], axis=0)


def _outproj_kernel(at_ref, hy_ref, po_ref, x_ref, gb_ref, w_ref, gp_ref, gt_ref, o_ref, ox_ref, m_ref,
                    *, tm, tiles, group_major):
    t = pl.program_id(0)

    @pl.when(t == 0)
    def _():
        ox_ref[...] = jnp.zeros_like(ox_ref)

    def finish_previous():
        o_ref[...] = x_ref[...] + gt_ref[...] * _rms(ox_ref[...], gp_ref[...])

    @pl.when(t < tiles)
    def _():
        finish_previous()
        hy = _load_group_major(hy_ref, tm) if group_major else hy_ref[...]
        a0, a1 = ATTN_WIDTH, ATTN_WIDTH + HYENA_WIDTH
        m_ref[:, :a0] = _rms(at_ref[...].astype(F32), gb_ref[:, :a0]).astype(BF16)
        m_ref[:, a0:a1] = _rms(hy, gb_ref[:, a0:a1]).astype(BF16)
        m_ref[:, a1:] = _rms(po_ref[...].astype(F32), gb_ref[:, a1:]).astype(BF16)
        ox_ref[...] = _dot(m_ref[...], w_ref[...])

    pl.when(t == tiles)(finish_previous)


def _out_projection(attn, hy, po, x, g_branch, w_out, layer, g_post, gate, group_major):
    b, n, d = x.shape
    tm = min(512, n)
    nt = n // tm
    tiles = b * nt

    def cur(t):
        return jnp.minimum(t, tiles - 1)

    def prev(t):
        return jnp.maximum(t - 1, 0)

    row = lambda width, sel: pl.BlockSpec((None, tm, width), lambda t: (sel(t) // nt, sel(t) % nt, 0))
    vec = lambda width: pl.BlockSpec((1, width), lambda t: (0, 0))
    if group_major:
        hy_spec = pl.BlockSpec((None, *_gm_shape((), tm // FFT_N2, HYENA_WIDTH)),
                               lambda t: (cur(t) // nt, 0, 0, cur(t) % nt, 0, 0))
    else:
        hy_spec = row(HYENA_WIDTH, cur)
    return pl.pallas_call(
        functools.partial(_outproj_kernel, tm=tm, tiles=tiles, group_major=group_major),
        out_shape=jax.ShapeDtypeStruct(x.shape, F32),
        grid=(tiles + 1,),
        in_specs=[row(ATTN_WIDTH, cur), hy_spec, row(POOL_WIDTH, cur), row(d, prev), vec(d),
                  pl.BlockSpec((None, *w_out.shape[1:]), lambda t: (layer, 0, 0), pipeline_mode=pl.Buffered(1)), vec(d),
                  pl.BlockSpec((None, 1, d), lambda t: (prev(t) // nt, 0, 0))],
        out_specs=row(d, prev),
        scratch_shapes=[pltpu.VMEM((tm, d), F32), pltpu.VMEM((tm, d), BF16)],
        compiler_params=_params(("arbitrary",)),
        name="out_projection",
    )(attn, hy, po, x, g_branch.reshape(1, d), w_out, g_post.reshape(1, d), gate)


def _mlp_kernel(xe_ref, xp_ref, g_ref, shn_ref, scn_ref, wu_ref, wd_ref, gp_ref, gtp_ref, o_ref,
                h0_ref, h1_ref, acc0_ref, acc1_ref, *, rc, tiles):
    t = pl.program_id(0)
    k = pl.program_id(1)
    rows = pl.ds(pl.multiple_of(k * rc, rc), rc)

    def normalise(h_ref):
        a = g_ref[...] * (1.0 + scn_ref[...])
        h_ref[rows, :] = (_rms(xp_ref[...], a) + shn_ref[...]).astype(BF16)

    def finish(acc_ref):
        o_ref[...] = xe_ref[...] + _rms(acc_ref[rows, :], gp_ref[...] * gtp_ref[...])

    @pl.when(t == 0)
    def _():
        normalise(h0_ref)
        acc0_ref[rows, :] = jnp.zeros((rc, acc0_ref.shape[1]), F32)
        acc1_ref[rows, :] = jnp.zeros((rc, acc1_ref.shape[1]), F32)

    def step(slot):
        h_cur, h_oth = (h0_ref, h1_ref) if slot == 0 else (h1_ref, h0_ref)
        acc_cur, acc_oth = (acc0_ref, acc1_ref) if slot == 0 else (acc1_ref, acc0_ref)
        finish(acc_oth)
        acc_oth[rows, :] = jnp.zeros((rc, acc_oth.shape[1]), F32)
        normalise(h_oth)
        u = jnp.maximum(_dot(h_cur[...], wu_ref[...]), 0.0)
        acc_cur[...] += _dot((u * u).astype(BF16), wd_ref[...])

    work = (t >= 1) & (t <= tiles)
    pl.when(work & (t % 2 == 1))(functools.partial(step, 0))
    pl.when(work & (t % 2 == 0))(functools.partial(step, 1))
    pl.when(t == tiles + 1)(functools.partial(finish, acc0_ref if (tiles - 1) % 2 == 0 else acc1_ref))


MLP_TH = 1024


def _mlp(x, g_pre, shift, scale, w_up, w_down, layer, g_post, gate):
    b, n, d = x.shape
    th = MLP_TH
    nk = w_up.shape[2] // th
    tm = min(1024, n)
    nt = n // tm
    tiles = b * nt
    rc = tm // nk
    assert rc * nk == tm and rc % BF16_ROWS == 0

    def norm_tile(t):
        return jnp.minimum(t, tiles - 1)

    def done_tile(t):
        return jnp.clip(t - 2, 0, tiles - 1)

    vec = pl.BlockSpec((1, d), lambda t, k: (0, 0))
    bvec = lambda sel: pl.BlockSpec((None, 1, d), lambda t, k: (sel(t) // nt, 0, 0))
    chunk = lambda sel, first: pl.BlockSpec(
        (None, rc, d), lambda t, k: (sel(t) // nt, (sel(t) % nt) * nk + jnp.where(t < first, 0, k), 0))
    busy = lambda t: (t >= 1) & (t <= tiles)
    return pl.pallas_call(
        functools.partial(_mlp_kernel, rc=rc, tiles=tiles),
        out_shape=jax.ShapeDtypeStruct(x.shape, F32),
        grid=(tiles + 2, nk),
        in_specs=[chunk(done_tile, 2), chunk(norm_tile, 0), vec, bvec(norm_tile), bvec(norm_tile),
                  pl.BlockSpec((None, d, th), lambda t, k: (layer, 0, jnp.where(busy(t), k, 0))),
                  pl.BlockSpec((None, th, d), lambda t, k: (layer, jnp.where(busy(t), k, 0), 0)),
                  vec, bvec(done_tile)],
        out_specs=chunk(done_tile, 2),
        scratch_shapes=[pltpu.VMEM((tm, d), BF16)] * 2 + [pltpu.VMEM((tm, d), F32)] * 2,
        compiler_params=_params(("arbitrary", "arbitrary")),
        name="mlp",
    )(x, x, g_pre.reshape(1, d), shift, scale, w_up, w_down, g_post.reshape(1, d), gate)


def kernel(x, c, ctx, c_ctx, w_mod, b_mod, g_pre_mix, g_post_mix, g_pre_mlp, g_post_mlp, w_in, w_out, g_branch,
           attn_sink, hy_conv_w, hy_conv_b, hy_w1, hy_b1, hy_freq1, hy_w2, hy_b2, hy_freq2, hy_w3, hy_bias,
           pool_w, pool_scale, w_up, w_down):
    b, n, d = x.shape
    n_ctx = ctx.shape[1]
    depth = w_mod.shape[0]
    assert b == 2 and d == D_MODEL and n % 512 == 0 and n_ctx % BLOCK == 0

    cond = jnp.concatenate([c, c_ctx[None], jnp.zeros((SUBLANES - b - 1, d), F32)], axis=0)
    mods = _modulation(cond, w_mod, b_mod)

    w_in_b = _prepare_w_in(w_in, IN_TN)
    w_up_b = w_up.astype(BF16)
    w_out_b, w_down_b = w_out.astype(BF16), w_down.astype(BF16)
    tables_x, tables_c = _rope_tables(n), _rope_tables(n_ctx)
    feats_x, feats_c = _filter_features(n), _filter_features(n_ctx)
    deltas = _filter_deltas()
    tables = _stage2_tables(n)

    for i in range(depth):
        last = i == depth - 1
        hy = (hy_conv_w[i], hy_conv_b[i], hy_w1[i], hy_b1[i], hy_freq1[i], hy_w2[i], hy_b2[i], hy_freq2[i],
              hy_w3[i], hy_bias[i])
        mx = [m[:, None, :] for m in jnp.split(mods[i, :b], N_MOD, axis=-1)]
        mc = [jnp.broadcast_to(m[None, None, :], (b, 1, d)) for m in jnp.split(mods[i, b], N_MOD, axis=-1)]

        qkv_x, px = _in_projection(x, g_pre_mix[i], mx[0], mx[1], w_in_b, i, tables_x, rope=True)
        qkv_c, pc = _in_projection(ctx, g_pre_mix[i], mc[0], mc[1], w_in_b, i, tables_c, rope=False)
        kv_ctx = qkv_c[..., Q_END:V_END]

        attn_x = _attention(qkv_x, kv_ctx, attn_sink[i], local=True)
        hy_x = _hyena_latent(px, hy, feats_x, deltas, tables, n)
        po_x = _pool_mixer(px, pool_w[i], pool_scale[i])
        x = _out_projection(attn_x, hy_x, po_x, x, g_branch[i], w_out_b, i, g_post_mix[i], mx[2], group_major=True)
        x = _mlp(x, g_pre_mlp[i], mx[3], mx[4], w_up_b, w_down_b, i, g_post_mlp[i], mx[5])

        if not last:
            attn_c = _attention(qkv_c, kv_ctx, attn_sink[i], local=False)
            hy_c = _hyena_context(pc, hy, feats_c, deltas, n_ctx)
            po_c = _pool_mixer(pc, pool_w[i], pool_scale[i])
            ctx = _out_projection(attn_c, hy_c, po_c, ctx, g_branch[i], w_out_b, i, g_post_mix[i], mc[2],
                                  group_major=False)
            ctx = _mlp(ctx, g_pre_mlp[i], mc[3], mc[4], w_up_b, w_down_b, i, g_post_mlp[i], mc[5])
    return x
```

```python
import functools
import math

import numpy as np
import jax
import jax.numpy as jnp
from jax import lax
from jax.experimental import pallas as pl
from jax.experimental.pallas import tpu as pltpu

F32 = jnp.float32
BF16 = jnp.bfloat16

D_MODEL = 2048
DEPTH = 4
GRID_W = 64
ATTN_WIDTH = D_MODEL // 2
HYENA_WIDTH = D_MODEL // 4
POOL_WIDTH = D_MODEL - ATTN_WIDTH - HYENA_WIDTH
HEAD_DIM = 128
N_HEADS = ATTN_WIDTH // HEAD_DIM
N_KV_HEADS = 2
KV_GROUP = N_HEADS // N_KV_HEADS
KV_WIDTH = N_KV_HEADS * HEAD_DIM
WINDOW = 128
BLOCK = 128
ROPE_BASE = 10000.0
HYENA_EMB_DIM = 33
HYENA_FILTER_HIDDEN = 64
HYENA_FAST_DECAY_PCT = 0.3
HYENA_SLOW_DECAY_PCT = 1.5
HYENA_DECAY_TARGET = 1e-2
POOL_WINDOWS = (2, 4, 8, 16)
POOL_GROUP = POOL_WIDTH // len(POOL_WINDOWS)
MLP_HIDDEN = 4 * D_MODEL
N_MOD = 6
EPS = 1e-6
NEG_INF = -1e30

Q_END = ATTN_WIDTH
K_END = Q_END + KV_WIDTH
V_END = K_END + KV_WIDTH
HY_END = V_END + 3 * HYENA_WIDTH
IN_WIDTH = HY_END + POOL_WIDTH

LANES = 128
SUBLANES = 8
FFT_N2 = 128
POOL_HALO = 16
VMEM_LIMIT = 56 * 1024 * 1024


def _params(sem, vmem=VMEM_LIMIT):
    return pltpu.CompilerParams(dimension_semantics=sem, vmem_limit_bytes=vmem)


def _split(x):
    hi = x.astype(BF16)
    lo = (x - hi.astype(F32)).astype(BF16)
    return hi, lo


def _dot(a, b):
    return jnp.dot(a, b, preferred_element_type=F32)


def _dot3(ah, al, bh, bl):
    return _dot(ah, bh) + _dot(ah, bl) + _dot(al, bh)


def _rms(x, g):
    return x * lax.rsqrt(jnp.mean(x * x, axis=-1, keepdims=True) + EPS) * g


BF16_ROWS = 2 * SUBLANES


def _mod_kernel(c_ref, w_ref, b_ref, o_ref):
    c = c_ref[...]
    s = c / (1.0 + jnp.exp(-c))
    sh, sl = _split(s)
    wh, wl = _split(w_ref[...])
    o_ref[...] = _dot3(sh, sl, wh, wl) + b_ref[...]


def _modulation(cond, w_mod, b_mod):
    depth, d, width = w_mod.shape
    tn = 1024
    return pl.pallas_call(
        _mod_kernel,
        out_shape=jax.ShapeDtypeStruct((depth, SUBLANES, width), F32),
        grid=(depth, width // tn),
        in_specs=[
            pl.BlockSpec((SUBLANES, d), lambda l, j: (0, 0)),
            pl.BlockSpec((None, d, tn), lambda l, j: (l, 0, j)),
            pl.BlockSpec((None, 1, tn), lambda l, j: (l, 0, j)),
        ],
        out_specs=pl.BlockSpec((None, SUBLANES, tn), lambda l, j: (l, 0, j)),
        compiler_params=_params(("parallel", "parallel")),
        name="modulation",
    )(cond, w_mod, b_mod.reshape(depth, 1, width))


QKV_WIDTH = V_END
REST_WIDTH = IN_WIDTH - V_END
SM_SCALE = HEAD_DIM ** -0.5


TABLE_Q, TABLE_K, TABLE_ID, TABLE_SCALE = range(4)


def _inproj_kernel(xn_ref, g_ref, shn_ref, scn_ref, w_ref, ca_ref, sa_ref, cb_ref, sb_ref,
                   qkv_ref, rest_ref, h0_ref, h1_ref, acc_ref, *, tm, tn, rc, tiles):
    t = pl.program_id(0)
    j = pl.program_id(1)

    def normalise(h_ref):
        rows = pl.ds(pl.multiple_of(jnp.minimum(j, tm // rc - 1) * rc, rc), rc)
        a = g_ref[...] * (1.0 + scn_ref[...])
        h_ref[rows, :] = (_rms(xn_ref[...], a) + shn_ref[...]).astype(BF16)

    def finish_previous():
        acc = acc_ref[...]
        rest_ref[...] = acc
        nch = tn // LANES
        for ch in range(nch):
            a = acc[:, ch * LANES:(ch + 1) * LANES]
            cos, sin = (ca_ref, sa_ref) if ch < nch // 2 else (cb_ref, sb_ref)
            qkv_ref[:, ch * LANES:(ch + 1) * LANES] = (a * cos[...] + pltpu.roll(a, HEAD_DIM // 2, 1) * sin[...]).astype(BF16)

    @pl.when(t == 0)
    def _():
        normalise(h0_ref)

    @pl.when((t == 0) & (j == 0))
    def _():
        acc_ref[...] = jnp.zeros_like(acc_ref)

    def step(slot):
        h_cur, h_oth = (h0_ref, h1_ref) if slot == 0 else (h1_ref, h0_ref)
        finish_previous()
        normalise(h_oth)
        acc_ref[...] = _dot(h_cur[...], w_ref[j])

    work = (t >= 1) & (t <= tiles)
    pl.when(work & (t % 2 == 1))(functools.partial(step, 0))
    pl.when(work & (t % 2 == 0))(functools.partial(step, 1))
    pl.when((t == tiles + 1) & (j == 0))(finish_previous)


IN_TN = 512
QKV_SPARE = QKV_WIDTH // IN_TN
REST_SPARE = REST_WIDTH // IN_TN


def _in_projection(x, g, shift, scale, w, layer, tables, rope):
    b, n, d = x.shape
    tm = min(1024, n)
    tn = w.shape[-1]
    nj = w.shape[1]
    nt = n // tm
    tiles = b * nt
    half = tn // 2
    assert tn == IN_TN and nj * tn == IN_WIDTH and Q_END % tn == 0 and K_END % tn == half and V_END % tn == 0
    nslices = 4
    rc = tm // nslices
    assert nslices <= nj and rc % BF16_ROWS == 0
    nq, nqkv = Q_END // tn, QKV_WIDTH // tn

    def norm_tile(t):
        return jnp.minimum(t, tiles - 1)

    def lagged(t, j):
        step = jnp.clip((t - 1) * nj + j - 1, 0, tiles * nj - 1)
        return step // nj, step % nj

    def kind_a(j):
        return jnp.where(j < nq, TABLE_Q if rope else TABLE_SCALE, jnp.where(j < nqkv, TABLE_K if rope else TABLE_ID, TABLE_ID))

    def kind_b(j):
        return jnp.where(j < nq, TABLE_Q if rope else TABLE_SCALE, TABLE_ID)

    vec = pl.BlockSpec((1, d), lambda t, j: (0, 0))
    bvec = lambda sel: pl.BlockSpec((None, 1, d), lambda t, j: (sel(t) // nt, 0, 0))
    def table(kind):
        def index(t, j):
            tp, jp = lagged(t, j)
            k = kind(jp)
            return k, jnp.where(k == TABLE_ID, 0, tp % nt), 0
        return pl.BlockSpec((None, tm, LANES), index)

    def out_spec(col_tile):
        return pl.BlockSpec((None, tm, tn), lambda t, j: (lagged(t, j)[0] // nt, lagged(t, j)[0] % nt,
                                                          col_tile(lagged(t, j)[1])))

    cos, sin = tables
    return pl.pallas_call(
        functools.partial(_inproj_kernel, tm=tm, tn=tn, rc=rc, tiles=tiles),
        out_shape=[jax.ShapeDtypeStruct((b, n, QKV_WIDTH + tn), BF16), jax.ShapeDtypeStruct((b, n, REST_WIDTH + tn), F32)],
        grid=(tiles + 2, nj),
        in_specs=[pl.BlockSpec((None, rc, d), lambda t, j: (norm_tile(t) // nt, (norm_tile(t) % nt) * nslices
                                                            + jnp.minimum(j, nslices - 1), 0)),
                  vec, bvec(norm_tile), bvec(norm_tile),
                  pl.BlockSpec((None, *w.shape[1:]), lambda t, j: (layer, 0, 0, 0), pipeline_mode=pl.Buffered(1)),
                  table(kind_a), table(kind_a), table(kind_b), table(kind_b)],
        out_specs=[out_spec(lambda jp: jnp.minimum(jp, QKV_SPARE)),
                   out_spec(lambda jp: jnp.where(jp >= nqkv, jp - nqkv, REST_SPARE))],
        scratch_shapes=[pltpu.VMEM((tm, d), BF16)] * 2 + [pltpu.VMEM((tm, tn), F32)],
        compiler_params=_params(("arbitrary", "arbitrary")),
        name="in_projection_rope" if rope else "in_projection",
    )(x, g.reshape(1, d), shift, scale, w, cos, sin, cos, sin)


def _prep_w_in_kernel(w_ref, o_ref, *, tn):
    j = pl.program_id(1)
    quarter = HEAD_DIM // 4
    nch = tn // LANES

    def write(n_swapped):
        for ch in range(nch):
            a = w_ref[:, ch * LANES:(ch + 1) * LANES]
            if ch < n_swapped:
                q = lax.broadcasted_iota(jnp.int32, a.shape, 1) // quarter
                a = jnp.where(q == 1, pltpu.roll(a, LANES - quarter, 1), jnp.where(q == 2, pltpu.roll(a, quarter, 1), a))
            o_ref[:, ch * LANES:(ch + 1) * LANES] = a.astype(BF16)

    nfull, nrem = divmod(K_END // LANES, nch)
    pl.when(j < nfull)(functools.partial(write, nch))
    pl.when(j == nfull)(functools.partial(write, nrem))
    pl.when(j > nfull)(functools.partial(write, 0))


def _prepare_w_in(w_in, tn):
    depth, d, width = w_in.shape
    return pl.pallas_call(
        functools.partial(_prep_w_in_kernel, tn=tn),
        out_shape=jax.ShapeDtypeStruct((depth, width // tn, d, tn), BF16),
        grid=(depth, width // tn),
        in_specs=[pl.BlockSpec((None, d, tn), lambda l, j: (l, 0, j))],
        out_specs=pl.BlockSpec((None, None, d, tn), lambda l, j: (l, j, 0, 0)),
        compiler_params=_params(("parallel", "parallel")),
        name="prepare_w_in",
    )(w_in)


def _rope_tables(n):
    quarter = HEAD_DIM // 4
    inv_freq = ROPE_BASE ** (-jnp.arange(quarter, dtype=F32) / quarter)
    grid_rows = n // GRID_W
    row = jnp.arange(grid_rows, dtype=F32)[:, None] * inv_freq[None, :]
    col = jnp.arange(GRID_W, dtype=F32)[:, None] * inv_freq[None, :]
    by_row = lambda a: jnp.repeat(a, GRID_W, axis=0)
    by_col = lambda a: jnp.tile(a, (grid_rows, 1))
    cos_r, cos_c, sin_r, sin_c = by_row(jnp.cos(row)), by_col(jnp.cos(col)), by_row(jnp.sin(row)), by_col(jnp.sin(col))
    cos = jnp.concatenate([cos_r, cos_c, cos_r, cos_c], axis=-1)
    sin = jnp.concatenate([-sin_r, -sin_c, sin_r, sin_c], axis=-1)
    one, zero = jnp.ones_like(cos), jnp.zeros_like(sin)
    return (jnp.stack([cos * SM_SCALE, cos, one, one * SM_SCALE]), jnp.stack([sin * SM_SCALE, sin, zero, zero]))


def _softmax_pv(parts, sink_col):
    m = sink_col
    for s, _ in parts:
        m = jnp.maximum(m, jnp.max(s, axis=-1, keepdims=True))
    den = jnp.exp(sink_col - m)
    out = None
    for s, v in parts:
        p = jnp.exp(s - m)
        den = den + jnp.sum(p, axis=-1, keepdims=True)
        pv = _dot(p.astype(BF16), v)
        out = pv if out is None else out + pv
    return out / den


def _ctx_attn_kernel(sink_ref, q_ref, kvc_ref, o_ref, *, tq):
    for g in range(N_KV_HEADS):
        heads = [g * KV_GROUP + h for h in range(KV_GROUP)]
        qg = jnp.concatenate([q_ref[:, h * HEAD_DIM:(h + 1) * HEAD_DIM] for h in heads], axis=0)
        sink_col = jnp.concatenate([jnp.full((tq, 1), sink_ref[h], F32) for h in heads], axis=0)
        kc = kvc_ref[:, g * HEAD_DIM:(g + 1) * HEAD_DIM]
        vc = kvc_ref[:, KV_WIDTH + g * HEAD_DIM:KV_WIDTH + (g + 1) * HEAD_DIM]
        s = lax.dot_general(qg, kc, (((1,), (1,)), ((), ())), preferred_element_type=F32)
        o = _softmax_pv([(s, vc)], sink_col)
        for hi, h in enumerate(heads):
            o_ref[:, h * HEAD_DIM:(h + 1) * HEAD_DIM] = o[hi * tq:(hi + 1) * tq]


ATTN_ROWS = KV_GROUP * BLOCK
SOFTMAX_CHUNK = 32


def _win_attn_kernel(sink_ref, q_ref, km_ref, kp_ref, kn_ref, vm_ref, vp_ref, vn_ref, kvc_ref, band_ref,
                     o_ref, ktw, vw, s_scr, p_scr, m_scr, *, tq, nb, nctx):
    i = pl.program_id(1)
    nsub = tq // BLOCK
    nloc = 3 * BLOCK

    def transposed(x):
        return x.astype(F32).T.astype(BF16)

    def block_rows(main_ref, prev_ref, next_ref, w, lanes):
        if w == 0:
            return prev_ref[:, lanes]
        if w == nsub + 1:
            return next_ref[:, lanes]
        return main_ref[(w - 1) * BLOCK:w * BLOCK, lanes]

    @pl.when(i == 0)
    def _():
        ones = jnp.ones((nloc + nctx, HEAD_DIM), BF16)
        for g in range(N_KV_HEADS):
            lanes = slice(g * HEAD_DIM, (g + 1) * HEAD_DIM)
            vlanes = slice(KV_WIDTH + g * HEAD_DIM, KV_WIDTH + (g + 1) * HEAD_DIM)
            kct = [transposed(kvc_ref[cb * BLOCK:(cb + 1) * BLOCK, lanes]) for cb in range(nctx // BLOCK)]
            for jb in range(nsub):
                for cb in range(nctx // BLOCK):
                    ktw[jb, g, :, nloc + cb * BLOCK:nloc + (cb + 1) * BLOCK] = kct[cb]
                vw[jb, g, nloc:, :HEAD_DIM] = kvc_ref[:, vlanes]
                vw[jb, g, :, HEAD_DIM:] = ones

    for g in range(N_KV_HEADS):
        lanes = slice(g * HEAD_DIM, (g + 1) * HEAD_DIM)
        kts = [transposed(block_rows(km_ref, kp_ref, kn_ref, w, lanes)) for w in range(nsub + 2)]
        for jb in range(nsub):
            for w in range(3):
                ktw[jb, g, :, w * BLOCK:(w + 1) * BLOCK] = kts[jb + w]
                vw[jb, g, w * BLOCK:(w + 1) * BLOCK, :HEAD_DIM] = block_rows(vm_ref, vp_ref, vn_ref, jb + w, lanes)

    col = lax.broadcasted_iota(jnp.int32, (1, nloc), 1)

    def rows_of(jb):
        start = jb * BLOCK
        return pl.ds(start if isinstance(start, int) else pl.multiple_of(start, BLOCK), BLOCK)

    def stage_a(jb, g):
        qg = jnp.concatenate([q_ref[rows_of(jb), (g * KV_GROUP + h) * HEAD_DIM:(g * KV_GROUP + h + 1) * HEAD_DIM]
                              for h in range(KV_GROUP)], axis=0)
        s_scr[g] = _dot(qg, ktw[jb, g])

    def stage_b(jb, g):
        blk = i * nsub + jb
        pen_prev = jnp.where(blk == 0, NEG_INF, 0.0).astype(F32)
        pen_next = jnp.where(blk == nb - 1, NEG_INF, 0.0).astype(F32)
        rowbias = jnp.where(col < BLOCK, pen_prev, jnp.where(col >= 2 * BLOCK, pen_next, 0.0))
        for c in range(ATTN_ROWS // SOFTMAX_CHUNK):
            rows = slice(c * SOFTMAX_CHUNK, (c + 1) * SOFTMAX_CHUNK)
            sink = sink_ref[g * KV_GROUP + (c * SOFTMAX_CHUNK) // BLOCK]
            s_loc = s_scr[g, rows, :nloc] + band_ref[rows, :] + rowbias
            s_ctx = s_scr[g, rows, nloc:]
            m = jnp.maximum(jnp.max(s_loc, axis=-1, keepdims=True), jnp.max(s_ctx, axis=-1, keepdims=True))
            m = jnp.maximum(m, sink)
            p_scr[g, rows, :nloc] = jnp.exp(s_loc - m).astype(BF16)
            p_scr[g, rows, nloc:] = jnp.exp(s_ctx - m).astype(BF16)
            m_scr[g, rows, :] = m

    def stage_c(jb, g):
        o = _dot(p_scr[g], vw[jb, g])
        for hi in range(KV_GROUP):
            h = g * KV_GROUP + hi
            rows = slice(hi * BLOCK, (hi + 1) * BLOCK)
            den = o[rows, HEAD_DIM:HEAD_DIM + 1] + jnp.exp(sink_ref[h] - m_scr[g, rows, :])
            o_ref[rows_of(jb), h * HEAD_DIM:(h + 1) * HEAD_DIM] = o[rows, :HEAD_DIM] / den

    stage_a(0, 0)
    stage_a(0, 1)
    stage_b(0, 0)

    def body(j, carry):
        stage_a(j, 0)
        stage_c(j - 1, 0)
        stage_b(j - 1, 1)
        stage_a(j, 1)
        stage_c(j - 1, 1)
        stage_b(j, 0)
        return carry

    lax.fori_loop(1, nsub, body, 0)
    stage_c(nsub - 1, 0)
    stage_b(nsub - 1, 1)
    stage_c(nsub - 1, 1)


def _band_bias():
    qi = np.arange(ATTN_ROWS)[:, None] % BLOCK
    sj = np.arange(3 * BLOCK)[None, :]
    return jnp.asarray(np.where(np.abs(sj - BLOCK - qi) <= WINDOW, 0.0, NEG_INF), dtype=F32)


def _attention(qkv, kv_ctx, sink, local):
    b, n, _ = qkv.shape
    c = kv_ctx.shape[1]
    tq = min(1024 if local else 512, n)
    nsub = tq // BLOCK
    nb = n // BLOCK
    kcol = Q_END // KV_WIDTH
    vcol = K_END // KV_WIDTH
    smem = pl.BlockSpec(memory_space=pltpu.SMEM)
    q_spec = pl.BlockSpec((None, tq, ATTN_WIDTH), lambda bi, i: (bi, i, 0))
    kvc_spec = pl.BlockSpec((None, c, 2 * KV_WIDTH), lambda bi, i: (bi, 0, 0))
    if local:
        def main(colblk):
            return pl.BlockSpec((None, tq, KV_WIDTH), lambda bi, i: (bi, i, colblk))

        def prev(colblk):
            return pl.BlockSpec((None, BLOCK, KV_WIDTH), lambda bi, i: (bi, jnp.maximum(i * nsub - 1, 0), colblk))

        def nxt(colblk):
            return pl.BlockSpec((None, BLOCK, KV_WIDTH), lambda bi, i: (bi, jnp.minimum((i + 1) * nsub, nb - 1), colblk))

        keys = 3 * BLOCK + c
        kern = functools.partial(_win_attn_kernel, tq=tq, nb=nb, nctx=c)
        in_specs = [smem, q_spec, main(kcol), prev(kcol), nxt(kcol), main(vcol), prev(vcol), nxt(vcol), kvc_spec,
                    pl.BlockSpec((ATTN_ROWS, 3 * BLOCK), lambda bi, i: (0, 0))]
        args = [sink, qkv, qkv, qkv, qkv, qkv, qkv, qkv, kv_ctx, _band_bias()]
        scratch = [pltpu.VMEM((nsub, N_KV_HEADS, HEAD_DIM, keys), BF16),
                   pltpu.VMEM((nsub, N_KV_HEADS, keys, 2 * HEAD_DIM), BF16),
                   pltpu.VMEM((2, ATTN_ROWS, keys), F32),
                   pltpu.VMEM((2, ATTN_ROWS, keys), BF16),
                   pltpu.VMEM((2, ATTN_ROWS, 1), F32)]
    else:
        kern = functools.partial(_ctx_attn_kernel, tq=tq)
        in_specs = [smem, q_spec, kvc_spec]
        args = [sink, qkv, kv_ctx]
        scratch = []
    return pl.pallas_call(
        kern,
        out_shape=jax.ShapeDtypeStruct((b, n, ATTN_WIDTH), F32),
        grid=(b, n // tq),
        in_specs=in_specs,
        out_specs=pl.BlockSpec((None, tq, ATTN_WIDTH), lambda bi, i: (bi, i, 0)),
        scratch_shapes=scratch,
        compiler_params=_params(("arbitrary", "arbitrary")),
        name="window_attention" if local else "context_attention",
    )(*args)


GROUPS = FFT_N2 // SUBLANES


def _gm_shape(lead, t1, width):
    return (*lead, width // LANES, GROUPS, t1, SUBLANES, LANES)


def _store_group_major(o_ref, val, tl):
    for t1 in range(tl // FFT_N2):
        for jg in range(GROUPS):
            r0 = (t1 * GROUPS + jg) * SUBLANES
            for cc in range(val.shape[1] // LANES):
                o_ref[cc, jg, t1] = val[r0:r0 + SUBLANES, cc * LANES:(cc + 1) * LANES]


def _hyena_prep_kernel(u_ref, p_ref, n_ref, w_ref, b_ref, vx_ref, x0_ref, *, tl, nt, group_major):
    i = pl.program_id(1)
    u = u_ref[...]
    prev_row = jnp.where(i > 0, p_ref[SUBLANES - 1:SUBLANES, :], 0.0)
    next_row = jnp.where(i < nt - 1, n_ref[0:1, :], 0.0)
    row = lax.broadcasted_iota(jnp.int32, u.shape, 0)
    um = jnp.where(row == 0, prev_row, pltpu.roll(u, 1, 0))
    up = jnp.where(row == tl - 1, next_row, pltpu.roll(u, tl - 1, 0))
    z = um * w_ref[0:1, :] + u * w_ref[1:2, :] + up * w_ref[2:3, :] + b_ref[...]
    x0 = z[:, :HYENA_WIDTH]
    vx = z[:, 2 * HYENA_WIDTH:] * z[:, HYENA_WIDTH:2 * HYENA_WIDTH]
    if group_major:
        _store_group_major(vx_ref, vx, tl)
        _store_group_major(x0_ref, x0, tl)
    else:
        vx_ref[...] = vx
        x0_ref[...] = x0


def _hyena_prep(p, conv_w, conv_b, group_major):
    b, n, _ = p.shape
    tl = min(512, n)
    nt = n // tl
    hw = 3 * HYENA_WIDTH
    colblk = 0
    nrow8 = n // SUBLANES
    per = tl // SUBLANES
    if group_major:
        shape = _gm_shape((b,), n // FFT_N2, HYENA_WIDTH)
        out_spec = pl.BlockSpec((None, *_gm_shape((), tl // FFT_N2, HYENA_WIDTH)), lambda bi, i: (bi, 0, 0, i, 0, 0))
    else:
        shape = (b, n, HYENA_WIDTH)
        out_spec = pl.BlockSpec((None, tl, HYENA_WIDTH), lambda bi, i: (bi, i, 0))
    return pl.pallas_call(
        functools.partial(_hyena_prep_kernel, tl=tl, nt=nt, group_major=group_major),
        out_shape=[jax.ShapeDtypeStruct(shape, F32)] * 2,
        grid=(b, nt),
        in_specs=[
            pl.BlockSpec((None, tl, hw), lambda bi, i: (bi, i, colblk)),
            pl.BlockSpec((None, SUBLANES, hw), lambda bi, i: (bi, jnp.maximum(i * per - 1, 0), colblk)),
            pl.BlockSpec((None, SUBLANES, hw), lambda bi, i: (bi, jnp.minimum((i + 1) * per, nrow8 - 1), colblk)),
            pl.BlockSpec((3, hw), lambda bi, i: (0, 0)),
            pl.BlockSpec((1, hw), lambda bi, i: (0, 0)),
        ],
        out_specs=[out_spec, out_spec],
        compiler_params=_params(("parallel", "parallel")),
        name="hyena_prep",
    )(p, p, p, conv_w, conv_b.reshape(1, hw))


def _filter_kernel(ft_ref, w1_ref, b1_ref, f1_ref, w2_ref, b2_ref, f2_ref, w3_ref, dl_ref, h_ref, s_ref,
                   *, tl, n, group_major):
    i = pl.program_id(0)

    def dense(a, w_ref):
        ah, al = _split(a)
        wh, wl = _split(w_ref[...])
        return _dot3(ah, al, wh, wl)

    h = jnp.sin(f1_ref[...] * (dense(ft_ref[...], w1_ref) + b1_ref[...]))
    h = jnp.sin(f2_ref[...] * (dense(h, w2_ref) + b2_ref[...]))
    h = dense(h, w3_ref)
    hw2 = 2 * HYENA_WIDTH
    h = jnp.concatenate([h[:, :hw2], h[:, hw2:]], axis=0)
    t = (i * tl + lax.broadcasted_iota(jnp.int32, (tl, HYENA_WIDTH), 0)).astype(F32) / float(n - 1)
    decay = jnp.exp(-t * dl_ref[...])
    h = h * jnp.concatenate([decay, decay], axis=1)
    if group_major:
        _store_group_major(h_ref, h, tl)
    else:
        h_ref[...] = h

    @pl.when(i == 0)
    def _():
        s_ref[...] = jnp.zeros_like(s_ref)

    s_ref[...] += jnp.sum(jnp.abs(h).reshape(tl // SUBLANES, SUBLANES, 2 * HYENA_WIDTH), axis=0)


def _filter_features(n):
    t = jnp.linspace(0.0, 1.0, n, dtype=F32)[:, None]
    bands = (HYENA_EMB_DIM - 1) // 2
    omega = 2.0 * math.pi * jnp.arange(n, dtype=F32)[:, None] / n
    f = jnp.linspace(1e-4, bands - 1, bands, dtype=F32)[None, :]
    feats = jnp.concatenate([t, jnp.cos(f * omega), -jnp.sin(f * omega)], axis=-1)
    return jnp.pad(feats, ((0, 0), (0, HYENA_FILTER_HIDDEN - HYENA_EMB_DIM)))


def _filter_deltas():
    max_decay = math.log(HYENA_DECAY_TARGET) / HYENA_FAST_DECAY_PCT
    min_decay = math.log(HYENA_DECAY_TARGET) / HYENA_SLOW_DECAY_PCT
    return jnp.abs(jnp.linspace(min_decay, max_decay, HYENA_WIDTH, dtype=F32)).reshape(1, HYENA_WIDTH)


def _hyena_filter(n, feats, deltas, w1, b1, f1, w2, b2, f2, w3, group_major):
    tl = min(512, n)
    hh = HYENA_FILTER_HIDDEN
    hid = 2 * hh
    assert hid == LANES and HYENA_EMB_DIM <= hh
    hw2 = 2 * HYENA_WIDTH
    blockdiag = lambda w: jnp.concatenate(
        [jnp.concatenate([w, jnp.zeros_like(w)], axis=1), jnp.concatenate([jnp.zeros_like(w), w], axis=1)], axis=0)
    w1p = blockdiag(jnp.pad(w1, ((0, hh - HYENA_EMB_DIM), (0, 0))))
    w2p = blockdiag(w2)
    w3p = blockdiag(w3)
    vec = lambda v: jnp.tile(v, 2).reshape(1, hid)
    feats = feats.reshape(n // tl, 2, tl // 2, hh).transpose(0, 2, 1, 3).reshape(n // 2, hid)
    full = lambda shape: pl.BlockSpec(shape, lambda i: (0,) * len(shape))
    if group_major:
        shape = _gm_shape((), n // FFT_N2, hw2)
        out_spec = pl.BlockSpec(_gm_shape((), tl // FFT_N2, hw2), lambda i: (0, 0, i, 0, 0))
    else:
        shape = (n, hw2)
        out_spec = pl.BlockSpec((tl, hw2), lambda i: (i, 0))
    return pl.pallas_call(
        functools.partial(_filter_kernel, tl=tl, n=n, group_major=group_major),
        out_shape=[jax.ShapeDtypeStruct(shape, F32), jax.ShapeDtypeStruct((SUBLANES, hw2), F32)],
        grid=(n // tl,),
        in_specs=[pl.BlockSpec((tl // 2, hid), lambda i: (i, 0)), full((hid, hid)), full((1, hid)), full((1, hid)),
                  full((hid, hid)), full((1, hid)), full((1, hid)), full((hid, 2 * hw2)), full((1, HYENA_WIDTH))],
        out_specs=[out_spec, full((SUBLANES, hw2))],
        compiler_params=_params(("arbitrary",)),
        name="hyena_filter",
    )(feats, w1p, vec(b1), vec(f1), w2p, vec(b2), vec(f2), w3p, deltas)


def _stack_complex(m):
    return np.block([[m.real, -m.imag], [m.imag, m.real]])


def _hilo(m):
    m = jnp.asarray(m, dtype=F32)
    return _split(m)


@functools.lru_cache(maxsize=None)
def _fft_constants(n):
    m = 2 * n
    n2 = FFT_N2
    n1 = m // n2
    n1h = n1 // 2
    k1 = np.arange(n1)
    t1 = np.arange(n1h)
    f1 = np.exp(-2j * np.pi * np.outer(k1, t1) / n1)
    f3 = np.exp(2j * np.pi * np.outer(t1, k1) / n1) / m
    k2 = np.arange(n2)
    t2 = np.arange(n2)
    w2 = np.exp(-2j * np.pi * np.outer(k2, t2) / n2)
    tw = np.exp(-2j * np.pi * np.outer(k1, t2) / m)
    return dict(
        n1=n1, n1h=n1h,
        f1c=_stack_complex(f1), f1r=np.concatenate([f1.real, f1.imag], axis=0),
        f3c=_stack_complex(f3),
        w2r=w2.real.astype(np.float32), w2i=w2.imag.astype(np.float32),
        twr=tw.real.astype(np.float32), twi=tw.imag.astype(np.float32),
    )


def _stage2_tables(n):
    c = _fft_constants(n)
    w2r, w2i = jnp.asarray(c["w2r"])[None], jnp.asarray(c["w2i"])[None]
    twr, twi = jnp.asarray(c["twr"])[:, None, :], jnp.asarray(c["twi"])[:, None, :]
    return (*_split(w2r * twr - w2i * twi), *_split(w2r * twi + w2i * twr))


def _stacked_tables(grh_ref, grl_ref, gih_ref, gil_ref, q):
    def stack(gr, gi):
        return jnp.concatenate([jnp.concatenate([gr, -gi], axis=1), jnp.concatenate([gi, gr], axis=1)], axis=0)
    return stack(grh_ref[q], gih_ref[q]), stack(grl_ref[q], gil_ref[q])


def _s1_kernel(x_ref, fh_ref, fl_ref, o_ref, *, nparts, ncw, n1, n1h):
    fh = fh_ref[...]
    fl = fl_ref[...]
    for r in range(SUBLANES):
        rows = pl.ds(r, n1h, stride=SUBLANES)
        xs = jnp.concatenate(
            [jnp.concatenate([x_ref[p, cc, rows, :] for p in range(nparts)], axis=0) for cc in range(ncw)], axis=1)
        xh, xl = _split(xs)
        res = _dot3(fh, fl, xh, xl)
        for ri in range(2):
            for cc in range(ncw):
                o_ref[ri, cc, pl.ds(r, n1, stride=SUBLANES), :] = res[ri * n1:(ri + 1) * n1, cc * LANES:(cc + 1) * LANES]


def _fft_stage1(x, fmat, n1, n1h, ncw):
    nparts, ncc, groups = x.shape[:3]
    fh, fl = _hilo(fmat)
    return pl.pallas_call(
        functools.partial(_s1_kernel, nparts=nparts, ncw=ncw, n1=n1, n1h=n1h),
        out_shape=jax.ShapeDtypeStruct((2, ncc, groups, n1 * SUBLANES, LANES), F32),
        grid=(groups, ncc // ncw),
        in_specs=[
            pl.BlockSpec((nparts, ncw, None, n1h * SUBLANES, LANES), lambda j, ci: (0, ci, j, 0, 0)),
            pl.BlockSpec(fh.shape, lambda j, ci: (0, 0)),
            pl.BlockSpec(fl.shape, lambda j, ci: (0, 0)),
        ],
        out_specs=pl.BlockSpec((2, ncw, None, n1 * SUBLANES, LANES), lambda j, ci: (0, ci, j, 0, 0)),
        compiler_params=_params(("parallel", "parallel")),
        name="fft_stage1",
    )(x, fh, fl)


def _load_k1(a_ref, q):
    ncc = a_ref.shape[1]
    return jnp.concatenate([a_ref[:, cc, :, q].reshape(2 * FFT_N2, LANES) for cc in range(ncc)], axis=1)


def _filter_spectrum_kernel(a_ref, grh_ref, grl_ref, gih_ref, gil_ref, s_ref, o_ref, *, kg):
    s = jnp.sum(s_ref[...], axis=0, keepdims=True)
    inv = 1.0 / (s[:, :HYENA_WIDTH] + s[:, HYENA_WIDTH:])
    half = FFT_N2
    for q in range(kg):
        ah, al = _split(_load_k1(a_ref, q))
        gh, gl = _stacked_tables(grh_ref, grl_ref, gih_ref, gil_ref, q)
        h = _dot3(gh, gl, ah, al)
        hf = h[:, :HYENA_WIDTH]
        hb = h[:, HYENA_WIDTH:]
        o_ref[q, :half] = (hf[:half] + hb[:half]) * inv
        o_ref[q, half:] = (hf[half:] - hb[half:]) * inv


FFT_K1_GROUP = 8


def _k1_spec(kg, width):
    return pl.BlockSpec((2, width // LANES, GROUPS, kg, SUBLANES, LANES), lambda i: (0, 0, 0, i, 0, 0))


def _filter_spectrum(a, tables, sums, n1):
    kg = FFT_K1_GROUP
    a6 = a.reshape(2, 2 * HYENA_WIDTH // LANES, GROUPS, n1, SUBLANES, LANES)
    tspec = pl.BlockSpec((kg, FFT_N2, FFT_N2), lambda i: (i, 0, 0))
    return pl.pallas_call(
        functools.partial(_filter_spectrum_kernel, kg=kg),
        out_shape=jax.ShapeDtypeStruct((n1, 2 * FFT_N2, HYENA_WIDTH), F32),
        grid=(n1 // kg,),
        in_specs=[_k1_spec(kg, 2 * HYENA_WIDTH), tspec, tspec, tspec, tspec,
                  pl.BlockSpec((SUBLANES, 2 * HYENA_WIDTH), lambda i: (0, 0))],
        out_specs=pl.BlockSpec((kg, 2 * FFT_N2, HYENA_WIDTH), lambda i: (i, 0, 0)),
        compiler_params=_params(("parallel",)),
        name="filter_spectrum",
    )(a6, *tables, sums)


def _dot_t(a, b):
    return lax.dot_general(a, b, (((0,), (0,)), ((), ())), preferred_element_type=F32)


def _s2_kernel(a_ref, kf_ref, grh_ref, grl_ref, gih_ref, gil_ref, o_ref, *, kg):
    half = FFT_N2
    for q in range(kg):
        ah, al = _split(_load_k1(a_ref, q))
        gh, gl = _stacked_tables(grh_ref, grl_ref, gih_ref, gil_ref, q)
        x = _dot3(gh, gl, ah, al)
        xr, xi = x[:half], x[half:]
        kr, ki = kf_ref[q, :half], kf_ref[q, half:]
        y = jnp.concatenate([xr * kr - xi * ki, xr * ki + xi * kr], axis=0)
        yh, yl = _split(y)
        bt = _dot_t(gh, yh) + _dot_t(gh, yl) + _dot_t(gl, yh)
        for cc in range(HYENA_WIDTH // LANES):
            o_ref[:, cc, :, q] = bt[:, cc * LANES:(cc + 1) * LANES].reshape(2, GROUPS, SUBLANES, LANES)


def _fft_stage2(a, kf, tables, n1):
    kg = FFT_K1_GROUP
    a6 = a.reshape(2, HYENA_WIDTH // LANES, GROUPS, n1, SUBLANES, LANES)
    tspec = pl.BlockSpec((kg, FFT_N2, FFT_N2), lambda i: (i, 0, 0))
    dspec = _k1_spec(kg, HYENA_WIDTH)
    out = pl.pallas_call(
        functools.partial(_s2_kernel, kg=kg),
        out_shape=jax.ShapeDtypeStruct(a6.shape, F32),
        grid=(n1 // kg,),
        in_specs=[dspec, pl.BlockSpec((kg, 2 * FFT_N2, HYENA_WIDTH), lambda i: (i, 0, 0)), tspec, tspec, tspec, tspec],
        out_specs=dspec,
        compiler_params=_params(("parallel",)),
        name="fft_stage2",
    )(a6, kf, *tables)
    return out.reshape(a.shape)


def _s3_kernel(b_ref, vx_ref, x0_ref, bias_ref, fh_ref, fl_ref, o_ref, *, ncw, n1, n1h):
    fh = fh_ref[...]
    fl = fl_ref[...]
    for r in range(SUBLANES):
        krows = pl.ds(r, n1, stride=SUBLANES)
        z = jnp.concatenate(
            [jnp.concatenate([b_ref[ri, cc, krows, :] for ri in range(2)], axis=0) for cc in range(ncw)], axis=1)
        zh, zl = _split(z)
        y = _dot3(fh, fl, zh, zl)
        trows = pl.ds(r, n1h, stride=SUBLANES)
        for p in range(2):
            for cc in range(ncw):
                yy = y[p * n1h:(p + 1) * n1h, cc * LANES:(cc + 1) * LANES]
                o_ref[p, cc, trows, :] = (yy + vx_ref[p, cc, trows, :] * bias_ref[cc]) * x0_ref[p, cc, trows, :]


def _fft_stage3(bt, vx, x0, bias, fmat, n1, n1h, ncw):
    ncc, groups = bt.shape[1:3]
    fh, fl = _hilo(fmat)
    tspec = pl.BlockSpec((2, ncw, None, n1h * SUBLANES, LANES), lambda j, ci: (0, ci, j, 0, 0))
    return pl.pallas_call(
        functools.partial(_s3_kernel, ncw=ncw, n1=n1, n1h=n1h),
        out_shape=jax.ShapeDtypeStruct(vx.shape, F32),
        grid=(groups, ncc // ncw),
        in_specs=[
            pl.BlockSpec((2, ncw, None, n1 * SUBLANES, LANES), lambda j, ci: (0, ci, j, 0, 0)),
            tspec, tspec,
            pl.BlockSpec((ncw, 1, LANES), lambda j, ci: (ci, 0, 0)),
            pl.BlockSpec(fh.shape, lambda j, ci: (0, 0)),
            pl.BlockSpec(fl.shape, lambda j, ci: (0, 0)),
        ],
        out_specs=tspec,
        compiler_params=_params(("parallel", "parallel")),
        name="fft_stage3",
    )(bt, vx, x0, bias, fh, fl)


def _hyena_latent(p, hy, feats, deltas, tables, n):
    conv_w, conv_b, w1, b1, f1, w2, b2, f2, w3, bias = hy
    c = _fft_constants(n)
    n1, n1h = c["n1"], c["n1h"]
    vx, x0 = _hyena_prep(p, conv_w, conv_b, group_major=True)
    gm_shape = vx.shape
    rows = lambda a: a.reshape(*a.shape[:-3], n1h * SUBLANES, LANES)
    vx, x0 = rows(vx), rows(x0)
    taps, sums = _hyena_filter(n, feats, deltas, w1, b1, f1, w2, b2, f2, w3, group_major=True)
    a_f = _fft_stage1(rows(taps)[None], c["f1r"], n1, n1h, ncw=2)
    kf = _filter_spectrum(a_f, tables, sums, n1)
    a = _fft_stage1(vx, c["f1c"], n1, n1h, ncw=2)
    bt = _fft_stage2(a, kf, tables, n1)
    out = _fft_stage3(bt, vx, x0, bias.reshape(HYENA_WIDTH // LANES, 1, LANES), c["f3c"], n1, n1h, ncw=2)
    return out.reshape(gm_shape)


@functools.lru_cache(maxsize=None)
def _dense_dft_constants(n):
    m = 2 * n
    k = np.arange(m)
    t = np.arange(n)
    f = np.exp(-2j * np.pi * np.outer(k, t) / m)
    finv = np.exp(2j * np.pi * np.outer(t, k) / m) / m
    return _stack_complex(f), np.concatenate([f.real, f.imag], axis=0), _stack_complex(finv)


def _dense_conv_kernel(vx_ref, x0_ref, hf_ref, hb_ref, sf_ref, sb_ref, bias_ref,
                       fch, fcl, frh, frl, fih, fil, o_ref, *, n):
    m = 2 * n
    z = jnp.concatenate([vx_ref[0], vx_ref[1]], axis=0)
    zh, zl = _split(z)
    zf = _dot3(fch[...], fcl[...], zh, zl)
    hfh, hfl = _split(hf_ref[...])
    hbh, hbl = _split(hb_ref[...])
    hf = _dot3(frh[...], frl[...], hfh, hfl)
    hb = _dot3(frh[...], frl[...], hbh, hbl)
    inv = 1.0 / (jnp.sum(sf_ref[...], axis=0, keepdims=True) + jnp.sum(sb_ref[...], axis=0, keepdims=True))
    kr = (hf[:m] + hb[:m]) * inv
    ki = (hf[m:] - hb[m:]) * inv
    zr, zi = zf[:m], zf[m:]
    y = jnp.concatenate([zr * kr - zi * ki, zr * ki + zi * kr], axis=0)
    yh, yl = _split(y)
    out = _dot3(fih[...], fil[...], yh, yl)
    bias = bias_ref[...]
    for p in range(2):
        o_ref[p] = (out[p * n:(p + 1) * n] + vx_ref[p] * bias) * x0_ref[p]


def _hyena_context(p, hy, feats, deltas, n):
    conv_w, conv_b, w1, b1, f1, w2, b2, f2, w3, bias = hy
    vx, x0 = _hyena_prep(p, conv_w, conv_b, group_major=False)
    taps, sums = _hyena_filter(n, feats, deltas, w1, b1, f1, w2, b2, f2, w3, group_major=False)
    fc, fr, fi = _dense_dft_constants(n)
    mats = [*_hilo(fc), *_hilo(fr), *_hilo(fi)]
    cw = 256
    nct = HYENA_WIDTH // cw
    dspec = pl.BlockSpec((2, n, cw), lambda ci: (0, 0, ci))
    return pl.pallas_call(
        functools.partial(_dense_conv_kernel, n=n),
        out_shape=jax.ShapeDtypeStruct(vx.shape, F32),
        grid=(nct,),
        in_specs=[dspec, dspec,
                  pl.BlockSpec((n, cw), lambda ci: (0, ci)), pl.BlockSpec((n, cw), lambda ci: (0, nct + ci)),
                  pl.BlockSpec((SUBLANES, cw), lambda ci: (0, ci)), pl.BlockSpec((SUBLANES, cw), lambda ci: (0, nct + ci)),
                  pl.BlockSpec((1, cw), lambda ci: (0, ci))]
                 + [pl.BlockSpec(mt.shape, lambda ci: (0, 0)) for mt in mats],
        out_specs=dspec,
        compiler_params=_params(("parallel",)),
        name="context_long_conv",
    )(vx, x0, taps, taps, sums, sums, bias.reshape(1, HYENA_WIDTH), *mats)


def _pool_kernel(x_ref, p_ref, n_ref, w_ref, sc_ref, o_ref, *, tl, nt, n):
    i = pl.program_id(1)
    x = x_ref[...]
    pv = jnp.where(i > 0, p_ref[...], 0.0)
    nx = jnp.where(i < nt - 1, n_ref[...], 0.0)
    ext = jnp.concatenate([pv, x, nx], axis=0)
    rows = tl + 2 * POOL_HALO
    t = i * tl + lax.broadcasted_iota(jnp.int32, (tl, POOL_GROUP), 0)
    for g, w in enumerate(POOL_WINDOWS):
        lanes = slice(g * POOL_GROUP, (g + 1) * POOL_GROUP)
        a = ext[:, lanes]
        c = a + pltpu.roll(a, 1, 0)
        h = 1
        while 2 * h < w:
            c = pltpu.roll(c, h, 0) + pltpu.roll(c, rows - h, 0)
            h *= 2
        total = c[POOL_HALO:POOL_HALO + tl]
        count = (jnp.minimum(t + h, n) - jnp.maximum(t - h, 0)).astype(F32)
        y = (total / count - x[:, lanes]).astype(BF16)
        o_ref[:, lanes] = _dot(y, w_ref[g].astype(BF16)) * sc_ref[:, lanes]


def _pool_mixer(p, w_pool, scale):
    b, n, _ = p.shape
    tl = min(512, n)
    nt = n // tl
    colblk = 3 * HYENA_WIDTH // POOL_WIDTH
    assert colblk * POOL_WIDTH == 3 * HYENA_WIDTH
    per = tl // POOL_HALO
    nrow = n // POOL_HALO
    return pl.pallas_call(
        functools.partial(_pool_kernel, tl=tl, nt=nt, n=n),
        out_shape=jax.ShapeDtypeStruct((b, n, POOL_WIDTH), F32),
        grid=(b, nt),
        in_specs=[
            pl.BlockSpec((None, tl, POOL_WIDTH), lambda bi, i: (bi, i, colblk)),
            pl.BlockSpec((None, POOL_HALO, POOL_WIDTH), lambda bi, i: (bi, jnp.maximum(i * per - 1, 0), colblk)),
            pl.BlockSpec((None, POOL_HALO, POOL_WIDTH), lambda bi, i: (bi, jnp.minimum((i + 1) * per, nrow - 1), colblk)),
            pl.BlockSpec(w_pool.shape, lambda bi, i: (0, 0, 0)),
            pl.BlockSpec((1, POOL_WIDTH), lambda bi, i: (0, 0)),
        ],
        out_specs=pl.BlockSpec((None, tl, POOL_WIDTH), lambda bi, i: (bi, i, 0)),
        compiler_params=_params(("parallel", "parallel")),
        name="pool_mixer",
    )(p, p, p, w_pool, scale.reshape(1, POOL_WIDTH))


def _load_group_major(ref, tl):
    ncc = ref.shape[0]
    return jnp.concatenate(
        [jnp.concatenate([ref[cc, jg, t1] for cc in range(ncc)], axis=1)
         for t1 in range(tl // FFT_N2) for jg in range(GROUPS)], axis=0)


def _outproj_kernel(at_ref, hy_ref, po_ref, x_ref, gb_ref, w_ref, gp_ref, gt_ref, o_ref, ox_ref, m_ref,
                    *, tm, tiles, group_major):
    t = pl.program_id(0)

    @pl.when(t == 0)
    def _():
        ox_ref[...] = jnp.zeros_like(ox_ref)

    def finish_previous():
        o_ref[...] = x_ref[...] + gt_ref[...] * _rms(ox_ref[...], gp_ref[...])

    @pl.when(t < tiles)
    def _():
        finish_previous()
        hy = _load_group_major(hy_ref, tm) if group_major else hy_ref[...]
        a0, a1 = ATTN_WIDTH, ATTN_WIDTH + HYENA_WIDTH
        m_ref[:, :a0] = _rms(at_ref[...], gb_ref[:, :a0]).astype(BF16)
        m_ref[:, a0:a1] = _rms(hy, gb_ref[:, a0:a1]).astype(BF16)
        m_ref[:, a1:] = _rms(po_ref[...], gb_ref[:, a1:]).astype(BF16)
        ox_ref[...] = _dot(m_ref[...], w_ref[...])

    pl.when(t == tiles)(finish_previous)


def _out_projection(attn, hy, po, x, g_branch, w_out, layer, g_post, gate, group_major):
    b, n, d = x.shape
    tm = min(512, n)
    nt = n // tm
    tiles = b * nt

    def cur(t):
        return jnp.minimum(t, tiles - 1)

    def prev(t):
        return jnp.maximum(t - 1, 0)

    row = lambda width, sel: pl.BlockSpec((None, tm, width), lambda t: (sel(t) // nt, sel(t) % nt, 0))
    vec = lambda width: pl.BlockSpec((1, width), lambda t: (0, 0))
    if group_major:
        hy_spec = pl.BlockSpec((None, *_gm_shape((), tm // FFT_N2, HYENA_WIDTH)),
                               lambda t: (cur(t) // nt, 0, 0, cur(t) % nt, 0, 0))
    else:
        hy_spec = row(HYENA_WIDTH, cur)
    return pl.pallas_call(
        functools.partial(_outproj_kernel, tm=tm, tiles=tiles, group_major=group_major),
        out_shape=jax.ShapeDtypeStruct(x.shape, F32),
        grid=(tiles + 1,),
        in_specs=[row(ATTN_WIDTH, cur), hy_spec, row(POOL_WIDTH, cur), row(d, prev), vec(d),
                  pl.BlockSpec((None, *w_out.shape[1:]), lambda t: (layer, 0, 0), pipeline_mode=pl.Buffered(1)), vec(d),
                  pl.BlockSpec((None, 1, d), lambda t: (prev(t) // nt, 0, 0))],
        out_specs=row(d, prev),
        scratch_shapes=[pltpu.VMEM((tm, d), F32), pltpu.VMEM((tm, d), BF16)],
        compiler_params=_params(("arbitrary",)),
        name="out_projection",
    )(attn, hy, po, x, g_branch.reshape(1, d), w_out, g_post.reshape(1, d), gate)


def _mlp_kernel(xe_ref, xp_ref, g_ref, shn_ref, scn_ref, wu_ref, wd_ref, gp_ref, gtp_ref, o_ref,
                h0_ref, h1_ref, acc0_ref, acc1_ref, *, rc, tiles):
    t = pl.program_id(0)
    k = pl.program_id(1)
    rows = pl.ds(pl.multiple_of(k * rc, rc), rc)

    def normalise(h_ref):
        a = g_ref[...] * (1.0 + scn_ref[...])
        h_ref[rows, :] = (_rms(xp_ref[...], a) + shn_ref[...]).astype(BF16)

    def finish(acc_ref):
        o_ref[...] = xe_ref[...] + _rms(acc_ref[rows, :], gp_ref[...] * gtp_ref[...])

    @pl.when(t == 0)
    def _():
        normalise(h0_ref)
        acc0_ref[rows, :] = jnp.zeros((rc, acc0_ref.shape[1]), F32)
        acc1_ref[rows, :] = jnp.zeros((rc, acc1_ref.shape[1]), F32)

    def step(slot):
        h_cur, h_oth = (h0_ref, h1_ref) if slot == 0 else (h1_ref, h0_ref)
        acc_cur, acc_oth = (acc0_ref, acc1_ref) if slot == 0 else (acc1_ref, acc0_ref)
        finish(acc_oth)
        acc_oth[rows, :] = jnp.zeros((rc, acc_oth.shape[1]), F32)
        normalise(h_oth)
        u = jnp.maximum(_dot(h_cur[...], wu_ref[...]), 0.0)
        acc_cur[...] += _dot((u * u).astype(BF16), wd_ref[...])

    work = (t >= 1) & (t <= tiles)
    pl.when(work & (t % 2 == 1))(functools.partial(step, 0))
    pl.when(work & (t % 2 == 0))(functools.partial(step, 1))
    pl.when(t == tiles + 1)(functools.partial(finish, acc0_ref if (tiles - 1) % 2 == 0 else acc1_ref))


MLP_TH = 1024


def _mlp(x, g_pre, shift, scale, w_up, w_down, layer, g_post, gate):
    b, n, d = x.shape
    th = MLP_TH
    nk = w_up.shape[2] // th
    tm = min(1024, n)
    nt = n // tm
    tiles = b * nt
    rc = tm // nk
    assert rc * nk == tm and rc % BF16_ROWS == 0

    def norm_tile(t):
        return jnp.minimum(t, tiles - 1)

    def done_tile(t):
        return jnp.clip(t - 2, 0, tiles - 1)

    vec = pl.BlockSpec((1, d), lambda t, k: (0, 0))
    bvec = lambda sel: pl.BlockSpec((None, 1, d), lambda t, k: (sel(t) // nt, 0, 0))
    chunk = lambda sel, first: pl.BlockSpec(
        (None, rc, d), lambda t, k: (sel(t) // nt, (sel(t) % nt) * nk + jnp.where(t < first, 0, k), 0))
    busy = lambda t: (t >= 1) & (t <= tiles)
    return pl.pallas_call(
        functools.partial(_mlp_kernel, rc=rc, tiles=tiles),
        out_shape=jax.ShapeDtypeStruct(x.shape, F32),
        grid=(tiles + 2, nk),
        in_specs=[chunk(done_tile, 2), chunk(norm_tile, 0), vec, bvec(norm_tile), bvec(norm_tile),
                  pl.BlockSpec((None, d, th), lambda t, k: (layer, 0, jnp.where(busy(t), k, 0))),
                  pl.BlockSpec((None, th, d), lambda t, k: (layer, jnp.where(busy(t), k, 0), 0)),
                  vec, bvec(done_tile)],
        out_specs=chunk(done_tile, 2),
        scratch_shapes=[pltpu.VMEM((tm, d), BF16)] * 2 + [pltpu.VMEM((tm, d), F32)] * 2,
        compiler_params=_params(("arbitrary", "arbitrary")),
        name="mlp",
    )(x, x, g_pre.reshape(1, d), shift, scale, w_up, w_down, g_post.reshape(1, d), gate)


def kernel(x, c, ctx, c_ctx, w_mod, b_mod, g_pre_mix, g_post_mix, g_pre_mlp, g_post_mlp, w_in, w_out, g_branch,
           attn_sink, hy_conv_w, hy_conv_b, hy_w1, hy_b1, hy_freq1, hy_w2, hy_b2, hy_freq2, hy_w3, hy_bias,
           pool_w, pool_scale, w_up, w_down):
    b, n, d = x.shape
    n_ctx = ctx.shape[1]
    depth = w_mod.shape[0]
    assert b == 2 and d == D_MODEL and n % 512 == 0 and n_ctx % BLOCK == 0

    cond = jnp.concatenate([c, c_ctx[None], jnp.zeros((SUBLANES - b - 1, d), F32)], axis=0)
    mods = _modulation(cond, w_mod, b_mod)

    w_in_b = _prepare_w_in(w_in, IN_TN)
    w_up_b = w_up.astype(BF16)
    w_out_b, w_down_b = w_out.astype(BF16), w_down.astype(BF16)
    tables_x, tables_c = _rope_tables(n), _rope_tables(n_ctx)
    feats_x, feats_c = _filter_features(n), _filter_features(n_ctx)
    deltas = _filter_deltas()
    tables = _stage2_tables(n)

    for i in range(depth):
        last = i == depth - 1
        hy = (hy_conv_w[i], hy_conv_b[i], hy_w1[i], hy_b1[i], hy_freq1[i], hy_w2[i], hy_b2[i], hy_freq2[i],
              hy_w3[i], hy_bias[i])
        mx = [m[:, None, :] for m in jnp.split(mods[i, :b], N_MOD, axis=-1)]
        mc = [jnp.broadcast_to(m[None, None, :], (b, 1, d)) for m in jnp.split(mods[i, b], N_MOD, axis=-1)]

        qkv_x, px = _in_projection(x, g_pre_mix[i], mx[0], mx[1], w_in_b, i, tables_x, rope=True)
        qkv_c, pc = _in_projection(ctx, g_pre_mix[i], mc[0], mc[1], w_in_b, i, tables_c, rope=False)
        kv_ctx = qkv_c[..., Q_END:V_END]

        attn_x = _attention(qkv_x, kv_ctx, attn_sink[i], local=True)
        hy_x = _hyena_latent(px, hy, feats_x, deltas, tables, n)
        po_x = _pool_mixer(px, pool_w[i], pool_scale[i])
        x = _out_projection(attn_x, hy_x, po_x, x, g_branch[i], w_out_b, i, g_post_mix[i], mx[2], group_major=True)
        x = _mlp(x, g_pre_mlp[i], mx[3], mx[4], w_up_b, w_down_b, i, g_post_mlp[i], mx[5])

        if not last:
            attn_c = _attention(qkv_c, kv_ctx, attn_sink[i], local=False)
            hy_c = _hyena_context(pc, hy, feats_c, deltas, n_ctx)
            po_c = _pool_mixer(pc, pool_w[i], pool_scale[i])
            ctx = _out_projection(attn_c, hy_c, po_c, ctx, g_branch[i], w_out_b, i, g_post_mix[i], mc[2],
                                  group_major=False)
            ctx = _mlp(ctx, g_pre_mlp[i], mc[3], mc[4], w_up_b, w_down_b, i, g_post_mlp[i], mc[5])
    return x
```
